```python
import jax, jax.numpy as jnp
from jax import lax
import numpy as np

D_MODEL = 1024
BATCH = 8
SEQ = 2048
DEPTH = 2
DEC_BATCH = 128
DEC_SEQ = 1
PAST_LEN = 2048
PAGE_SIZE = 128

D_CONV = 256
D_POOL = 256
N_HEADS = 8
N_KV_HEADS = 2
HEAD_DIM = 64
D_ATTN = N_HEADS * HEAD_DIM
D_MIX = D_CONV + D_POOL + D_ATTN
CONV_WIDTH = 31
CONV_STATE = CONV_WIDTH - 1
POOL_WINDOWS = (2, 4, 8, 16)
N_POOL_GROUPS = len(POOL_WINDOWS)
POOL_GROUP = D_POOL // N_POOL_GROUPS
POOL_STATE = max(POOL_WINDOWS) - 1
CMP_BLOCK = 32
CMP_STRIDE = 16
SLC_BLOCK = 64
N_SEL = 8
WINDOW = 256
Q_BLOCK = 128
ROT_DIM = HEAD_DIM // 4
ROPE_THETA = 500000.0
N_BRANCH = 3
D_KV = N_BRANCH * 2 * N_KV_HEADS * HEAD_DIM
D_IN = 3 * D_CONV + 2 * D_POOL + 2 * D_ATTN + D_KV + N_BRANCH * N_HEADS
EPS = 1e-6
NEG = -1e30
BIG = 1e4

kernel_name = 'hymba_conv_pool_nsa_step'


def rmsnorm(x, g):
    xf = x.astype(jnp.float32)
    y = xf * lax.rsqrt(jnp.mean(xf * xf, axis=-1, keepdims=True) + EPS)
    return (y * g.astype(jnp.float32)).astype(x.dtype)


def layernorm(x, g, b):
    xf = x.astype(jnp.float32)
    mu = jnp.mean(xf, axis=-1, keepdims=True)
    xc = xf - mu
    var = jnp.mean(xc * xc, axis=-1, keepdims=True)
    return (xc * lax.rsqrt(var + EPS) * g.astype(jnp.float32) + b.astype(jnp.float32)).astype(x.dtype)


def rope(x, pos):
    half = ROT_DIM // 2
    inv = ROPE_THETA ** (-jnp.arange(half, dtype=jnp.float32) * 2.0 / ROT_DIM)
    ang = pos.astype(jnp.float32)[:, None] * inv[None, :]
    shp = (pos.shape[0],) + (1,) * (x.ndim - 3) + (half,)
    cos = jnp.cos(ang).reshape(shp)
    sin = jnp.sin(ang).reshape(shp)
    x1 = x[..., :half].astype(jnp.float32)
    x2 = x[..., half:ROT_DIM].astype(jnp.float32)
    out = jnp.concatenate([x1 * cos - x2 * sin, x2 * cos + x1 * sin, x[..., ROT_DIM:].astype(jnp.float32)], axis=-1)
    return out.astype(x.dtype)


def masked_probs(s, mask):
    s = jnp.where(mask, s, NEG)
    m = jnp.max(s, axis=-1, keepdims=True)
    p = jnp.where(mask, jnp.exp(s - m), 0.0)
    return p / jnp.maximum(jnp.sum(p, axis=-1, keepdims=True), 1e-30)


def conv_mixer(u, buf, w_dw, b_dw, ln_g, ln_b, w_pw, b_pw):
    xcat = jnp.concatenate([buf.astype(u.dtype), u], axis=1)
    y = lax.conv_general_dilated(xcat, w_dw[:, None, :].astype(u.dtype), (1,), 'VALID',
                                 dimension_numbers=('NWC', 'WIO', 'NWC'), feature_group_count=u.shape[-1])
    y = layernorm(y + b_dw, ln_g, ln_b)
    y = jax.nn.silu(y) @ w_pw + b_pw
    return y.astype(u.dtype), xcat[:, -CONV_STATE:]


def pool_mixer(u, buf, pos, w_pool, pool_scale):
    N, T, _ = u.shape
    L0 = buf.shape[1]
    xcat = jnp.concatenate([buf.astype(u.dtype), u], axis=1)
    xf = xcat.astype(jnp.float32)
    cs = jnp.pad(jnp.cumsum(xf, axis=1), ((0, 0), (1, 0), (0, 0)))
    end = L0 + 1 + jnp.arange(T)
    diffs = []
    for gi, w in enumerate(POOL_WINDOWS):
        c0, c1 = gi * POOL_GROUP, (gi + 1) * POOL_GROUP
        total = cs[:, end, c0:c1] - cs[:, end - w, c0:c1]
        cnt = jnp.minimum(w, pos + 1).astype(jnp.float32)[None, :, None]
        diffs.append(total / cnt - xf[:, L0:, c0:c1])
    d = jnp.stack(diffs, axis=2)
    y = jnp.einsum('ntgc,gce->ntge', d, w_pool.astype(jnp.float32)).reshape(N, T, D_POOL)
    y = y * pool_scale.astype(jnp.float32)
    return y.astype(u.dtype), xcat[:, -POOL_STATE:]


def compress(k, pe, w1, w2):
    N, L, G, D = k.shape
    r = CMP_BLOCK // CMP_STRIDE
    n_pc = L // CMP_STRIDE
    n_blk = n_pc - r + 1
    pieces = k.reshape(N, n_pc, CMP_STRIDE, G, D)
    blocks = jnp.concatenate([pieces[:, j:j + n_blk] for j in range(r)], axis=2)
    blocks = blocks + pe[None, None, :, None, :]
    flat = jnp.swapaxes(blocks, 2, 3).reshape(N, n_blk, G, CMP_BLOCK * D)
    return jax.nn.gelu(flat @ w1) @ w2


def cmp_slc_chunk(q, pos, kc, vc, cmp_end, ksb, vsb, overlap):
    N, T, H, D = q.shape
    G = N_KV_HEADS
    R = H // G
    scale = HEAD_DIM ** -0.5
    qg = q.reshape(N, T, G, R, D)
    s = jnp.einsum('ntgrd,ncgd->ngrtc', qg, kc).astype(jnp.float32) * scale
    p_c = masked_probs(s, cmp_end[None, :] <= pos[:, None])
    o_c = jnp.einsum('ngrtc,ncgd->ntgrd', p_c.astype(vc.dtype), vc)
    imp = jnp.einsum('ngrtc,cs->ngts', p_c, overlap)
    NS = overlap.shape[1]
    j = jnp.arange(NS)[None, :]
    cur = (pos // SLC_BLOCK)[:, None]
    forced = (j == 0) | (j == cur) | (j == cur - 1)
    score = jnp.where(j <= cur, jnp.where(forced, BIG, imp), -BIG)
    _, idx = lax.top_k(score, min(N_SEL, NS))
    K = idx.shape[-1]
    n_i = jnp.arange(N)[:, None, None, None]
    g_i = jnp.arange(G)[None, :, None, None]
    kg = ksb[n_i, g_i, idx]
    vg = vsb[n_i, g_i, idx]
    kpos = idx[..., None] * SLC_BLOCK + jnp.arange(SLC_BLOCK)
    mask = (kpos <= pos[None, None, :, None, None]).reshape(N, G, 1, T, K * SLC_BLOCK)
    s2 = jnp.einsum('ntgrd,ngtksd->ngrtks', qg, kg).astype(jnp.float32).reshape(N, G, R, T, K * SLC_BLOCK) * scale
    p_s = masked_probs(s2, mask)
    o_s = jnp.einsum('ngrtm,ngtmd->ntgrd', p_s.astype(vg.dtype), vg.reshape(N, G, T, K * SLC_BLOCK, D))
    return o_c.reshape(N, T, H, D), o_s.reshape(N, T, H, D)


def sparse_branches(q, qpos, cmp_rows, slc_rows, pe, w1, w2, n_chunks):
    N, L = cmp_rows.shape[:2]
    Lp = -(-L // SLC_BLOCK) * SLC_BLOCK
    pad = ((0, 0), (0, Lp - L), (0, 0), (0, 0), (0, 0))
    cmp_rows = jnp.pad(cmp_rows, pad)
    slc_rows = jnp.pad(slc_rows, pad)
    kcb = compress(cmp_rows[:, :, 0], pe[0], w1[0], w2[0])
    vcb = compress(cmp_rows[:, :, 1], pe[1], w1[1], w2[1])
    n_cmp = kcb.shape[1]
    n_slc = Lp // SLC_BLOCK
    cmp_start = jnp.arange(n_cmp) * CMP_STRIDE
    cmp_end = cmp_start + CMP_BLOCK - 1
    slc_start = jnp.arange(n_slc) * SLC_BLOCK
    overlap = ((cmp_end[:, None] >= slc_start[None, :]) &
               (cmp_start[:, None] < slc_start[None, :] + SLC_BLOCK)).astype(jnp.float32)
    def to_blocks(a):
        return jnp.transpose(a.reshape(N, n_slc, SLC_BLOCK, N_KV_HEADS, HEAD_DIM), (0, 3, 1, 2, 4))
    ksb = to_blocks(slc_rows[:, :, 0])
    vsb = to_blocks(slc_rows[:, :, 1])
    def chunk(qc, pc):
        return cmp_slc_chunk(qc, pc, kcb, vcb, cmp_end, ksb, vsb, overlap)
    if n_chunks == 1:
        return chunk(q, qpos)
    T = q.shape[1]
    Tc = T // n_chunks
    qs = jnp.moveaxis(q.reshape(N, n_chunks, Tc, N_HEADS, HEAD_DIM), 1, 0)
    ps = qpos.reshape(n_chunks, Tc)
    oc, osl = lax.map(lambda a: chunk(a[0], a[1]), (qs, ps))
    oc = jnp.moveaxis(oc, 0, 1).reshape(N, T, N_HEADS, HEAD_DIM)
    osl = jnp.moveaxis(osl, 0, 1).reshape(N, T, N_HEADS, HEAD_DIM)
    return oc, osl


def window_attn_banded(q, k, v):
    N, S, H, D = q.shape
    G = N_KV_HEADS
    R = H // G
    nb = S // Q_BLOCK
    n_prev = -(-(WINDOW - 1) // Q_BLOCK)
    KW = (n_prev + 1) * Q_BLOCK
    def band(a):
        ap = jnp.pad(a, ((0, 0), (n_prev * Q_BLOCK, 0), (0, 0), (0, 0))).reshape(N, nb + n_prev, Q_BLOCK, G, D)
        return jnp.concatenate([ap[:, j:j + nb] for j in range(n_prev + 1)], axis=2)
    kb, vb = band(k), band(v)
    qb = q.reshape(N, nb, Q_BLOCK, G, R, D)
    qpos = jnp.arange(nb)[:, None] * Q_BLOCK + jnp.arange(Q_BLOCK)[None, :]
    kpos = (jnp.arange(nb)[:, None] - n_prev) * Q_BLOCK + jnp.arange(KW)[None, :]
    kp, qp = kpos[:, None, :], qpos[:, :, None]
    mask = (kp <= qp) & (kp > qp - WINDOW) & (kp >= 0)
    s = jnp.einsum('nbqgrd,nbkgd->nbgrqk', qb, kb).astype(jnp.float32) * (HEAD_DIM ** -0.5)
    p = masked_probs(s, mask[None, :, None, None])
    o = jnp.einsum('nbgrqk,nbkgd->nbqgrd', p.astype(vb.dtype), vb)
    return o.reshape(N, S, H, D)


def window_attn_dense(q, qpos, k, v, kpos):
    N, T, H, D = q.shape
    G = N_KV_HEADS
    R = H // G
    qg = q.reshape(N, T, G, R, D)
    s = jnp.einsum('ntgrd,nsgd->ngrts', qg, k).astype(jnp.float32) * (HEAD_DIM ** -0.5)
    mask = (kpos[None, :] <= qpos[:, None]) & (kpos[None, :] > qpos[:, None] - WINDOW)
    p = masked_probs(s, mask)
    o = jnp.einsum('ngrts,nsgd->ntgrd', p.astype(v.dtype), v)
    return o.reshape(N, T, H, D)


def mixer_inputs(x, pos, w_norm, w_in, g_q, g_k):
    N, T, _ = x.shape
    h = rmsnorm(x, w_norm)
    sizes = (D_CONV, D_CONV, D_CONV, D_POOL, D_POOL, D_ATTN, D_KV, N_BRANCH * N_HEADS, D_ATTN)
    offs = [int(v) for v in np.cumsum(sizes)[:-1]]
    a_val, a_gate, z_a, b_in, z_b, q, kv, gate, z_c = jnp.split(h @ w_in, offs, axis=-1)
    q = rope(rmsnorm(q.reshape(N, T, N_HEADS, HEAD_DIM), g_q), pos)
    kv = kv.reshape(N, T, N_BRANCH, 2, N_KV_HEADS, HEAD_DIM)
    k = rope(rmsnorm(kv[:, :, :, 0], g_k[:, None, :]), pos)
    rows = jnp.stack([k, kv[:, :, :, 1]], axis=3)
    glu = a_val * jax.nn.sigmoid(a_gate)
    gates = jax.nn.sigmoid(gate).reshape(N, T, N_BRANCH, N_HEADS)
    return glu, z_a, b_in, z_b, q, rows, gates, z_c


def mixer_output(x, y_a, z_a, y_b, z_b, o_c, o_s, o_w, gates, z_c, w_out):
    N, T, _ = x.shape
    o = gates[..., 0, :, None] * o_c + gates[..., 1, :, None] * o_s + gates[..., 2, :, None] * o_w
    y = jnp.concatenate([y_a * jax.nn.silu(z_a), y_b * jax.nn.silu(z_b),
                         o.reshape(N, T, D_ATTN) * jax.nn.silu(z_c)], axis=-1)
    return x + (y @ w_out).astype(x.dtype)


def setup_inputs(seed: int = 0) -> dict:
    key = jax.random.key(seed)
    ks = jax.random.split(key, 32)
    f = jnp.float32
    def nrm(k, shape, s=1.0):
        return jax.random.normal(k, shape, f) * s
    n_pages = PAST_LEN // PAGE_SIZE
    n_used = DEC_BATCH * n_pages
    n_phys = n_used + max(1, n_used // 4)
    win_len = min(WINDOW, PAST_LEN)
    page_table = jax.random.permutation(ks[7], n_phys)[:n_used].reshape(DEC_BATCH, n_pages).astype(jnp.int32)
    return {
        'x_prompt': nrm(ks[0], (BATCH, SEQ, D_MODEL)),
        'x_sample': nrm(ks[1], (DEC_BATCH, DEC_SEQ, D_MODEL)),
        'state_conv': nrm(ks[2], (DEPTH, DEC_BATCH, CONV_STATE, D_CONV), 0.5),
        'state_pool': nrm(ks[3], (DEPTH, DEC_BATCH, POOL_STATE, D_POOL)),
        'cache_win_kv': nrm(ks[4], (DEPTH, DEC_BATCH, win_len, 2, N_KV_HEADS, HEAD_DIM)),
        'cache_cmp_kv': nrm(ks[5], (DEPTH, n_phys, PAGE_SIZE, 2, N_KV_HEADS, HEAD_DIM)),
        'cache_slc_kv': nrm(ks[6], (DEPTH, n_phys, PAGE_SIZE, 2, N_KV_HEADS, HEAD_DIM)),
        'page_table': page_table,
        'w_norm': 1.0 + nrm(ks[8], (DEPTH, D_MODEL), 0.02),
        'w_in': nrm(ks[9], (DEPTH, D_MODEL, D_IN), D_MODEL ** -0.5),
        'w_out': nrm(ks[10], (DEPTH, D_MIX, D_MODEL), D_MIX ** -0.5),
        'w_dw': nrm(ks[11], (DEPTH, CONV_WIDTH, D_CONV), CONV_WIDTH ** -0.5),
        'b_dw': nrm(ks[12], (DEPTH, D_CONV), 0.01),
        'ln_g': 1.0 + nrm(ks[13], (DEPTH, D_CONV), 0.02),
        'ln_b': nrm(ks[14], (DEPTH, D_CONV), 0.01),
        'w_pw': nrm(ks[15], (DEPTH, D_CONV, D_CONV), D_CONV ** -0.5),
        'b_pw': nrm(ks[16], (DEPTH, D_CONV), 0.01),
        'w_pool': nrm(ks[17], (DEPTH, N_POOL_GROUPS, POOL_GROUP, POOL_GROUP), POOL_GROUP ** -0.5),
        'pool_scale': 1.0 + nrm(ks[18], (DEPTH, D_POOL), 0.1),
        'g_q': 1.0 + nrm(ks[19], (DEPTH, HEAD_DIM), 0.02),
        'g_k': 1.0 + nrm(ks[20], (DEPTH, N_BRANCH, HEAD_DIM), 0.02),
        'cmp_pe': nrm(ks[21], (DEPTH, 2, CMP_BLOCK, HEAD_DIM), 0.1),
        'cmp_w1': nrm(ks[22], (DEPTH, 2, CMP_BLOCK * HEAD_DIM, HEAD_DIM), (CMP_BLOCK * HEAD_DIM) ** -0.5),
        'cmp_w2': nrm(ks[23], (DEPTH, 2, HEAD_DIM, HEAD_DIM), HEAD_DIM ** -0.5),
    }


def reference(x_prompt, x_sample, state_conv, state_pool, cache_win_kv, cache_cmp_kv, cache_slc_kv, page_table,
              w_norm, w_in, w_out, w_dw, b_dw, ln_g, ln_b, w_pw, b_pw, w_pool, pool_scale,
              g_q, g_k, cmp_pe, cmp_w1, cmp_w2):
    Bp, S, _ = x_prompt.shape
    Bs, T, _ = x_sample.shape
    P = page_table.shape[1] * PAGE_SIZE
    pos_p = jnp.arange(S)
    pos_s = P + jnp.arange(T)
    xp, xs = x_prompt, x_sample
    conv_p, pool_p, win_p, cmp_p, slc_p = [], [], [], [], []
    conv_s, pool_s, win_s, cmp_s, slc_s = [], [], [], [], []
    for l in range(DEPTH):
        glu, z_a, b_in, z_b, q, rows, gates, z_c = mixer_inputs(xp, pos_p, w_norm[l], w_in[l], g_q[l], g_k[l])
        y_a, cv = conv_mixer(glu, jnp.zeros((Bp, CONV_STATE, D_CONV), glu.dtype),
                             w_dw[l], b_dw[l], ln_g[l], ln_b[l], w_pw[l], b_pw[l])
        y_b, pl = pool_mixer(b_in, jnp.zeros((Bp, POOL_STATE, D_POOL), b_in.dtype), pos_p, w_pool[l], pool_scale[l])
        cmp_rows, slc_rows, win_rows = rows[:, :, 0], rows[:, :, 1], rows[:, :, 2]
        o_c, o_s = sparse_branches(q, pos_p, cmp_rows, slc_rows, cmp_pe[l], cmp_w1[l], cmp_w2[l], S // Q_BLOCK)
        o_w = window_attn_banded(q, win_rows[:, :, 0], win_rows[:, :, 1])
        xp = mixer_output(xp, y_a, z_a, y_b, z_b, o_c, o_s, o_w, gates, z_c, w_out[l])
        conv_p.append(cv)
        pool_p.append(pl)
        win_p.append(win_rows[:, S - min(WINDOW, S):])
        cmp_p.append(cmp_rows)
        slc_p.append(slc_rows)
        glu, z_a, b_in, z_b, q, rows, gates, z_c = mixer_inputs(xs, pos_s, w_norm[l], w_in[l], g_q[l], g_k[l])
        y_a, cv = conv_mixer(glu, state_conv[l], w_dw[l], b_dw[l], ln_g[l], ln_b[l], w_pw[l], b_pw[l])
        y_b, pl = pool_mixer(b_in, state_pool[l], pos_s, w_pool[l], pool_scale[l])
        cmp_rows, slc_rows, win_rows = rows[:, :, 0], rows[:, :, 1], rows[:, :, 2]
        past_c = cache_cmp_kv[l][page_table].reshape(Bs, P, 2, N_KV_HEADS, HEAD_DIM)
        past_s = cache_slc_kv[l][page_table].reshape(Bs, P, 2, N_KV_HEADS, HEAD_DIM)
        full_c = jnp.concatenate([past_c.astype(cmp_rows.dtype), cmp_rows], axis=1)
        full_s = jnp.concatenate([past_s.astype(slc_rows.dtype), slc_rows], axis=1)
        o_c, o_s = sparse_branches(q, pos_s, full_c, full_s, cmp_pe[l], cmp_w1[l], cmp_w2[l], 1)
        buf = cache_win_kv[l].astype(win_rows.dtype)
        Lb = buf.shape[1]
        full_w = jnp.concatenate([buf, win_rows], axis=1)
        kpos = P - Lb + jnp.arange(Lb + T)
        o_w = window_attn_dense(q, pos_s, full_w[:, :, 0], full_w[:, :, 1], kpos)
        xs = mixer_output(xs, y_a, z_a, y_b, z_b, o_c, o_s, o_w, gates, z_c, w_out[l])
        conv_s.append(cv)
        pool_s.append(pl)
        win_s.append(full_w[:, T:])
        cmp_s.append(cmp_rows)
        slc_s.append(slc_rows)
    return (xp, xs,
            jnp.stack(conv_p), jnp.stack(pool_p), jnp.stack(win_p), jnp.stack(cmp_p), jnp.stack(slc_p),
            jnp.stack(conv_s), jnp.stack(pool_s), jnp.stack(win_s), jnp.stack(cmp_s), jnp.stack(slc_s))
```

```python
import functools

import numpy as np
import jax
import jax.numpy as jnp
from jax import lax
from jax.experimental import pallas as pl
from jax.experimental.pallas import tpu as pltpu

D_MODEL = 1024
D_CONV = 256
D_POOL = 256
N_HEADS = 8
N_KV_HEADS = 2
HEAD_DIM = 64
D_ATTN = N_HEADS * HEAD_DIM
CONV_WIDTH = 31
CONV_STATE = CONV_WIDTH - 1
POOL_WINDOWS = (2, 4, 8, 16)
POOL_GROUP = D_POOL // len(POOL_WINDOWS)
POOL_STATE = max(POOL_WINDOWS) - 1
CMP_BLOCK = 32
CMP_STRIDE = 16
SLC_BLOCK = 64
N_SEL = 8
WINDOW = 256
Q_BLOCK = 128
ROT_DIM = HEAD_DIM // 4
ROT_HALF = ROT_DIM // 2
ROPE_THETA = 500000.0
N_BRANCH = 3
D_KV = N_BRANCH * 2 * N_KV_HEADS * HEAD_DIM
D_ROWS = 2 * N_KV_HEADS * HEAD_DIM
N_REP = N_HEADS // N_KV_HEADS
EPS = 1e-6
NEG = -1e30
BIG = 1e4

GATE_PAD = 128
D_TOK = 3 * D_CONV + 2 * D_POOL + D_ATTN + GATE_PAD
D_FEAT = D_ATTN + D_KV
N_FEAT_BLOCKS = D_FEAT // HEAD_DIM
SEL_KEYS = 256
VMEM_LIMIT = 56 * 1024 * 1024

F32 = jnp.float32
BF16 = jnp.bfloat16
NT_DIMS = (((1,), (1,)), ((), ()))


def _sigmoid(x):
    return 1.0 / (1.0 + jnp.exp(-x))


def _silu(x):
    return x * _sigmoid(x)


def _gelu_tanh(x):
    return 0.5 * x * (1.0 + jnp.tanh(np.sqrt(2.0 / np.pi).astype(np.float32) * (x + 0.044715 * (x * x * x))))


def _dot(a, b):
    return jnp.dot(a, b, preferred_element_type=F32)


def _dot_nt(a, b):
    return lax.dot_general(a, b, NT_DIMS, preferred_element_type=F32)


def _masked_softmax(s, mask):
    s = jnp.where(mask, s, NEG)
    m = jnp.max(s, axis=-1, keepdims=True)
    p = jnp.where(mask, jnp.exp(s - m), 0.0)
    return p / jnp.maximum(jnp.sum(p, axis=-1, keepdims=True), 1e-30)


def _in_proj_kernel(x_ref, wnorm_ref, wtok_ref, wfeat_ref, gcol_ref, cos_ref, sin_ref,
                    glu_ref, bin_ref, sz_ref, gates_ref, qt_ref, cmp_ref, slc_ref, win_ref):
    x = x_ref[...]
    ms = jnp.mean(x * x, axis=-1, keepdims=True)
    h = (x * lax.rsqrt(ms + EPS) * wnorm_ref[...]).astype(BF16)

    a = _dot(h, wtok_ref[...])
    o = 0
    glu_ref[...] = a[:, o:o + D_CONV] * _sigmoid(a[:, o + D_CONV:o + 2 * D_CONV])
    o += 2 * D_CONV
    bin_ref[...] = a[:, o:o + D_POOL]
    o += D_POOL
    z = a[:, o:o + D_CONV + D_POOL + D_ATTN]
    sz_ref[...] = _silu(z)
    o += D_CONV + D_POOL + D_ATTN
    gates_ref[...] = _sigmoid(a[:, o:o + GATE_PAD])

    f = _dot_nt(wfeat_ref[...], h)
    cos = cos_ref[...]
    sin = sin_ref[...]
    kv_refs = (cmp_ref, slc_ref, win_ref)
    for hb in range(N_FEAT_BLOCKS):
        blk = f[hb * HEAD_DIM:(hb + 1) * HEAD_DIM, :]
        kv_blk = hb - N_HEADS
        is_value = kv_blk >= 0 and (kv_blk % (2 * N_KV_HEADS)) >= N_KV_HEADS
        if not is_value:
            bms = jnp.mean(blk * blk, axis=0, keepdims=True)
            y = blk * lax.rsqrt(bms + EPS) * gcol_ref[hb * HEAD_DIM:(hb + 1) * HEAD_DIM, :]
            x1 = y[0:ROT_HALF]
            x2 = y[ROT_HALF:ROT_DIM]
            blk = jnp.concatenate([x1 * cos - x2 * sin, x2 * cos + x1 * sin, y[ROT_DIM:]], axis=0)
        if kv_blk < 0:
            qt_ref[hb * HEAD_DIM:(hb + 1) * HEAD_DIM, :] = blk
        else:
            r = kv_blk % (2 * N_KV_HEADS)
            kv_refs[kv_blk // (2 * N_KV_HEADS)][r * HEAD_DIM:(r + 1) * HEAD_DIM, :] = blk


def _in_proj(x, wnorm, wtok, wfeat, gcol, cos_t, sin_t, tm):
    n, t, _ = x.shape
    grid = (n, t // tm)
    tok = lambda w: pl.BlockSpec((None, tm, w), lambda i, j: (i, j, 0))
    feat = lambda w: pl.BlockSpec((None, w, tm), lambda i, j: (i, 0, j))
    full = lambda a: pl.BlockSpec(a.shape, lambda i, j: (0,) * a.ndim)
    tok_shape = lambda w: jax.ShapeDtypeStruct((n, t, w), F32)
    feat_shape = lambda w: jax.ShapeDtypeStruct((n, w, t), F32)
    return pl.pallas_call(
        _in_proj_kernel,
        grid=grid,
        in_specs=[tok(D_MODEL), full(wnorm), full(wtok), full(wfeat), full(gcol),
                  pl.BlockSpec((ROT_HALF, tm), lambda i, j: (0, j)),
                  pl.BlockSpec((ROT_HALF, tm), lambda i, j: (0, j))],
        out_specs=[tok(D_CONV), tok(D_POOL), tok(D_CONV + D_POOL + D_ATTN), tok(GATE_PAD),
                   feat(D_ATTN), feat(D_ROWS), feat(D_ROWS), feat(D_ROWS)],
        out_shape=[tok_shape(D_CONV), tok_shape(D_POOL), tok_shape(D_CONV + D_POOL + D_ATTN), tok_shape(GATE_PAD),
                   feat_shape(D_ATTN), feat_shape(D_ROWS), feat_shape(D_ROWS), feat_shape(D_ROWS)],
        compiler_params=pltpu.CompilerParams(dimension_semantics=("arbitrary", "arbitrary"),
                                             vmem_limit_bytes=VMEM_LIMIT),
        name="in_proj",
    )(x, wnorm, wtok, wfeat, gcol, cos_t, sin_t)


def _conv_tail(acc, sz_a, bdw, lng, lnb, wpw, bpw):
    y = acc + bdw
    mu = jnp.mean(y, axis=-1, keepdims=True)
    yc = y - mu
    var = jnp.mean(yc * yc, axis=-1, keepdims=True)
    y = yc * lax.rsqrt(var + EPS) * lng + lnb
    y = _dot(_silu(y).astype(BF16), wpw) + bpw
    return y * sz_a


def _pool_tail(total, cnt, xcur, sz_b, wpool, pscale):
    d = total / cnt - xcur
    y = _dot(d.astype(BF16), wpool) * pscale
    return y * sz_b


def _pool_window_of_lane(shape):
    lane = lax.broadcasted_iota(jnp.int32, shape, len(shape) - 1)
    w = jnp.full(shape, POOL_WINDOWS[0], jnp.int32)
    for gi in range(1, len(POOL_WINDOWS)):
        w = jnp.where(lane >= gi * POOL_GROUP, POOL_WINDOWS[gi], w)
    return w


def _pool_select(sums, shape):
    lane = lax.broadcasted_iota(jnp.int32, shape, len(shape) - 1)
    total = sums[0]
    for gi in range(1, len(POOL_WINDOWS)):
        total = jnp.where(lane >= gi * POOL_GROUP, sums[gi], total)
    return total


CONV_HALO = 32
POOL_HALO = 16


def _convpool_prompt_kernel(glu_ref, gprev_ref, bin_ref, bprev_ref, sz_ref, wdw_ref, bdw_ref, lng_ref, lnb_ref,
                            wpw_ref, bpw_ref, wpool_ref, pscale_ref, y_ref, cbuf, pbuf, *, t):
    i = pl.program_id(1)
    keep = (i > 0).astype(F32)
    cbuf[0:CONV_HALO, :] = gprev_ref[...] * keep
    cbuf[CONV_HALO:CONV_HALO + t, :] = glu_ref[...]
    acc = jnp.zeros((t, D_CONV), F32)
    for k in range(CONV_WIDTH):
        acc = acc + wdw_ref[k:k + 1, :] * cbuf[pl.ds(CONV_HALO - CONV_STATE + k, t), :]
    sz = sz_ref[...]
    y_ref[:, 0:D_CONV] = _conv_tail(acc, sz[:, 0:D_CONV], bdw_ref[...], lng_ref[...], lnb_ref[...],
                                    wpw_ref[...], bpw_ref[...])

    pbuf[0:POOL_HALO, :] = bprev_ref[...] * keep
    xcur = bin_ref[...]
    pbuf[POOL_HALO:POOL_HALO + t, :] = xcur
    sums = []
    run = jnp.zeros((t, D_POOL), F32)
    j = 0
    for w in POOL_WINDOWS:
        while j < w:
            run = run + pbuf[pl.ds(POOL_HALO - j, t), :]
            j += 1
        sums.append(run)
    total = _pool_select(sums, (t, D_POOL))
    pos = i * t + lax.broadcasted_iota(jnp.int32, (t, D_POOL), 0)
    cnt = jnp.minimum(_pool_window_of_lane((t, D_POOL)), pos + 1).astype(F32)
    y_ref[:, D_CONV:D_CONV + D_POOL] = _pool_tail(total, cnt, xcur, sz[:, D_CONV:D_CONV + D_POOL],
                                                  wpool_ref[...], pscale_ref[...])


def _convpool_prompt(glu, b_in, sz, wdw, bdw, lng, lnb, wpw, bpw, wpool, pscale, t):
    n, s, _ = glu.shape
    grid = (n, s // t)
    cur = lambda w: pl.BlockSpec((None, t, w), lambda i, j: (i, j, 0))
    prev = lambda rows, w: pl.BlockSpec((None, rows, w), lambda i, j: (i, jnp.maximum(j * (t // rows) - 1, 0), 0))
    full = lambda a: pl.BlockSpec(a.shape, lambda i, j: (0,) * a.ndim)
    return pl.pallas_call(
        functools.partial(_convpool_prompt_kernel, t=t),
        grid=grid,
        in_specs=[cur(D_CONV), prev(CONV_HALO, D_CONV), cur(D_POOL), prev(POOL_HALO, D_POOL),
                  cur(D_CONV + D_POOL),
                  full(wdw), full(bdw), full(lng), full(lnb), full(wpw), full(bpw), full(wpool), full(pscale)],
        out_specs=cur(D_CONV + D_POOL),
        out_shape=jax.ShapeDtypeStruct((n, s, D_CONV + D_POOL), F32),
        scratch_shapes=[pltpu.VMEM((CONV_HALO + t, D_CONV), F32), pltpu.VMEM((POOL_HALO + t, D_POOL), F32)],
        compiler_params=pltpu.CompilerParams(dimension_semantics=("arbitrary", "arbitrary"),
                                             vmem_limit_bytes=VMEM_LIMIT),
        name="convpool_prompt",
    )(glu, glu, b_in, b_in, sz, wdw, bdw, lng, lnb, wpw, bpw, wpool, pscale)


def _convpool_sample_kernel(sc_ref, sp_ref, glu_ref, bin_ref, sz_ref, wdw_ref, bdw_ref, lng_ref, lnb_ref,
                            wpw_ref, bpw_ref, wpool_ref, pscale_ref, y_ref, sc_out, sp_out, *, pos):
    glu = glu_ref[...]
    acc = wdw_ref[CONV_STATE:CONV_WIDTH, :] * glu
    for k in range(CONV_STATE):
        acc = acc + wdw_ref[k:k + 1, :] * sc_ref[k]
    sz = sz_ref[...]
    y_ref[:, 0:D_CONV] = _conv_tail(acc, sz[:, 0:D_CONV], bdw_ref[...], lng_ref[...], lnb_ref[...],
                                    wpw_ref[...], bpw_ref[...])
    for k in range(CONV_STATE - 1):
        sc_out[k] = sc_ref[k + 1]
    sc_out[CONV_STATE - 1] = glu

    xcur = bin_ref[...]
    shape = xcur.shape
    sums = []
    run = xcur
    j = 1
    for w in POOL_WINDOWS:
        while j < w:
            run = run + sp_ref[POOL_STATE - j]
            j += 1
        sums.append(run)
    total = _pool_select(sums, shape)
    cnt = jnp.minimum(_pool_window_of_lane(shape), pos + 1).astype(F32)
    y_ref[:, D_CONV:D_CONV + D_POOL] = _pool_tail(total, cnt, xcur, sz[:, D_CONV:D_CONV + D_POOL],
                                                  wpool_ref[...], pscale_ref[...])
    for k in range(POOL_STATE - 1):
        sp_out[k] = sp_ref[k + 1]
    sp_out[POOL_STATE - 1] = xcur


def _convpool_sample(sc, sp, glu, b_in, sz, wdw, bdw, lng, lnb, wpw, bpw, wpool, pscale, pos):
    nb = glu.shape[0]
    return pl.pallas_call(
        functools.partial(_convpool_sample_kernel, pos=pos),
        out_shape=[jax.ShapeDtypeStruct((nb, D_CONV + D_POOL), F32),
                   jax.ShapeDtypeStruct(sc.shape, F32), jax.ShapeDtypeStruct(sp.shape, F32)],
        compiler_params=pltpu.CompilerParams(vmem_limit_bytes=VMEM_LIMIT),
        name="convpool_sample",
    )(sc, sp, glu, b_in, sz[:, 0:D_CONV + D_POOL], wdw, bdw, lng, lnb, wpw, bpw, wpool, pscale)


def _out_proj_kernel(x_ref, yab_ref, oat_ref, szc_ref, w_ref, o_ref):
    ya = yab_ref[...].astype(BF16)
    yc = (oat_ref[...] * szc_ref[...]).astype(BF16)
    nab = D_CONV + D_POOL
    o_ref[...] = x_ref[...] + _dot(ya, w_ref[0:nab, :]) + _dot(yc, w_ref[nab:nab + D_ATTN, :])


def _out_proj(x, yab, o_attn, sz, w_out, tm):
    n, t, _ = x.shape
    nab = D_CONV + D_POOL
    assert nab == D_ATTN
    grid = (n, t // tm)
    tok = lambda w, cb=0: pl.BlockSpec((None, tm, w), lambda i, j: (i, j, cb))
    return pl.pallas_call(
        _out_proj_kernel,
        grid=grid,
        in_specs=[tok(D_MODEL), tok(nab), tok(D_ATTN), tok(D_ATTN, 1),
                  pl.BlockSpec(w_out.shape, lambda i, j: (0, 0))],
        out_specs=tok(D_MODEL),
        out_shape=jax.ShapeDtypeStruct(x.shape, F32),
        compiler_params=pltpu.CompilerParams(dimension_semantics=("arbitrary", "arbitrary"),
                                             vmem_limit_bytes=VMEM_LIMIT),
        name="out_proj",
    )(x, yab, o_attn, sz, w_out)


def _compress_flat(xt_ref, n_pc, pe, flat_lo, flat_hi):
    for r in range(CMP_STRIDE):
        rows = xt_ref[pl.ds(r, n_pc, stride=CMP_STRIDE), :]
        flat_lo[0:n_pc, r * HEAD_DIM:(r + 1) * HEAD_DIM] = (rows + pe[r:r + 1, :]).astype(BF16)
        flat_hi[0:n_pc, r * HEAD_DIM:(r + 1) * HEAD_DIM] = (rows + pe[CMP_STRIDE + r:CMP_STRIDE + r + 1, :]).astype(BF16)


def _compress_mlp(flat_lo, flat_hi, w1, w2):
    half = CMP_STRIDE * HEAD_DIM
    h_lo = _dot(flat_lo, w1[0:half, :])
    h_hi = _dot(flat_hi, w1[half:2 * half, :])
    n_pc = h_lo.shape[0]
    h = h_lo + jnp.concatenate([h_hi[1:n_pc], jnp.zeros((1, HEAD_DIM), F32)], axis=0)
    out = _dot(_gelu_tanh(h).astype(BF16), w2)
    row = lax.broadcasted_iota(jnp.int32, out.shape, 0)
    return jnp.where(row < n_pc - 1, out, 0.0)


def _compress_prompt_kernel(x_ref, pe_ref, w1_ref, w2_ref, o_ref, xt_ref, flat_lo, flat_hi, *, s):
    n_pc = s // CMP_STRIDE
    for kv in range(2):
        pe = pe_ref[kv]
        for g in range(N_KV_HEADS):
            row0 = (kv * N_KV_HEADS + g) * HEAD_DIM
            for t in range(s // 128):
                xt_ref[t * 128:(t + 1) * 128, :] = x_ref[row0:row0 + HEAD_DIM, t * 128:(t + 1) * 128].T
            _compress_flat(xt_ref, n_pc, pe, flat_lo, flat_hi)
            o_ref[g, :, kv * HEAD_DIM:(kv + 1) * HEAD_DIM] = _compress_mlp(flat_lo[...], flat_hi[...],
                                                                           w1_ref[kv], w2_ref[kv])


def _compress_prompt(cmp_t, pe, w1, w2):
    n, _, s = cmp_t.shape
    n_pc = s // CMP_STRIDE
    full = lambda a: pl.BlockSpec(a.shape, lambda i: (0,) * a.ndim)
    return pl.pallas_call(
        functools.partial(_compress_prompt_kernel, s=s),
        grid=(n,),
        in_specs=[pl.BlockSpec((None, D_ROWS, s), lambda i: (i, 0, 0)), full(pe), full(w1), full(w2)],
        out_specs=pl.BlockSpec((None, N_KV_HEADS, n_pc, 2 * HEAD_DIM), lambda i: (i, 0, 0, 0)),
        out_shape=jax.ShapeDtypeStruct((n, N_KV_HEADS, n_pc, 2 * HEAD_DIM), F32),
        scratch_shapes=[pltpu.VMEM((s, HEAD_DIM), F32),
                        pltpu.VMEM((n_pc, CMP_STRIDE * HEAD_DIM), BF16),
                        pltpu.VMEM((n_pc, CMP_STRIDE * HEAD_DIM), BF16)],
        compiler_params=pltpu.CompilerParams(dimension_semantics=("arbitrary",), vmem_limit_bytes=VMEM_LIMIT),
        name="compress_prompt",
    )(cmp_t, pe, w1, w2)


def _select_blocks(imp_t, pos_row, n_blocks):
    shape = imp_t.shape
    j = lax.broadcasted_iota(jnp.int32, shape, 0)
    cur = pos_row // SLC_BLOCK
    forced = (j == 0) | (j == cur) | (j == cur - 1)
    score = jnp.where(j <= cur, jnp.where(forced, BIG, imp_t), -BIG)
    rank = jnp.zeros(shape, F32)
    for jp in range(n_blocks):
        sj = score[jp:jp + 1, :]
        beats = (sj > score) | ((sj == score) & (j > jp))
        rank = rank + beats.astype(F32)
    return (rank < float(min(N_SEL, n_blocks))).astype(F32)


def _attn_prompt_kernel(qt_ref, kvc_ref, slc_ref, win_ref, gates_ref, ovt_ref, o_ref, m_ref, l_ref, acc_ref,
                        *, s, n_blk):
    tq = Q_BLOCK
    qc = pl.program_id(1)
    q0 = qc * tq
    pos_col = q0 + lax.broadcasted_iota(jnp.int32, (tq, 1), 0)
    pos4 = jnp.concatenate([pos_col] * N_REP, axis=0)
    pos_row = q0 + lax.broadcasted_iota(jnp.int32, (1, tq), 1)
    n_slc = s // SLC_BLOCK
    n_cmp = kvc_ref.shape[1]
    gates = gates_ref[...]
    scale = HEAD_DIM ** -0.5

    for g in range(N_KV_HEADS):
        q = jnp.concatenate([qt_ref[(g * N_REP + r) * HEAD_DIM:(g * N_REP + r + 1) * HEAD_DIM, :].T
                             for r in range(N_REP)], axis=0)
        q = (q * scale).astype(BF16)

        kc = kvc_ref[g, :, 0:HEAD_DIM].astype(BF16)
        vc = kvc_ref[g, :, HEAD_DIM:2 * HEAD_DIM].astype(BF16)
        c_idx = lax.broadcasted_iota(jnp.int32, (1, n_cmp), 1)
        valid = (c_idx * CMP_STRIDE + (CMP_BLOCK - 1) <= pos4) & (c_idx < n_blk)
        p_c = _masked_softmax(_dot_nt(q, kc), valid)
        o_c = _dot(p_c.astype(BF16), vc)

        pc_sum = p_c[0:tq]
        for r in range(1, N_REP):
            pc_sum = pc_sum + p_c[r * tq:(r + 1) * tq]
        pc_hi = pc_sum.astype(BF16)
        pc_lo = (pc_sum - pc_hi.astype(F32)).astype(BF16)
        ovt = ovt_ref[...]
        imp_t = _dot_nt(ovt, pc_hi) + _dot_nt(ovt, pc_lo)
        sel = _select_blocks(imp_t, pos_row, n_slc).T.astype(BF16)

        m_ref[...] = jnp.full(m_ref.shape, NEG, F32)
        l_ref[...] = jnp.zeros(l_ref.shape, F32)
        acc_ref[...] = jnp.zeros(acc_ref.shape, F32)
        k_row0 = g * HEAD_DIM
        v_row0 = (N_KV_HEADS + g) * HEAD_DIM

        def sel_step(i, carry):
            k0 = pl.multiple_of(i * SEL_KEYS, SEL_KEYS)
            k_t = slc_ref[k_row0:k_row0 + HEAD_DIM, pl.ds(k0, SEL_KEYS)].astype(BF16)
            v_t = slc_ref[v_row0:v_row0 + HEAD_DIM, pl.ds(k0, SEL_KEYS)].astype(BF16)
            key = k0 + lax.broadcasted_iota(jnp.int32, (1, SEL_KEYS), 1)
            blk_of_key = (key // SLC_BLOCK == lax.broadcasted_iota(jnp.int32, (n_slc, SEL_KEYS), 0)).astype(BF16)
            chosen = _dot(sel, blk_of_key) > 0.5
            mask1 = chosen & (key <= pos_col)
            mask = jnp.concatenate([mask1] * N_REP, axis=0)
            sc = jnp.where(mask, _dot(q, k_t), NEG)
            m_old = m_ref[...]
            m_new = jnp.maximum(m_old, jnp.max(sc, axis=-1, keepdims=True))
            p = jnp.where(mask, jnp.exp(sc - m_new), 0.0)
            alpha = jnp.exp(m_old - m_new)
            l_ref[...] = alpha * l_ref[...] + jnp.sum(p, axis=-1, keepdims=True)
            acc_ref[...] = alpha * acc_ref[...] + _dot_nt(p.astype(BF16), v_t)
            m_ref[...] = m_new
            return carry

        lax.fori_loop(0, (q0 + tq + SEL_KEYS - 1) // SEL_KEYS, sel_step, 0)
        o_s = acc_ref[...] / jnp.maximum(l_ref[...], 1e-30)

        n_prev = -(-(WINDOW - 1) // tq)
        s_parts, m_parts, v_parts = [], [], []
        for b in range(n_prev + 1):
            kb = qc - n_prev + b
            k0 = pl.multiple_of(jnp.maximum(kb, 0) * tq, tq)
            k_t = win_ref[k_row0:k_row0 + HEAD_DIM, pl.ds(k0, tq)].astype(BF16)
            v_parts.append(win_ref[v_row0:v_row0 + HEAD_DIM, pl.ds(k0, tq)].astype(BF16))
            kp = kb * tq + lax.broadcasted_iota(jnp.int32, (1, tq), 1)
            m_parts.append((kp <= pos4) & (kp > pos4 - WINDOW) & (kp >= 0))
            s_parts.append(_dot(q, k_t))
        p_w = _masked_softmax(jnp.concatenate(s_parts, axis=-1), jnp.concatenate(m_parts, axis=-1))
        o_w = _dot_nt(p_w[:, 0:tq].astype(BF16), v_parts[0])
        for b in range(1, n_prev + 1):
            o_w = o_w + _dot_nt(p_w[:, b * tq:(b + 1) * tq].astype(BF16), v_parts[b])

        for r in range(N_REP):
            h = g * N_REP + r
            rows = slice(r * tq, (r + 1) * tq)
            o = (gates[:, h:h + 1] * o_c[rows] + gates[:, N_HEADS + h:N_HEADS + h + 1] * o_s[rows]
                 + gates[:, 2 * N_HEADS + h:2 * N_HEADS + h + 1] * o_w[rows])
            o_ref[:, h * HEAD_DIM:(h + 1) * HEAD_DIM] = o


def _overlap_t(n_cmp_rows, n_blk, n_slc):
    c = np.arange(n_cmp_rows)
    start = c * CMP_STRIDE
    end = start + CMP_BLOCK - 1
    j0 = np.arange(n_slc) * SLC_BLOCK
    ov = (end[None, :] >= j0[:, None]) & (start[None, :] < j0[:, None] + SLC_BLOCK) & (c[None, :] < n_blk)
    return jnp.asarray(ov.astype(np.float32), BF16)


def _attn_prompt(q_t, kvc, slc_t, win_t, gates):
    n, _, s = q_t.shape
    n_cmp = kvc.shape[2]
    n_blk = s // CMP_STRIDE - CMP_BLOCK // CMP_STRIDE + 1
    n_slc = s // SLC_BLOCK
    ovt = _overlap_t(n_cmp, n_blk, n_slc)
    tq = Q_BLOCK
    rows = N_REP * tq
    return pl.pallas_call(
        functools.partial(_attn_prompt_kernel, s=s, n_blk=n_blk),
        grid=(n, s // tq),
        in_specs=[pl.BlockSpec((None, D_ATTN, tq), lambda i, j: (i, 0, j)),
                  pl.BlockSpec((None,) + kvc.shape[1:], lambda i, j: (i, 0, 0, 0)),
                  pl.BlockSpec((None, D_ROWS, s), lambda i, j: (i, 0, 0)),
                  pl.BlockSpec((None, D_ROWS, s), lambda i, j: (i, 0, 0)),
                  pl.BlockSpec((None, tq, GATE_PAD), lambda i, j: (i, j, 0)),
                  pl.BlockSpec(ovt.shape, lambda i, j: (0, 0))],
        out_specs=pl.BlockSpec((None, tq, D_ATTN), lambda i, j: (i, j, 0)),
        out_shape=jax.ShapeDtypeStruct((n, s, D_ATTN), F32),
        scratch_shapes=[pltpu.VMEM((rows, 1), F32), pltpu.VMEM((rows, 1), F32), pltpu.VMEM((rows, HEAD_DIM), F32)],
        compiler_params=pltpu.CompilerParams(dimension_semantics=("arbitrary", "arbitrary"),
                                             vmem_limit_bytes=VMEM_LIMIT),
        name="attn_prompt",
    )(q_t, kvc, slc_t, win_t, gates, ovt)


PAGE = 128


def _sample_geometry(past):
    length = past + 1
    padded = -(-length // SLC_BLOCK) * SLC_BLOCK
    n_pc = padded // CMP_STRIDE
    n_pc_pad = -(-n_pc // 8) * 8
    return dict(n_pc=n_pc, n_pc_pad=n_pc_pad, n_blk=n_pc - CMP_BLOCK // CMP_STRIDE + 1, n_slc=padded // SLC_BLOCK)


def _attn_sample_kernel(pt_ref, q_ref, new_ref, gates_ref, wincol_ref, cwin_ref, ov_ref, bok_ref,
                        pe_ref, w1_ref, w2_ref, *rest, past, n_pages):
    del pt_ref
    cmp_pages = rest[0:n_pages]
    slc_pages = rest[n_pages:2 * n_pages]
    o_ref, wout_ref, xt_ref, flat_lo, flat_hi = rest[2 * n_pages:]
    geo = _sample_geometry(past)
    n_pc_pad, n_blk, n_slc = geo["n_pc_pad"], geo["n_blk"], geo["n_slc"]
    pos = past
    lb = cwin_ref.shape[-1]
    scale = HEAD_DIM ** -0.5
    rows = 8
    row_id = lax.broadcasted_iota(jnp.int32, (rows, 1), 0)
    real_row = row_id < N_REP
    gates = gates_ref[...]
    lane = lax.broadcasted_iota(jnp.int32, (1, 128), 1)

    def new_row(branch, kv, g):
        o = ((branch * 2 + kv) * N_KV_HEADS + g) * HEAD_DIM
        return new_ref[:, o:o + HEAD_DIM]

    for g in range(N_KV_HEADS):
        qf = jnp.concatenate([q_ref[:, (g * N_REP + r) * HEAD_DIM:(g * N_REP + r + 1) * HEAD_DIM]
                              for r in range(N_REP)] + [jnp.zeros((rows - N_REP, HEAD_DIM), F32)], axis=0)
        qf = (qf * scale).astype(BF16)
        q32 = qf.astype(F32)

        kv_c = []
        for kv in range(2):
            for j in range(n_pages):
                xt_ref[j * PAGE:(j + 1) * PAGE, :] = cmp_pages[j][kv, g].T
            tail = n_pc_pad * CMP_STRIDE - past
            xt_ref[past:past + tail, :] = jnp.zeros((tail, HEAD_DIM), F32)
            xt_ref[past:past + 1, :] = new_row(0, kv, g)
            _compress_flat(xt_ref, n_pc_pad, pe_ref[kv], flat_lo, flat_hi)
            kv_c.append(_compress_mlp(flat_lo[...], flat_hi[...], w1_ref[kv], w2_ref[kv]).astype(BF16))
        kc, vc = kv_c
        c_idx = lax.broadcasted_iota(jnp.int32, (1, n_pc_pad), 1)
        valid = (c_idx * CMP_STRIDE + (CMP_BLOCK - 1) <= pos) & (c_idx < n_blk)
        p_c = _masked_softmax(_dot_nt(qf, kc), valid)
        o_c = _dot(p_c.astype(BF16), vc)

        pc_sum = jnp.sum(jnp.where(real_row, p_c, 0.0), axis=0, keepdims=True)
        pc_hi = pc_sum.astype(BF16)
        pc_lo = (pc_sum - pc_hi.astype(F32)).astype(BF16)
        imp = _dot(pc_hi, ov_ref[...]) + _dot(pc_lo, ov_ref[...])
        cur = pos // SLC_BLOCK
        forced = (lane == 0) | (lane == cur) | (lane == cur - 1)
        score = jnp.where(lane <= cur, jnp.where(forced, BIG, imp), -BIG)
        score = jnp.where(lane < n_slc, score, -2.0 * BIG)
        score_b = jnp.broadcast_to(score, (128, 128))
        score_a = score_b.T
        jp = lax.broadcasted_iota(jnp.int32, (128, 128), 0)
        jj = lax.broadcasted_iota(jnp.int32, (128, 128), 1)
        beats = (score_a > score_b) | ((score_a == score_b) & (jp < jj))
        rank = jnp.sum(beats.astype(F32), axis=0, keepdims=True)
        sel = ((rank < float(min(N_SEL, n_slc))) & (lane < n_slc)).astype(F32)

        chosen = _dot(jnp.broadcast_to(sel, (rows, 128)).astype(BF16), bok_ref[...]) > 0.5
        s_parts = [_dot(qf, slc_pages[j][0, g].astype(BF16)) for j in range(n_pages)]
        s_past = jnp.where(chosen, jnp.concatenate(s_parts, axis=-1), NEG)
        k_new = new_row(1, 0, g)
        v_new = new_row(1, 1, g)
        new_blk = pos // SLC_BLOCK
        new_ok = jnp.sum(jnp.where(lane == new_blk, sel, 0.0), axis=-1, keepdims=True) > 0.5
        s_new = jnp.where(new_ok, jnp.sum(q32 * k_new.astype(BF16).astype(F32), axis=-1, keepdims=True), NEG)
        m = jnp.maximum(jnp.max(s_past, axis=-1, keepdims=True), s_new)
        p_past = jnp.where(chosen, jnp.exp(s_past - m), 0.0)
        p_new = jnp.where(new_ok, jnp.exp(s_new - m), 0.0)
        denom = jnp.maximum(jnp.sum(p_past, axis=-1, keepdims=True) + p_new, 1e-30)
        o_s = p_new.astype(BF16).astype(F32) * v_new.astype(BF16).astype(F32)
        for j in range(n_pages):
            o_s = o_s + _dot_nt(p_past[:, j * PAGE:(j + 1) * PAGE].astype(BF16), slc_pages[j][1, g].astype(BF16))
        o_s = o_s / denom

        kw_new = new_row(2, 0, g)
        vw_new = new_row(2, 1, g)
        kpos = (past - lb) + lax.broadcasted_iota(jnp.int32, (1, lb), 1)
        w_ok = (kpos <= pos) & (kpos > pos - WINDOW)
        s_w = jnp.where(w_ok, _dot(qf, cwin_ref[0, g].astype(BF16)), NEG)
        s_wn = jnp.sum(q32 * kw_new.astype(BF16).astype(F32), axis=-1, keepdims=True)
        m = jnp.maximum(jnp.max(s_w, axis=-1, keepdims=True), s_wn)
        p_w = jnp.where(w_ok, jnp.exp(s_w - m), 0.0)
        p_wn = jnp.exp(s_wn - m)
        denom = jnp.maximum(jnp.sum(p_w, axis=-1, keepdims=True) + p_wn, 1e-30)
        o_w = (_dot_nt(p_w.astype(BF16), cwin_ref[1, g].astype(BF16))
               + p_wn.astype(BF16).astype(F32) * vw_new.astype(BF16).astype(F32)) / denom

        for r in range(N_REP):
            h = g * N_REP + r
            o = (gates[:, h:h + 1] * o_c[r:r + 1] + gates[:, N_HEADS + h:N_HEADS + h + 1] * o_s[r:r + 1]
                 + gates[:, 2 * N_HEADS + h:2 * N_HEADS + h + 1] * o_w[r:r + 1])
            o_ref[:, h * HEAD_DIM:(h + 1) * HEAD_DIM] = o

        for kv in range(2):
            row0 = (kv * N_KV_HEADS + g) * HEAD_DIM
            wout_ref[kv, g, :, 0:lb - 1] = cwin_ref[kv, g, :, 1:lb]
            wout_ref[kv, g, :, lb - 1:lb] = wincol_ref[row0:row0 + HEAD_DIM, :]


def _attn_sample(l, q_tok, new_tok, gates, win_col, cwin_t, ccmp_t, cslc_t, page_table, pe, w1, w2):
    nb = q_tok.shape[0]
    n_pages = page_table.shape[1]
    past = n_pages * PAGE
    geo = _sample_geometry(past)
    lb = cwin_t.shape[-1]
    ov = _overlap_t(geo["n_pc_pad"], geo["n_blk"], 128).T
    ov = jnp.where(jnp.arange(128)[None, :] < geo["n_slc"], ov, 0).astype(BF16)
    bok = jnp.asarray((np.arange(past)[None, :] // SLC_BLOCK == np.arange(128)[:, None]).astype(np.float32), BF16)
    seq = lambda a: pl.BlockSpec((None,) + a.shape[1:], lambda i, pt: (i,) + (0,) * (a.ndim - 1))
    full = lambda a: pl.BlockSpec(a.shape, lambda i, pt: (0,) * a.ndim)
    page_block = (None, None, 2, N_KV_HEADS, HEAD_DIM, PAGE)
    page_spec = lambda j: pl.BlockSpec(page_block, lambda i, pt, j=j: (l, pt[i, j], 0, 0, 0, 0))
    win_block = (None, None, 2, N_KV_HEADS, HEAD_DIM, lb)
    grid_spec = pltpu.PrefetchScalarGridSpec(
        num_scalar_prefetch=1,
        grid=(nb,),
        in_specs=[seq(q_tok), seq(new_tok), seq(gates), seq(win_col),
                  pl.BlockSpec(win_block, lambda i, pt: (l, i, 0, 0, 0, 0)),
                  full(ov), full(bok), full(pe), full(w1), full(w2)]
                 + [page_spec(j) for j in range(n_pages)] * 2,
        out_specs=[pl.BlockSpec((None, 1, D_ATTN), lambda i, pt: (i, 0, 0)),
                   pl.BlockSpec((None, 2, N_KV_HEADS, HEAD_DIM, lb), lambda i, pt: (i, 0, 0, 0, 0))],
        scratch_shapes=[pltpu.VMEM((geo["n_pc_pad"] * CMP_STRIDE, HEAD_DIM), F32),
                        pltpu.VMEM((geo["n_pc_pad"], CMP_STRIDE * HEAD_DIM), BF16),
                        pltpu.VMEM((geo["n_pc_pad"], CMP_STRIDE * HEAD_DIM), BF16)],
    )
    return pl.pallas_call(
        functools.partial(_attn_sample_kernel, past=past, n_pages=n_pages),
        grid_spec=grid_spec,
        out_shape=[jax.ShapeDtypeStruct((nb, 1, D_ATTN), F32),
                   jax.ShapeDtypeStruct((nb, 2, N_KV_HEADS, HEAD_DIM, lb), F32)],
        compiler_params=pltpu.CompilerParams(dimension_semantics=("arbitrary",), vmem_limit_bytes=VMEM_LIMIT),
        name="attn_sample",
    )(page_table, q_tok, new_tok, gates, win_col, cwin_t, ov, bok, pe, w1, w2,
      *([ccmp_t] * n_pages), *([cslc_t] * n_pages))


def _prep_layer(l, w_norm, w_in, w_out, w_dw, b_dw, ln_g, ln_b, w_pw, b_pw, w_pool, pool_scale,
                g_q, g_k, cmp_pe, cmp_w1, cmp_w2):
    sizes = (D_CONV, D_CONV, D_CONV, D_POOL, D_POOL, D_ATTN, D_KV, N_BRANCH * N_HEADS, D_ATTN)
    offs = [0] + [int(v) for v in np.cumsum(sizes)]
    col = lambda k: w_in[l][:, offs[k]:offs[k + 1]]
    a_val, a_gate, z_a, b_in, z_b, q, kv, gate, z_c = (col(k) for k in range(9))
    pad = jnp.zeros((D_MODEL, GATE_PAD - N_BRANCH * N_HEADS), F32)
    wtok = jnp.concatenate([a_val, a_gate, b_in, z_a, z_b, z_c, gate, pad], axis=1).astype(BF16)
    wfeat = jnp.concatenate([q, kv], axis=1).T.astype(BF16)
    ones = jnp.ones((N_KV_HEADS * HEAD_DIM,), F32)
    gcol = jnp.concatenate([jnp.tile(g_q[l], N_HEADS)]
                           + [piece for br in range(N_BRANCH)
                              for piece in (jnp.tile(g_k[l, br], N_KV_HEADS), ones)])[:, None]
    wpool_bd = jnp.zeros((D_POOL, D_POOL), F32)
    for gi in range(len(POOL_WINDOWS)):
        wpool_bd = wpool_bd.at[gi * POOL_GROUP:(gi + 1) * POOL_GROUP, gi * POOL_GROUP:(gi + 1) * POOL_GROUP].set(w_pool[l, gi])
    row = lambda v: v[None, :]
    return dict(
        wnorm=row(w_norm[l]), wtok=wtok, wfeat=wfeat, gcol=gcol,
        wdw=w_dw[l], bdw=row(b_dw[l]), lng=row(ln_g[l]), lnb=row(ln_b[l]),
        wpw=w_pw[l].astype(BF16), bpw=row(b_pw[l]),
        wpool=wpool_bd.astype(BF16), pscale=row(pool_scale[l]),
        pe=cmp_pe[l], w1=cmp_w1[l].astype(BF16), w2=cmp_w2[l].astype(BF16),
        wout=w_out[l].astype(BF16),
    )


def _rope_tables(pos):
    inv = ROPE_THETA ** (-jnp.arange(ROT_HALF, dtype=F32) * 2.0 / ROT_DIM)
    ang = pos.astype(F32)[:, None] * inv[None, :]
    return jnp.cos(ang).T, jnp.sin(ang).T


def _prompt_layer(x, p, cos_t, sin_t):
    glu, b_in, sz, gates, q_t, cmp_t, slc_t, win_t = _in_proj(
        x, p["wnorm"], p["wtok"], p["wfeat"], p["gcol"], cos_t, sin_t, tm=min(512, x.shape[1]))
    yab = _convpool_prompt(glu, b_in, sz, p["wdw"], p["bdw"], p["lng"], p["lnb"], p["wpw"], p["bpw"],
                           p["wpool"], p["pscale"], t=min(256, x.shape[1]))
    kvc = _compress_prompt(cmp_t, p["pe"], p["w1"], p["w2"])
    o_attn = _attn_prompt(q_t, kvc, slc_t, win_t, gates)
    y = _out_proj(x, yab, o_attn, sz, p["wout"], tm=min(512, x.shape[1]))
    return y, glu, b_in, cmp_t, slc_t, win_t


def _sample_layer(l, x, p, cos_t, sin_t, sc, sp, cwin_t, ccmp_t, cslc_t, page_table, past):
    nb = x.shape[0]
    glu, b_in, sz, gates, q_t, cmp_t, slc_t, win_t = _in_proj(
        x[None], p["wnorm"], p["wtok"], p["wfeat"], p["gcol"], cos_t, sin_t, tm=nb)
    glu, b_in, sz, gates = glu[0], b_in[0], sz[0], gates[0]
    yab, sc_new, sp_new = _convpool_sample(sc, sp, glu, b_in, sz, p["wdw"], p["bdw"], p["lng"], p["lnb"],
                                           p["wpw"], p["bpw"], p["wpool"], p["pscale"], pos=past)
    new_tok = jnp.concatenate([cmp_t[0], slc_t[0], win_t[0]], axis=0).T[:, None, :]
    q_tok = q_t[0].T[:, None, :]
    win_col = win_t[0].T[:, :, None]
    o_attn, win_next = _attn_sample(l, q_tok, new_tok, gates[:, None, :], win_col, cwin_t, ccmp_t, cslc_t,
                                    page_table, p["pe"], p["w1"], p["w2"])
    y = _out_proj(x[None], yab[None], o_attn.reshape(1, nb, D_ATTN), sz[None], p["wout"], tm=nb)[0]
    return y, sc_new, sp_new, win_next, cmp_t[0], slc_t[0]


def _rows_from_feat(a):
    lead = a.shape[:-2]
    a = a.reshape(lead + (2, N_KV_HEADS, HEAD_DIM, a.shape[-1]))
    nl = len(lead)
    return jnp.transpose(a, tuple(range(nl)) + (nl + 3, nl, nl + 1, nl + 2))


def kernel(x_prompt, x_sample, state_conv, state_pool, cache_win_kv, cache_cmp_kv, cache_slc_kv, page_table,
           w_norm, w_in, w_out, w_dw, b_dw, ln_g, ln_b, w_pw, b_pw, w_pool, pool_scale,
           g_q, g_k, cmp_pe, cmp_w1, cmp_w2):
    bp, s, _ = x_prompt.shape
    bs, t_new, _ = x_sample.shape
    depth = w_in.shape[0]
    assert t_new == 1 and cache_cmp_kv.shape[2] == PAGE and s % 512 == 0
    past = page_table.shape[1] * PAGE
    lb = cache_win_kv.shape[2]

    to_feat = lambda c: jnp.transpose(c, (0, 1, 3, 4, 5, 2))
    cwin_t, ccmp_t, cslc_t = to_feat(cache_win_kv), to_feat(cache_cmp_kv), to_feat(cache_slc_kv)
    sc_all = jnp.transpose(state_conv, (0, 2, 1, 3))
    sp_all = jnp.transpose(state_pool, (0, 2, 1, 3))
    cos_p, sin_p = _rope_tables(jnp.arange(s))
    cos_s, sin_s = _rope_tables(jnp.full((bs,), past))

    xp, xs = x_prompt, x_sample[:, 0, :]
    outs = [[] for _ in range(10)]
    for l in range(depth):
        p = _prep_layer(l, w_norm, w_in, w_out, w_dw, b_dw, ln_g, ln_b, w_pw, b_pw, w_pool, pool_scale,
                        g_q, g_k, cmp_pe, cmp_w1, cmp_w2)
        xp, glu, b_in, cmp_t, slc_t, win_t = _prompt_layer(xp, p, cos_p, sin_p)
        outs[0].append(glu[:, s - CONV_STATE:, :])
        outs[1].append(b_in[:, s - POOL_STATE:, :])
        outs[2].append(_rows_from_feat(win_t[:, :, s - min(WINDOW, s):]))
        outs[3].append(_rows_from_feat(cmp_t))
        outs[4].append(_rows_from_feat(slc_t))
        xs, sc_new, sp_new, win_next, cmp_new, slc_new = _sample_layer(
            l, xs, p, cos_s, sin_s, sc_all[l], sp_all[l], cwin_t, ccmp_t, cslc_t, page_table, past)
        outs[5].append(jnp.transpose(sc_new, (1, 0, 2)))
        outs[6].append(jnp.transpose(sp_new, (1, 0, 2)))
        outs[7].append(jnp.transpose(win_next, (0, 4, 1, 2, 3)))
        outs[8].append(_rows_from_feat(cmp_new)[:, None])
        outs[9].append(_rows_from_feat(slc_new)[:, None])
    return (xp, xs[:, None, :]) + tuple(jnp.stack(o) for o in outs)
```

```python
import functools

import numpy as np
import jax
import jax.numpy as jnp
from jax import lax
from jax.experimental import pallas as pl
from jax.experimental.pallas import tpu as pltpu

D_MODEL = 1024
D_CONV = 256
D_POOL = 256
N_HEADS = 8
N_KV_HEADS = 2
HEAD_DIM = 64
D_ATTN = N_HEADS * HEAD_DIM
CONV_WIDTH = 31
CONV_STATE = CONV_WIDTH - 1
POOL_WINDOWS = (2, 4, 8, 16)
POOL_GROUP = D_POOL // len(POOL_WINDOWS)
POOL_STATE = max(POOL_WINDOWS) - 1
CMP_BLOCK = 32
CMP_STRIDE = 16
SLC_BLOCK = 64
N_SEL = 8
WINDOW = 256
Q_BLOCK = 128
ROT_DIM = HEAD_DIM // 4
ROT_HALF = ROT_DIM // 2
ROPE_THETA = 500000.0
N_BRANCH = 3
D_KV = N_BRANCH * 2 * N_KV_HEADS * HEAD_DIM
D_ROWS = 2 * N_KV_HEADS * HEAD_DIM
N_REP = N_HEADS // N_KV_HEADS
EPS = 1e-6
NEG = -1e30
BIG = 1e4

GATE_PAD = 128
D_TOK = 3 * D_CONV + 2 * D_POOL + D_ATTN + GATE_PAD
D_FEAT = D_ATTN + D_KV
N_FEAT_BLOCKS = D_FEAT // HEAD_DIM
SEL_KEYS = 256
VMEM_LIMIT = 56 * 1024 * 1024

F32 = jnp.float32
BF16 = jnp.bfloat16
NT_DIMS = (((1,), (1,)), ((), ()))


def _sigmoid(x):
    return 1.0 / (1.0 + jnp.exp(-x))


def _silu(x):
    return x * _sigmoid(x)


def _gelu_tanh(x):
    return 0.5 * x * (1.0 + jnp.tanh(np.sqrt(2.0 / np.pi).astype(np.float32) * (x + 0.044715 * (x * x * x))))


def _dot(a, b):
    return jnp.dot(a, b, preferred_element_type=F32)


def _dot_nt(a, b):
    return lax.dot_general(a, b, NT_DIMS, preferred_element_type=F32)


def _masked_softmax(s, mask):
    s = jnp.where(mask, s, NEG)
    m = jnp.max(s, axis=-1, keepdims=True)
    p = jnp.where(mask, jnp.exp(s - m), 0.0)
    return p / jnp.maximum(jnp.sum(p, axis=-1, keepdims=True), 1e-30)


def _in_proj_kernel(x_ref, wnorm_ref, wtok_ref, wfeat_ref, gcol_ref, cos_ref, sin_ref,
                    glu_ref, bin_ref, sz_ref, gates_ref, qt_ref, cmp_ref, slc_ref, win_ref):
    x = x_ref[...]
    ms = jnp.mean(x * x, axis=-1, keepdims=True)
    h = (x * lax.rsqrt(ms + EPS) * wnorm_ref[...]).astype(BF16)

    a = _dot(h, wtok_ref[...])
    o = 0
    glu_ref[...] = a[:, o:o + D_CONV] * _sigmoid(a[:, o + D_CONV:o + 2 * D_CONV])
    o += 2 * D_CONV
    bin_ref[...] = a[:, o:o + D_POOL]
    o += D_POOL
    z = a[:, o:o + D_CONV + D_POOL + D_ATTN]
    sz_ref[...] = _silu(z)
    o += D_CONV + D_POOL + D_ATTN
    gates_ref[...] = _sigmoid(a[:, o:o + GATE_PAD])

    f = _dot_nt(wfeat_ref[...], h)
    cos = cos_ref[...]
    sin = sin_ref[...]
    kv_refs = (cmp_ref, slc_ref, win_ref)
    for hb in range(N_FEAT_BLOCKS):
        blk = f[hb * HEAD_DIM:(hb + 1) * HEAD_DIM, :]
        kv_blk = hb - N_HEADS
        is_value = kv_blk >= 0 and (kv_blk % (2 * N_KV_HEADS)) >= N_KV_HEADS
        if not is_value:
            bms = jnp.mean(blk * blk, axis=0, keepdims=True)
            y = blk * lax.rsqrt(bms + EPS) * gcol_ref[hb * HEAD_DIM:(hb + 1) * HEAD_DIM, :]
            x1 = y[0:ROT_HALF]
            x2 = y[ROT_HALF:ROT_DIM]
            blk = jnp.concatenate([x1 * cos - x2 * sin, x2 * cos + x1 * sin, y[ROT_DIM:]], axis=0)
        if kv_blk < 0:
            qt_ref[hb * HEAD_DIM:(hb + 1) * HEAD_DIM, :] = blk
        else:
            r = kv_blk % (2 * N_KV_HEADS)
            kv_refs[kv_blk // (2 * N_KV_HEADS)][r * HEAD_DIM:(r + 1) * HEAD_DIM, :] = blk


def _in_proj(x, wnorm, wtok, wfeat, gcol, cos_t, sin_t, tm):
    n, t, _ = x.shape
    grid = (n, t // tm)
    tok = lambda w: pl.BlockSpec((None, tm, w), lambda i, j: (i, j, 0))
    feat = lambda w: pl.BlockSpec((None, w, tm), lambda i, j: (i, 0, j))
    full = lambda a: pl.BlockSpec(a.shape, lambda i, j: (0,) * a.ndim)
    tok_shape = lambda w: jax.ShapeDtypeStruct((n, t, w), F32)
    feat_shape = lambda w: jax.ShapeDtypeStruct((n, w, t), F32)
    return pl.pallas_call(
        _in_proj_kernel,
        grid=grid,
        in_specs=[tok(D_MODEL), full(wnorm), full(wtok), full(wfeat), full(gcol),
                  pl.BlockSpec((ROT_HALF, tm), lambda i, j: (0, j)),
                  pl.BlockSpec((ROT_HALF, tm), lambda i, j: (0, j))],
        out_specs=[tok(D_CONV), tok(D_POOL), tok(D_CONV + D_POOL + D_ATTN), tok(GATE_PAD),
                   feat(D_ATTN), feat(D_ROWS), feat(D_ROWS), feat(D_ROWS)],
        out_shape=[tok_shape(D_CONV), tok_shape(D_POOL), tok_shape(D_CONV + D_POOL + D_ATTN), tok_shape(GATE_PAD),
                   feat_shape(D_ATTN), feat_shape(D_ROWS), feat_shape(D_ROWS), feat_shape(D_ROWS)],
        compiler_params=pltpu.CompilerParams(dimension_semantics=("arbitrary", "arbitrary"),
                                             vmem_limit_bytes=VMEM_LIMIT),
        name="in_proj",
    )(x, wnorm, wtok, wfeat, gcol, cos_t, sin_t)


def _conv_tail(acc, sz_a, bdw, lng, lnb, wpw, bpw):
    y = acc + bdw
    mu = jnp.mean(y, axis=-1, keepdims=True)
    yc = y - mu
    var = jnp.mean(yc * yc, axis=-1, keepdims=True)
    y = yc * lax.rsqrt(var + EPS) * lng + lnb
    y = _dot(_silu(y).astype(BF16), wpw) + bpw
    return y * sz_a


def _pool_tail(total, cnt, xcur, sz_b, wpool, pscale):
    d = total / cnt - xcur
    y = _dot(d.astype(BF16), wpool) * pscale
    return y * sz_b


def _pool_window_of_lane(shape):
    lane = lax.broadcasted_iota(jnp.int32, shape, len(shape) - 1)
    w = jnp.full(shape, POOL_WINDOWS[0], jnp.int32)
    for gi in range(1, len(POOL_WINDOWS)):
        w = jnp.where(lane >= gi * POOL_GROUP, POOL_WINDOWS[gi], w)
    return w


def _pool_select(sums, shape):
    lane = lax.broadcasted_iota(jnp.int32, shape, len(shape) - 1)
    total = sums[0]
    for gi in range(1, len(POOL_WINDOWS)):
        total = jnp.where(lane >= gi * POOL_GROUP, sums[gi], total)
    return total


CONV_HALO = 32
POOL_HALO = 16


def _convpool_prompt_kernel(glu_ref, gprev_ref, bin_ref, bprev_ref, sz_ref, wdw_ref, bdw_ref, lng_ref, lnb_ref,
                            wpw_ref, bpw_ref, wpool_ref, pscale_ref, y_ref, cbuf, pbuf, *, t):
    i = pl.program_id(1)
    keep = (i > 0).astype(F32)
    cbuf[0:CONV_HALO, :] = gprev_ref[...] * keep
    cbuf[CONV_HALO:CONV_HALO + t, :] = glu_ref[...]
    acc = jnp.zeros((t, D_CONV), F32)
    for k in range(CONV_WIDTH):
        acc = acc + wdw_ref[k:k + 1, :] * cbuf[pl.ds(CONV_HALO - CONV_STATE + k, t), :]
    sz = sz_ref[...]
    y_ref[:, 0:D_CONV] = _conv_tail(acc, sz[:, 0:D_CONV], bdw_ref[...], lng_ref[...], lnb_ref[...],
                                    wpw_ref[...], bpw_ref[...])

    pbuf[0:POOL_HALO, :] = bprev_ref[...] * keep
    xcur = bin_ref[...]
    pbuf[POOL_HALO:POOL_HALO + t, :] = xcur
    sums = []
    run = jnp.zeros((t, D_POOL), F32)
    j = 0
    for w in POOL_WINDOWS:
        while j < w:
            run = run + pbuf[pl.ds(POOL_HALO - j, t), :]
            j += 1
        sums.append(run)
    total = _pool_select(sums, (t, D_POOL))
    pos = i * t + lax.broadcasted_iota(jnp.int32, (t, D_POOL), 0)
    cnt = jnp.minimum(_pool_window_of_lane((t, D_POOL)), pos + 1).astype(F32)
    y_ref[:, D_CONV:D_CONV + D_POOL] = _pool_tail(total, cnt, xcur, sz[:, D_CONV:D_CONV + D_POOL],
                                                  wpool_ref[...], pscale_ref[...])


def _convpool_prompt(glu, b_in, sz, wdw, bdw, lng, lnb, wpw, bpw, wpool, pscale, t):
    n, s, _ = glu.shape
    grid = (n, s // t)
    cur = lambda w: pl.BlockSpec((None, t, w), lambda i, j: (i, j, 0))
    prev = lambda rows, w: pl.BlockSpec((None, rows, w), lambda i, j: (i, jnp.maximum(j * (t // rows) - 1, 0), 0))
    full = lambda a: pl.BlockSpec(a.shape, lambda i, j: (0,) * a.ndim)
    return pl.pallas_call(
        functools.partial(_convpool_prompt_kernel, t=t),
        grid=grid,
        in_specs=[cur(D_CONV), prev(CONV_HALO, D_CONV), cur(D_POOL), prev(POOL_HALO, D_POOL),
                  cur(D_CONV + D_POOL),
                  full(wdw), full(bdw), full(lng), full(lnb), full(wpw), full(bpw), full(wpool), full(pscale)],
        out_specs=cur(D_CONV + D_POOL),
        out_shape=jax.ShapeDtypeStruct((n, s, D_CONV + D_POOL), F32),
        scratch_shapes=[pltpu.VMEM((CONV_HALO + t, D_CONV), F32), pltpu.VMEM((POOL_HALO + t, D_POOL), F32)],
        compiler_params=pltpu.CompilerParams(dimension_semantics=("arbitrary", "arbitrary"),
                                             vmem_limit_bytes=VMEM_LIMIT),
        name="convpool_prompt",
    )(glu, glu, b_in, b_in, sz, wdw, bdw, lng, lnb, wpw, bpw, wpool, pscale)


def _convpool_sample_kernel(sc_ref, sp_ref, glu_ref, bin_ref, sz_ref, wdw_ref, bdw_ref, lng_ref, lnb_ref,
                            wpw_ref, bpw_ref, wpool_ref, pscale_ref, y_ref, sc_out, sp_out, *, pos):
    glu = glu_ref[...]
    acc = wdw_ref[CONV_STATE:CONV_WIDTH, :] * glu
    for k in range(CONV_STATE):
        acc = acc + wdw_ref[k:k + 1, :] * sc_ref[k]
    sz = sz_ref[...]
    y_ref[:, 0:D_CONV] = _conv_tail(acc, sz[:, 0:D_CONV], bdw_ref[...], lng_ref[...], lnb_ref[...],
                                    wpw_ref[...], bpw_ref[...])
    for k in range(CONV_STATE - 1):
        sc_out[k] = sc_ref[k + 1]
    sc_out[CONV_STATE - 1] = glu

    xcur = bin_ref[...]
    shape = xcur.shape
    sums = []
    run = xcur
    j = 1
    for w in POOL_WINDOWS:
        while j < w:
            run = run + sp_ref[POOL_STATE - j]
            j += 1
        sums.append(run)
    total = _pool_select(sums, shape)
    cnt = jnp.minimum(_pool_window_of_lane(shape), pos + 1).astype(F32)
    y_ref[:, D_CONV:D_CONV + D_POOL] = _pool_tail(total, cnt, xcur, sz[:, D_CONV:D_CONV + D_POOL],
                                                  wpool_ref[...], pscale_ref[...])
    for k in range(POOL_STATE - 1):
        sp_out[k] = sp_ref[k + 1]
    sp_out[POOL_STATE - 1] = xcur


def _convpool_sample(sc, sp, glu, b_in, sz, wdw, bdw, lng, lnb, wpw, bpw, wpool, pscale, pos):
    nb = glu.shape[0]
    return pl.pallas_call(
        functools.partial(_convpool_sample_kernel, pos=pos),
        out_shape=[jax.ShapeDtypeStruct((nb, D_CONV + D_POOL), F32),
                   jax.ShapeDtypeStruct(sc.shape, F32), jax.ShapeDtypeStruct(sp.shape, F32)],
        compiler_params=pltpu.CompilerParams(vmem_limit_bytes=VMEM_LIMIT),
        name="convpool_sample",
    )(sc, sp, glu, b_in, sz[:, 0:D_CONV + D_POOL], wdw, bdw, lng, lnb, wpw, bpw, wpool, pscale)


def _out_proj_kernel(x_ref, yab_ref, oat_ref, szc_ref, w_ref, o_ref):
    ya = yab_ref[...].astype(BF16)
    yc = (oat_ref[...] * szc_ref[...]).astype(BF16)
    nab = D_CONV + D_POOL
    o_ref[...] = x_ref[...] + _dot(ya, w_ref[0:nab, :]) + _dot(yc, w_ref[nab:nab + D_ATTN, :])


def _out_proj(x, yab, o_attn, sz, w_out, tm):
    n, t, _ = x.shape
    nab = D_CONV + D_POOL
    assert nab == D_ATTN
    grid = (n, t // tm)
    tok = lambda w, cb=0: pl.BlockSpec((None, tm, w), lambda i, j: (i, j, cb))
    return pl.pallas_call(
        _out_proj_kernel,
        grid=grid,
        in_specs=[tok(D_MODEL), tok(nab), tok(D_ATTN), tok(D_ATTN, 1),
                  pl.BlockSpec(w_out.shape, lambda i, j: (0, 0))],
        out_specs=tok(D_MODEL),
        out_shape=jax.ShapeDtypeStruct(x.shape, F32),
        compiler_params=pltpu.CompilerParams(dimension_semantics=("arbitrary", "arbitrary"),
                                             vmem_limit_bytes=VMEM_LIMIT),
        name="out_proj",
    )(x, yab, o_attn, sz, w_out)


GD = N_KV_HEADS * HEAD_DIM
CMP_FLAT = CMP_STRIDE * GD


FLAT_SPAN = CMP_STRIDE * CMP_STRIDE


def _piece_perm():
    m = np.arange(FLAT_SPAN)
    src = (m % CMP_STRIDE) * CMP_STRIDE + m // CMP_STRIDE
    return jnp.asarray((src[:, None] == np.arange(FLAT_SPAN)[None, :]).astype(np.float32), BF16)


def _fill_flat(x_t, perm, flat_ref, piece0):
    y = _dot_nt(perm, x_t).astype(BF16)
    for kv in range(2):
        for r in range(CMP_STRIDE):
            flat_ref[kv, piece0:piece0 + CMP_STRIDE, r * GD:(r + 1) * GD] = (
                y[r * CMP_STRIDE:(r + 1) * CMP_STRIDE, kv * GD:(kv + 1) * GD])


def _compress_mlp(flat, w1ab, bias, w2):
    n_pc = flat.shape[0]
    hb = _dot(flat, w1ab)
    h = hb[:, 0:GD] + jnp.concatenate([hb[1:n_pc, GD:2 * GD], jnp.zeros((1, GD), F32)], axis=0) + bias
    out = _dot(_gelu_tanh(h).astype(BF16), w2)
    row = lax.broadcasted_iota(jnp.int32, out.shape, 0)
    return jnp.where(row < n_pc - 1, out, 0.0)


def _cmp_bias_kernel(pe_ref, w1_ref, o_ref):
    for kv in range(2):
        o_ref[kv] = _dot(pe_ref[kv].astype(BF16), w1_ref[kv])


def _cmp_bias(pe_flat, w1):
    return pl.pallas_call(_cmp_bias_kernel, out_shape=jax.ShapeDtypeStruct((2, 1, HEAD_DIM), F32),
                          name="cmp_bias")(pe_flat, w1)


def _compress_prompt_kernel(x_ref, perm_ref, w1ab_ref, bias_ref, w2_ref, o_ref, flat_ref, *, s):
    for t in range(s // FLAT_SPAN):
        _fill_flat(x_ref[:, t * FLAT_SPAN:(t + 1) * FLAT_SPAN].astype(BF16), perm_ref[...], flat_ref, t * CMP_STRIDE)
    for kv in range(2):
        o_ref[:, kv * GD:(kv + 1) * GD] = _compress_mlp(flat_ref[kv], w1ab_ref[kv], bias_ref[kv], w2_ref[kv])


def _compress_prompt(cmp_t, w1ab, bias, w2):
    n, _, s = cmp_t.shape
    assert s % FLAT_SPAN == 0
    n_pc = s // CMP_STRIDE
    perm = _piece_perm()
    full = lambda a: pl.BlockSpec(a.shape, lambda i: (0,) * a.ndim)
    return pl.pallas_call(
        functools.partial(_compress_prompt_kernel, s=s),
        grid=(n,),
        in_specs=[pl.BlockSpec((None, D_ROWS, s), lambda i: (i, 0, 0)), full(perm), full(w1ab), full(bias), full(w2)],
        out_specs=pl.BlockSpec((None, n_pc, 2 * GD), lambda i: (i, 0, 0)),
        out_shape=jax.ShapeDtypeStruct((n, n_pc, 2 * GD), F32),
        scratch_shapes=[pltpu.VMEM((2, n_pc, CMP_FLAT), BF16)],
        compiler_params=pltpu.CompilerParams(dimension_semantics=("arbitrary",), vmem_limit_bytes=VMEM_LIMIT),
        name="compress_prompt",
    )(cmp_t, perm, w1ab, bias, w2)


def _select_blocks(imp_t, pos_row, n_blocks):
    shape = imp_t.shape
    assert n_blocks % 8 == 0
    j = lax.broadcasted_iota(jnp.int32, shape, 0)
    cur = pos_row // SLC_BLOCK
    forced = (j == 0) | (j == cur) | (j == cur - 1)
    score = jnp.where(j <= cur, jnp.where(forced, BIG, imp_t), -BIG)
    tiles = [score[8 * v:8 * v + 8] for v in range(n_blocks // 8)]
    j8 = lax.broadcasted_iota(jnp.int32, (8, shape[1]), 0)
    ranks = [jnp.zeros((8, shape[1]), F32) for _ in tiles]
    for jp in range(n_blocks):
        sj = score[jp:jp + 1, :]
        for v, tile in enumerate(tiles):
            if jp < 8 * v:
                beats = sj >= tile
            elif jp >= 8 * v + 8:
                beats = sj > tile
            else:
                beats = (sj > tile) | ((sj == tile) & (j8 > jp - 8 * v))
            ranks[v] = ranks[v] + jnp.where(beats, 1.0, 0.0)
    rank = jnp.concatenate(ranks, axis=0)
    return (rank < float(min(N_SEL, n_blocks))).astype(F32)


def _col_softmax(s):
    m = jnp.max(s, axis=0, keepdims=True)
    e = jnp.exp(s - m)
    inv = jnp.where(m > 0.5 * NEG, 1.0 / jnp.maximum(jnp.sum(e, axis=0, keepdims=True), 1e-30), 0.0)
    return e, inv


def _mask_heads(sc, ok, tq):
    return jnp.concatenate([jnp.where(ok, sc[:, r * tq:(r + 1) * tq], NEG) for r in range(N_REP)], axis=1)


def _attn_prompt_kernel(qt_ref, kcv_ref, slc_ref, win_ref, gates_ref, ovt_ref, o_ref,
                        ks_ref, kw_ref, s_ref, sel_ref, acc_ref, *, s, n_blk):
    tq = Q_BLOCK
    cols = N_REP * tq
    groups = range(N_KV_HEADS)
    qc = pl.program_id(1)
    q0 = qc * tq
    pos_row = q0 + lax.broadcasted_iota(jnp.int32, (1, tq), 1)
    n_slc = s // SLC_BLOCK
    n_cmp = kcv_ref.shape[0]
    scale = HEAD_DIM ** -0.5
    blocks_per_step = SEL_KEYS // SLC_BLOCK
    v_row0 = lambda g: (N_KV_HEADS + g) * HEAD_DIM

    @pl.when(qc == 0)
    def _():
        for g in range(N_KV_HEADS):
            for t in range(s // 128):
                cols_t = slice(t * 128, (t + 1) * 128)
                ks_ref[g, cols_t, :] = slc_ref[g * HEAD_DIM:(g + 1) * HEAD_DIM, cols_t].T.astype(BF16)
                kw_ref[g, cols_t, :] = win_ref[g * HEAD_DIM:(g + 1) * HEAD_DIM, cols_t].T.astype(BF16)

    gates_t = gates_ref[...].T
    vc_t = kcv_ref[:, GD:2 * GD].T.astype(BF16)
    q_ts = []
    for g in groups:
        q_t = jnp.concatenate([qt_ref[(g * N_REP + r) * HEAD_DIM:(g * N_REP + r + 1) * HEAD_DIM, :]
                               for r in range(N_REP)], axis=1)
        q_ts.append((q_t * scale).astype(BF16))

    c_idx = lax.broadcasted_iota(jnp.int32, (n_cmp, 1), 0)
    valid_c = (c_idx * CMP_STRIDE + (CMP_BLOCK - 1) <= pos_row) & (c_idx < n_blk)
    ovt = ovt_ref[...]
    o_cs = []
    for g in groups:
        kc = kcv_ref[:, g * HEAD_DIM:(g + 1) * HEAD_DIM].astype(BF16)
        e_c, inv_c = _col_softmax(_mask_heads(_dot(kc, q_ts[g]), valid_c, tq))
        o_cs.append(_dot(vc_t[g * HEAD_DIM:(g + 1) * HEAD_DIM, :], e_c.astype(BF16)) * inv_c)
        pc_sum = e_c[:, 0:tq] * inv_c[:, 0:tq]
        for r in range(1, N_REP):
            pc_sum = pc_sum + e_c[:, r * tq:(r + 1) * tq] * inv_c[:, r * tq:(r + 1) * tq]
        pc_hi = pc_sum.astype(BF16)
        pc_lo = (pc_sum - pc_hi.astype(F32)).astype(BF16)
        imp_t = _dot(ovt, pc_hi) + _dot(ovt, pc_lo)
        sel_ref[g] = _select_blocks(imp_t, pos_row, n_slc)

    def score_step(i, m_run):
        k0 = pl.multiple_of(i * SEL_KEYS, SEL_KEYS)
        key = k0 + lax.broadcasted_iota(jnp.int32, (SEL_KEYS, 1), 0)
        causal = key <= pos_row
        m_new = []
        for g in groups:
            sc = _dot(ks_ref[g, pl.ds(k0, SEL_KEYS), :], q_ts[g])
            parts = []
            for jj in range(blocks_per_step):
                rows = slice(jj * SLC_BLOCK, (jj + 1) * SLC_BLOCK)
                chosen = sel_ref[g, pl.ds(i * blocks_per_step + jj, 1), :] > 0.5
                parts.append(_mask_heads(sc[rows], chosen & causal[rows], tq))
            sc = jnp.concatenate(parts, axis=0)
            s_ref[g, pl.ds(k0, SEL_KEYS), :] = sc
            m_new.append(jnp.maximum(m_run[g], jnp.max(sc, axis=0, keepdims=True)))
        return tuple(m_new)

    n_steps = (q0 + tq + SEL_KEYS - 1) // SEL_KEYS
    m_sel = lax.fori_loop(0, n_steps, score_step, tuple(jnp.full((1, cols), NEG, F32) for _ in groups))

    acc_ref[...] = jnp.zeros(acc_ref.shape, F32)

    def value_step(i, l_run):
        k0 = pl.multiple_of(i * SEL_KEYS, SEL_KEYS)
        l_new = []
        for g in groups:
            p = jnp.exp(s_ref[g, pl.ds(k0, SEL_KEYS), :] - m_sel[g])
            v_t = slc_ref[v_row0(g):v_row0(g) + HEAD_DIM, pl.ds(k0, SEL_KEYS)].astype(BF16)
            acc_ref[g] += _dot(v_t, p.astype(BF16))
            l_new.append(l_run[g] + jnp.sum(p, axis=0, keepdims=True))
        return tuple(l_new)

    l_sel = lax.fori_loop(0, n_steps, value_step, tuple(jnp.zeros((1, cols), F32) for _ in groups))

    n_prev = -(-(WINDOW - 1) // tq)
    k0s, oks = [], []
    for b in range(n_prev + 1):
        kb = qc - n_prev + b
        k0s.append(pl.multiple_of(jnp.maximum(kb, 0) * tq, tq))
        kp = kb * tq + lax.broadcasted_iota(jnp.int32, (tq, 1), 0)
        oks.append((kp <= pos_row) & (kp > pos_row - WINDOW) & (kp >= 0))

    out_blocks = []
    for g in groups:
        o_s = acc_ref[g] * (1.0 / jnp.maximum(l_sel[g], 1e-30))
        s_w = jnp.concatenate([_mask_heads(_dot(kw_ref[g, pl.ds(k0s[b], tq), :], q_ts[g]), oks[b], tq)
                               for b in range(n_prev + 1)], axis=0)
        e_w, inv_w = _col_softmax(s_w)
        o_w = _dot(win_ref[v_row0(g):v_row0(g) + HEAD_DIM, pl.ds(k0s[0], tq)].astype(BF16), e_w[0:tq].astype(BF16))
        for b in range(1, n_prev + 1):
            o_w = o_w + _dot(win_ref[v_row0(g):v_row0(g) + HEAD_DIM, pl.ds(k0s[b], tq)].astype(BF16),
                             e_w[b * tq:(b + 1) * tq].astype(BF16))
        o_w = o_w * inv_w
        for r in range(N_REP):
            h = g * N_REP + r
            c = slice(r * tq, (r + 1) * tq)
            out_blocks.append(gates_t[h:h + 1] * o_cs[g][:, c] + gates_t[N_HEADS + h:N_HEADS + h + 1] * o_s[:, c]
                              + gates_t[2 * N_HEADS + h:2 * N_HEADS + h + 1] * o_w[:, c])
    o_ref[...] = jnp.concatenate(out_blocks, axis=0).T


def _overlap_t(n_cmp_rows, n_blk, n_slc):
    c = np.arange(n_cmp_rows)
    start = c * CMP_STRIDE
    end = start + CMP_BLOCK - 1
    j0 = np.arange(n_slc) * SLC_BLOCK
    ov = (end[None, :] >= j0[:, None]) & (start[None, :] < j0[:, None] + SLC_BLOCK) & (c[None, :] < n_blk)
    return jnp.asarray(ov.astype(np.float32), BF16)


def _attn_prompt(q_t, kcv, slc_t, win_t, gates):
    n, _, s = q_t.shape
    n_cmp = kcv.shape[1]
    n_blk = s // CMP_STRIDE - CMP_BLOCK // CMP_STRIDE + 1
    n_slc = s // SLC_BLOCK
    ovt = _overlap_t(n_cmp, n_blk, n_slc)
    tq = Q_BLOCK
    cols = N_REP * tq
    assert s % SEL_KEYS == 0 and SEL_KEYS % SLC_BLOCK == 0
    return pl.pallas_call(
        functools.partial(_attn_prompt_kernel, s=s, n_blk=n_blk),
        grid=(n, s // tq),
        in_specs=[pl.BlockSpec((None, D_ATTN, tq), lambda i, j: (i, 0, j)),
                  pl.BlockSpec((None,) + kcv.shape[1:], lambda i, j: (i, 0, 0)),
                  pl.BlockSpec((None, D_ROWS, s), lambda i, j: (i, 0, 0)),
                  pl.BlockSpec((None, D_ROWS, s), lambda i, j: (i, 0, 0)),
                  pl.BlockSpec((None, tq, GATE_PAD), lambda i, j: (i, j, 0)),
                  pl.BlockSpec(ovt.shape, lambda i, j: (0, 0))],
        out_specs=pl.BlockSpec((None, tq, D_ATTN), lambda i, j: (i, j, 0)),
        out_shape=jax.ShapeDtypeStruct((n, s, D_ATTN), F32),
        scratch_shapes=[pltpu.VMEM((N_KV_HEADS, s, HEAD_DIM), BF16),
                        pltpu.VMEM((N_KV_HEADS, s, HEAD_DIM), BF16),
                        pltpu.VMEM((N_KV_HEADS, s, cols), F32),
                        pltpu.VMEM((N_KV_HEADS, n_slc, tq), F32),
                        pltpu.VMEM((N_KV_HEADS, HEAD_DIM, cols), F32)],
        compiler_params=pltpu.CompilerParams(dimension_semantics=("arbitrary", "arbitrary"),
                                             vmem_limit_bytes=VMEM_LIMIT),
        name="attn_prompt",
    )(q_t, kcv, slc_t, win_t, gates, ovt)


PAGE = 128
SAMPLE_SEQS = 4


def _sample_geometry(past):
    length = past + 1
    padded = -(-length // SLC_BLOCK) * SLC_BLOCK
    n_pc = padded // CMP_STRIDE
    n_pc_pad = -(-n_pc // CMP_STRIDE) * CMP_STRIDE
    return dict(n_pc=n_pc, n_pc_pad=n_pc_pad, n_blk=n_pc - CMP_BLOCK // CMP_STRIDE + 1, n_slc=padded // SLC_BLOCK)


def _attn_sample_kernel(pt_ref, q_ref, new_ref, gates_ref, wincol_ref, cwin_ref, ov_ref, bok_ref,
                        perm_ref, w1ab_ref, bias_ref, w2_ref, *rest, past, n_pages, nseq):
    del pt_ref
    cmp_pages = [rest[u * n_pages:(u + 1) * n_pages] for u in range(nseq)]
    slc_pages = [rest[(nseq + u) * n_pages:(nseq + u + 1) * n_pages] for u in range(nseq)]
    o_ref, wout_ref, flat_ref = rest[2 * nseq * n_pages:]
    geo = _sample_geometry(past)
    n_pc_pad, n_blk, n_slc = geo["n_pc_pad"], geo["n_blk"], geo["n_slc"]
    pos = past
    lb = cwin_ref.shape[-1]
    scale = HEAD_DIM ** -0.5
    head = lax.broadcasted_iota(jnp.int32, (N_HEADS, 1), 0)
    lane = lax.broadcasted_iota(jnp.int32, (1, GD), 1)
    lane_h = lax.broadcasted_iota(jnp.int32, (N_HEADS, GD), 1)
    own = (lane_h // HEAD_DIM) == (head // N_REP)

    def new_row(u, branch, kv):
        o = (branch * 2 + kv) * GD
        return new_ref[u, :, o:o + GD]

    def rounded(x):
        return x.astype(BF16).astype(F32)

    kcv = []
    pages_per_span = FLAT_SPAN // PAGE
    past_pc = past // CMP_STRIDE
    for u in range(nseq):
        row0 = u * n_pc_pad
        for t in range(n_pages // pages_per_span):
            x_t = jnp.concatenate([cmp_pages[u][t * pages_per_span + w][...].reshape(2 * GD, PAGE)
                                   for w in range(pages_per_span)], axis=1)
            _fill_flat(x_t.astype(BF16), perm_ref[...], flat_ref, row0 + t * CMP_STRIDE)
        for kv in range(2):
            flat_ref[kv, row0 + past_pc:row0 + n_pc_pad, :] = jnp.zeros((n_pc_pad - past_pc, CMP_FLAT), BF16)
            flat_ref[kv, row0 + past_pc:row0 + past_pc + 1, 0:GD] = new_row(u, 0, kv).astype(BF16)
    for kv in range(2):
        kcv.append(_compress_mlp(flat_ref[kv], w1ab_ref[kv], bias_ref[kv], w2_ref[kv]).astype(BF16))
    for u in range(nseq):
        _attn_sample_one(u, kcv[0][u * n_pc_pad:(u + 1) * n_pc_pad], kcv[1][u * n_pc_pad:(u + 1) * n_pc_pad],
                         q_ref, new_row, rounded, gates_ref, wincol_ref, cwin_ref, ov_ref, bok_ref,
                         slc_pages[u], o_ref, wout_ref, geo=geo, past=past, consts=(head, lane, lane_h, own))


def _attn_sample_one(u, kc, vc, q_ref, new_row, rounded, gates_ref, wincol_ref, cwin_ref, ov_ref, bok_ref,
                     slc_pages, o_ref, wout_ref, *, geo, past, consts):
    head, lane, lane_h, own = consts
    n_pc_pad, n_blk, n_slc = geo["n_pc_pad"], geo["n_blk"], geo["n_slc"]
    n_pages = len(slc_pages)
    pos = past
    lb = cwin_ref.shape[-1]
    scale = HEAD_DIM ** -0.5
    q_rows = jnp.concatenate([q_ref[u, :, h * HEAD_DIM:(h + 1) * HEAD_DIM] for h in range(N_HEADS)], axis=0)
    qx = (jnp.where(own, jnp.concatenate([q_rows] * N_KV_HEADS, axis=1), 0.0) * scale).astype(BF16)
    qx32 = qx.astype(F32)
    c_idx = lax.broadcasted_iota(jnp.int32, (1, n_pc_pad), 1)
    valid = (c_idx * CMP_STRIDE + (CMP_BLOCK - 1) <= pos) & (c_idx < n_blk)
    p_c = _masked_softmax(_dot_nt(qx, kc), valid)
    o_c = _dot(p_c.astype(BF16), vc)

    cur = pos // SLC_BLOCK
    forced = (lane == 0) | (lane == cur) | (lane == cur - 1)
    jp = lax.broadcasted_iota(jnp.int32, (GD, GD), 0)
    jj = lax.broadcasted_iota(jnp.int32, (GD, GD), 1)
    sels = []
    for g in range(N_KV_HEADS):
        pc_sum = jnp.sum(p_c[g * N_REP:(g + 1) * N_REP], axis=0, keepdims=True)
        pc_hi = pc_sum.astype(BF16)
        pc_lo = (pc_sum - pc_hi.astype(F32)).astype(BF16)
        imp = _dot(pc_hi, ov_ref[...]) + _dot(pc_lo, ov_ref[...])
        score = jnp.where(lane <= cur, jnp.where(forced, BIG, imp), -BIG)
        score = jnp.where(lane < n_slc, score, -2.0 * BIG)
        score_b = jnp.broadcast_to(score, (GD, GD))
        score_a = score_b.T
        beats = (score_a > score_b) | ((score_a == score_b) & (jp < jj))
        rank = jnp.sum(beats.astype(F32), axis=0, keepdims=True)
        sel = ((rank < float(min(N_SEL, n_slc))) & (lane < n_slc)).astype(F32)
        sels.append(jnp.broadcast_to(sel, (N_REP, GD)))
    sel_h = jnp.concatenate(sels, axis=0)

    chosen = _dot(sel_h.astype(BF16), bok_ref[...]) > 0.5
    s_past = jnp.concatenate([_dot(qx, slc_pages[j][0].astype(BF16)) for j in range(n_pages)], axis=-1)
    s_past = jnp.where(chosen, s_past, NEG)
    new_ok = jnp.sum(jnp.where(lane == pos // SLC_BLOCK, sel_h, 0.0), axis=-1, keepdims=True) > 0.5
    s_new = jnp.where(new_ok, jnp.sum(qx32 * rounded(new_row(u, 1, 0)), axis=-1, keepdims=True), NEG)
    m = jnp.maximum(jnp.max(s_past, axis=-1, keepdims=True), s_new)
    p_past = jnp.where(chosen, jnp.exp(s_past - m), 0.0)
    p_new = jnp.where(new_ok, jnp.exp(s_new - m), 0.0)
    denom = jnp.maximum(jnp.sum(p_past, axis=-1, keepdims=True) + p_new, 1e-30)
    o_s = rounded(p_new) * rounded(new_row(u, 1, 1))
    for j in range(n_pages):
        o_s = o_s + _dot_nt(p_past[:, j * PAGE:(j + 1) * PAGE].astype(BF16), slc_pages[j][1].astype(BF16))
    o_s = o_s / denom

    kpos = (past - lb) + lax.broadcasted_iota(jnp.int32, (1, lb), 1)
    w_ok = (kpos <= pos) & (kpos > pos - WINDOW)
    s_w = jnp.where(w_ok, _dot(qx, cwin_ref[u, 0].astype(BF16)), NEG)
    s_wn = jnp.sum(qx32 * rounded(new_row(u, 2, 0)), axis=-1, keepdims=True)
    m = jnp.maximum(jnp.max(s_w, axis=-1, keepdims=True), s_wn)
    p_w = jnp.where(w_ok, jnp.exp(s_w - m), 0.0)
    p_wn = jnp.exp(s_wn - m)
    denom = jnp.maximum(jnp.sum(p_w, axis=-1, keepdims=True) + p_wn, 1e-30)
    o_w = (_dot_nt(p_w.astype(BF16), cwin_ref[u, 1].astype(BF16))
           + rounded(p_wn) * rounded(new_row(u, 2, 1))) / denom

    gates = gates_ref[u]
    gate = lambda br: jnp.sum(jnp.where(lane_h == head + br * N_HEADS, gates, 0.0), axis=-1, keepdims=True)
    o = gate(0) * o_c + gate(1) * o_s + gate(2) * o_w
    for h in range(N_HEADS):
        g = h // N_REP
        o_ref[u, :, h * HEAD_DIM:(h + 1) * HEAD_DIM] = o[h:h + 1, g * HEAD_DIM:(g + 1) * HEAD_DIM]

    for kv in range(2):
        wout_ref[u, kv, :, 0:lb - 1] = cwin_ref[u, kv, :, 1:lb]
        wout_ref[u, kv, :, lb - 1:lb] = wincol_ref[u, kv * GD:(kv + 1) * GD, :]


def _attn_sample(l, q_tok, new_tok, gates, win_col, cwin_t, ccmp_t, cslc_t, page_table, w1ab, bias, w2):
    nb = q_tok.shape[0]
    n_pages = page_table.shape[1]
    past = n_pages * PAGE
    geo = _sample_geometry(past)
    lb = cwin_t.shape[-1]
    assert GATE_PAD == GD and FLAT_SPAN % PAGE == 0 and n_pages % (FLAT_SPAN // PAGE) == 0
    perm = _piece_perm()
    ov = _overlap_t(geo["n_pc_pad"], geo["n_blk"], GD).T
    ov = jnp.where(jnp.arange(GD)[None, :] < geo["n_slc"], ov, 0).astype(BF16)
    bok = jnp.asarray((np.arange(past)[None, :] // SLC_BLOCK == np.arange(GD)[:, None]).astype(np.float32), BF16)
    nseq = SAMPLE_SEQS if nb % SAMPLE_SEQS == 0 else 1
    seq = lambda a: pl.BlockSpec((nseq,) + a.shape[1:], lambda i, pt: (i,) + (0,) * (a.ndim - 1))
    full = lambda a: pl.BlockSpec(a.shape, lambda i, pt: (0,) * a.ndim)
    page_spec = lambda u, j: pl.BlockSpec((None, None, 2, GD, PAGE),
                                          lambda i, pt, u=u, j=j: (l, pt[i * nseq + u, j], 0, 0, 0))
    page_specs = [page_spec(u, j) for u in range(nseq) for j in range(n_pages)]
    grid_spec = pltpu.PrefetchScalarGridSpec(
        num_scalar_prefetch=1,
        grid=(nb // nseq,),
        in_specs=[seq(q_tok), seq(new_tok), seq(gates), seq(win_col),
                  pl.BlockSpec((None, nseq, 2, GD, lb), lambda i, pt: (l, i, 0, 0, 0)),
                  full(ov), full(bok), full(perm), full(w1ab), full(bias), full(w2)] + page_specs * 2,
        out_specs=[pl.BlockSpec((nseq, 1, D_ATTN), lambda i, pt: (i, 0, 0)),
                   pl.BlockSpec((nseq, 2, GD, lb), lambda i, pt: (i, 0, 0, 0))],
        scratch_shapes=[pltpu.VMEM((2, nseq * geo["n_pc_pad"], CMP_FLAT), BF16)],
    )
    return pl.pallas_call(
        functools.partial(_attn_sample_kernel, past=past, n_pages=n_pages, nseq=nseq),
        grid_spec=grid_spec,
        out_shape=[jax.ShapeDtypeStruct((nb, 1, D_ATTN), F32),
                   jax.ShapeDtypeStruct((nb, 2, GD, lb), F32)],
        compiler_params=pltpu.CompilerParams(dimension_semantics=("arbitrary",), vmem_limit_bytes=VMEM_LIMIT),
        name="attn_sample",
    )(page_table, q_tok, new_tok, gates, win_col, cwin_t, ov, bok, perm, w1ab, bias, w2,
      *([ccmp_t] * (nseq * n_pages)), *([cslc_t] * (nseq * n_pages)))


def _prep_layer(l, w_norm, w_in, w_out, w_dw, b_dw, ln_g, ln_b, w_pw, b_pw, w_pool, pool_scale,
                g_q, g_k, cmp_pe, cmp_w1, cmp_w2):
    sizes = (D_CONV, D_CONV, D_CONV, D_POOL, D_POOL, D_ATTN, D_KV, N_BRANCH * N_HEADS, D_ATTN)
    offs = [0] + [int(v) for v in np.cumsum(sizes)]
    col = lambda k: w_in[l][:, offs[k]:offs[k + 1]]
    a_val, a_gate, z_a, b_in, z_b, q, kv, gate, z_c = (col(k) for k in range(9))
    pad = jnp.zeros((D_MODEL, GATE_PAD - N_BRANCH * N_HEADS), F32)
    wtok = jnp.concatenate([a_val, a_gate, b_in, z_a, z_b, z_c, gate, pad], axis=1).astype(BF16)
    wfeat = jnp.concatenate([q, kv], axis=1).T.astype(BF16)
    ones = jnp.ones((N_KV_HEADS * HEAD_DIM,), F32)
    gcol = jnp.concatenate([jnp.tile(g_q[l], N_HEADS)]
                           + [piece for br in range(N_BRANCH)
                              for piece in (jnp.tile(g_k[l, br], N_KV_HEADS), ones)])[:, None]
    wpool_bd = jnp.zeros((D_POOL, D_POOL), F32)
    for gi in range(len(POOL_WINDOWS)):
        wpool_bd = wpool_bd.at[gi * POOL_GROUP:(gi + 1) * POOL_GROUP, gi * POOL_GROUP:(gi + 1) * POOL_GROUP].set(w_pool[l, gi])
    row = lambda v: v[None, :]

    def both_groups(w):
        z = jnp.zeros_like(w)
        return jnp.stack([jnp.concatenate([w, z], axis=2), jnp.concatenate([z, w], axis=2)], axis=1).reshape(-1, GD)

    half = CMP_STRIDE * HEAD_DIM
    w1 = cmp_w1[l].astype(BF16)
    w1ab = jnp.stack([jnp.concatenate(
        [both_groups(w1[kv, :half].reshape(CMP_STRIDE, HEAD_DIM, HEAD_DIM)),
         both_groups(w1[kv, half:].reshape(CMP_STRIDE, HEAD_DIM, HEAD_DIM))], axis=1) for kv in range(2)])
    w2p = jnp.stack([both_groups(cmp_w2[l, kv].astype(BF16)[None]) for kv in range(2)])
    cbias = jnp.tile(_cmp_bias(cmp_pe[l].reshape(2, 1, CMP_BLOCK * HEAD_DIM), w1), (1, 1, N_KV_HEADS))
    return dict(
        wnorm=row(w_norm[l]), wtok=wtok, wfeat=wfeat, gcol=gcol,
        wdw=w_dw[l], bdw=row(b_dw[l]), lng=row(ln_g[l]), lnb=row(ln_b[l]),
        wpw=w_pw[l].astype(BF16), bpw=row(b_pw[l]),
        wpool=wpool_bd.astype(BF16), pscale=row(pool_scale[l]),
        w1ab=w1ab, cbias=cbias, w2p=w2p,
        wout=w_out[l].astype(BF16),
    )


def _rope_tables(pos):
    inv = ROPE_THETA ** (-jnp.arange(ROT_HALF, dtype=F32) * 2.0 / ROT_DIM)
    ang = pos.astype(F32)[:, None] * inv[None, :]
    return jnp.cos(ang).T, jnp.sin(ang).T


def _prompt_layer(x, p, cos_t, sin_t):
    glu, b_in, sz, gates, q_t, cmp_t, slc_t, win_t = _in_proj(
        x, p["wnorm"], p["wtok"], p["wfeat"], p["gcol"], cos_t, sin_t, tm=min(512, x.shape[1]))
    yab = _convpool_prompt(glu, b_in, sz, p["wdw"], p["bdw"], p["lng"], p["lnb"], p["wpw"], p["bpw"],
                           p["wpool"], p["pscale"], t=min(256, x.shape[1]))
    kcv = _compress_prompt(cmp_t, p["w1ab"], p["cbias"], p["w2p"])
    o_attn = _attn_prompt(q_t, kcv, slc_t, win_t, gates)
    y = _out_proj(x, yab, o_attn, sz, p["wout"], tm=min(512, x.shape[1]))
    return y, glu, b_in, cmp_t, slc_t, win_t


def _sample_layer(l, x, p, cos_t, sin_t, sc, sp, cwin_t, ccmp_t, cslc_t, page_table, past):
    nb = x.shape[0]
    glu, b_in, sz, gates, q_t, cmp_t, slc_t, win_t = _in_proj(
        x[None], p["wnorm"], p["wtok"], p["wfeat"], p["gcol"], cos_t, sin_t, tm=nb)
    glu, b_in, sz, gates = glu[0], b_in[0], sz[0], gates[0]
    yab, sc_new, sp_new = _convpool_sample(sc, sp, glu, b_in, sz, p["wdw"], p["bdw"], p["lng"], p["lnb"],
                                           p["wpw"], p["bpw"], p["wpool"], p["pscale"], pos=past)
    new_tok = jnp.concatenate([cmp_t[0], slc_t[0], win_t[0]], axis=0).T[:, None, :]
    q_tok = q_t[0].T[:, None, :]
    win_col = win_t[0].T[:, :, None]
    o_attn, win_next = _attn_sample(l, q_tok, new_tok, gates[:, None, :], win_col, cwin_t, ccmp_t, cslc_t,
                                    page_table, p["w1ab"], p["cbias"], p["w2p"])
    y = _out_proj(x[None], yab[None], o_attn.reshape(1, nb, D_ATTN), sz[None], p["wout"], tm=nb)[0]
    return y, sc_new, sp_new, win_next, cmp_t[0], slc_t[0]


def _rows_from_feat(a):
    lead = a.shape[:-2]
    a = a.reshape(lead + (2, N_KV_HEADS, HEAD_DIM, a.shape[-1]))
    nl = len(lead)
    return jnp.transpose(a, tuple(range(nl)) + (nl + 3, nl, nl + 1, nl + 2))


def kernel(x_prompt, x_sample, state_conv, state_pool, cache_win_kv, cache_cmp_kv, cache_slc_kv, page_table,
           w_norm, w_in, w_out, w_dw, b_dw, ln_g, ln_b, w_pw, b_pw, w_pool, pool_scale,
           g_q, g_k, cmp_pe, cmp_w1, cmp_w2):
    bp, s, _ = x_prompt.shape
    bs, t_new, _ = x_sample.shape
    depth = w_in.shape[0]
    assert t_new == 1 and cache_cmp_kv.shape[2] == PAGE and s % 512 == 0
    past = page_table.shape[1] * PAGE
    lb = cache_win_kv.shape[2]

    def to_feat(c):
        c = jnp.transpose(c, (0, 1, 3, 4, 5, 2))
        return c.reshape(c.shape[:3] + (GD, c.shape[-1]))
    cwin_t, ccmp_t, cslc_t = to_feat(cache_win_kv), to_feat(cache_cmp_kv), to_feat(cache_slc_kv)
    sc_all = jnp.transpose(state_conv, (0, 2, 1, 3))
    sp_all = jnp.transpose(state_pool, (0, 2, 1, 3))
    cos_p, sin_p = _rope_tables(jnp.arange(s))
    cos_s, sin_s = _rope_tables(jnp.full((bs,), past))

    xp, xs = x_prompt, x_sample[:, 0, :]
    outs = [[] for _ in range(10)]
    for l in range(depth):
        p = _prep_layer(l, w_norm, w_in, w_out, w_dw, b_dw, ln_g, ln_b, w_pw, b_pw, w_pool, pool_scale,
                        g_q, g_k, cmp_pe, cmp_w1, cmp_w2)
        xp, glu, b_in, cmp_t, slc_t, win_t = _prompt_layer(xp, p, cos_p, sin_p)
        outs[0].append(glu[:, s - CONV_STATE:, :])
        outs[1].append(b_in[:, s - POOL_STATE:, :])
        outs[2].append(_rows_from_feat(win_t[:, :, s - min(WINDOW, s):]))
        outs[3].append(_rows_from_feat(cmp_t))
        outs[4].append(_rows_from_feat(slc_t))
        xs, sc_new, sp_new, win_next, cmp_new, slc_new = _sample_layer(
            l, xs, p, cos_s, sin_s, sc_all[l], sp_all[l], cwin_t, ccmp_t, cslc_t, page_table, past)
        outs[5].append(jnp.transpose(sc_new, (1, 0, 2)))
        outs[6].append(jnp.transpose(sp_new, (1, 0, 2)))
        outs[7].append(_rows_from_feat(win_next.reshape(bs, D_ROWS, lb)))
        outs[8].append(_rows_from_feat(cmp_new)[:, None])
        outs[9].append(_rows_from_feat(slc_new)[:, None])
    return (xp, xs[:, None, :]) + tuple(jnp.stack(o) for o in outs)
```

```python
import functools

import numpy as np
import jax
import jax.numpy as jnp
from jax import lax
from jax.experimental import pallas as pl
from jax.experimental.pallas import tpu as pltpu

D_MODEL = 1024
D_CONV = 256
D_POOL = 256
N_HEADS = 8
N_KV_HEADS = 2
HEAD_DIM = 64
D_ATTN = N_HEADS * HEAD_DIM
CONV_WIDTH = 31
CONV_STATE = CONV_WIDTH - 1
POOL_WINDOWS = (2, 4, 8, 16)
POOL_GROUP = D_POOL // len(POOL_WINDOWS)
POOL_STATE = max(POOL_WINDOWS) - 1
CMP_BLOCK = 32
CMP_STRIDE = 16
SLC_BLOCK = 64
N_SEL = 8
WINDOW = 256
Q_BLOCK = 128
ROT_DIM = HEAD_DIM // 4
ROT_HALF = ROT_DIM // 2
ROPE_THETA = 500000.0
N_BRANCH = 3
D_KV = N_BRANCH * 2 * N_KV_HEADS * HEAD_DIM
D_ROWS = 2 * N_KV_HEADS * HEAD_DIM
N_REP = N_HEADS // N_KV_HEADS
EPS = 1e-6
NEG = -1e30
BIG = 1e4

GATE_PAD = 128
D_TOK = 3 * D_CONV + 2 * D_POOL + D_ATTN + GATE_PAD
D_FEAT = D_ATTN + D_KV
N_FEAT_BLOCKS = D_FEAT // HEAD_DIM
SEL_KEYS = 512
VMEM_LIMIT = 56 * 1024 * 1024

F32 = jnp.float32
BF16 = jnp.bfloat16
NT_DIMS = (((1,), (1,)), ((), ()))


def _sigmoid(x):
    return 1.0 / (1.0 + jnp.exp(-x))


def _silu(x):
    return x * _sigmoid(x)


def _gelu_tanh(x):
    return 0.5 * x * (1.0 + jnp.tanh(np.sqrt(2.0 / np.pi).astype(np.float32) * (x + 0.044715 * (x * x * x))))


def _dot(a, b):
    return jnp.dot(a, b, preferred_element_type=F32)


def _dot_nt(a, b):
    return lax.dot_general(a, b, NT_DIMS, preferred_element_type=F32)


def _masked_softmax(s, mask):
    s = jnp.where(mask, s, NEG)
    m = jnp.max(s, axis=-1, keepdims=True)
    p = jnp.where(mask, jnp.exp(s - m), 0.0)
    return p / jnp.maximum(jnp.sum(p, axis=-1, keepdims=True), 1e-30)


def _in_proj_kernel(x_ref, wnorm_ref, wtok_ref, wfeat_ref, gcol_ref, cos_ref, sin_ref, *rest):
    glu_ref, bin_ref, sz_ref, gates_ref, qt_ref, cmp_ref, slc_ref, win_ref = rest[-8:]
    x = x_ref[...]
    ms = jnp.mean(x * x, axis=-1, keepdims=True)
    h = (x * lax.rsqrt(ms + EPS) * wnorm_ref[...]).astype(BF16)

    a = _dot(h, wtok_ref[...])
    o = 0
    glu_ref[...] = a[:, o:o + D_CONV] * _sigmoid(a[:, o + D_CONV:o + 2 * D_CONV])
    o += 2 * D_CONV
    bin_ref[...] = a[:, o:o + D_POOL]
    o += D_POOL
    z = a[:, o:o + D_CONV + D_POOL + D_ATTN]
    sz_ref[...] = _silu(z)
    o += D_CONV + D_POOL + D_ATTN
    gates_ref[...] = _sigmoid(a[:, o:o + GATE_PAD])

    f = _dot_nt(wfeat_ref[...], h)
    cos = cos_ref[...]
    sin = sin_ref[...]
    kv_refs = (cmp_ref, slc_ref, win_ref)
    for hb in range(N_FEAT_BLOCKS):
        blk = f[hb * HEAD_DIM:(hb + 1) * HEAD_DIM, :]
        kv_blk = hb - N_HEADS
        is_value = kv_blk >= 0 and (kv_blk % (2 * N_KV_HEADS)) >= N_KV_HEADS
        if not is_value:
            bms = jnp.mean(blk * blk, axis=0, keepdims=True)
            y = blk * lax.rsqrt(bms + EPS) * gcol_ref[hb * HEAD_DIM:(hb + 1) * HEAD_DIM, :]
            x1 = y[0:ROT_HALF]
            x2 = y[ROT_HALF:ROT_DIM]
            blk = jnp.concatenate([x1 * cos - x2 * sin, x2 * cos + x1 * sin, y[ROT_DIM:]], axis=0)
        if kv_blk < 0:
            qt_ref[hb * HEAD_DIM:(hb + 1) * HEAD_DIM, :] = blk
        else:
            r = kv_blk % (2 * N_KV_HEADS)
            kv_refs[kv_blk // (2 * N_KV_HEADS)][r * HEAD_DIM:(r + 1) * HEAD_DIM, :] = blk


def _in_proj(x, wnorm, wtok, wfeat, gcol, cos_t, sin_t, tm, l, depth, kv_prev):
    n, t, _ = x.shape
    grid = (n, t // tm)
    tok = lambda w: pl.BlockSpec((None, tm, w), lambda i, j: (i, j, 0))
    feat = lambda w: pl.BlockSpec((None, w, tm), lambda i, j: (i, 0, j))
    kv_spec = pl.BlockSpec((None, None, D_ROWS, tm), lambda i, j: (l, i, 0, j))
    full = lambda a: pl.BlockSpec(a.shape, lambda i, j: (0,) * a.ndim)
    tok_shape = lambda w: jax.ShapeDtypeStruct((n, t, w), F32)
    kv_shape = jax.ShapeDtypeStruct((depth, n, D_ROWS, t), F32)
    n_in = 7
    kv_prev = () if kv_prev is None else tuple(kv_prev)
    return pl.pallas_call(
        _in_proj_kernel,
        grid=grid,
        in_specs=[tok(D_MODEL), full(wnorm), full(wtok), full(wfeat), full(gcol),
                  pl.BlockSpec((ROT_HALF, tm), lambda i, j: (0, j)),
                  pl.BlockSpec((ROT_HALF, tm), lambda i, j: (0, j))]
                 + [pl.BlockSpec(memory_space=pl.ANY)] * len(kv_prev),
        out_specs=[tok(D_CONV), tok(D_POOL), tok(D_CONV + D_POOL + D_ATTN), tok(GATE_PAD),
                   feat(D_ATTN), kv_spec, kv_spec, kv_spec],
        out_shape=[tok_shape(D_CONV), tok_shape(D_POOL), tok_shape(D_CONV + D_POOL + D_ATTN), tok_shape(GATE_PAD),
                   jax.ShapeDtypeStruct((n, D_ATTN, t), F32), kv_shape, kv_shape, kv_shape],
        input_output_aliases={n_in + k: 5 + k for k in range(len(kv_prev))},
        compiler_params=pltpu.CompilerParams(dimension_semantics=("arbitrary", "arbitrary"),
                                             vmem_limit_bytes=VMEM_LIMIT),
        name="in_proj",
    )(x, wnorm, wtok, wfeat, gcol, cos_t, sin_t, *kv_prev)


def _conv_tail(acc, sz_a, bdw, lng, lnb, wpw, bpw):
    y = acc + bdw
    mu = jnp.mean(y, axis=-1, keepdims=True)
    yc = y - mu
    var = jnp.mean(yc * yc, axis=-1, keepdims=True)
    y = yc * lax.rsqrt(var + EPS) * lng + lnb
    y = _dot(_silu(y).astype(BF16), wpw) + bpw
    return y * sz_a


def _pool_tail(total, cnt, xcur, sz_b, wpool, pscale):
    d = total / cnt - xcur
    y = _dot(d.astype(BF16), wpool) * pscale
    return y * sz_b


def _pool_window_of_lane(shape):
    lane = lax.broadcasted_iota(jnp.int32, shape, len(shape) - 1)
    w = jnp.full(shape, POOL_WINDOWS[0], jnp.int32)
    for gi in range(1, len(POOL_WINDOWS)):
        w = jnp.where(lane >= gi * POOL_GROUP, POOL_WINDOWS[gi], w)
    return w


def _pool_select(sums, shape):
    lane = lax.broadcasted_iota(jnp.int32, shape, len(shape) - 1)
    total = sums[0]
    for gi in range(1, len(POOL_WINDOWS)):
        total = jnp.where(lane >= gi * POOL_GROUP, sums[gi], total)
    return total


CONV_HALO = 32
POOL_HALO = 16


def _convpool_prompt_kernel(glu_ref, gprev_ref, bin_ref, bprev_ref, sz_ref, wdw_ref, bdw_ref, lng_ref, lnb_ref,
                            wpw_ref, bpw_ref, wpool_ref, pscale_ref, y_ref, cbuf, pbuf, *, t):
    i = pl.program_id(1)
    keep = (i > 0).astype(F32)
    cbuf[0:CONV_HALO, :] = gprev_ref[...] * keep
    cbuf[CONV_HALO:CONV_HALO + t, :] = glu_ref[...]
    acc = jnp.zeros((t, D_CONV), F32)
    for k in range(CONV_WIDTH):
        acc = acc + wdw_ref[k:k + 1, :] * cbuf[pl.ds(CONV_HALO - CONV_STATE + k, t), :]
    sz = sz_ref[...]
    y_ref[:, 0:D_CONV] = _conv_tail(acc, sz[:, 0:D_CONV], bdw_ref[...], lng_ref[...], lnb_ref[...],
                                    wpw_ref[...], bpw_ref[...])

    pbuf[0:POOL_HALO, :] = bprev_ref[...] * keep
    xcur = bin_ref[...]
    pbuf[POOL_HALO:POOL_HALO + t, :] = xcur
    sums = []
    run = jnp.zeros((t, D_POOL), F32)
    j = 0
    for w in POOL_WINDOWS:
        while j < w:
            run = run + pbuf[pl.ds(POOL_HALO - j, t), :]
            j += 1
        sums.append(run)
    total = _pool_select(sums, (t, D_POOL))
    pos = i * t + lax.broadcasted_iota(jnp.int32, (t, D_POOL), 0)
    cnt = jnp.minimum(_pool_window_of_lane((t, D_POOL)), pos + 1).astype(F32)
    y_ref[:, D_CONV:D_CONV + D_POOL] = _pool_tail(total, cnt, xcur, sz[:, D_CONV:D_CONV + D_POOL],
                                                  wpool_ref[...], pscale_ref[...])


def _convpool_prompt(glu, b_in, sz, wdw, bdw, lng, lnb, wpw, bpw, wpool, pscale, t):
    n, s, _ = glu.shape
    grid = (n, s // t)
    cur = lambda w: pl.BlockSpec((None, t, w), lambda i, j: (i, j, 0))
    prev = lambda rows, w: pl.BlockSpec((None, rows, w), lambda i, j: (i, jnp.maximum(j * (t // rows) - 1, 0), 0))
    full = lambda a: pl.BlockSpec(a.shape, lambda i, j: (0,) * a.ndim)
    return pl.pallas_call(
        functools.partial(_convpool_prompt_kernel, t=t),
        grid=grid,
        in_specs=[cur(D_CONV), prev(CONV_HALO, D_CONV), cur(D_POOL), prev(POOL_HALO, D_POOL),
                  cur(D_CONV + D_POOL),
                  full(wdw), full(bdw), full(lng), full(lnb), full(wpw), full(bpw), full(wpool), full(pscale)],
        out_specs=cur(D_CONV + D_POOL),
        out_shape=jax.ShapeDtypeStruct((n, s, D_CONV + D_POOL), F32),
        scratch_shapes=[pltpu.VMEM((CONV_HALO + t, D_CONV), F32), pltpu.VMEM((POOL_HALO + t, D_POOL), F32)],
        compiler_params=pltpu.CompilerParams(dimension_semantics=("arbitrary", "arbitrary"),
                                             vmem_limit_bytes=VMEM_LIMIT),
        name="convpool_prompt",
    )(glu, glu, b_in, b_in, sz, wdw, bdw, lng, lnb, wpw, bpw, wpool, pscale)


def _convpool_sample_kernel(sc_ref, sp_ref, glu_ref, bin_ref, sz_ref, wdw_ref, bdw_ref, lng_ref, lnb_ref,
                            wpw_ref, bpw_ref, wpool_ref, pscale_ref, y_ref, sc_out, sp_out, *, pos):
    glu = glu_ref[...]
    acc = wdw_ref[CONV_STATE:CONV_WIDTH, :] * glu
    for k in range(CONV_STATE):
        acc = acc + wdw_ref[k:k + 1, :] * sc_ref[k]
    sz = sz_ref[...]
    y_ref[:, 0:D_CONV] = _conv_tail(acc, sz[:, 0:D_CONV], bdw_ref[...], lng_ref[...], lnb_ref[...],
                                    wpw_ref[...], bpw_ref[...])
    for k in range(CONV_STATE - 1):
        sc_out[k] = sc_ref[k + 1]
    sc_out[CONV_STATE - 1] = glu

    xcur = bin_ref[...]
    shape = xcur.shape
    sums = []
    run = xcur
    j = 1
    for w in POOL_WINDOWS:
        while j < w:
            run = run + sp_ref[POOL_STATE - j]
            j += 1
        sums.append(run)
    total = _pool_select(sums, shape)
    cnt = jnp.minimum(_pool_window_of_lane(shape), pos + 1).astype(F32)
    y_ref[:, D_CONV:D_CONV + D_POOL] = _pool_tail(total, cnt, xcur, sz[:, D_CONV:D_CONV + D_POOL],
                                                  wpool_ref[...], pscale_ref[...])
    for k in range(POOL_STATE - 1):
        sp_out[k] = sp_ref[k + 1]
    sp_out[POOL_STATE - 1] = xcur


def _convpool_sample(sc, sp, glu, b_in, sz, wdw, bdw, lng, lnb, wpw, bpw, wpool, pscale, pos):
    nb = glu.shape[0]
    return pl.pallas_call(
        functools.partial(_convpool_sample_kernel, pos=pos),
        out_shape=[jax.ShapeDtypeStruct((nb, D_CONV + D_POOL), F32),
                   jax.ShapeDtypeStruct(sc.shape, F32), jax.ShapeDtypeStruct(sp.shape, F32)],
        compiler_params=pltpu.CompilerParams(vmem_limit_bytes=VMEM_LIMIT),
        name="convpool_sample",
    )(sc, sp, glu, b_in, sz[:, 0:D_CONV + D_POOL], wdw, bdw, lng, lnb, wpw, bpw, wpool, pscale)


def _out_proj_kernel(x_ref, yab_ref, oat_ref, szc_ref, w_ref, o_ref):
    ya = yab_ref[...].astype(BF16)
    yc = (oat_ref[...] * szc_ref[...]).astype(BF16)
    nab = D_CONV + D_POOL
    o_ref[...] = x_ref[...] + _dot(ya, w_ref[0:nab, :]) + _dot(yc, w_ref[nab:nab + D_ATTN, :])


def _out_proj(x, yab, o_attn, sz, w_out, tm):
    n, t, _ = x.shape
    nab = D_CONV + D_POOL
    assert nab == D_ATTN
    grid = (n, t // tm)
    tok = lambda w, cb=0: pl.BlockSpec((None, tm, w), lambda i, j: (i, j, cb))
    return pl.pallas_call(
        _out_proj_kernel,
        grid=grid,
        in_specs=[tok(D_MODEL), tok(nab), tok(D_ATTN), tok(D_ATTN, 1),
                  pl.BlockSpec(w_out.shape, lambda i, j: (0, 0))],
        out_specs=tok(D_MODEL),
        out_shape=jax.ShapeDtypeStruct(x.shape, F32),
        compiler_params=pltpu.CompilerParams(dimension_semantics=("arbitrary", "arbitrary"),
                                             vmem_limit_bytes=VMEM_LIMIT),
        name="out_proj",
    )(x, yab, o_attn, sz, w_out)


GD = N_KV_HEADS * HEAD_DIM
CMP_FLAT = CMP_STRIDE * GD


FLAT_SPAN = CMP_STRIDE * CMP_STRIDE


def _piece_perm():
    m = np.arange(FLAT_SPAN)
    src = (m % CMP_STRIDE) * CMP_STRIDE + m // CMP_STRIDE
    return jnp.asarray((src[:, None] == np.arange(FLAT_SPAN)[None, :]).astype(np.float32), BF16)


def _fill_flat(x_t, perm, flat_ref, piece0):
    y = _dot_nt(perm, x_t).astype(BF16)
    for kv in range(2):
        for r in range(CMP_STRIDE):
            flat_ref[kv, piece0:piece0 + CMP_STRIDE, r * GD:(r + 1) * GD] = (
                y[r * CMP_STRIDE:(r + 1) * CMP_STRIDE, kv * GD:(kv + 1) * GD])


def _compress_mlp(flat, w1ab, bias, w2):
    n_pc = flat.shape[0]
    hb = _dot(flat, w1ab)
    h = hb[:, 0:GD] + jnp.concatenate([hb[1:n_pc, GD:2 * GD], jnp.zeros((1, GD), F32)], axis=0) + bias
    out = _dot(_gelu_tanh(h).astype(BF16), w2)
    row = lax.broadcasted_iota(jnp.int32, out.shape, 0)
    return jnp.where(row < n_pc - 1, out, 0.0)


def _cmp_bias_kernel(pe_ref, w1_ref, o_ref):
    for kv in range(2):
        o_ref[kv] = _dot(pe_ref[kv].astype(BF16), w1_ref[kv])


def _cmp_bias(pe_flat, w1):
    return pl.pallas_call(_cmp_bias_kernel, out_shape=jax.ShapeDtypeStruct((2, 1, HEAD_DIM), F32),
                          name="cmp_bias")(pe_flat, w1)


def _compress_prompt_kernel(x_ref, perm_ref, w1ab_ref, bias_ref, w2_ref, o_ref, flat_ref, *, s):
    for t in range(s // FLAT_SPAN):
        _fill_flat(x_ref[:, t * FLAT_SPAN:(t + 1) * FLAT_SPAN].astype(BF16), perm_ref[...], flat_ref, t * CMP_STRIDE)
    for kv in range(2):
        o_ref[:, kv * GD:(kv + 1) * GD] = _compress_mlp(flat_ref[kv], w1ab_ref[kv], bias_ref[kv], w2_ref[kv])


def _compress_prompt(cmp_t, l, w1ab, bias, w2):
    _, n, _, s = cmp_t.shape
    assert s % FLAT_SPAN == 0
    n_pc = s // CMP_STRIDE
    perm = _piece_perm()
    full = lambda a: pl.BlockSpec(a.shape, lambda i: (0,) * a.ndim)
    return pl.pallas_call(
        functools.partial(_compress_prompt_kernel, s=s),
        grid=(n,),
        in_specs=[pl.BlockSpec((None, None, D_ROWS, s), lambda i: (l, i, 0, 0)),
                  full(perm), full(w1ab), full(bias), full(w2)],
        out_specs=pl.BlockSpec((None, n_pc, 2 * GD), lambda i: (i, 0, 0)),
        out_shape=jax.ShapeDtypeStruct((n, n_pc, 2 * GD), F32),
        scratch_shapes=[pltpu.VMEM((2, n_pc, CMP_FLAT), BF16)],
        compiler_params=pltpu.CompilerParams(dimension_semantics=("arbitrary",), vmem_limit_bytes=VMEM_LIMIT),
        name="compress_prompt",
    )(cmp_t, perm, w1ab, bias, w2)


def _select_blocks(imp_t, pos_row, n_blocks):
    shape = imp_t.shape
    assert n_blocks % 8 == 0
    j = lax.broadcasted_iota(jnp.int32, shape, 0)
    cur = pos_row // SLC_BLOCK
    forced = (j == 0) | (j == cur) | (j == cur - 1)
    score = jnp.where(j <= cur, jnp.where(forced, BIG, imp_t), -BIG)
    tiles = [score[8 * v:8 * v + 8] for v in range(n_blocks // 8)]
    j8 = lax.broadcasted_iota(jnp.int32, (8, shape[1]), 0)
    ranks = [jnp.zeros((8, shape[1]), F32) for _ in tiles]
    for jp in range(n_blocks):
        sj = score[jp:jp + 1, :]
        for v, tile in enumerate(tiles):
            if jp < 8 * v:
                beats = sj >= tile
            elif jp >= 8 * v + 8:
                beats = sj > tile
            else:
                beats = (sj > tile) | ((sj == tile) & (j8 > jp - 8 * v))
            ranks[v] = ranks[v] + jnp.where(beats, 1.0, 0.0)
    rank = jnp.concatenate(ranks, axis=0)
    return (rank < float(min(N_SEL, n_blocks))).astype(F32)


def _col_softmax(s):
    m = jnp.max(s, axis=0, keepdims=True)
    e = jnp.exp(s - m)
    inv = jnp.where(m > 0.5 * NEG, 1.0 / jnp.maximum(jnp.sum(e, axis=0, keepdims=True), 1e-30), 0.0)
    return e, inv


ONES_ROWS = 16


def _with_ones(v_t):
    return jnp.concatenate([v_t, jnp.ones((ONES_ROWS, v_t.shape[1]), BF16)], axis=0)


def _normalised(acc):
    return acc[0:HEAD_DIM] * (1.0 / jnp.maximum(acc[HEAD_DIM:HEAD_DIM + 1], 1e-30))


def _mask_heads(sc, ok, tq):
    return jnp.concatenate([jnp.where(ok, sc[:, r * tq:(r + 1) * tq], NEG) for r in range(N_REP)], axis=1)


def _attn_prompt_kernel(qt_ref, kcv_ref, slc_ref, win_ref, gates_ref, ovt_ref, o_ref,
                        ks_ref, kw_ref, s_ref, sel_ref, acc_ref, *, s, n_blk):
    tq = Q_BLOCK
    cols = N_REP * tq
    groups = range(N_KV_HEADS)
    qc = pl.program_id(1)
    q0 = qc * tq
    pos_row = q0 + lax.broadcasted_iota(jnp.int32, (1, tq), 1)
    n_slc = s // SLC_BLOCK
    n_cmp = kcv_ref.shape[0]
    scale = HEAD_DIM ** -0.5
    blocks_per_step = SEL_KEYS // SLC_BLOCK
    v_row0 = lambda g: (N_KV_HEADS + g) * HEAD_DIM

    @pl.when(qc == 0)
    def _():
        for g in range(N_KV_HEADS):
            for t in range(s // 128):
                cols_t = slice(t * 128, (t + 1) * 128)
                ks_ref[g, cols_t, :] = slc_ref[g * HEAD_DIM:(g + 1) * HEAD_DIM, cols_t].T.astype(BF16)
                kw_ref[g, cols_t, :] = win_ref[g * HEAD_DIM:(g + 1) * HEAD_DIM, cols_t].T.astype(BF16)

    gates_t = gates_ref[...].T
    vc_t = kcv_ref[:, GD:2 * GD].T.astype(BF16)
    q_ts = []
    for g in groups:
        q_t = jnp.concatenate([qt_ref[(g * N_REP + r) * HEAD_DIM:(g * N_REP + r + 1) * HEAD_DIM, :]
                               for r in range(N_REP)], axis=1)
        q_ts.append((q_t * scale).astype(BF16))

    c_idx = lax.broadcasted_iota(jnp.int32, (n_cmp, 1), 0)
    valid_c = (c_idx * CMP_STRIDE + (CMP_BLOCK - 1) <= pos_row) & (c_idx < n_blk)
    ovt = ovt_ref[...]
    o_cs = []
    for g in groups:
        kc = kcv_ref[:, g * HEAD_DIM:(g + 1) * HEAD_DIM].astype(BF16)
        e_c, inv_c = _col_softmax(_mask_heads(_dot(kc, q_ts[g]), valid_c, tq))
        o_cs.append(_dot(vc_t[g * HEAD_DIM:(g + 1) * HEAD_DIM, :], e_c.astype(BF16)) * inv_c)
        pc_sum = e_c[:, 0:tq] * inv_c[:, 0:tq]
        for r in range(1, N_REP):
            pc_sum = pc_sum + e_c[:, r * tq:(r + 1) * tq] * inv_c[:, r * tq:(r + 1) * tq]
        pc_hi = pc_sum.astype(BF16)
        pc_lo = (pc_sum - pc_hi.astype(F32)).astype(BF16)
        imp_t = _dot(ovt, pc_hi) + _dot(ovt, pc_lo)
        sel_ref[g] = _select_blocks(imp_t, pos_row, n_slc)

    def score_step(i, m_run):
        k0 = pl.multiple_of(i * SEL_KEYS, SEL_KEYS)
        key = k0 + lax.broadcasted_iota(jnp.int32, (SEL_KEYS, 1), 0)
        causal = key <= pos_row
        m_new = []
        for g in groups:
            sc = _dot(ks_ref[g, pl.ds(k0, SEL_KEYS), :], q_ts[g])
            parts = []
            for jj in range(blocks_per_step):
                rows = slice(jj * SLC_BLOCK, (jj + 1) * SLC_BLOCK)
                chosen = sel_ref[g, pl.ds(i * blocks_per_step + jj, 1), :] > 0.5
                parts.append(_mask_heads(sc[rows], chosen & causal[rows], tq))
            sc = jnp.concatenate(parts, axis=0)
            s_ref[g, pl.ds(k0, SEL_KEYS), :] = sc.astype(BF16)
            m_new.append(jnp.maximum(m_run[g], jnp.max(sc, axis=0, keepdims=True)))
        return tuple(m_new)

    n_steps = (q0 + tq + SEL_KEYS - 1) // SEL_KEYS
    m_sel = lax.fori_loop(0, n_steps, score_step, tuple(jnp.full((1, cols), NEG, F32) for _ in groups))
    m_sel = [m.astype(BF16) for m in m_sel]

    acc_ref[...] = jnp.zeros(acc_ref.shape, F32)

    def value_step(i, carry):
        k0 = pl.multiple_of(i * SEL_KEYS, SEL_KEYS)
        for g in groups:
            p = jnp.exp(s_ref[g, pl.ds(k0, SEL_KEYS), :] - m_sel[g])
            v_t = slc_ref[v_row0(g):v_row0(g) + HEAD_DIM, pl.ds(k0, SEL_KEYS)].astype(BF16)
            acc_ref[g] += _dot(_with_ones(v_t), p)
        return carry

    lax.fori_loop(0, n_steps, value_step, 0)

    n_prev = -(-(WINDOW - 1) // tq)
    k0s, oks = [], []
    for b in range(n_prev + 1):
        kb = qc - n_prev + b
        k0s.append(pl.multiple_of(jnp.maximum(kb, 0) * tq, tq))
        kp = kb * tq + lax.broadcasted_iota(jnp.int32, (tq, 1), 0)
        oks.append((kp <= pos_row) & (kp > pos_row - WINDOW) & (kp >= 0))

    out_blocks = []
    for g in groups:
        o_s = _normalised(acc_ref[g])
        s_w = jnp.concatenate([_mask_heads(_dot(kw_ref[g, pl.ds(k0s[b], tq), :], q_ts[g]), oks[b], tq)
                               for b in range(n_prev + 1)], axis=0)
        e_w = jnp.exp(s_w.astype(BF16) - jnp.max(s_w, axis=0, keepdims=True).astype(BF16))
        o_w = _dot(_with_ones(win_ref[v_row0(g):v_row0(g) + HEAD_DIM, pl.ds(k0s[0], tq)].astype(BF16)), e_w[0:tq])
        for b in range(1, n_prev + 1):
            o_w = o_w + _dot(_with_ones(win_ref[v_row0(g):v_row0(g) + HEAD_DIM, pl.ds(k0s[b], tq)].astype(BF16)),
                             e_w[b * tq:(b + 1) * tq])
        o_w = _normalised(o_w)
        for r in range(N_REP):
            h = g * N_REP + r
            c = slice(r * tq, (r + 1) * tq)
            out_blocks.append(gates_t[h:h + 1] * o_cs[g][:, c] + gates_t[N_HEADS + h:N_HEADS + h + 1] * o_s[:, c]
                              + gates_t[2 * N_HEADS + h:2 * N_HEADS + h + 1] * o_w[:, c])
    o_ref[...] = jnp.concatenate(out_blocks, axis=0).T


def _overlap_t(n_cmp_rows, n_blk, n_slc):
    c = np.arange(n_cmp_rows)
    start = c * CMP_STRIDE
    end = start + CMP_BLOCK - 1
    j0 = np.arange(n_slc) * SLC_BLOCK
    ov = (end[None, :] >= j0[:, None]) & (start[None, :] < j0[:, None] + SLC_BLOCK) & (c[None, :] < n_blk)
    return jnp.asarray(ov.astype(np.float32), BF16)


def _attn_prompt(q_t, kcv, slc_t, win_t, gates, l):
    n, _, s = q_t.shape
    n_cmp = kcv.shape[1]
    n_blk = s // CMP_STRIDE - CMP_BLOCK // CMP_STRIDE + 1
    n_slc = s // SLC_BLOCK
    ovt = _overlap_t(n_cmp, n_blk, n_slc)
    tq = Q_BLOCK
    cols = N_REP * tq
    assert s % SEL_KEYS == 0 and SEL_KEYS % SLC_BLOCK == 0
    return pl.pallas_call(
        functools.partial(_attn_prompt_kernel, s=s, n_blk=n_blk),
        grid=(n, s // tq),
        in_specs=[pl.BlockSpec((None, D_ATTN, tq), lambda i, j: (i, 0, j)),
                  pl.BlockSpec((None,) + kcv.shape[1:], lambda i, j: (i, 0, 0)),
                  pl.BlockSpec((None, None, D_ROWS, s), lambda i, j: (l, i, 0, 0)),
                  pl.BlockSpec((None, None, D_ROWS, s), lambda i, j: (l, i, 0, 0)),
                  pl.BlockSpec((None, tq, GATE_PAD), lambda i, j: (i, j, 0)),
                  pl.BlockSpec(ovt.shape, lambda i, j: (0, 0))],
        out_specs=pl.BlockSpec((None, tq, D_ATTN), lambda i, j: (i, j, 0)),
        out_shape=jax.ShapeDtypeStruct((n, s, D_ATTN), F32),
        scratch_shapes=[pltpu.VMEM((N_KV_HEADS, s, HEAD_DIM), BF16),
                        pltpu.VMEM((N_KV_HEADS, s, HEAD_DIM), BF16),
                        pltpu.VMEM((N_KV_HEADS, s, cols), BF16),
                        pltpu.VMEM((N_KV_HEADS, n_slc, tq), F32),
                        pltpu.VMEM((N_KV_HEADS, HEAD_DIM + ONES_ROWS, cols), F32)],
        compiler_params=pltpu.CompilerParams(dimension_semantics=("arbitrary", "arbitrary"),
                                             vmem_limit_bytes=VMEM_LIMIT),
        name="attn_prompt",
    )(q_t, kcv, slc_t, win_t, gates, ovt)


PAGE = 128
SAMPLE_SEQS = 4


def _sample_geometry(past):
    length = past + 1
    padded = -(-length // SLC_BLOCK) * SLC_BLOCK
    n_pc = padded // CMP_STRIDE
    n_pc_pad = -(-n_pc // CMP_STRIDE) * CMP_STRIDE
    return dict(n_pc=n_pc, n_pc_pad=n_pc_pad, n_blk=n_pc - CMP_BLOCK // CMP_STRIDE + 1, n_slc=padded // SLC_BLOCK)


def _attn_sample_kernel(pt_ref, q_ref, new_ref, gates_ref, wincol_ref, cwin_ref, ov_ref, bok_ref,
                        perm_ref, w1ab_ref, bias_ref, w2_ref, *rest, past, n_pages, nseq):
    del pt_ref
    cmp_pages = [rest[u * n_pages:(u + 1) * n_pages] for u in range(nseq)]
    slc_pages = [rest[(nseq + u) * n_pages:(nseq + u + 1) * n_pages] for u in range(nseq)]
    o_ref, wout_ref, flat_ref = rest[-3:]
    geo = _sample_geometry(past)
    n_pc_pad, n_blk, n_slc = geo["n_pc_pad"], geo["n_blk"], geo["n_slc"]
    pos = past
    lb = cwin_ref.shape[-1]
    scale = HEAD_DIM ** -0.5
    head = lax.broadcasted_iota(jnp.int32, (N_HEADS, 1), 0)
    lane = lax.broadcasted_iota(jnp.int32, (1, GD), 1)
    lane_h = lax.broadcasted_iota(jnp.int32, (N_HEADS, GD), 1)
    own = (lane_h // HEAD_DIM) == (head // N_REP)

    def new_row(u, branch, kv):
        o = (branch * 2 + kv) * GD
        return new_ref[u, :, o:o + GD]

    def rounded(x):
        return x.astype(BF16).astype(F32)

    kcv = []
    pages_per_span = FLAT_SPAN // PAGE
    past_pc = past // CMP_STRIDE
    for u in range(nseq):
        row0 = u * n_pc_pad
        for t in range(n_pages // pages_per_span):
            x_t = jnp.concatenate([cmp_pages[u][t * pages_per_span + w][...].reshape(2 * GD, PAGE)
                                   for w in range(pages_per_span)], axis=1)
            _fill_flat(x_t.astype(BF16), perm_ref[...], flat_ref, row0 + t * CMP_STRIDE)
        for kv in range(2):
            flat_ref[kv, row0 + past_pc:row0 + n_pc_pad, :] = jnp.zeros((n_pc_pad - past_pc, CMP_FLAT), BF16)
            flat_ref[kv, row0 + past_pc:row0 + past_pc + 1, 0:GD] = new_row(u, 0, kv).astype(BF16)
    for kv in range(2):
        kcv.append(_compress_mlp(flat_ref[kv], w1ab_ref[kv], bias_ref[kv], w2_ref[kv]).astype(BF16))
    for u in range(nseq):
        _attn_sample_one(u, kcv[0][u * n_pc_pad:(u + 1) * n_pc_pad], kcv[1][u * n_pc_pad:(u + 1) * n_pc_pad],
                         q_ref, new_row, rounded, gates_ref, wincol_ref, cwin_ref, ov_ref, bok_ref,
                         slc_pages[u], o_ref, wout_ref, geo=geo, past=past, consts=(head, lane, lane_h, own))


def _attn_sample_one(u, kc, vc, q_ref, new_row, rounded, gates_ref, wincol_ref, cwin_ref, ov_ref, bok_ref,
                     slc_pages, o_ref, wout_ref, *, geo, past, consts):
    head, lane, lane_h, own = consts
    n_pc_pad, n_blk, n_slc = geo["n_pc_pad"], geo["n_blk"], geo["n_slc"]
    n_pages = len(slc_pages)
    pos = past
    lb = cwin_ref.shape[-1]
    scale = HEAD_DIM ** -0.5
    q_rows = jnp.concatenate([q_ref[u, :, h * HEAD_DIM:(h + 1) * HEAD_DIM] for h in range(N_HEADS)], axis=0)
    qx = (jnp.where(own, jnp.concatenate([q_rows] * N_KV_HEADS, axis=1), 0.0) * scale).astype(BF16)
    qx32 = qx.astype(F32)
    c_idx = lax.broadcasted_iota(jnp.int32, (1, n_pc_pad), 1)
    valid = (c_idx * CMP_STRIDE + (CMP_BLOCK - 1) <= pos) & (c_idx < n_blk)
    p_c = _masked_softmax(_dot_nt(qx, kc), valid)
    o_c = _dot(p_c.astype(BF16), vc)

    cur = pos // SLC_BLOCK
    forced = (lane == 0) | (lane == cur) | (lane == cur - 1)
    jp = lax.broadcasted_iota(jnp.int32, (GD, GD), 0)
    jj = lax.broadcasted_iota(jnp.int32, (GD, GD), 1)
    sels = []
    for g in range(N_KV_HEADS):
        pc_sum = jnp.sum(p_c[g * N_REP:(g + 1) * N_REP], axis=0, keepdims=True)
        pc_hi = pc_sum.astype(BF16)
        pc_lo = (pc_sum - pc_hi.astype(F32)).astype(BF16)
        imp = _dot(pc_hi, ov_ref[...]) + _dot(pc_lo, ov_ref[...])
        score = jnp.where(lane <= cur, jnp.where(forced, BIG, imp), -BIG)
        score = jnp.where(lane < n_slc, score, -2.0 * BIG)
        score_b = jnp.broadcast_to(score, (GD, GD))
        score_a = score_b.T
        beats = (score_a > score_b) | ((score_a == score_b) & (jp < jj))
        rank = jnp.sum(beats.astype(F32), axis=0, keepdims=True)
        sel = ((rank < float(min(N_SEL, n_slc))) & (lane < n_slc)).astype(F32)
        sels.append(jnp.broadcast_to(sel, (N_REP, GD)))
    sel_h = jnp.concatenate(sels, axis=0)

    chosen = _dot(sel_h.astype(BF16), bok_ref[...]) > 0.5
    s_past = jnp.concatenate([_dot(qx, slc_pages[j][0].astype(BF16)) for j in range(n_pages)], axis=-1)
    s_past = jnp.where(chosen, s_past, NEG)
    new_ok = jnp.sum(jnp.where(lane == pos // SLC_BLOCK, sel_h, 0.0), axis=-1, keepdims=True) > 0.5
    s_new = jnp.where(new_ok, jnp.sum(qx32 * rounded(new_row(u, 1, 0)), axis=-1, keepdims=True), NEG)
    m = jnp.maximum(jnp.max(s_past, axis=-1, keepdims=True), s_new)
    p_past = jnp.where(chosen, jnp.exp(s_past - m), 0.0)
    p_new = jnp.where(new_ok, jnp.exp(s_new - m), 0.0)
    denom = jnp.maximum(jnp.sum(p_past, axis=-1, keepdims=True) + p_new, 1e-30)
    o_s = rounded(p_new) * rounded(new_row(u, 1, 1))
    for j in range(n_pages):
        o_s = o_s + _dot_nt(p_past[:, j * PAGE:(j + 1) * PAGE].astype(BF16), slc_pages[j][1].astype(BF16))
    o_s = o_s / denom

    kpos = (past - lb) + lax.broadcasted_iota(jnp.int32, (1, lb), 1)
    w_ok = (kpos <= pos) & (kpos > pos - WINDOW)
    s_w = jnp.where(w_ok, _dot(qx, cwin_ref[u, 0].astype(BF16)), NEG)
    s_wn = jnp.sum(qx32 * rounded(new_row(u, 2, 0)), axis=-1, keepdims=True)
    m = jnp.maximum(jnp.max(s_w, axis=-1, keepdims=True), s_wn)
    p_w = jnp.where(w_ok, jnp.exp(s_w - m), 0.0)
    p_wn = jnp.exp(s_wn - m)
    denom = jnp.maximum(jnp.sum(p_w, axis=-1, keepdims=True) + p_wn, 1e-30)
    o_w = (_dot_nt(p_w.astype(BF16), cwin_ref[u, 1].astype(BF16))
           + rounded(p_wn) * rounded(new_row(u, 2, 1))) / denom

    gates = gates_ref[u]
    gate = lambda br: jnp.sum(jnp.where(lane_h == head + br * N_HEADS, gates, 0.0), axis=-1, keepdims=True)
    o = gate(0) * o_c + gate(1) * o_s + gate(2) * o_w
    for h in range(N_HEADS):
        g = h // N_REP
        o_ref[u, :, h * HEAD_DIM:(h + 1) * HEAD_DIM] = o[h:h + 1, g * HEAD_DIM:(g + 1) * HEAD_DIM]

    for kv in range(2):
        wout_ref[u, kv, :, 0:lb - 1] = cwin_ref[u, kv, :, 1:lb]
        wout_ref[u, kv, :, lb - 1:lb] = wincol_ref[u, kv * GD:(kv + 1) * GD, :]


def _attn_sample(l, q_tok, new_tok, gates, win_col, cwin_t, ccmp_t, cslc_t, page_table, w1ab, bias, w2, win_prev):
    nb = q_tok.shape[0]
    n_pages = page_table.shape[1]
    past = n_pages * PAGE
    geo = _sample_geometry(past)
    lb = cwin_t.shape[-1]
    assert GATE_PAD == GD and FLAT_SPAN % PAGE == 0 and n_pages % (FLAT_SPAN // PAGE) == 0
    perm = _piece_perm()
    ov = _overlap_t(geo["n_pc_pad"], geo["n_blk"], GD).T
    ov = jnp.where(jnp.arange(GD)[None, :] < geo["n_slc"], ov, 0).astype(BF16)
    bok = jnp.asarray((np.arange(past)[None, :] // SLC_BLOCK == np.arange(GD)[:, None]).astype(np.float32), BF16)
    nseq = SAMPLE_SEQS if nb % SAMPLE_SEQS == 0 else 1
    seq = lambda a: pl.BlockSpec((nseq,) + a.shape[1:], lambda i, pt: (i,) + (0,) * (a.ndim - 1))
    full = lambda a: pl.BlockSpec(a.shape, lambda i, pt: (0,) * a.ndim)
    page_spec = lambda u, j: pl.BlockSpec((None, None, 2, GD, PAGE),
                                          lambda i, pt, u=u, j=j: (l, pt[i * nseq + u, j], 0, 0, 0))
    page_specs = [page_spec(u, j) for u in range(nseq) for j in range(n_pages)]
    operands = [page_table, q_tok, new_tok, gates, win_col, cwin_t, ov, bok, perm, w1ab, bias, w2,
                *([ccmp_t] * (nseq * n_pages)), *([cslc_t] * (nseq * n_pages))]
    win_prev = () if win_prev is None else (win_prev,)
    grid_spec = pltpu.PrefetchScalarGridSpec(
        num_scalar_prefetch=1,
        grid=(nb // nseq,),
        in_specs=[seq(q_tok), seq(new_tok), seq(gates), seq(win_col),
                  pl.BlockSpec((None, nseq, 2, GD, lb), lambda i, pt: (l, i, 0, 0, 0)),
                  full(ov), full(bok), full(perm), full(w1ab), full(bias), full(w2)] + page_specs * 2
                 + [pl.BlockSpec(memory_space=pl.ANY)] * len(win_prev),
        out_specs=[pl.BlockSpec((nseq, 1, D_ATTN), lambda i, pt: (i, 0, 0)),
                   pl.BlockSpec((None, nseq, 2, GD, lb), lambda i, pt: (l, i, 0, 0, 0))],
        scratch_shapes=[pltpu.VMEM((2, nseq * geo["n_pc_pad"], CMP_FLAT), BF16)],
    )
    return pl.pallas_call(
        functools.partial(_attn_sample_kernel, past=past, n_pages=n_pages, nseq=nseq),
        grid_spec=grid_spec,
        out_shape=[jax.ShapeDtypeStruct((nb, 1, D_ATTN), F32),
                   jax.ShapeDtypeStruct((cwin_t.shape[0], nb, 2, GD, lb), F32)],
        input_output_aliases={len(operands): 1} if win_prev else {},
        compiler_params=pltpu.CompilerParams(dimension_semantics=("arbitrary",), vmem_limit_bytes=VMEM_LIMIT),
        name="attn_sample",
    )(*operands, *win_prev)


def _prep_layer(l, w_norm, w_in, w_out, w_dw, b_dw, ln_g, ln_b, w_pw, b_pw, w_pool, pool_scale,
                g_q, g_k, cmp_pe, cmp_w1, cmp_w2):
    sizes = (D_CONV, D_CONV, D_CONV, D_POOL, D_POOL, D_ATTN, D_KV, N_BRANCH * N_HEADS, D_ATTN)
    offs = [0] + [int(v) for v in np.cumsum(sizes)]
    col = lambda k: w_in[l][:, offs[k]:offs[k + 1]]
    a_val, a_gate, z_a, b_in, z_b, q, kv, gate, z_c = (col(k) for k in range(9))
    pad = jnp.zeros((D_MODEL, GATE_PAD - N_BRANCH * N_HEADS), F32)
    wtok = jnp.concatenate([a_val, a_gate, b_in, z_a, z_b, z_c, gate, pad], axis=1).astype(BF16)
    wfeat = jnp.concatenate([q, kv], axis=1).T.astype(BF16)
    ones = jnp.ones((N_KV_HEADS * HEAD_DIM,), F32)
    gcol = jnp.concatenate([jnp.tile(g_q[l], N_HEADS)]
                           + [piece for br in range(N_BRANCH)
                              for piece in (jnp.tile(g_k[l, br], N_KV_HEADS), ones)])[:, None]
    wpool_bd = jnp.zeros((D_POOL, D_POOL), F32)
    for gi in range(len(POOL_WINDOWS)):
        wpool_bd = wpool_bd.at[gi * POOL_GROUP:(gi + 1) * POOL_GROUP, gi * POOL_GROUP:(gi + 1) * POOL_GROUP].set(w_pool[l, gi])
    row = lambda v: v[None, :]

    def both_groups(w):
        z = jnp.zeros_like(w)
        return jnp.stack([jnp.concatenate([w, z], axis=2), jnp.concatenate([z, w], axis=2)], axis=1).reshape(-1, GD)

    half = CMP_STRIDE * HEAD_DIM
    w1 = cmp_w1[l].astype(BF16)
    w1ab = jnp.stack([jnp.concatenate(
        [both_groups(w1[kv, :half].reshape(CMP_STRIDE, HEAD_DIM, HEAD_DIM)),
         both_groups(w1[kv, half:].reshape(CMP_STRIDE, HEAD_DIM, HEAD_DIM))], axis=1) for kv in range(2)])
    w2p = jnp.stack([both_groups(cmp_w2[l, kv].astype(BF16)[None]) for kv in range(2)])
    cbias = jnp.tile(_cmp_bias(cmp_pe[l].reshape(2, 1, CMP_BLOCK * HEAD_DIM), w1), (1, 1, N_KV_HEADS))
    return dict(
        wnorm=row(w_norm[l]), wtok=wtok, wfeat=wfeat, gcol=gcol,
        wdw=w_dw[l], bdw=row(b_dw[l]), lng=row(ln_g[l]), lnb=row(ln_b[l]),
        wpw=w_pw[l].astype(BF16), bpw=row(b_pw[l]),
        wpool=wpool_bd.astype(BF16), pscale=row(pool_scale[l]),
        w1ab=w1ab, cbias=cbias, w2p=w2p,
        wout=w_out[l].astype(BF16),
    )


def _rope_tables(pos):
    inv = ROPE_THETA ** (-jnp.arange(ROT_HALF, dtype=F32) * 2.0 / ROT_DIM)
    ang = pos.astype(F32)[:, None] * inv[None, :]
    return jnp.cos(ang).T, jnp.sin(ang).T


def _prompt_layer(l, depth, x, p, cos_t, sin_t, kv_prev):
    glu, b_in, sz, gates, q_t, *kv = _in_proj(
        x, p["wnorm"], p["wtok"], p["wfeat"], p["gcol"], cos_t, sin_t, min(512, x.shape[1]), l, depth, kv_prev)
    cmp_t, slc_t, win_t = kv
    yab = _convpool_prompt(glu, b_in, sz, p["wdw"], p["bdw"], p["lng"], p["lnb"], p["wpw"], p["bpw"],
                           p["wpool"], p["pscale"], t=min(256, x.shape[1]))
    kcv = _compress_prompt(cmp_t, l, p["w1ab"], p["cbias"], p["w2p"])
    o_attn = _attn_prompt(q_t, kcv, slc_t, win_t, gates, l)
    y = _out_proj(x, yab, o_attn, sz, p["wout"], tm=min(512, x.shape[1]))
    return y, glu, b_in, kv


def _sample_layer(l, depth, x, p, cos_t, sin_t, sc, sp, cwin_t, ccmp_t, cslc_t, page_table, past, kv_prev, win_prev):
    nb = x.shape[0]
    glu, b_in, sz, gates, q_t, *kv = _in_proj(
        x[None], p["wnorm"], p["wtok"], p["wfeat"], p["gcol"], cos_t, sin_t, nb, l, depth, kv_prev)
    glu, b_in, sz, gates = glu[0], b_in[0], sz[0], gates[0]
    yab, sc_new, sp_new = _convpool_sample(sc, sp, glu, b_in, sz, p["wdw"], p["bdw"], p["lng"], p["lnb"],
                                           p["wpw"], p["bpw"], p["wpool"], p["pscale"], pos=past)
    new_t = jnp.concatenate([a[l, 0] for a in kv], axis=0)
    new_tok = new_t.T[:, None, :]
    q_tok = q_t[0].T[:, None, :]
    win_col = kv[2][l, 0].T[:, :, None]
    o_attn, win_next = _attn_sample(l, q_tok, new_tok, gates[:, None, :], win_col, cwin_t, ccmp_t, cslc_t,
                                    page_table, p["w1ab"], p["cbias"], p["w2p"], win_prev)
    y = _out_proj(x[None], yab[None], o_attn.reshape(1, nb, D_ATTN), sz[None], p["wout"], tm=nb)[0]
    return y, sc_new, sp_new, win_next, kv


def _rows_from_feat(a):
    lead = a.shape[:-2]
    a = a.reshape(lead + (2, N_KV_HEADS, HEAD_DIM, a.shape[-1]))
    nl = len(lead)
    return jnp.transpose(a, tuple(range(nl)) + (nl + 3, nl, nl + 1, nl + 2))


def kernel(x_prompt, x_sample, state_conv, state_pool, cache_win_kv, cache_cmp_kv, cache_slc_kv, page_table,
           w_norm, w_in, w_out, w_dw, b_dw, ln_g, ln_b, w_pw, b_pw, w_pool, pool_scale,
           g_q, g_k, cmp_pe, cmp_w1, cmp_w2):
    bp, s, _ = x_prompt.shape
    bs, t_new, _ = x_sample.shape
    depth = w_in.shape[0]
    assert t_new == 1 and cache_cmp_kv.shape[2] == PAGE and s % 512 == 0
    past = page_table.shape[1] * PAGE
    lb = cache_win_kv.shape[2]

    def to_feat(c):
        c = jnp.transpose(c, (0, 1, 3, 4, 5, 2))
        return c.reshape(c.shape[:3] + (GD, c.shape[-1]))
    cwin_t, ccmp_t, cslc_t = to_feat(cache_win_kv), to_feat(cache_cmp_kv), to_feat(cache_slc_kv)
    sc_all = jnp.transpose(state_conv, (0, 2, 1, 3))
    sp_all = jnp.transpose(state_pool, (0, 2, 1, 3))
    cos_p, sin_p = _rope_tables(jnp.arange(s))
    cos_s, sin_s = _rope_tables(jnp.full((bs,), past))

    xp, xs = x_prompt, x_sample[:, 0, :]
    conv_p, pool_p, conv_s, pool_s = [], [], [], []
    kv_p = kv_s = win_s = None
    for l in range(depth):
        p = _prep_layer(l, w_norm, w_in, w_out, w_dw, b_dw, ln_g, ln_b, w_pw, b_pw, w_pool, pool_scale,
                        g_q, g_k, cmp_pe, cmp_w1, cmp_w2)
        xp, glu, b_in, kv_p = _prompt_layer(l, depth, xp, p, cos_p, sin_p, kv_p)
        conv_p.append(glu[:, s - CONV_STATE:, :])
        pool_p.append(b_in[:, s - POOL_STATE:, :])
        xs, sc_new, sp_new, win_s, kv_s = _sample_layer(
            l, depth, xs, p, cos_s, sin_s, sc_all[l], sp_all[l], cwin_t, ccmp_t, cslc_t, page_table, past,
            kv_s, win_s)
        conv_s.append(jnp.transpose(sc_new, (1, 0, 2)))
        pool_s.append(jnp.transpose(sp_new, (1, 0, 2)))
    cmp_p, slc_p, win_p = kv_p
    cmp_s, slc_s, _ = kv_s
    new_rows = lambda a: _rows_from_feat(a[:, 0])[:, :, None]
    return (xp, xs[:, None, :], jnp.stack(conv_p), jnp.stack(pool_p),
            _rows_from_feat(win_p[:, :, :, s - min(WINDOW, s):]), _rows_from_feat(cmp_p), _rows_from_feat(slc_p),
            jnp.stack(conv_s), jnp.stack(pool_s), _rows_from_feat(win_s.reshape(depth, bs, D_ROWS, lb)),
            new_rows(cmp_s), new_rows(slc_s))
```

```python
import functools

import numpy as np
import jax
import jax.numpy as jnp
from jax import lax
from jax.experimental import pallas as pl
from jax.experimental.pallas import tpu as pltpu

D_MODEL = 1024
D_CONV = 256
D_POOL = 256
N_HEADS = 8
N_KV_HEADS = 2
HEAD_DIM = 64
D_ATTN = N_HEADS * HEAD_DIM
CONV_WIDTH = 31
CONV_STATE = CONV_WIDTH - 1
POOL_WINDOWS = (2, 4, 8, 16)
POOL_GROUP = D_POOL // len(POOL_WINDOWS)
POOL_STATE = max(POOL_WINDOWS) - 1
CMP_BLOCK = 32
CMP_STRIDE = 16
SLC_BLOCK = 64
N_SEL = 8
WINDOW = 256
Q_BLOCK = 128
ROT_DIM = HEAD_DIM // 4
ROT_HALF = ROT_DIM // 2
ROPE_THETA = 500000.0
N_BRANCH = 3
D_KV = N_BRANCH * 2 * N_KV_HEADS * HEAD_DIM
D_ROWS = 2 * N_KV_HEADS * HEAD_DIM
N_REP = N_HEADS // N_KV_HEADS
EPS = 1e-6
NEG = -1e30
BIG = 1e4

GATE_PAD = 128
D_TOK = 3 * D_CONV + 2 * D_POOL + D_ATTN + GATE_PAD
D_FEAT = D_ATTN + D_KV
N_FEAT_BLOCKS = D_FEAT // HEAD_DIM
SEL_KEYS = 512
VMEM_LIMIT = 56 * 1024 * 1024

F32 = jnp.float32
BF16 = jnp.bfloat16
NT_DIMS = (((1,), (1,)), ((), ()))


def _sigmoid(x):
    return 1.0 / (1.0 + jnp.exp(-x))


def _silu(x):
    return x * _sigmoid(x)


def _gelu_tanh(x):
    return 0.5 * x * (1.0 + jnp.tanh(np.sqrt(2.0 / np.pi).astype(np.float32) * (x + 0.044715 * (x * x * x))))


def _dot(a, b):
    return jnp.dot(a, b, preferred_element_type=F32)


def _dot_nt(a, b):
    return lax.dot_general(a, b, NT_DIMS, preferred_element_type=F32)


def _masked_softmax(s, mask):
    s = jnp.where(mask, s, NEG)
    m = jnp.max(s, axis=-1, keepdims=True)
    p = jnp.where(mask, jnp.exp(s - m), 0.0)
    return p / jnp.maximum(jnp.sum(p, axis=-1, keepdims=True), 1e-30)


def _in_proj_kernel(x_ref, wnorm_ref, wtok_ref, wfeat_ref, gcol_ref, cos_ref, sin_ref, *rest):
    glu_ref, bin_ref, sz_ref, gates_ref, qt_ref, cmp_ref, slc_ref, win_ref = rest[-8:]
    x = x_ref[...]
    ms = jnp.mean(x * x, axis=-1, keepdims=True)
    h = (x * lax.rsqrt(ms + EPS) * wnorm_ref[...]).astype(BF16)

    a = _dot(h, wtok_ref[...])
    o = 0
    glu_ref[...] = a[:, o:o + D_CONV] * _sigmoid(a[:, o + D_CONV:o + 2 * D_CONV])
    o += 2 * D_CONV
    bin_ref[...] = a[:, o:o + D_POOL]
    o += D_POOL
    z = a[:, o:o + D_CONV + D_POOL + D_ATTN]
    sz_ref[...] = _silu(z)
    o += D_CONV + D_POOL + D_ATTN
    gates_ref[...] = _sigmoid(a[:, o:o + GATE_PAD])

    f = _dot_nt(wfeat_ref[...], h)
    cos = cos_ref[...]
    sin = sin_ref[...]
    kv_refs = (cmp_ref, slc_ref, win_ref)
    for hb in range(N_FEAT_BLOCKS):
        blk = f[hb * HEAD_DIM:(hb + 1) * HEAD_DIM, :]
        kv_blk = hb - N_HEADS
        is_value = kv_blk >= 0 and (kv_blk % (2 * N_KV_HEADS)) >= N_KV_HEADS
        if not is_value:
            bms = jnp.mean(blk * blk, axis=0, keepdims=True)
            y = blk * lax.rsqrt(bms + EPS) * gcol_ref[hb * HEAD_DIM:(hb + 1) * HEAD_DIM, :]
            x1 = y[0:ROT_HALF]
            x2 = y[ROT_HALF:ROT_DIM]
            blk = jnp.concatenate([x1 * cos - x2 * sin, x2 * cos + x1 * sin, y[ROT_DIM:]], axis=0)
        if kv_blk < 0:
            qt_ref[hb * HEAD_DIM:(hb + 1) * HEAD_DIM, :] = blk
        else:
            r = kv_blk % (2 * N_KV_HEADS)
            kv_refs[kv_blk // (2 * N_KV_HEADS)][r * HEAD_DIM:(r + 1) * HEAD_DIM, :] = blk


def _in_proj(x, wnorm, wtok, wfeat, gcol, cos_t, sin_t, tm, l, depth, kv_prev):
    n, t, _ = x.shape
    grid = (n, t // tm)
    tok = lambda w: pl.BlockSpec((None, tm, w), lambda i, j: (i, j, 0))
    feat = lambda w: pl.BlockSpec((None, w, tm), lambda i, j: (i, 0, j))
    kv_spec = pl.BlockSpec((None, None, D_ROWS, tm), lambda i, j: (l, i, 0, j))
    full = lambda a: pl.BlockSpec(a.shape, lambda i, j: (0,) * a.ndim)
    tok_shape = lambda w: jax.ShapeDtypeStruct((n, t, w), F32)
    kv_shape = jax.ShapeDtypeStruct((depth, n, D_ROWS, t), F32)
    n_in = 7
    kv_prev = () if kv_prev is None else tuple(kv_prev)
    return pl.pallas_call(
        _in_proj_kernel,
        grid=grid,
        in_specs=[tok(D_MODEL), full(wnorm), full(wtok), full(wfeat), full(gcol),
                  pl.BlockSpec((ROT_HALF, tm), lambda i, j: (0, j)),
                  pl.BlockSpec((ROT_HALF, tm), lambda i, j: (0, j))]
                 + [pl.BlockSpec(memory_space=pl.ANY)] * len(kv_prev),
        out_specs=[tok(D_CONV), tok(D_POOL), tok(D_CONV + D_POOL + D_ATTN), tok(GATE_PAD),
                   feat(D_ATTN), kv_spec, kv_spec, kv_spec],
        out_shape=[tok_shape(D_CONV), tok_shape(D_POOL), tok_shape(D_CONV + D_POOL + D_ATTN), tok_shape(GATE_PAD),
                   jax.ShapeDtypeStruct((n, D_ATTN, t), F32), kv_shape, kv_shape, kv_shape],
        input_output_aliases={n_in + k: 5 + k for k in range(len(kv_prev))},
        compiler_params=pltpu.CompilerParams(dimension_semantics=("arbitrary", "arbitrary"),
                                             vmem_limit_bytes=VMEM_LIMIT),
        name="in_proj",
    )(x, wnorm, wtok, wfeat, gcol, cos_t, sin_t, *kv_prev)


def _conv_tail(acc, sz_a, bdw, lng, lnb, wpw, bpw):
    y = acc + bdw
    mu = jnp.mean(y, axis=-1, keepdims=True)
    yc = y - mu
    var = jnp.mean(yc * yc, axis=-1, keepdims=True)
    y = yc * lax.rsqrt(var + EPS) * lng + lnb
    y = _dot(_silu(y).astype(BF16), wpw) + bpw
    return y * sz_a


def _pool_tail(total, cnt, xcur, sz_b, wpool, pscale):
    d = total / cnt - xcur
    y = _dot(d.astype(BF16), wpool) * pscale
    return y * sz_b


def _pool_window_of_lane(shape):
    lane = lax.broadcasted_iota(jnp.int32, shape, len(shape) - 1)
    w = jnp.full(shape, POOL_WINDOWS[0], jnp.int32)
    for gi in range(1, len(POOL_WINDOWS)):
        w = jnp.where(lane >= gi * POOL_GROUP, POOL_WINDOWS[gi], w)
    return w


def _pool_select(sums, shape):
    lane = lax.broadcasted_iota(jnp.int32, shape, len(shape) - 1)
    total = sums[0]
    for gi in range(1, len(POOL_WINDOWS)):
        total = jnp.where(lane >= gi * POOL_GROUP, sums[gi], total)
    return total


CONV_HALO = 32
POOL_HALO = 16


MIX_ROWS = 64
SUBLANES = 8


def _conv_rows(cbuf, wdw_ref, r0, rows):
    off = CONV_HALO - CONV_STATE
    y = None
    for c in range(SUBLANES):
        part = None
        for k in range(CONV_WIDTH):
            if (k + off) % SUBLANES != c:
                continue
            base = r0 + k + off - c
            term = wdw_ref[k:k + 1, :] * cbuf[base:base + rows + SUBLANES, :]
            part = term if part is None else part + term
        if part is not None:
            y = part[c:c + rows] if y is None else y + part[c:c + rows]
    return y


def _mix_out_prompt_kernel(x_ref, glu_ref, gprev_ref, bin_ref, bprev_ref, sz_ref, oat_ref, wdw_ref, bdw_ref, lng_ref,
                           lnb_ref, wpw_ref, bpw_ref, wpool_ref, pscale_ref, wout_ref, o_ref, cbuf, pbuf, ymix, *, t):
    i = pl.program_id(1)
    keep = (i > 0).astype(F32)
    cbuf[0:CONV_HALO, :] = gprev_ref[...] * keep
    cbuf[CONV_HALO:CONV_HALO + t, :] = glu_ref[...]
    cbuf[CONV_HALO + t:CONV_HALO + t + SUBLANES, :] = jnp.zeros((SUBLANES, D_CONV), F32)
    pbuf[0:POOL_HALO, :] = bprev_ref[...] * keep
    pbuf[POOL_HALO:POOL_HALO + t, :] = bin_ref[...]
    nab = D_CONV + D_POOL
    for r0 in range(0, t, MIX_ROWS):
        rows = slice(r0, r0 + MIX_ROWS)
        sz = sz_ref[rows, :]
        y_a = _conv_tail(_conv_rows(cbuf, wdw_ref, r0, MIX_ROWS), sz[:, 0:D_CONV], bdw_ref[...], lng_ref[...],
                         lnb_ref[...], wpw_ref[...], bpw_ref[...])
        ymix[rows, 0:D_CONV] = y_a.astype(BF16)

        sums = []
        run = jnp.zeros((MIX_ROWS, D_POOL), F32)
        j = 0
        for w in POOL_WINDOWS:
            while j < w:
                run = run + pbuf[pl.ds(POOL_HALO + r0 - j, MIX_ROWS), :]
                j += 1
            sums.append(run)
        total = _pool_select(sums, (MIX_ROWS, D_POOL))
        pos = i * t + r0 + lax.broadcasted_iota(jnp.int32, (MIX_ROWS, D_POOL), 0)
        cnt = jnp.minimum(_pool_window_of_lane((MIX_ROWS, D_POOL)), pos + 1).astype(F32)
        y_b = _pool_tail(total, cnt, bin_ref[rows, :], sz[:, D_CONV:nab], wpool_ref[...], pscale_ref[...])
        ymix[rows, D_CONV:nab] = y_b.astype(BF16)
        ymix[rows, nab:nab + D_ATTN] = (oat_ref[rows, :] * sz[:, nab:nab + D_ATTN]).astype(BF16)
    o_ref[...] = x_ref[...] + _dot(ymix[...], wout_ref[...])


def _mix_out_prompt(x, glu, b_in, sz, o_attn, wdw, bdw, lng, lnb, wpw, bpw, wpool, pscale, wout, t):
    n, s, _ = glu.shape
    assert t % MIX_ROWS == 0 and s % t == 0
    grid = (n, s // t)
    cur = lambda w: pl.BlockSpec((None, t, w), lambda i, j: (i, j, 0))
    prev = lambda rows, w: pl.BlockSpec((None, rows, w), lambda i, j: (i, jnp.maximum(j * (t // rows) - 1, 0), 0))
    full = lambda a: pl.BlockSpec(a.shape, lambda i, j: (0,) * a.ndim)
    d_mix = D_CONV + D_POOL + D_ATTN
    return pl.pallas_call(
        functools.partial(_mix_out_prompt_kernel, t=t),
        grid=grid,
        in_specs=[cur(D_MODEL), cur(D_CONV), prev(CONV_HALO, D_CONV), cur(D_POOL), prev(POOL_HALO, D_POOL),
                  cur(d_mix), cur(D_ATTN),
                  full(wdw), full(bdw), full(lng), full(lnb), full(wpw), full(bpw), full(wpool), full(pscale),
                  full(wout)],
        out_specs=cur(D_MODEL),
        out_shape=jax.ShapeDtypeStruct(x.shape, F32),
        scratch_shapes=[pltpu.VMEM((CONV_HALO + t + SUBLANES, D_CONV), F32), pltpu.VMEM((POOL_HALO + t, D_POOL), F32),
                        pltpu.VMEM((t, d_mix), BF16)],
        compiler_params=pltpu.CompilerParams(dimension_semantics=("arbitrary", "arbitrary"),
                                             vmem_limit_bytes=VMEM_LIMIT),
        name="mix_out_prompt",
    )(x, glu, glu, b_in, b_in, sz, o_attn, wdw, bdw, lng, lnb, wpw, bpw, wpool, pscale, wout)


def _convpool_sample_kernel(sc_ref, sp_ref, glu_ref, bin_ref, sz_ref, wdw_ref, bdw_ref, lng_ref, lnb_ref,
                            wpw_ref, bpw_ref, wpool_ref, pscale_ref, y_ref, sc_out, sp_out, *, pos):
    glu = glu_ref[...]
    acc = wdw_ref[CONV_STATE:CONV_WIDTH, :] * glu
    for k in range(CONV_STATE):
        acc = acc + wdw_ref[k:k + 1, :] * sc_ref[k]
    sz = sz_ref[...]
    y_ref[:, 0:D_CONV] = _conv_tail(acc, sz[:, 0:D_CONV], bdw_ref[...], lng_ref[...], lnb_ref[...],
                                    wpw_ref[...], bpw_ref[...])
    for k in range(CONV_STATE - 1):
        sc_out[k] = sc_ref[k + 1]
    sc_out[CONV_STATE - 1] = glu

    xcur = bin_ref[...]
    shape = xcur.shape
    sums = []
    run = xcur
    j = 1
    for w in POOL_WINDOWS:
        while j < w:
            run = run + sp_ref[POOL_STATE - j]
            j += 1
        sums.append(run)
    total = _pool_select(sums, shape)
    cnt = jnp.minimum(_pool_window_of_lane(shape), pos + 1).astype(F32)
    y_ref[:, D_CONV:D_CONV + D_POOL] = _pool_tail(total, cnt, xcur, sz[:, D_CONV:D_CONV + D_POOL],
                                                  wpool_ref[...], pscale_ref[...])
    for k in range(POOL_STATE - 1):
        sp_out[k] = sp_ref[k + 1]
    sp_out[POOL_STATE - 1] = xcur


def _convpool_sample(sc, sp, glu, b_in, sz, wdw, bdw, lng, lnb, wpw, bpw, wpool, pscale, pos):
    nb = glu.shape[0]
    return pl.pallas_call(
        functools.partial(_convpool_sample_kernel, pos=pos),
        out_shape=[jax.ShapeDtypeStruct((nb, D_CONV + D_POOL), F32),
                   jax.ShapeDtypeStruct(sc.shape, F32), jax.ShapeDtypeStruct(sp.shape, F32)],
        compiler_params=pltpu.CompilerParams(vmem_limit_bytes=VMEM_LIMIT),
        name="convpool_sample",
    )(sc, sp, glu, b_in, sz[:, 0:D_CONV + D_POOL], wdw, bdw, lng, lnb, wpw, bpw, wpool, pscale)


def _out_proj_kernel(x_ref, yab_ref, oat_ref, szc_ref, w_ref, o_ref):
    ya = yab_ref[...].astype(BF16)
    yc = (oat_ref[...] * szc_ref[...]).astype(BF16)
    nab = D_CONV + D_POOL
    o_ref[...] = x_ref[...] + _dot(ya, w_ref[0:nab, :]) + _dot(yc, w_ref[nab:nab + D_ATTN, :])


def _out_proj(x, yab, o_attn, sz, w_out, tm):
    n, t, _ = x.shape
    nab = D_CONV + D_POOL
    assert nab == D_ATTN
    grid = (n, t // tm)
    tok = lambda w, cb=0: pl.BlockSpec((None, tm, w), lambda i, j: (i, j, cb))
    return pl.pallas_call(
        _out_proj_kernel,
        grid=grid,
        in_specs=[tok(D_MODEL), tok(nab), tok(D_ATTN), tok(D_ATTN, 1),
                  pl.BlockSpec(w_out.shape, lambda i, j: (0, 0))],
        out_specs=tok(D_MODEL),
        out_shape=jax.ShapeDtypeStruct(x.shape, F32),
        compiler_params=pltpu.CompilerParams(dimension_semantics=("arbitrary", "arbitrary"),
                                             vmem_limit_bytes=VMEM_LIMIT),
        name="out_proj",
    )(x, yab, o_attn, sz, w_out)


GD = N_KV_HEADS * HEAD_DIM
CMP_FLAT = CMP_STRIDE * GD


FLAT_SPAN = CMP_STRIDE * CMP_STRIDE


def _piece_perm():
    m = np.arange(FLAT_SPAN)
    src = (m % CMP_STRIDE) * CMP_STRIDE + m // CMP_STRIDE
    return jnp.asarray((src[:, None] == np.arange(FLAT_SPAN)[None, :]).astype(np.float32), BF16)


def _fill_flat(x_t, perm, flat_ref, piece0):
    y = _dot_nt(perm, x_t).astype(BF16)
    for kv in range(2):
        for r in range(CMP_STRIDE):
            flat_ref[kv, piece0:piece0 + CMP_STRIDE, r * GD:(r + 1) * GD] = (
                y[r * CMP_STRIDE:(r + 1) * CMP_STRIDE, kv * GD:(kv + 1) * GD])


def _compress_mlp(flat, w1ab, bias, w2):
    n_pc = flat.shape[0]
    hb = _dot(flat, w1ab)
    h = hb[:, 0:GD] + jnp.concatenate([hb[1:n_pc, GD:2 * GD], jnp.zeros((1, GD), F32)], axis=0) + bias
    out = _dot(_gelu_tanh(h).astype(BF16), w2)
    row = lax.broadcasted_iota(jnp.int32, out.shape, 0)
    return jnp.where(row < n_pc - 1, out, 0.0)


def _cmp_bias_kernel(pe_ref, w1_ref, o_ref):
    for kv in range(2):
        o_ref[kv] = _dot(pe_ref[kv].astype(BF16), w1_ref[kv])


def _cmp_bias(pe_flat, w1):
    return pl.pallas_call(_cmp_bias_kernel, out_shape=jax.ShapeDtypeStruct((2, 1, HEAD_DIM), F32),
                          name="cmp_bias")(pe_flat, w1)


def _compress_prompt_kernel(x_ref, perm_ref, w1ab_ref, bias_ref, w2_ref, o_ref, flat_ref, *, s):
    for t in range(s // FLAT_SPAN):
        _fill_flat(x_ref[:, t * FLAT_SPAN:(t + 1) * FLAT_SPAN].astype(BF16), perm_ref[...], flat_ref, t * CMP_STRIDE)
    for kv in range(2):
        o_ref[:, kv * GD:(kv + 1) * GD] = _compress_mlp(flat_ref[kv], w1ab_ref[kv], bias_ref[kv], w2_ref[kv])


def _compress_prompt(cmp_t, l, w1ab, bias, w2):
    _, n, _, s = cmp_t.shape
    assert s % FLAT_SPAN == 0
    n_pc = s // CMP_STRIDE
    perm = _piece_perm()
    full = lambda a: pl.BlockSpec(a.shape, lambda i: (0,) * a.ndim)
    return pl.pallas_call(
        functools.partial(_compress_prompt_kernel, s=s),
        grid=(n,),
        in_specs=[pl.BlockSpec((None, None, D_ROWS, s), lambda i: (l, i, 0, 0)),
                  full(perm), full(w1ab), full(bias), full(w2)],
        out_specs=pl.BlockSpec((None, n_pc, 2 * GD), lambda i: (i, 0, 0)),
        out_shape=jax.ShapeDtypeStruct((n, n_pc, 2 * GD), F32),
        scratch_shapes=[pltpu.VMEM((2, n_pc, CMP_FLAT), BF16)],
        compiler_params=pltpu.CompilerParams(dimension_semantics=("arbitrary",), vmem_limit_bytes=VMEM_LIMIT),
        name="compress_prompt",
    )(cmp_t, perm, w1ab, bias, w2)


def _select_blocks(imp_t, pos_row, n_blocks):
    shape = imp_t.shape
    assert n_blocks % 8 == 0
    j = lax.broadcasted_iota(jnp.int32, shape, 0)
    cur = pos_row // SLC_BLOCK
    forced = (j == 0) | (j == cur) | (j == cur - 1)
    score = jnp.where(j <= cur, jnp.where(forced, BIG, imp_t), -BIG)
    tiles = [score[8 * v:8 * v + 8] for v in range(n_blocks // 8)]
    j8 = lax.broadcasted_iota(jnp.int32, (8, shape[1]), 0)
    ranks = [jnp.zeros((8, shape[1]), F32) for _ in tiles]
    for jp in range(n_blocks):
        sj = score[jp:jp + 1, :]
        for v, tile in enumerate(tiles):
            if jp < 8 * v:
                beats = sj >= tile
            elif jp >= 8 * v + 8:
                beats = sj > tile
            else:
                beats = (sj > tile) | ((sj == tile) & (j8 > jp - 8 * v))
            ranks[v] = ranks[v] + jnp.where(beats, 1.0, 0.0)
    rank = jnp.concatenate(ranks, axis=0)
    return (rank < float(min(N_SEL, n_blocks))).astype(F32)


def _col_softmax(s):
    m = jnp.max(s, axis=0, keepdims=True)
    e = jnp.exp(s - m)
    inv = jnp.where(m > 0.5 * NEG, 1.0 / jnp.maximum(jnp.sum(e, axis=0, keepdims=True), 1e-30), 0.0)
    return e, inv


ONES_ROWS = 16


def _with_ones(v_t):
    return jnp.concatenate([v_t, jnp.ones((ONES_ROWS, v_t.shape[1]), BF16)], axis=0)


def _normalised(acc):
    return acc[0:HEAD_DIM] * (1.0 / jnp.maximum(acc[HEAD_DIM:HEAD_DIM + 1], 1e-30))


def _mask_heads(sc, ok, tq):
    return jnp.concatenate([jnp.where(ok, sc[:, r * tq:(r + 1) * tq], NEG) for r in range(N_REP)], axis=1)


def _attn_prompt_kernel(qt_ref, kcv_ref, slc_ref, win_ref, gates_ref, ovt_ref, o_ref,
                        ks_ref, kw_ref, s_ref, sel_ref, acc_ref, *, s, n_blk):
    tq = Q_BLOCK
    cols = N_REP * tq
    groups = range(N_KV_HEADS)
    qc = pl.program_id(1)
    q0 = qc * tq
    pos_row = q0 + lax.broadcasted_iota(jnp.int32, (1, tq), 1)
    n_slc = s // SLC_BLOCK
    n_cmp = kcv_ref.shape[0]
    scale = HEAD_DIM ** -0.5
    blocks_per_step = SEL_KEYS // SLC_BLOCK
    v_row0 = lambda g: (N_KV_HEADS + g) * HEAD_DIM

    @pl.when(qc == 0)
    def _():
        for g in range(N_KV_HEADS):
            for t in range(s // 128):
                cols_t = slice(t * 128, (t + 1) * 128)
                ks_ref[g, cols_t, :] = slc_ref[g * HEAD_DIM:(g + 1) * HEAD_DIM, cols_t].T.astype(BF16)
                kw_ref[g, cols_t, :] = win_ref[g * HEAD_DIM:(g + 1) * HEAD_DIM, cols_t].T.astype(BF16)

    gates_t = gates_ref[...].T
    vc_t = kcv_ref[:, GD:2 * GD].T.astype(BF16)
    q_ts = []
    for g in groups:
        q_t = jnp.concatenate([qt_ref[(g * N_REP + r) * HEAD_DIM:(g * N_REP + r + 1) * HEAD_DIM, :]
                               for r in range(N_REP)], axis=1)
        q_ts.append((q_t * scale).astype(BF16))

    c_idx = lax.broadcasted_iota(jnp.int32, (n_cmp, 1), 0)
    valid_c = (c_idx * CMP_STRIDE + (CMP_BLOCK - 1) <= pos_row) & (c_idx < n_blk)
    ovt = ovt_ref[...]
    o_cs = []
    for g in groups:
        kc = kcv_ref[:, g * HEAD_DIM:(g + 1) * HEAD_DIM].astype(BF16)
        e_c, inv_c = _col_softmax(_mask_heads(_dot(kc, q_ts[g]), valid_c, tq))
        o_cs.append(_dot(vc_t[g * HEAD_DIM:(g + 1) * HEAD_DIM, :], e_c.astype(BF16)) * inv_c)
        pc_sum = e_c[:, 0:tq] * inv_c[:, 0:tq]
        for r in range(1, N_REP):
            pc_sum = pc_sum + e_c[:, r * tq:(r + 1) * tq] * inv_c[:, r * tq:(r + 1) * tq]
        pc_hi = pc_sum.astype(BF16)
        pc_lo = (pc_sum - pc_hi.astype(F32)).astype(BF16)
        imp_t = _dot(ovt, pc_hi) + _dot(ovt, pc_lo)
        sel_ref[g] = _select_blocks(imp_t, pos_row, n_slc)

    def score_step(i, m_run):
        k0 = pl.multiple_of(i * SEL_KEYS, SEL_KEYS)
        key = k0 + lax.broadcasted_iota(jnp.int32, (SEL_KEYS, 1), 0)
        causal = key <= pos_row
        m_new = []
        for g in groups:
            sc = _dot(ks_ref[g, pl.ds(k0, SEL_KEYS), :], q_ts[g])
            parts = []
            for jj in range(blocks_per_step):
                rows = slice(jj * SLC_BLOCK, (jj + 1) * SLC_BLOCK)
                chosen = sel_ref[g, pl.ds(i * blocks_per_step + jj, 1), :] > 0.5
                parts.append(_mask_heads(sc[rows], chosen & causal[rows], tq))
            sc = jnp.concatenate(parts, axis=0)
            s_ref[g, pl.ds(k0, SEL_KEYS), :] = sc.astype(BF16)
            m_new.append(jnp.maximum(m_run[g], jnp.max(sc, axis=0, keepdims=True)))
        return tuple(m_new)

    n_steps = (q0 + tq + SEL_KEYS - 1) // SEL_KEYS
    m_sel = lax.fori_loop(0, n_steps, score_step, tuple(jnp.full((1, cols), NEG, F32) for _ in groups))
    m_sel = [m.astype(BF16) for m in m_sel]

    acc_ref[...] = jnp.zeros(acc_ref.shape, F32)

    def value_step(i, carry):
        k0 = pl.multiple_of(i * SEL_KEYS, SEL_KEYS)
        for g in groups:
            p = jnp.exp(s_ref[g, pl.ds(k0, SEL_KEYS), :] - m_sel[g])
            v_t = slc_ref[v_row0(g):v_row0(g) + HEAD_DIM, pl.ds(k0, SEL_KEYS)].astype(BF16)
            acc_ref[g] += _dot(_with_ones(v_t), p)
        return carry

    lax.fori_loop(0, n_steps, value_step, 0)

    n_prev = -(-(WINDOW - 1) // tq)
    k0s, oks = [], []
    for b in range(n_prev + 1):
        kb = qc - n_prev + b
        k0s.append(pl.multiple_of(jnp.maximum(kb, 0) * tq, tq))
        kp = kb * tq + lax.broadcasted_iota(jnp.int32, (tq, 1), 0)
        oks.append((kp <= pos_row) & (kp > pos_row - WINDOW) & (kp >= 0))

    out_blocks = []
    for g in groups:
        o_s = _normalised(acc_ref[g])
        s_w = jnp.concatenate([_mask_heads(_dot(kw_ref[g, pl.ds(k0s[b], tq), :], q_ts[g]), oks[b], tq)
                               for b in range(n_prev + 1)], axis=0)
        e_w = jnp.exp(s_w.astype(BF16) - jnp.max(s_w, axis=0, keepdims=True).astype(BF16))
        o_w = _dot(_with_ones(win_ref[v_row0(g):v_row0(g) + HEAD_DIM, pl.ds(k0s[0], tq)].astype(BF16)), e_w[0:tq])
        for b in range(1, n_prev + 1):
            o_w = o_w + _dot(_with_ones(win_ref[v_row0(g):v_row0(g) + HEAD_DIM, pl.ds(k0s[b], tq)].astype(BF16)),
                             e_w[b * tq:(b + 1) * tq])
        o_w = _normalised(o_w)
        for r in range(N_REP):
            h = g * N_REP + r
            c = slice(r * tq, (r + 1) * tq)
            out_blocks.append(gates_t[h:h + 1] * o_cs[g][:, c] + gates_t[N_HEADS + h:N_HEADS + h + 1] * o_s[:, c]
                              + gates_t[2 * N_HEADS + h:2 * N_HEADS + h + 1] * o_w[:, c])
    o_ref[...] = jnp.concatenate(out_blocks, axis=0).T


def _overlap_t(n_cmp_rows, n_blk, n_slc):
    c = np.arange(n_cmp_rows)
    start = c * CMP_STRIDE
    end = start + CMP_BLOCK - 1
    j0 = np.arange(n_slc) * SLC_BLOCK
    ov = (end[None, :] >= j0[:, None]) & (start[None, :] < j0[:, None] + SLC_BLOCK) & (c[None, :] < n_blk)
    return jnp.asarray(ov.astype(np.float32), BF16)


def _attn_prompt(q_t, kcv, slc_t, win_t, gates, l):
    n, _, s = q_t.shape
    n_cmp = kcv.shape[1]
    n_blk = s // CMP_STRIDE - CMP_BLOCK // CMP_STRIDE + 1
    n_slc = s // SLC_BLOCK
    ovt = _overlap_t(n_cmp, n_blk, n_slc)
    tq = Q_BLOCK
    cols = N_REP * tq
    assert s % SEL_KEYS == 0 and SEL_KEYS % SLC_BLOCK == 0
    return pl.pallas_call(
        functools.partial(_attn_prompt_kernel, s=s, n_blk=n_blk),
        grid=(n, s // tq),
        in_specs=[pl.BlockSpec((None, D_ATTN, tq), lambda i, j: (i, 0, j)),
                  pl.BlockSpec((None,) + kcv.shape[1:], lambda i, j: (i, 0, 0)),
                  pl.BlockSpec((None, None, D_ROWS, s), lambda i, j: (l, i, 0, 0)),
                  pl.BlockSpec((None, None, D_ROWS, s), lambda i, j: (l, i, 0, 0)),
                  pl.BlockSpec((None, tq, GATE_PAD), lambda i, j: (i, j, 0)),
                  pl.BlockSpec(ovt.shape, lambda i, j: (0, 0))],
        out_specs=pl.BlockSpec((None, tq, D_ATTN), lambda i, j: (i, j, 0)),
        out_shape=jax.ShapeDtypeStruct((n, s, D_ATTN), F32),
        scratch_shapes=[pltpu.VMEM((N_KV_HEADS, s, HEAD_DIM), BF16),
                        pltpu.VMEM((N_KV_HEADS, s, HEAD_DIM), BF16),
                        pltpu.VMEM((N_KV_HEADS, s, cols), BF16),
                        pltpu.VMEM((N_KV_HEADS, n_slc, tq), F32),
                        pltpu.VMEM((N_KV_HEADS, HEAD_DIM + ONES_ROWS, cols), F32)],
        compiler_params=pltpu.CompilerParams(dimension_semantics=("arbitrary", "arbitrary"),
                                             vmem_limit_bytes=VMEM_LIMIT),
        name="attn_prompt",
    )(q_t, kcv, slc_t, win_t, gates, ovt)


PAGE = 128
SAMPLE_SEQS = 4


def _sample_geometry(past):
    length = past + 1
    padded = -(-length // SLC_BLOCK) * SLC_BLOCK
    n_pc = padded // CMP_STRIDE
    n_pc_pad = -(-n_pc // CMP_STRIDE) * CMP_STRIDE
    return dict(n_pc=n_pc, n_pc_pad=n_pc_pad, n_blk=n_pc - CMP_BLOCK // CMP_STRIDE + 1, n_slc=padded // SLC_BLOCK)


def _attn_sample_kernel(pt_ref, q_ref, new_ref, gates_ref, wincol_ref, cwin_ref, ov_ref, bok_ref,
                        perm_ref, w1ab_ref, bias_ref, w2_ref, *rest, past, n_pages, nseq):
    del pt_ref
    cmp_pages = [rest[u * n_pages:(u + 1) * n_pages] for u in range(nseq)]
    slc_pages = [rest[(nseq + u) * n_pages:(nseq + u + 1) * n_pages] for u in range(nseq)]
    o_ref, wout_ref, flat_ref = rest[-3:]
    geo = _sample_geometry(past)
    n_pc_pad, n_blk, n_slc = geo["n_pc_pad"], geo["n_blk"], geo["n_slc"]
    pos = past
    lb = cwin_ref.shape[-1]
    scale = HEAD_DIM ** -0.5
    head = lax.broadcasted_iota(jnp.int32, (N_HEADS, 1), 0)
    lane = lax.broadcasted_iota(jnp.int32, (1, GD), 1)
    lane_h = lax.broadcasted_iota(jnp.int32, (N_HEADS, GD), 1)
    own = (lane_h // HEAD_DIM) == (head // N_REP)

    def new_row(u, branch, kv):
        o = (branch * 2 + kv) * GD
        return new_ref[u, :, o:o + GD]

    def rounded(x):
        return x.astype(BF16).astype(F32)

    kcv = []
    pages_per_span = FLAT_SPAN // PAGE
    past_pc = past // CMP_STRIDE
    for u in range(nseq):
        row0 = u * n_pc_pad
        for t in range(n_pages // pages_per_span):
            x_t = jnp.concatenate([cmp_pages[u][t * pages_per_span + w][...].reshape(2 * GD, PAGE)
                                   for w in range(pages_per_span)], axis=1)
            _fill_flat(x_t.astype(BF16), perm_ref[...], flat_ref, row0 + t * CMP_STRIDE)
        for kv in range(2):
            flat_ref[kv, row0 + past_pc:row0 + n_pc_pad, :] = jnp.zeros((n_pc_pad - past_pc, CMP_FLAT), BF16)
            flat_ref[kv, row0 + past_pc:row0 + past_pc + 1, 0:GD] = new_row(u, 0, kv).astype(BF16)
    for kv in range(2):
        kcv.append(_compress_mlp(flat_ref[kv], w1ab_ref[kv], bias_ref[kv], w2_ref[kv]).astype(BF16))
    for u in range(nseq):
        _attn_sample_one(u, kcv[0][u * n_pc_pad:(u + 1) * n_pc_pad], kcv[1][u * n_pc_pad:(u + 1) * n_pc_pad],
                         q_ref, new_row, rounded, gates_ref, wincol_ref, cwin_ref, ov_ref, bok_ref,
                         slc_pages[u], o_ref, wout_ref, geo=geo, past=past, consts=(head, lane, lane_h, own))


def _attn_sample_one(u, kc, vc, q_ref, new_row, rounded, gates_ref, wincol_ref, cwin_ref, ov_ref, bok_ref,
                     slc_pages, o_ref, wout_ref, *, geo, past, consts):
    head, lane, lane_h, own = consts
    n_pc_pad, n_blk, n_slc = geo["n_pc_pad"], geo["n_blk"], geo["n_slc"]
    n_pages = len(slc_pages)
    pos = past
    lb = cwin_ref.shape[-1]
    scale = HEAD_DIM ** -0.5
    q_rows = jnp.concatenate([q_ref[u, :, h * HEAD_DIM:(h + 1) * HEAD_DIM] for h in range(N_HEADS)], axis=0)
    qx = (jnp.where(own, jnp.concatenate([q_rows] * N_KV_HEADS, axis=1), 0.0) * scale).astype(BF16)
    qx32 = qx.astype(F32)
    c_idx = lax.broadcasted_iota(jnp.int32, (1, n_pc_pad), 1)
    valid = (c_idx * CMP_STRIDE + (CMP_BLOCK - 1) <= pos) & (c_idx < n_blk)
    p_c = _masked_softmax(_dot_nt(qx, kc), valid)
    o_c = _dot(p_c.astype(BF16), vc)

    cur = pos // SLC_BLOCK
    forced = (lane == 0) | (lane == cur) | (lane == cur - 1)
    jp = lax.broadcasted_iota(jnp.int32, (GD, GD), 0)
    jj = lax.broadcasted_iota(jnp.int32, (GD, GD), 1)
    sels = []
    for g in range(N_KV_HEADS):
        pc_sum = jnp.sum(p_c[g * N_REP:(g + 1) * N_REP], axis=0, keepdims=True)
        pc_hi = pc_sum.astype(BF16)
        pc_lo = (pc_sum - pc_hi.astype(F32)).astype(BF16)
        imp = _dot(pc_hi, ov_ref[...]) + _dot(pc_lo, ov_ref[...])
        score = jnp.where(lane <= cur, jnp.where(forced, BIG, imp), -BIG)
        score = jnp.where(lane < n_slc, score, -2.0 * BIG)
        score_b = jnp.broadcast_to(score, (GD, GD))
        score_a = score_b.T
        beats = (score_a > score_b) | ((score_a == score_b) & (jp < jj))
        rank = jnp.sum(beats.astype(F32), axis=0, keepdims=True)
        sel = ((rank < float(min(N_SEL, n_slc))) & (lane < n_slc)).astype(F32)
        sels.append(jnp.broadcast_to(sel, (N_REP, GD)))
    sel_h = jnp.concatenate(sels, axis=0)

    chosen = _dot(sel_h.astype(BF16), bok_ref[...]) > 0.5
    s_past = jnp.concatenate([_dot(qx, slc_pages[j][0].astype(BF16)) for j in range(n_pages)], axis=-1)
    s_past = jnp.where(chosen, s_past, NEG)
    new_ok = jnp.sum(jnp.where(lane == pos // SLC_BLOCK, sel_h, 0.0), axis=-1, keepdims=True) > 0.5
    s_new = jnp.where(new_ok, jnp.sum(qx32 * rounded(new_row(u, 1, 0)), axis=-1, keepdims=True), NEG)
    m = jnp.maximum(jnp.max(s_past, axis=-1, keepdims=True), s_new)
    p_past = jnp.where(chosen, jnp.exp(s_past - m), 0.0)
    p_new = jnp.where(new_ok, jnp.exp(s_new - m), 0.0)
    denom = jnp.maximum(jnp.sum(p_past, axis=-1, keepdims=True) + p_new, 1e-30)
    o_s = rounded(p_new) * rounded(new_row(u, 1, 1))
    for j in range(n_pages):
        o_s = o_s + _dot_nt(p_past[:, j * PAGE:(j + 1) * PAGE].astype(BF16), slc_pages[j][1].astype(BF16))
    o_s = o_s / denom

    kpos = (past - lb) + lax.broadcasted_iota(jnp.int32, (1, lb), 1)
    w_ok = (kpos <= pos) & (kpos > pos - WINDOW)
    s_w = jnp.where(w_ok, _dot(qx, cwin_ref[u, 0].astype(BF16)), NEG)
    s_wn = jnp.sum(qx32 * rounded(new_row(u, 2, 0)), axis=-1, keepdims=True)
    m = jnp.maximum(jnp.max(s_w, axis=-1, keepdims=True), s_wn)
    p_w = jnp.where(w_ok, jnp.exp(s_w - m), 0.0)
    p_wn = jnp.exp(s_wn - m)
    denom = jnp.maximum(jnp.sum(p_w, axis=-1, keepdims=True) + p_wn, 1e-30)
    o_w = (_dot_nt(p_w.astype(BF16), cwin_ref[u, 1].astype(BF16))
           + rounded(p_wn) * rounded(new_row(u, 2, 1))) / denom

    gates = gates_ref[u]
    gate = lambda br: jnp.sum(jnp.where(lane_h == head + br * N_HEADS, gates, 0.0), axis=-1, keepdims=True)
    o = gate(0) * o_c + gate(1) * o_s + gate(2) * o_w
    for h in range(N_HEADS):
        g = h // N_REP
        o_ref[u, :, h * HEAD_DIM:(h + 1) * HEAD_DIM] = o[h:h + 1, g * HEAD_DIM:(g + 1) * HEAD_DIM]

    for kv in range(2):
        wout_ref[u, kv, :, 0:lb - 1] = cwin_ref[u, kv, :, 1:lb]
        wout_ref[u, kv, :, lb - 1:lb] = wincol_ref[u, kv * GD:(kv + 1) * GD, :]


def _attn_sample(l, q_tok, new_tok, gates, win_col, cwin_t, ccmp_t, cslc_t, page_table, w1ab, bias, w2, win_prev):
    nb = q_tok.shape[0]
    n_pages = page_table.shape[1]
    past = n_pages * PAGE
    geo = _sample_geometry(past)
    lb = cwin_t.shape[-1]
    assert GATE_PAD == GD and FLAT_SPAN % PAGE == 0 and n_pages % (FLAT_SPAN // PAGE) == 0
    perm = _piece_perm()
    ov = _overlap_t(geo["n_pc_pad"], geo["n_blk"], GD).T
    ov = jnp.where(jnp.arange(GD)[None, :] < geo["n_slc"], ov, 0).astype(BF16)
    bok = jnp.asarray((np.arange(past)[None, :] // SLC_BLOCK == np.arange(GD)[:, None]).astype(np.float32), BF16)
    nseq = SAMPLE_SEQS if nb % SAMPLE_SEQS == 0 else 1
    seq = lambda a: pl.BlockSpec((nseq,) + a.shape[1:], lambda i, pt: (i,) + (0,) * (a.ndim - 1))
    full = lambda a: pl.BlockSpec(a.shape, lambda i, pt: (0,) * a.ndim)
    page_spec = lambda u, j: pl.BlockSpec((None, None, 2, GD, PAGE),
                                          lambda i, pt, u=u, j=j: (l, pt[i * nseq + u, j], 0, 0, 0))
    page_specs = [page_spec(u, j) for u in range(nseq) for j in range(n_pages)]
    operands = [page_table, q_tok, new_tok, gates, win_col, cwin_t, ov, bok, perm, w1ab, bias, w2,
                *([ccmp_t] * (nseq * n_pages)), *([cslc_t] * (nseq * n_pages))]
    win_prev = () if win_prev is None else (win_prev,)
    grid_spec = pltpu.PrefetchScalarGridSpec(
        num_scalar_prefetch=1,
        grid=(nb // nseq,),
        in_specs=[seq(q_tok), seq(new_tok), seq(gates), seq(win_col),
                  pl.BlockSpec((None, nseq, 2, GD, lb), lambda i, pt: (l, i, 0, 0, 0)),
                  full(ov), full(bok), full(perm), full(w1ab), full(bias), full(w2)] + page_specs * 2
                 + [pl.BlockSpec(memory_space=pl.ANY)] * len(win_prev),
        out_specs=[pl.BlockSpec((nseq, 1, D_ATTN), lambda i, pt: (i, 0, 0)),
                   pl.BlockSpec((None, nseq, 2, GD, lb), lambda i, pt: (l, i, 0, 0, 0))],
        scratch_shapes=[pltpu.VMEM((2, nseq * geo["n_pc_pad"], CMP_FLAT), BF16)],
    )
    return pl.pallas_call(
        functools.partial(_attn_sample_kernel, past=past, n_pages=n_pages, nseq=nseq),
        grid_spec=grid_spec,
        out_shape=[jax.ShapeDtypeStruct((nb, 1, D_ATTN), F32),
                   jax.ShapeDtypeStruct((cwin_t.shape[0], nb, 2, GD, lb), F32)],
        input_output_aliases={len(operands): 1} if win_prev else {},
        compiler_params=pltpu.CompilerParams(dimension_semantics=("arbitrary",), vmem_limit_bytes=VMEM_LIMIT),
        name="attn_sample",
    )(*operands, *win_prev)


def _prep_layer(l, w_norm, w_in, w_out, w_dw, b_dw, ln_g, ln_b, w_pw, b_pw, w_pool, pool_scale,
                g_q, g_k, cmp_pe, cmp_w1, cmp_w2):
    sizes = (D_CONV, D_CONV, D_CONV, D_POOL, D_POOL, D_ATTN, D_KV, N_BRANCH * N_HEADS, D_ATTN)
    offs = [0] + [int(v) for v in np.cumsum(sizes)]
    col = lambda k: w_in[l][:, offs[k]:offs[k + 1]]
    a_val, a_gate, z_a, b_in, z_b, q, kv, gate, z_c = (col(k) for k in range(9))
    pad = jnp.zeros((D_MODEL, GATE_PAD - N_BRANCH * N_HEADS), F32)
    wtok = jnp.concatenate([a_val, a_gate, b_in, z_a, z_b, z_c, gate, pad], axis=1).astype(BF16)
    wfeat = jnp.concatenate([q, kv], axis=1).T.astype(BF16)
    ones = jnp.ones((N_KV_HEADS * HEAD_DIM,), F32)
    gcol = jnp.concatenate([jnp.tile(g_q[l], N_HEADS)]
                           + [piece for br in range(N_BRANCH)
                              for piece in (jnp.tile(g_k[l, br], N_KV_HEADS), ones)])[:, None]
    wpool_bd = jnp.zeros((D_POOL, D_POOL), F32)
    for gi in range(len(POOL_WINDOWS)):
        wpool_bd = wpool_bd.at[gi * POOL_GROUP:(gi + 1) * POOL_GROUP, gi * POOL_GROUP:(gi + 1) * POOL_GROUP].set(w_pool[l, gi])
    row = lambda v: v[None, :]

    def both_groups(w):
        z = jnp.zeros_like(w)
        return jnp.stack([jnp.concatenate([w, z], axis=2), jnp.concatenate([z, w], axis=2)], axis=1).reshape(-1, GD)

    half = CMP_STRIDE * HEAD_DIM
    w1 = cmp_w1[l].astype(BF16)
    w1ab = jnp.stack([jnp.concatenate(
        [both_groups(w1[kv, :half].reshape(CMP_STRIDE, HEAD_DIM, HEAD_DIM)),
         both_groups(w1[kv, half:].reshape(CMP_STRIDE, HEAD_DIM, HEAD_DIM))], axis=1) for kv in range(2)])
    w2p = jnp.stack([both_groups(cmp_w2[l, kv].astype(BF16)[None]) for kv in range(2)])
    cbias = jnp.tile(_cmp_bias(cmp_pe[l].reshape(2, 1, CMP_BLOCK * HEAD_DIM), w1), (1, 1, N_KV_HEADS))
    return dict(
        wnorm=row(w_norm[l]), wtok=wtok, wfeat=wfeat, gcol=gcol,
        wdw=w_dw[l], bdw=row(b_dw[l]), lng=row(ln_g[l]), lnb=row(ln_b[l]),
        wpw=w_pw[l].astype(BF16), bpw=row(b_pw[l]),
        wpool=wpool_bd.astype(BF16), pscale=row(pool_scale[l]),
        w1ab=w1ab, cbias=cbias, w2p=w2p,
        wout=w_out[l].astype(BF16),
    )


def _rope_tables(pos):
    inv = ROPE_THETA ** (-jnp.arange(ROT_HALF, dtype=F32) * 2.0 / ROT_DIM)
    ang = pos.astype(F32)[:, None] * inv[None, :]
    return jnp.cos(ang).T, jnp.sin(ang).T


def _prompt_layer(l, depth, x, p, cos_t, sin_t, kv_prev):
    glu, b_in, sz, gates, q_t, *kv = _in_proj(
        x, p["wnorm"], p["wtok"], p["wfeat"], p["gcol"], cos_t, sin_t, min(512, x.shape[1]), l, depth, kv_prev)
    cmp_t, slc_t, win_t = kv
    kcv = _compress_prompt(cmp_t, l, p["w1ab"], p["cbias"], p["w2p"])
    o_attn = _attn_prompt(q_t, kcv, slc_t, win_t, gates, l)
    y = _mix_out_prompt(x, glu, b_in, sz, o_attn, p["wdw"], p["bdw"], p["lng"], p["lnb"], p["wpw"], p["bpw"],
                        p["wpool"], p["pscale"], p["wout"], t=min(512, x.shape[1]))
    return y, glu, b_in, kv


def _sample_layer(l, depth, x, p, cos_t, sin_t, sc, sp, cwin_t, ccmp_t, cslc_t, page_table, past, kv_prev, win_prev):
    nb = x.shape[0]
    glu, b_in, sz, gates, q_t, *kv = _in_proj(
        x[None], p["wnorm"], p["wtok"], p["wfeat"], p["gcol"], cos_t, sin_t, nb, l, depth, kv_prev)
    glu, b_in, sz, gates = glu[0], b_in[0], sz[0], gates[0]
    yab, sc_new, sp_new = _convpool_sample(sc, sp, glu, b_in, sz, p["wdw"], p["bdw"], p["lng"], p["lnb"],
                                           p["wpw"], p["bpw"], p["wpool"], p["pscale"], pos=past)
    new_t = jnp.concatenate([a[l, 0] for a in kv], axis=0)
    new_tok = new_t.T[:, None, :]
    q_tok = q_t[0].T[:, None, :]
    win_col = kv[2][l, 0].T[:, :, None]
    o_attn, win_next = _attn_sample(l, q_tok, new_tok, gates[:, None, :], win_col, cwin_t, ccmp_t, cslc_t,
                                    page_table, p["w1ab"], p["cbias"], p["w2p"], win_prev)
    y = _out_proj(x[None], yab[None], o_attn.reshape(1, nb, D_ATTN), sz[None], p["wout"], tm=nb)[0]
    return y, sc_new, sp_new, win_next, kv


def _rows_from_feat(a):
    lead = a.shape[:-2]
    a = a.reshape(lead + (2, N_KV_HEADS, HEAD_DIM, a.shape[-1]))
    nl = len(lead)
    return jnp.transpose(a, tuple(range(nl)) + (nl + 3, nl, nl + 1, nl + 2))


def kernel(x_prompt, x_sample, state_conv, state_pool, cache_win_kv, cache_cmp_kv, cache_slc_kv, page_table,
           w_norm, w_in, w_out, w_dw, b_dw, ln_g, ln_b, w_pw, b_pw, w_pool, pool_scale,
           g_q, g_k, cmp_pe, cmp_w1, cmp_w2):
    bp, s, _ = x_prompt.shape
    bs, t_new, _ = x_sample.shape
    depth = w_in.shape[0]
    assert t_new == 1 and cache_cmp_kv.shape[2] == PAGE and s % 512 == 0
    past = page_table.shape[1] * PAGE
    lb = cache_win_kv.shape[2]

    def to_feat(c):
        c = jnp.transpose(c, (0, 1, 3, 4, 5, 2))
        return c.reshape(c.shape[:3] + (GD, c.shape[-1]))
    cwin_t, ccmp_t, cslc_t = to_feat(cache_win_kv), to_feat(cache_cmp_kv), to_feat(cache_slc_kv)
    sc_all = jnp.transpose(state_conv, (0, 2, 1, 3))
    sp_all = jnp.transpose(state_pool, (0, 2, 1, 3))
    cos_p, sin_p = _rope_tables(jnp.arange(s))
    cos_s, sin_s = _rope_tables(jnp.full((bs,), past))

    xp, xs = x_prompt, x_sample[:, 0, :]
    conv_p, pool_p, conv_s, pool_s = [], [], [], []
    kv_p = kv_s = win_s = None
    for l in range(depth):
        p = _prep_layer(l, w_norm, w_in, w_out, w_dw, b_dw, ln_g, ln_b, w_pw, b_pw, w_pool, pool_scale,
                        g_q, g_k, cmp_pe, cmp_w1, cmp_w2)
        xp, glu, b_in, kv_p = _prompt_layer(l, depth, xp, p, cos_p, sin_p, kv_p)
        conv_p.append(glu[:, s - CONV_STATE:, :])
        pool_p.append(b_in[:, s - POOL_STATE:, :])
        xs, sc_new, sp_new, win_s, kv_s = _sample_layer(
            l, depth, xs, p, cos_s, sin_s, sc_all[l], sp_all[l], cwin_t, ccmp_t, cslc_t, page_table, past,
            kv_s, win_s)
        conv_s.append(jnp.transpose(sc_new, (1, 0, 2)))
        pool_s.append(jnp.transpose(sp_new, (1, 0, 2)))
    cmp_p, slc_p, win_p = kv_p
    cmp_s, slc_s, _ = kv_s
    new_rows = lambda a: _rows_from_feat(a[:, 0])[:, :, None]
    return (xp, xs[:, None, :], jnp.stack(conv_p), jnp.stack(pool_p),
            _rows_from_feat(win_p[:, :, :, s - min(WINDOW, s):]), _rows_from_feat(cmp_p), _rows_from_feat(slc_p),
            jnp.stack(conv_s), jnp.stack(pool_s), _rows_from_feat(win_s.reshape(depth, bs, D_ROWS, lb)),
            new_rows(cmp_s), new_rows(slc_s))
```

```python
import functools

import numpy as np
import jax
import jax.numpy as jnp
from jax import lax
from jax.experimental import pallas as pl
from jax.experimental.pallas import tpu as pltpu

D_MODEL = 1024
D_CONV = 256
D_POOL = 256
N_HEADS = 8
N_KV_HEADS = 2
HEAD_DIM = 64
D_ATTN = N_HEADS * HEAD_DIM
CONV_WIDTH = 31
CONV_STATE = CONV_WIDTH - 1
POOL_WINDOWS = (2, 4, 8, 16)
POOL_GROUP = D_POOL // len(POOL_WINDOWS)
POOL_STATE = max(POOL_WINDOWS) - 1
CMP_BLOCK = 32
CMP_STRIDE = 16
SLC_BLOCK = 64
N_SEL = 8
WINDOW = 256
Q_BLOCK = 128
ROT_DIM = HEAD_DIM // 4
ROT_HALF = ROT_DIM // 2
ROPE_THETA = 500000.0
N_BRANCH = 3
D_KV = N_BRANCH * 2 * N_KV_HEADS * HEAD_DIM
D_ROWS = 2 * N_KV_HEADS * HEAD_DIM
N_REP = N_HEADS // N_KV_HEADS
EPS = 1e-6
NEG = -1e30
BIG = 1e4

GATE_PAD = 128
D_TOK = 3 * D_CONV + 2 * D_POOL + D_ATTN + GATE_PAD
D_FEAT = D_ATTN + D_KV
N_FEAT_BLOCKS = D_FEAT // HEAD_DIM
SEL_KEYS = 512
VMEM_LIMIT = 56 * 1024 * 1024

F32 = jnp.float32
BF16 = jnp.bfloat16
NT_DIMS = (((1,), (1,)), ((), ()))


def _sigmoid(x):
    return 1.0 / (1.0 + jnp.exp(-x))


def _silu(x):
    return x * _sigmoid(x)


def _gelu_tanh(x):
    return 0.5 * x * (1.0 + jnp.tanh(np.sqrt(2.0 / np.pi).astype(np.float32) * (x + 0.044715 * (x * x * x))))


def _dot(a, b):
    return jnp.dot(a, b, preferred_element_type=F32)


def _dot_nt(a, b):
    return lax.dot_general(a, b, NT_DIMS, preferred_element_type=F32)


def _masked_softmax(s, mask):
    s = jnp.where(mask, s, NEG)
    m = jnp.max(s, axis=-1, keepdims=True)
    p = jnp.where(mask, jnp.exp(s - m), 0.0)
    return p / jnp.maximum(jnp.sum(p, axis=-1, keepdims=True), 1e-30)


def _in_proj_kernel(x_ref, wnorm_ref, wtok_ref, wfeat_ref, gcol_ref, cos_ref, sin_ref, *rest):
    glu_ref, bin_ref, sz_ref, gates_ref, qt_ref, cmp_ref, slc_ref, win_ref = rest[-8:]
    x = x_ref[...]
    ms = jnp.mean(x * x, axis=-1, keepdims=True)
    h = (x * lax.rsqrt(ms + EPS) * wnorm_ref[...]).astype(BF16)

    a = _dot(h, wtok_ref[...])
    o = 0
    glu_ref[...] = a[:, o:o + D_CONV] * _sigmoid(a[:, o + D_CONV:o + 2 * D_CONV])
    o += 2 * D_CONV
    bin_ref[...] = a[:, o:o + D_POOL]
    o += D_POOL
    z = a[:, o:o + D_CONV + D_POOL + D_ATTN]
    sz_ref[...] = _silu(z)
    o += D_CONV + D_POOL + D_ATTN
    gates_ref[...] = _sigmoid(a[:, o:o + GATE_PAD])

    f = _dot_nt(wfeat_ref[...], h)
    cos = cos_ref[...]
    sin = sin_ref[...]
    kv_refs = (cmp_ref, slc_ref, win_ref)
    for hb in range(N_FEAT_BLOCKS):
        blk = f[hb * HEAD_DIM:(hb + 1) * HEAD_DIM, :]
        kv_blk = hb - N_HEADS
        is_value = kv_blk >= 0 and (kv_blk % (2 * N_KV_HEADS)) >= N_KV_HEADS
        if not is_value:
            bms = jnp.mean(blk * blk, axis=0, keepdims=True)
            y = blk * lax.rsqrt(bms + EPS) * gcol_ref[hb * HEAD_DIM:(hb + 1) * HEAD_DIM, :]
            x1 = y[0:ROT_HALF]
            x2 = y[ROT_HALF:ROT_DIM]
            blk = jnp.concatenate([x1 * cos - x2 * sin, x2 * cos + x1 * sin, y[ROT_DIM:]], axis=0)
        if kv_blk < 0:
            qt_ref[hb * HEAD_DIM:(hb + 1) * HEAD_DIM, :] = blk
        else:
            r = kv_blk % (2 * N_KV_HEADS)
            kv_refs[kv_blk // (2 * N_KV_HEADS)][r * HEAD_DIM:(r + 1) * HEAD_DIM, :] = blk


def _in_proj(x, wnorm, wtok, wfeat, gcol, cos_t, sin_t, tm, l, depth, kv_prev):
    n, t, _ = x.shape
    grid = (n, t // tm)
    tok = lambda w: pl.BlockSpec((None, tm, w), lambda i, j: (i, j, 0))
    feat = lambda w: pl.BlockSpec((None, w, tm), lambda i, j: (i, 0, j))
    kv_spec = pl.BlockSpec((None, None, D_ROWS, tm), lambda i, j: (l, i, 0, j))
    full = lambda a: pl.BlockSpec(a.shape, lambda i, j: (0,) * a.ndim)
    tok_shape = lambda w: jax.ShapeDtypeStruct((n, t, w), F32)
    kv_shape = jax.ShapeDtypeStruct((depth, n, D_ROWS, t), F32)
    n_in = 7
    kv_prev = () if kv_prev is None else tuple(kv_prev)
    return pl.pallas_call(
        _in_proj_kernel,
        grid=grid,
        in_specs=[tok(D_MODEL), full(wnorm), full(wtok), full(wfeat), full(gcol),
                  pl.BlockSpec((ROT_HALF, tm), lambda i, j: (0, j)),
                  pl.BlockSpec((ROT_HALF, tm), lambda i, j: (0, j))]
                 + [pl.BlockSpec(memory_space=pl.ANY)] * len(kv_prev),
        out_specs=[tok(D_CONV), tok(D_POOL), tok(D_CONV + D_POOL + D_ATTN), tok(GATE_PAD),
                   feat(D_ATTN), kv_spec, kv_spec, kv_spec],
        out_shape=[tok_shape(D_CONV), tok_shape(D_POOL), tok_shape(D_CONV + D_POOL + D_ATTN), tok_shape(GATE_PAD),
                   jax.ShapeDtypeStruct((n, D_ATTN, t), F32), kv_shape, kv_shape, kv_shape],
        input_output_aliases={n_in + k: 5 + k for k in range(len(kv_prev))},
        compiler_params=pltpu.CompilerParams(dimension_semantics=("arbitrary", "arbitrary"),
                                             vmem_limit_bytes=VMEM_LIMIT),
        name="in_proj",
    )(x, wnorm, wtok, wfeat, gcol, cos_t, sin_t, *kv_prev)


def _conv_tail(acc, sz_a, bdw, lng, lnb, wpw, bpw):
    y = acc + bdw
    mu = jnp.mean(y, axis=-1, keepdims=True)
    yc = y - mu
    var = jnp.mean(yc * yc, axis=-1, keepdims=True)
    y = yc * lax.rsqrt(var + EPS) * lng + lnb
    y = _dot(_silu(y).astype(BF16), wpw) + bpw
    return y * sz_a


def _pool_tail(total, cnt, xcur, sz_b, wpool, pscale):
    d = total / cnt - xcur
    y = _dot(d.astype(BF16), wpool) * pscale
    return y * sz_b


def _pool_window_of_lane(shape):
    lane = lax.broadcasted_iota(jnp.int32, shape, len(shape) - 1)
    w = jnp.full(shape, POOL_WINDOWS[0], jnp.int32)
    for gi in range(1, len(POOL_WINDOWS)):
        w = jnp.where(lane >= gi * POOL_GROUP, POOL_WINDOWS[gi], w)
    return w


def _pool_select(sums, shape):
    lane = lax.broadcasted_iota(jnp.int32, shape, len(shape) - 1)
    total = sums[0]
    for gi in range(1, len(POOL_WINDOWS)):
        total = jnp.where(lane >= gi * POOL_GROUP, sums[gi], total)
    return total


CONV_HALO = 32
POOL_HALO = 16


MIX_ROWS = 64
SUBLANES = 8


def _conv_rows(cbuf, wdw_ref, r0, rows):
    off = CONV_HALO - CONV_STATE
    y = None
    for c in range(SUBLANES):
        part = None
        for k in range(CONV_WIDTH):
            if (k + off) % SUBLANES != c:
                continue
            base = r0 + k + off - c
            term = wdw_ref[k:k + 1, :] * cbuf[base:base + rows + SUBLANES, :]
            part = term if part is None else part + term
        if part is not None:
            y = part[c:c + rows] if y is None else y + part[c:c + rows]
    return y


def _mix_out_prompt_kernel(x_ref, glu_ref, gprev_ref, bin_ref, bprev_ref, sz_ref, oat_ref, wdw_ref, bdw_ref, lng_ref,
                           lnb_ref, wpw_ref, bpw_ref, wpool_ref, pscale_ref, wout_ref, o_ref, cbuf, pbuf, ymix, *, t):
    i = pl.program_id(1)
    keep = (i > 0).astype(F32)
    cbuf[0:CONV_HALO, :] = gprev_ref[...] * keep
    cbuf[CONV_HALO:CONV_HALO + t, :] = glu_ref[...]
    cbuf[CONV_HALO + t:CONV_HALO + t + SUBLANES, :] = jnp.zeros((SUBLANES, D_CONV), F32)
    pbuf[0:POOL_HALO, :] = bprev_ref[...] * keep
    pbuf[POOL_HALO:POOL_HALO + t, :] = bin_ref[...]
    nab = D_CONV + D_POOL
    for r0 in range(0, t, MIX_ROWS):
        rows = slice(r0, r0 + MIX_ROWS)
        sz = sz_ref[rows, :]
        y_a = _conv_tail(_conv_rows(cbuf, wdw_ref, r0, MIX_ROWS), sz[:, 0:D_CONV], bdw_ref[...], lng_ref[...],
                         lnb_ref[...], wpw_ref[...], bpw_ref[...])
        ymix[rows, 0:D_CONV] = y_a.astype(BF16)

        sums = []
        run = jnp.zeros((MIX_ROWS, D_POOL), F32)
        j = 0
        for w in POOL_WINDOWS:
            while j < w:
                run = run + pbuf[pl.ds(POOL_HALO + r0 - j, MIX_ROWS), :]
                j += 1
            sums.append(run)
        total = _pool_select(sums, (MIX_ROWS, D_POOL))
        pos = i * t + r0 + lax.broadcasted_iota(jnp.int32, (MIX_ROWS, D_POOL), 0)
        cnt = jnp.minimum(_pool_window_of_lane((MIX_ROWS, D_POOL)), pos + 1).astype(F32)
        y_b = _pool_tail(total, cnt, bin_ref[rows, :], sz[:, D_CONV:nab], wpool_ref[...], pscale_ref[...])
        ymix[rows, D_CONV:nab] = y_b.astype(BF16)
        ymix[rows, nab:nab + D_ATTN] = (oat_ref[rows, :] * sz[:, nab:nab + D_ATTN]).astype(BF16)
    o_ref[...] = x_ref[...] + _dot(ymix[...], wout_ref[...])


def _mix_out_prompt(x, glu, b_in, sz, o_attn, wdw, bdw, lng, lnb, wpw, bpw, wpool, pscale, wout, t):
    n, s, _ = glu.shape
    assert t % MIX_ROWS == 0 and s % t == 0
    grid = (n, s // t)
    cur = lambda w: pl.BlockSpec((None, t, w), lambda i, j: (i, j, 0))
    prev = lambda rows, w: pl.BlockSpec((None, rows, w), lambda i, j: (i, jnp.maximum(j * (t // rows) - 1, 0), 0))
    full = lambda a: pl.BlockSpec(a.shape, lambda i, j: (0,) * a.ndim)
    d_mix = D_CONV + D_POOL + D_ATTN
    return pl.pallas_call(
        functools.partial(_mix_out_prompt_kernel, t=t),
        grid=grid,
        in_specs=[cur(D_MODEL), cur(D_CONV), prev(CONV_HALO, D_CONV), cur(D_POOL), prev(POOL_HALO, D_POOL),
                  cur(d_mix), cur(D_ATTN),
                  full(wdw), full(bdw), full(lng), full(lnb), full(wpw), full(bpw), full(wpool), full(pscale),
                  full(wout)],
        out_specs=cur(D_MODEL),
        out_shape=jax.ShapeDtypeStruct(x.shape, F32),
        scratch_shapes=[pltpu.VMEM((CONV_HALO + t + SUBLANES, D_CONV), F32), pltpu.VMEM((POOL_HALO + t, D_POOL), F32),
                        pltpu.VMEM((t, d_mix), BF16)],
        compiler_params=pltpu.CompilerParams(dimension_semantics=("arbitrary", "arbitrary"),
                                             vmem_limit_bytes=VMEM_LIMIT),
        name="mix_out_prompt",
    )(x, glu, glu, b_in, b_in, sz, o_attn, wdw, bdw, lng, lnb, wpw, bpw, wpool, pscale, wout)


def _convpool_sample_kernel(sc_ref, sp_ref, glu_ref, bin_ref, sz_ref, wdw_ref, bdw_ref, lng_ref, lnb_ref,
                            wpw_ref, bpw_ref, wpool_ref, pscale_ref, y_ref, sc_out, sp_out, *, pos):
    glu = glu_ref[...]
    acc = wdw_ref[CONV_STATE:CONV_WIDTH, :] * glu
    for k in range(CONV_STATE):
        acc = acc + wdw_ref[k:k + 1, :] * sc_ref[k]
    sz = sz_ref[...]
    y_ref[:, 0:D_CONV] = _conv_tail(acc, sz[:, 0:D_CONV], bdw_ref[...], lng_ref[...], lnb_ref[...],
                                    wpw_ref[...], bpw_ref[...])
    for k in range(CONV_STATE - 1):
        sc_out[k] = sc_ref[k + 1]
    sc_out[CONV_STATE - 1] = glu

    xcur = bin_ref[...]
    shape = xcur.shape
    sums = []
    run = xcur
    j = 1
    for w in POOL_WINDOWS:
        while j < w:
            run = run + sp_ref[POOL_STATE - j]
            j += 1
        sums.append(run)
    total = _pool_select(sums, shape)
    cnt = jnp.minimum(_pool_window_of_lane(shape), pos + 1).astype(F32)
    y_ref[:, D_CONV:D_CONV + D_POOL] = _pool_tail(total, cnt, xcur, sz[:, D_CONV:D_CONV + D_POOL],
                                                  wpool_ref[...], pscale_ref[...])
    for k in range(POOL_STATE - 1):
        sp_out[k] = sp_ref[k + 1]
    sp_out[POOL_STATE - 1] = xcur


def _convpool_sample(sc, sp, glu, b_in, sz, wdw, bdw, lng, lnb, wpw, bpw, wpool, pscale, pos):
    nb = glu.shape[0]
    return pl.pallas_call(
        functools.partial(_convpool_sample_kernel, pos=pos),
        out_shape=[jax.ShapeDtypeStruct((nb, D_CONV + D_POOL), F32),
                   jax.ShapeDtypeStruct(sc.shape, F32), jax.ShapeDtypeStruct(sp.shape, F32)],
        compiler_params=pltpu.CompilerParams(vmem_limit_bytes=VMEM_LIMIT),
        name="convpool_sample",
    )(sc, sp, glu, b_in, sz[:, 0:D_CONV + D_POOL], wdw, bdw, lng, lnb, wpw, bpw, wpool, pscale)


def _out_proj_kernel(x_ref, yab_ref, oat_ref, szc_ref, w_ref, o_ref):
    ya = yab_ref[...].astype(BF16)
    yc = (oat_ref[...] * szc_ref[...]).astype(BF16)
    nab = D_CONV + D_POOL
    o_ref[...] = x_ref[...] + _dot(ya, w_ref[0:nab, :]) + _dot(yc, w_ref[nab:nab + D_ATTN, :])


def _out_proj(x, yab, o_attn, sz, w_out, tm):
    n, t, _ = x.shape
    nab = D_CONV + D_POOL
    assert nab == D_ATTN
    grid = (n, t // tm)
    tok = lambda w, cb=0: pl.BlockSpec((None, tm, w), lambda i, j: (i, j, cb))
    return pl.pallas_call(
        _out_proj_kernel,
        grid=grid,
        in_specs=[tok(D_MODEL), tok(nab), tok(D_ATTN), tok(D_ATTN, 1),
                  pl.BlockSpec(w_out.shape, lambda i, j: (0, 0))],
        out_specs=tok(D_MODEL),
        out_shape=jax.ShapeDtypeStruct(x.shape, F32),
        compiler_params=pltpu.CompilerParams(dimension_semantics=("arbitrary", "arbitrary"),
                                             vmem_limit_bytes=VMEM_LIMIT),
        name="out_proj",
    )(x, yab, o_attn, sz, w_out)


GD = N_KV_HEADS * HEAD_DIM
CMP_FLAT = CMP_STRIDE * GD


FLAT_SPAN = CMP_STRIDE * CMP_STRIDE


def _piece_perm():
    m = np.arange(FLAT_SPAN)
    src = (m % CMP_STRIDE) * CMP_STRIDE + m // CMP_STRIDE
    return jnp.asarray((src[:, None] == np.arange(FLAT_SPAN)[None, :]).astype(np.float32), BF16)


def _fill_flat(x_t, perm, flat_ref, piece0):
    y = _dot_nt(perm, x_t).astype(BF16)
    for kv in range(2):
        for r in range(CMP_STRIDE):
            flat_ref[kv, piece0:piece0 + CMP_STRIDE, r * GD:(r + 1) * GD] = (
                y[r * CMP_STRIDE:(r + 1) * CMP_STRIDE, kv * GD:(kv + 1) * GD])


def _compress_mlp(flat, w1ab, bias, w2):
    n_pc = flat.shape[0]
    hb = _dot(flat, w1ab)
    h = hb[:, 0:GD] + jnp.concatenate([hb[1:n_pc, GD:2 * GD], jnp.zeros((1, GD), F32)], axis=0) + bias
    out = _dot(_gelu_tanh(h).astype(BF16), w2)
    row = lax.broadcasted_iota(jnp.int32, out.shape, 0)
    return jnp.where(row < n_pc - 1, out, 0.0)


def _cmp_bias_kernel(pe_ref, w1_ref, o_ref):
    for kv in range(2):
        o_ref[kv] = _dot(pe_ref[kv].astype(BF16), w1_ref[kv])


def _cmp_bias(pe_flat, w1):
    return pl.pallas_call(_cmp_bias_kernel, out_shape=jax.ShapeDtypeStruct((2, 1, HEAD_DIM), F32),
                          name="cmp_bias")(pe_flat, w1)


def _compress_prompt_kernel(x_ref, perm_ref, w1ab_ref, bias_ref, w2_ref, o_ref, flat_ref, *, s):
    for t in range(s // FLAT_SPAN):
        _fill_flat(x_ref[:, t * FLAT_SPAN:(t + 1) * FLAT_SPAN].astype(BF16), perm_ref[...], flat_ref, t * CMP_STRIDE)
    for kv in range(2):
        o_ref[:, kv * GD:(kv + 1) * GD] = _compress_mlp(flat_ref[kv], w1ab_ref[kv], bias_ref[kv], w2_ref[kv])


def _compress_prompt(cmp_t, l, w1ab, bias, w2):
    _, n, _, s = cmp_t.shape
    assert s % FLAT_SPAN == 0
    n_pc = s // CMP_STRIDE
    perm = _piece_perm()
    full = lambda a: pl.BlockSpec(a.shape, lambda i: (0,) * a.ndim)
    return pl.pallas_call(
        functools.partial(_compress_prompt_kernel, s=s),
        grid=(n,),
        in_specs=[pl.BlockSpec((None, None, D_ROWS, s), lambda i: (l, i, 0, 0)),
                  full(perm), full(w1ab), full(bias), full(w2)],
        out_specs=pl.BlockSpec((None, n_pc, 2 * GD), lambda i: (i, 0, 0)),
        out_shape=jax.ShapeDtypeStruct((n, n_pc, 2 * GD), F32),
        scratch_shapes=[pltpu.VMEM((2, n_pc, CMP_FLAT), BF16)],
        compiler_params=pltpu.CompilerParams(dimension_semantics=("arbitrary",), vmem_limit_bytes=VMEM_LIMIT),
        name="compress_prompt",
    )(cmp_t, perm, w1ab, bias, w2)


def _select_blocks(imp_t, pos_row, n_blocks):
    shape = imp_t.shape
    assert n_blocks % 8 == 0
    j = lax.broadcasted_iota(jnp.int32, shape, 0)
    cur = pos_row // SLC_BLOCK
    forced = (j == 0) | (j == cur) | (j == cur - 1)
    score = jnp.where(j <= cur, jnp.where(forced, BIG, imp_t), -BIG)
    tiles = [score[8 * v:8 * v + 8] for v in range(n_blocks // 8)]
    j8 = lax.broadcasted_iota(jnp.int32, (8, shape[1]), 0)
    ranks = [jnp.zeros((8, shape[1]), F32) for _ in tiles]
    for jp in range(n_blocks):
        sj = score[jp:jp + 1, :]
        for v, tile in enumerate(tiles):
            if jp < 8 * v:
                beats = sj >= tile
            elif jp >= 8 * v + 8:
                beats = sj > tile
            else:
                beats = (sj > tile) | ((sj == tile) & (j8 > jp - 8 * v))
            ranks[v] = ranks[v] + jnp.where(beats, 1.0, 0.0)
    rank = jnp.concatenate(ranks, axis=0)
    return (rank < float(min(N_SEL, n_blocks))).astype(F32)


def _col_softmax(s):
    m = jnp.max(s, axis=0, keepdims=True)
    e = jnp.exp(s - m)
    inv = jnp.where(m > 0.5 * NEG, 1.0 / jnp.maximum(jnp.sum(e, axis=0, keepdims=True), 1e-30), 0.0)
    return e, inv


ONES_ROWS = 16


def _with_ones(v_t):
    return jnp.concatenate([v_t, jnp.ones((ONES_ROWS, v_t.shape[1]), BF16)], axis=0)


def _normalised(acc):
    return acc[0:HEAD_DIM] * (1.0 / jnp.maximum(acc[HEAD_DIM:HEAD_DIM + 1], 1e-30))


def _mask_heads(sc, ok, tq):
    return jnp.concatenate([jnp.where(ok, sc[:, r * tq:(r + 1) * tq], NEG) for r in range(N_REP)], axis=1)


def _attn_prompt_kernel(qt_ref, kcv_ref, slc_ref, win_ref, gates_ref, ovt_ref, o_ref,
                        ks_ref, kw_ref, s_ref, sel_ref, acc_ref, *, s, n_blk):
    tq = Q_BLOCK
    cols = N_REP * tq
    groups = range(N_KV_HEADS)
    qc = pl.program_id(1)
    q0 = qc * tq
    pos_row = q0 + lax.broadcasted_iota(jnp.int32, (1, tq), 1)
    n_slc = s // SLC_BLOCK
    n_cmp = kcv_ref.shape[0]
    scale = HEAD_DIM ** -0.5
    blocks_per_step = SEL_KEYS // SLC_BLOCK
    v_row0 = lambda g: (N_KV_HEADS + g) * HEAD_DIM

    @pl.when(qc == 0)
    def _():
        for g in range(N_KV_HEADS):
            for t in range(s // 128):
                cols_t = slice(t * 128, (t + 1) * 128)
                ks_ref[g, cols_t, :] = slc_ref[g * HEAD_DIM:(g + 1) * HEAD_DIM, cols_t].T.astype(BF16)
                kw_ref[g, cols_t, :] = win_ref[g * HEAD_DIM:(g + 1) * HEAD_DIM, cols_t].T.astype(BF16)

    gates_t = gates_ref[...].T
    vc_t = kcv_ref[:, GD:2 * GD].T.astype(BF16)
    q_ts = []
    for g in groups:
        q_t = jnp.concatenate([qt_ref[(g * N_REP + r) * HEAD_DIM:(g * N_REP + r + 1) * HEAD_DIM, :]
                               for r in range(N_REP)], axis=1)
        q_ts.append((q_t * scale).astype(BF16))

    c_idx = lax.broadcasted_iota(jnp.int32, (n_cmp, 1), 0)
    valid_c = (c_idx * CMP_STRIDE + (CMP_BLOCK - 1) <= pos_row) & (c_idx < n_blk)
    ovt = ovt_ref[...]
    o_cs = []
    for g in groups:
        kc = kcv_ref[:, g * HEAD_DIM:(g + 1) * HEAD_DIM].astype(BF16)
        e_c, inv_c = _col_softmax(_mask_heads(_dot(kc, q_ts[g]), valid_c, tq))
        o_cs.append(_dot(vc_t[g * HEAD_DIM:(g + 1) * HEAD_DIM, :], e_c.astype(BF16)) * inv_c)
        pc_sum = e_c[:, 0:tq] * inv_c[:, 0:tq]
        for r in range(1, N_REP):
            pc_sum = pc_sum + e_c[:, r * tq:(r + 1) * tq] * inv_c[:, r * tq:(r + 1) * tq]
        pc_hi = pc_sum.astype(BF16)
        pc_lo = (pc_sum - pc_hi.astype(F32)).astype(BF16)
        imp_t = _dot(ovt, pc_hi) + _dot(ovt, pc_lo)
        sel_ref[g] = _select_blocks(imp_t, pos_row, n_slc)

    def score_step(i, m_run):
        k0 = pl.multiple_of(i * SEL_KEYS, SEL_KEYS)
        key = k0 + lax.broadcasted_iota(jnp.int32, (SEL_KEYS, 1), 0)
        causal = key <= pos_row
        m_new = []
        for g in groups:
            sc = _dot(ks_ref[g, pl.ds(k0, SEL_KEYS), :], q_ts[g])
            parts = []
            for jj in range(blocks_per_step):
                rows = slice(jj * SLC_BLOCK, (jj + 1) * SLC_BLOCK)
                chosen = sel_ref[g, pl.ds(i * blocks_per_step + jj, 1), :] > 0.5
                parts.append(_mask_heads(sc[rows], chosen & causal[rows], tq))
            sc = jnp.concatenate(parts, axis=0)
            s_ref[g, pl.ds(k0, SEL_KEYS), :] = sc.astype(BF16)
            m_new.append(jnp.maximum(m_run[g], jnp.max(sc, axis=0, keepdims=True)))
        return tuple(m_new)

    n_steps = (q0 + tq + SEL_KEYS - 1) // SEL_KEYS
    m_sel = lax.fori_loop(0, n_steps, score_step, tuple(jnp.full((1, cols), NEG, F32) for _ in groups))
    m_sel = [m.astype(BF16) for m in m_sel]

    acc_ref[...] = jnp.zeros(acc_ref.shape, F32)

    def value_step(i, carry):
        k0 = pl.multiple_of(i * SEL_KEYS, SEL_KEYS)
        for g in groups:
            p = jnp.exp(s_ref[g, pl.ds(k0, SEL_KEYS), :] - m_sel[g])
            v_t = slc_ref[v_row0(g):v_row0(g) + HEAD_DIM, pl.ds(k0, SEL_KEYS)].astype(BF16)
            acc_ref[g] += _dot(_with_ones(v_t), p)
        return carry

    lax.fori_loop(0, n_steps, value_step, 0)

    n_prev = -(-(WINDOW - 1) // tq)
    k0s, oks = [], []
    for b in range(n_prev + 1):
        kb = qc - n_prev + b
        k0s.append(pl.multiple_of(jnp.maximum(kb, 0) * tq, tq))
        kp = kb * tq + lax.broadcasted_iota(jnp.int32, (tq, 1), 0)
        oks.append((kp <= pos_row) & (kp > pos_row - WINDOW) & (kp >= 0))

    out_blocks = []
    for g in groups:
        o_s = _normalised(acc_ref[g])
        s_w = jnp.concatenate([_mask_heads(_dot(kw_ref[g, pl.ds(k0s[b], tq), :], q_ts[g]), oks[b], tq)
                               for b in range(n_prev + 1)], axis=0)
        e_w = jnp.exp(s_w.astype(BF16) - jnp.max(s_w, axis=0, keepdims=True).astype(BF16))
        o_w = _dot(_with_ones(win_ref[v_row0(g):v_row0(g) + HEAD_DIM, pl.ds(k0s[0], tq)].astype(BF16)), e_w[0:tq])
        for b in range(1, n_prev + 1):
            o_w = o_w + _dot(_with_ones(win_ref[v_row0(g):v_row0(g) + HEAD_DIM, pl.ds(k0s[b], tq)].astype(BF16)),
                             e_w[b * tq:(b + 1) * tq])
        o_w = _normalised(o_w)
        for r in range(N_REP):
            h = g * N_REP + r
            c = slice(r * tq, (r + 1) * tq)
            out_blocks.append(gates_t[h:h + 1] * o_cs[g][:, c] + gates_t[N_HEADS + h:N_HEADS + h + 1] * o_s[:, c]
                              + gates_t[2 * N_HEADS + h:2 * N_HEADS + h + 1] * o_w[:, c])
    o_ref[...] = jnp.concatenate(out_blocks, axis=0).T


def _overlap_t(n_cmp_rows, n_blk, n_slc):
    c = np.arange(n_cmp_rows)
    start = c * CMP_STRIDE
    end = start + CMP_BLOCK - 1
    j0 = np.arange(n_slc) * SLC_BLOCK
    ov = (end[None, :] >= j0[:, None]) & (start[None, :] < j0[:, None] + SLC_BLOCK) & (c[None, :] < n_blk)
    return jnp.asarray(ov.astype(np.float32), BF16)


def _attn_prompt(q_t, kcv, slc_t, win_t, gates, l):
    n, _, s = q_t.shape
    n_cmp = kcv.shape[1]
    n_blk = s // CMP_STRIDE - CMP_BLOCK // CMP_STRIDE + 1
    n_slc = s // SLC_BLOCK
    ovt = _overlap_t(n_cmp, n_blk, n_slc)
    tq = Q_BLOCK
    cols = N_REP * tq
    assert s % SEL_KEYS == 0 and SEL_KEYS % SLC_BLOCK == 0
    return pl.pallas_call(
        functools.partial(_attn_prompt_kernel, s=s, n_blk=n_blk),
        grid=(n, s // tq),
        in_specs=[pl.BlockSpec((None, D_ATTN, tq), lambda i, j: (i, 0, j)),
                  pl.BlockSpec((None,) + kcv.shape[1:], lambda i, j: (i, 0, 0)),
                  pl.BlockSpec((None, None, D_ROWS, s), lambda i, j: (l, i, 0, 0)),
                  pl.BlockSpec((None, None, D_ROWS, s), lambda i, j: (l, i, 0, 0)),
                  pl.BlockSpec((None, tq, GATE_PAD), lambda i, j: (i, j, 0)),
                  pl.BlockSpec(ovt.shape, lambda i, j: (0, 0))],
        out_specs=pl.BlockSpec((None, tq, D_ATTN), lambda i, j: (i, j, 0)),
        out_shape=jax.ShapeDtypeStruct((n, s, D_ATTN), F32),
        scratch_shapes=[pltpu.VMEM((N_KV_HEADS, s, HEAD_DIM), BF16),
                        pltpu.VMEM((N_KV_HEADS, s, HEAD_DIM), BF16),
                        pltpu.VMEM((N_KV_HEADS, s, cols), BF16),
                        pltpu.VMEM((N_KV_HEADS, n_slc, tq), F32),
                        pltpu.VMEM((N_KV_HEADS, HEAD_DIM + ONES_ROWS, cols), F32)],
        compiler_params=pltpu.CompilerParams(dimension_semantics=("arbitrary", "arbitrary"),
                                             vmem_limit_bytes=VMEM_LIMIT),
        name="attn_prompt",
    )(q_t, kcv, slc_t, win_t, gates, ovt)


PAGE = 128
SAMPLE_SEQS = 4


def _sample_geometry(past):
    length = past + 1
    padded = -(-length // SLC_BLOCK) * SLC_BLOCK
    n_pc = padded // CMP_STRIDE
    n_pc_pad = -(-n_pc // CMP_STRIDE) * CMP_STRIDE
    return dict(n_pc=n_pc, n_pc_pad=n_pc_pad, n_blk=n_pc - CMP_BLOCK // CMP_STRIDE + 1, n_slc=padded // SLC_BLOCK)


def _attn_sample_kernel(pt_ref, q_ref, new_ref, gates_ref, wincol_ref, cwin_ref, ov_ref, bok_ref,
                        perm_ref, w1ab_ref, bias_ref, w2_ref, ccmp_hbm, cslc_hbm, *rest, l, past, n_pages, nseq):
    o_ref, wout_ref, flat_ref, cmp_buf, slc_buf, sem = rest[-6:]
    step = pl.program_id(0)
    slot = step % 2

    def page_copies(of_step, into_slot, lookup):
        copies = []
        for u in range(nseq):
            for j in range(n_pages):
                page = pt_ref[of_step * nseq + u, j] if lookup else 0
                k = u * n_pages + j
                copies.append(pltpu.make_async_copy(ccmp_hbm.at[l, page], cmp_buf.at[into_slot, k], sem.at[into_slot, 0]))
                copies.append(pltpu.make_async_copy(cslc_hbm.at[l, page], slc_buf.at[into_slot, k], sem.at[into_slot, 1]))
        return copies

    @pl.when(step == 0)
    def _():
        for c in page_copies(0, 0, True):
            c.start()

    @pl.when(step + 1 < pl.num_programs(0))
    def _():
        for c in page_copies(step + 1, 1 - slot, True):
            c.start()

    for c in page_copies(step, slot, False):
        c.wait()
    cmp_pages = [[cmp_buf.at[slot, u * n_pages + j] for j in range(n_pages)] for u in range(nseq)]
    slc_pages = [[slc_buf.at[slot, u * n_pages + j] for j in range(n_pages)] for u in range(nseq)]
    geo = _sample_geometry(past)
    n_pc_pad, n_blk, n_slc = geo["n_pc_pad"], geo["n_blk"], geo["n_slc"]
    pos = past
    lb = cwin_ref.shape[-1]
    scale = HEAD_DIM ** -0.5
    head = lax.broadcasted_iota(jnp.int32, (N_HEADS, 1), 0)
    lane = lax.broadcasted_iota(jnp.int32, (1, GD), 1)
    lane_h = lax.broadcasted_iota(jnp.int32, (N_HEADS, GD), 1)
    own = (lane_h // HEAD_DIM) == (head // N_REP)

    def new_row(u, branch, kv):
        o = (branch * 2 + kv) * GD
        return new_ref[u, :, o:o + GD]

    def rounded(x):
        return x.astype(BF16).astype(F32)

    kcv = []
    pages_per_span = FLAT_SPAN // PAGE
    past_pc = past // CMP_STRIDE
    for u in range(nseq):
        row0 = u * n_pc_pad
        for t in range(n_pages // pages_per_span):
            x_t = jnp.concatenate([cmp_pages[u][t * pages_per_span + w][...].reshape(2 * GD, PAGE)
                                   for w in range(pages_per_span)], axis=1)
            _fill_flat(x_t.astype(BF16), perm_ref[...], flat_ref, row0 + t * CMP_STRIDE)
        for kv in range(2):
            flat_ref[kv, row0 + past_pc:row0 + n_pc_pad, :] = jnp.zeros((n_pc_pad - past_pc, CMP_FLAT), BF16)
            flat_ref[kv, row0 + past_pc:row0 + past_pc + 1, 0:GD] = new_row(u, 0, kv).astype(BF16)
    for kv in range(2):
        kcv.append(_compress_mlp(flat_ref[kv], w1ab_ref[kv], bias_ref[kv], w2_ref[kv]).astype(BF16))
    for u in range(nseq):
        _attn_sample_one(u, kcv[0][u * n_pc_pad:(u + 1) * n_pc_pad], kcv[1][u * n_pc_pad:(u + 1) * n_pc_pad],
                         q_ref, new_row, rounded, gates_ref, wincol_ref, cwin_ref, ov_ref, bok_ref,
                         slc_pages[u], o_ref, wout_ref, geo=geo, past=past, consts=(head, lane, lane_h, own))


def _attn_sample_one(u, kc, vc, q_ref, new_row, rounded, gates_ref, wincol_ref, cwin_ref, ov_ref, bok_ref,
                     slc_pages, o_ref, wout_ref, *, geo, past, consts):
    head, lane, lane_h, own = consts
    n_pc_pad, n_blk, n_slc = geo["n_pc_pad"], geo["n_blk"], geo["n_slc"]
    n_pages = len(slc_pages)
    pos = past
    lb = cwin_ref.shape[-1]
    scale = HEAD_DIM ** -0.5
    q_rows = jnp.concatenate([q_ref[u, :, h * HEAD_DIM:(h + 1) * HEAD_DIM] for h in range(N_HEADS)], axis=0)
    qx = (jnp.where(own, jnp.concatenate([q_rows] * N_KV_HEADS, axis=1), 0.0) * scale).astype(BF16)
    qx32 = qx.astype(F32)
    c_idx = lax.broadcasted_iota(jnp.int32, (1, n_pc_pad), 1)
    valid = (c_idx * CMP_STRIDE + (CMP_BLOCK - 1) <= pos) & (c_idx < n_blk)
    p_c = _masked_softmax(_dot_nt(qx, kc), valid)
    o_c = _dot(p_c.astype(BF16), vc)

    cur = pos // SLC_BLOCK
    forced = (lane == 0) | (lane == cur) | (lane == cur - 1)
    jp = lax.broadcasted_iota(jnp.int32, (GD, GD), 0)
    jj = lax.broadcasted_iota(jnp.int32, (GD, GD), 1)
    sels = []
    for g in range(N_KV_HEADS):
        pc_sum = jnp.sum(p_c[g * N_REP:(g + 1) * N_REP], axis=0, keepdims=True)
        pc_hi = pc_sum.astype(BF16)
        pc_lo = (pc_sum - pc_hi.astype(F32)).astype(BF16)
        imp = _dot(pc_hi, ov_ref[...]) + _dot(pc_lo, ov_ref[...])
        score = jnp.where(lane <= cur, jnp.where(forced, BIG, imp), -BIG)
        score = jnp.where(lane < n_slc, score, -2.0 * BIG)
        score_b = jnp.broadcast_to(score, (GD, GD))
        score_a = score_b.T
        beats = (score_a > score_b) | ((score_a == score_b) & (jp < jj))
        rank = jnp.sum(beats.astype(F32), axis=0, keepdims=True)
        sel = ((rank < float(min(N_SEL, n_slc))) & (lane < n_slc)).astype(F32)
        sels.append(jnp.broadcast_to(sel, (N_REP, GD)))
    sel_h = jnp.concatenate(sels, axis=0)

    chosen = _dot(sel_h.astype(BF16), bok_ref[...]) > 0.5
    s_past = jnp.concatenate([_dot(qx, slc_pages[j][0].astype(BF16)) for j in range(n_pages)], axis=-1)
    s_past = jnp.where(chosen, s_past, NEG)
    new_ok = jnp.sum(jnp.where(lane == pos // SLC_BLOCK, sel_h, 0.0), axis=-1, keepdims=True) > 0.5
    s_new = jnp.where(new_ok, jnp.sum(qx32 * rounded(new_row(u, 1, 0)), axis=-1, keepdims=True), NEG)
    m = jnp.maximum(jnp.max(s_past, axis=-1, keepdims=True), s_new)
    p_past = jnp.where(chosen, jnp.exp(s_past - m), 0.0)
    p_new = jnp.where(new_ok, jnp.exp(s_new - m), 0.0)
    denom = jnp.maximum(jnp.sum(p_past, axis=-1, keepdims=True) + p_new, 1e-30)
    o_s = rounded(p_new) * rounded(new_row(u, 1, 1))
    for j in range(n_pages):
        o_s = o_s + _dot_nt(p_past[:, j * PAGE:(j + 1) * PAGE].astype(BF16), slc_pages[j][1].astype(BF16))
    o_s = o_s / denom

    kpos = (past - lb) + lax.broadcasted_iota(jnp.int32, (1, lb), 1)
    w_ok = (kpos <= pos) & (kpos > pos - WINDOW)
    s_w = jnp.where(w_ok, _dot(qx, cwin_ref[u, 0].astype(BF16)), NEG)
    s_wn = jnp.sum(qx32 * rounded(new_row(u, 2, 0)), axis=-1, keepdims=True)
    m = jnp.maximum(jnp.max(s_w, axis=-1, keepdims=True), s_wn)
    p_w = jnp.where(w_ok, jnp.exp(s_w - m), 0.0)
    p_wn = jnp.exp(s_wn - m)
    denom = jnp.maximum(jnp.sum(p_w, axis=-1, keepdims=True) + p_wn, 1e-30)
    o_w = (_dot_nt(p_w.astype(BF16), cwin_ref[u, 1].astype(BF16))
           + rounded(p_wn) * rounded(new_row(u, 2, 1))) / denom

    gates = gates_ref[u]
    gate = lambda br: jnp.sum(jnp.where(lane_h == head + br * N_HEADS, gates, 0.0), axis=-1, keepdims=True)
    o = gate(0) * o_c + gate(1) * o_s + gate(2) * o_w
    for h in range(N_HEADS):
        g = h // N_REP
        o_ref[u, :, h * HEAD_DIM:(h + 1) * HEAD_DIM] = o[h:h + 1, g * HEAD_DIM:(g + 1) * HEAD_DIM]

    for kv in range(2):
        wout_ref[u, kv, :, 0:lb - 1] = cwin_ref[u, kv, :, 1:lb]
        wout_ref[u, kv, :, lb - 1:lb] = wincol_ref[u, kv * GD:(kv + 1) * GD, :]


def _attn_sample(l, q_tok, new_tok, gates, win_col, cwin_t, ccmp_t, cslc_t, page_table, w1ab, bias, w2, win_prev):
    nb = q_tok.shape[0]
    n_pages = page_table.shape[1]
    past = n_pages * PAGE
    geo = _sample_geometry(past)
    lb = cwin_t.shape[-1]
    assert GATE_PAD == GD and FLAT_SPAN % PAGE == 0 and n_pages % (FLAT_SPAN // PAGE) == 0
    perm = _piece_perm()
    ov = _overlap_t(geo["n_pc_pad"], geo["n_blk"], GD).T
    ov = jnp.where(jnp.arange(GD)[None, :] < geo["n_slc"], ov, 0).astype(BF16)
    bok = jnp.asarray((np.arange(past)[None, :] // SLC_BLOCK == np.arange(GD)[:, None]).astype(np.float32), BF16)
    nseq = SAMPLE_SEQS if nb % SAMPLE_SEQS == 0 else 1
    seq = lambda a: pl.BlockSpec((nseq,) + a.shape[1:], lambda i, pt: (i,) + (0,) * (a.ndim - 1))
    full = lambda a: pl.BlockSpec(a.shape, lambda i, pt: (0,) * a.ndim)
    operands = [page_table, q_tok, new_tok, gates, win_col, cwin_t, ov, bok, perm, w1ab, bias, w2, ccmp_t, cslc_t]
    win_prev = () if win_prev is None else (win_prev,)
    page_buf = pltpu.VMEM((2, nseq * n_pages, 2, GD, PAGE), F32)
    grid_spec = pltpu.PrefetchScalarGridSpec(
        num_scalar_prefetch=1,
        grid=(nb // nseq,),
        in_specs=[seq(q_tok), seq(new_tok), seq(gates), seq(win_col),
                  pl.BlockSpec((None, nseq, 2, GD, lb), lambda i, pt: (l, i, 0, 0, 0)),
                  full(ov), full(bok), full(perm), full(w1ab), full(bias), full(w2)]
                 + [pl.BlockSpec(memory_space=pl.ANY)] * (2 + len(win_prev)),
        out_specs=[pl.BlockSpec((nseq, 1, D_ATTN), lambda i, pt: (i, 0, 0)),
                   pl.BlockSpec((None, nseq, 2, GD, lb), lambda i, pt: (l, i, 0, 0, 0))],
        scratch_shapes=[pltpu.VMEM((2, nseq * geo["n_pc_pad"], CMP_FLAT), BF16), page_buf, page_buf,
                        pltpu.SemaphoreType.DMA((2, 2))],
    )
    return pl.pallas_call(
        functools.partial(_attn_sample_kernel, l=l, past=past, n_pages=n_pages, nseq=nseq),
        grid_spec=grid_spec,
        out_shape=[jax.ShapeDtypeStruct((nb, 1, D_ATTN), F32),
                   jax.ShapeDtypeStruct((cwin_t.shape[0], nb, 2, GD, lb), F32)],
        input_output_aliases={len(operands): 1} if win_prev else {},
        compiler_params=pltpu.CompilerParams(dimension_semantics=("arbitrary",), vmem_limit_bytes=VMEM_LIMIT),
        name="attn_sample",
    )(*operands, *win_prev)


def _prep_layer(l, w_norm, w_in, w_out, w_dw, b_dw, ln_g, ln_b, w_pw, b_pw, w_pool, pool_scale,
                g_q, g_k, cmp_pe, cmp_w1, cmp_w2):
    sizes = (D_CONV, D_CONV, D_CONV, D_POOL, D_POOL, D_ATTN, D_KV, N_BRANCH * N_HEADS, D_ATTN)
    offs = [0] + [int(v) for v in np.cumsum(sizes)]
    col = lambda k: w_in[l][:, offs[k]:offs[k + 1]]
    a_val, a_gate, z_a, b_in, z_b, q, kv, gate, z_c = (col(k) for k in range(9))
    pad = jnp.zeros((D_MODEL, GATE_PAD - N_BRANCH * N_HEADS), F32)
    wtok = jnp.concatenate([a_val, a_gate, b_in, z_a, z_b, z_c, gate, pad], axis=1).astype(BF16)
    wfeat = jnp.concatenate([q, kv], axis=1).T.astype(BF16)
    ones = jnp.ones((N_KV_HEADS * HEAD_DIM,), F32)
    gcol = jnp.concatenate([jnp.tile(g_q[l], N_HEADS)]
                           + [piece for br in range(N_BRANCH)
                              for piece in (jnp.tile(g_k[l, br], N_KV_HEADS), ones)])[:, None]
    wpool_bd = jnp.zeros((D_POOL, D_POOL), F32)
    for gi in range(len(POOL_WINDOWS)):
        wpool_bd = wpool_bd.at[gi * POOL_GROUP:(gi + 1) * POOL_GROUP, gi * POOL_GROUP:(gi + 1) * POOL_GROUP].set(w_pool[l, gi])
    row = lambda v: v[None, :]

    def both_groups(w):
        z = jnp.zeros_like(w)
        return jnp.stack([jnp.concatenate([w, z], axis=2), jnp.concatenate([z, w], axis=2)], axis=1).reshape(-1, GD)

    half = CMP_STRIDE * HEAD_DIM
    w1 = cmp_w1[l].astype(BF16)
    w1ab = jnp.stack([jnp.concatenate(
        [both_groups(w1[kv, :half].reshape(CMP_STRIDE, HEAD_DIM, HEAD_DIM)),
         both_groups(w1[kv, half:].reshape(CMP_STRIDE, HEAD_DIM, HEAD_DIM))], axis=1) for kv in range(2)])
    w2p = jnp.stack([both_groups(cmp_w2[l, kv].astype(BF16)[None]) for kv in range(2)])
    cbias = jnp.tile(_cmp_bias(cmp_pe[l].reshape(2, 1, CMP_BLOCK * HEAD_DIM), w1), (1, 1, N_KV_HEADS))
    return dict(
        wnorm=row(w_norm[l]), wtok=wtok, wfeat=wfeat, gcol=gcol,
        wdw=w_dw[l], bdw=row(b_dw[l]), lng=row(ln_g[l]), lnb=row(ln_b[l]),
        wpw=w_pw[l].astype(BF16), bpw=row(b_pw[l]),
        wpool=wpool_bd.astype(BF16), pscale=row(pool_scale[l]),
        w1ab=w1ab, cbias=cbias, w2p=w2p,
        wout=w_out[l].astype(BF16),
    )


def _rope_tables(pos):
    inv = ROPE_THETA ** (-jnp.arange(ROT_HALF, dtype=F32) * 2.0 / ROT_DIM)
    ang = pos.astype(F32)[:, None] * inv[None, :]
    return jnp.cos(ang).T, jnp.sin(ang).T


def _prompt_layer(l, depth, x, p, cos_t, sin_t, kv_prev):
    glu, b_in, sz, gates, q_t, *kv = _in_proj(
        x, p["wnorm"], p["wtok"], p["wfeat"], p["gcol"], cos_t, sin_t, min(512, x.shape[1]), l, depth, kv_prev)
    cmp_t, slc_t, win_t = kv
    kcv = _compress_prompt(cmp_t, l, p["w1ab"], p["cbias"], p["w2p"])
    o_attn = _attn_prompt(q_t, kcv, slc_t, win_t, gates, l)
    y = _mix_out_prompt(x, glu, b_in, sz, o_attn, p["wdw"], p["bdw"], p["lng"], p["lnb"], p["wpw"], p["bpw"],
                        p["wpool"], p["pscale"], p["wout"], t=min(512, x.shape[1]))
    return y, glu, b_in, kv


def _sample_layer(l, depth, x, p, cos_t, sin_t, sc, sp, cwin_t, ccmp_t, cslc_t, page_table, past, kv_prev, win_prev):
    nb = x.shape[0]
    glu, b_in, sz, gates, q_t, *kv = _in_proj(
        x[None], p["wnorm"], p["wtok"], p["wfeat"], p["gcol"], cos_t, sin_t, nb, l, depth, kv_prev)
    glu, b_in, sz, gates = glu[0], b_in[0], sz[0], gates[0]
    yab, sc_new, sp_new = _convpool_sample(sc, sp, glu, b_in, sz, p["wdw"], p["bdw"], p["lng"], p["lnb"],
                                           p["wpw"], p["bpw"], p["wpool"], p["pscale"], pos=past)
    new_t = jnp.concatenate([a[l, 0] for a in kv], axis=0)
    new_tok = new_t.T[:, None, :]
    q_tok = q_t[0].T[:, None, :]
    win_col = kv[2][l, 0].T[:, :, None]
    o_attn, win_next = _attn_sample(l, q_tok, new_tok, gates[:, None, :], win_col, cwin_t, ccmp_t, cslc_t,
                                    page_table, p["w1ab"], p["cbias"], p["w2p"], win_prev)
    y = _out_proj(x[None], yab[None], o_attn.reshape(1, nb, D_ATTN), sz[None], p["wout"], tm=nb)[0]
    return y, sc_new, sp_new, win_next, kv


def _rows_from_feat(a):
    lead = a.shape[:-2]
    a = a.reshape(lead + (2, N_KV_HEADS, HEAD_DIM, a.shape[-1]))
    nl = len(lead)
    return jnp.transpose(a, tuple(range(nl)) + (nl + 3, nl, nl + 1, nl + 2))


def kernel(x_prompt, x_sample, state_conv, state_pool, cache_win_kv, cache_cmp_kv, cache_slc_kv, page_table,
           w_norm, w_in, w_out, w_dw, b_dw, ln_g, ln_b, w_pw, b_pw, w_pool, pool_scale,
           g_q, g_k, cmp_pe, cmp_w1, cmp_w2):
    bp, s, _ = x_prompt.shape
    bs, t_new, _ = x_sample.shape
    depth = w_in.shape[0]
    assert t_new == 1 and cache_cmp_kv.shape[2] == PAGE and s % 512 == 0
    past = page_table.shape[1] * PAGE
    lb = cache_win_kv.shape[2]

    def to_feat(c):
        c = jnp.transpose(c, (0, 1, 3, 4, 5, 2))
        return c.reshape(c.shape[:3] + (GD, c.shape[-1]))
    cwin_t, ccmp_t, cslc_t = to_feat(cache_win_kv), to_feat(cache_cmp_kv), to_feat(cache_slc_kv)
    sc_all = jnp.transpose(state_conv, (0, 2, 1, 3))
    sp_all = jnp.transpose(state_pool, (0, 2, 1, 3))
    cos_p, sin_p = _rope_tables(jnp.arange(s))
    cos_s, sin_s = _rope_tables(jnp.full((bs,), past))

    xp, xs = x_prompt, x_sample[:, 0, :]
    conv_p, pool_p, conv_s, pool_s = [], [], [], []
    kv_p = kv_s = win_s = None
    for l in range(depth):
        p = _prep_layer(l, w_norm, w_in, w_out, w_dw, b_dw, ln_g, ln_b, w_pw, b_pw, w_pool, pool_scale,
                        g_q, g_k, cmp_pe, cmp_w1, cmp_w2)
        xp, glu, b_in, kv_p = _prompt_layer(l, depth, xp, p, cos_p, sin_p, kv_p)
        conv_p.append(glu[:, s - CONV_STATE:, :])
        pool_p.append(b_in[:, s - POOL_STATE:, :])
        xs, sc_new, sp_new, win_s, kv_s = _sample_layer(
            l, depth, xs, p, cos_s, sin_s, sc_all[l], sp_all[l], cwin_t, ccmp_t, cslc_t, page_table, past,
            kv_s, win_s)
        conv_s.append(jnp.transpose(sc_new, (1, 0, 2)))
        pool_s.append(jnp.transpose(sp_new, (1, 0, 2)))
    cmp_p, slc_p, win_p = kv_p
    cmp_s, slc_s, _ = kv_s
    new_rows = lambda a: _rows_from_feat(a[:, 0])[:, :, None]
    return (xp, xs[:, None, :], jnp.stack(conv_p), jnp.stack(pool_p),
            _rows_from_feat(win_p[:, :, :, s - min(WINDOW, s):]), _rows_from_feat(cmp_p), _rows_from_feat(slc_p),
            jnp.stack(conv_s), jnp.stack(pool_s), _rows_from_feat(win_s.reshape(depth, bs, D_ROWS, lb)),
            new_rows(cmp_s), new_rows(slc_s))
```

```python
import functools

import numpy as np
import jax
import jax.numpy as jnp
from jax import lax
from jax.experimental import pallas as pl
from jax.experimental.pallas import tpu as pltpu

D_MODEL = 1024
D_CONV = 256
D_POOL = 256
N_HEADS = 8
N_KV_HEADS = 2
HEAD_DIM = 64
D_ATTN = N_HEADS * HEAD_DIM
CONV_WIDTH = 31
CONV_STATE = CONV_WIDTH - 1
POOL_WINDOWS = (2, 4, 8, 16)
POOL_GROUP = D_POOL // len(POOL_WINDOWS)
POOL_STATE = max(POOL_WINDOWS) - 1
CMP_BLOCK = 32
CMP_STRIDE = 16
SLC_BLOCK = 64
N_SEL = 8
WINDOW = 256
Q_BLOCK = 128
ROT_DIM = HEAD_DIM // 4
ROT_HALF = ROT_DIM // 2
ROPE_THETA = 500000.0
N_BRANCH = 3
D_KV = N_BRANCH * 2 * N_KV_HEADS * HEAD_DIM
D_ROWS = 2 * N_KV_HEADS * HEAD_DIM
N_REP = N_HEADS // N_KV_HEADS
EPS = 1e-6
NEG = -1e30
BIG = 1e4

GATE_PAD = 128
D_TOK = 3 * D_CONV + 2 * D_POOL + D_ATTN + GATE_PAD
D_FEAT = D_ATTN + D_KV
N_FEAT_BLOCKS = D_FEAT // HEAD_DIM
SEL_KEYS = 512
VMEM_LIMIT = 56 * 1024 * 1024

F32 = jnp.float32
BF16 = jnp.bfloat16
NT_DIMS = (((1,), (1,)), ((), ()))


def _sigmoid(x):
    return 1.0 / (1.0 + jnp.exp(-x))


def _silu(x):
    return x * _sigmoid(x)


def _gelu_tanh(x):
    return 0.5 * x * (1.0 + jnp.tanh(np.sqrt(2.0 / np.pi).astype(np.float32) * (x + 0.044715 * (x * x * x))))


def _dot(a, b):
    return jnp.dot(a, b, preferred_element_type=F32)


def _dot_nt(a, b):
    return lax.dot_general(a, b, NT_DIMS, preferred_element_type=F32)


def _masked_softmax(s, mask):
    s = jnp.where(mask, s, NEG)
    m = jnp.max(s, axis=-1, keepdims=True)
    p = jnp.where(mask, jnp.exp(s - m), 0.0)
    return p / jnp.maximum(jnp.sum(p, axis=-1, keepdims=True), 1e-30)


def _in_proj_kernel(x_ref, wnorm_ref, wtok_ref, wfeat_ref, gcol_ref, cos_ref, sin_ref, *rest):
    glu_ref, bin_ref, sz_ref, gates_ref, qt_ref, cmp_ref, slc_ref, win_ref = rest[-8:]
    x = x_ref[...]
    ms = jnp.mean(x * x, axis=-1, keepdims=True)
    h = (x * lax.rsqrt(ms + EPS) * wnorm_ref[...]).astype(BF16)

    a = _dot(h, wtok_ref[...])
    o = 0
    glu_ref[...] = a[:, o:o + D_CONV] * _sigmoid(a[:, o + D_CONV:o + 2 * D_CONV])
    o += 2 * D_CONV
    bin_ref[...] = a[:, o:o + D_POOL]
    o += D_POOL
    z = a[:, o:o + D_CONV + D_POOL + D_ATTN]
    sz_ref[...] = _silu(z)
    o += D_CONV + D_POOL + D_ATTN
    gates_ref[...] = _sigmoid(a[:, o:o + GATE_PAD])

    f = _dot_nt(wfeat_ref[...], h)
    cos = cos_ref[...]
    sin = sin_ref[...]
    kv_refs = (cmp_ref, slc_ref, win_ref)
    for hb in range(N_FEAT_BLOCKS):
        blk = f[hb * HEAD_DIM:(hb + 1) * HEAD_DIM, :]
        kv_blk = hb - N_HEADS
        is_value = kv_blk >= 0 and (kv_blk % (2 * N_KV_HEADS)) >= N_KV_HEADS
        if not is_value:
            bms = jnp.mean(blk * blk, axis=0, keepdims=True)
            y = blk * lax.rsqrt(bms + EPS) * gcol_ref[hb * HEAD_DIM:(hb + 1) * HEAD_DIM, :]
            x1 = y[0:ROT_HALF]
            x2 = y[ROT_HALF:ROT_DIM]
            blk = jnp.concatenate([x1 * cos - x2 * sin, x2 * cos + x1 * sin, y[ROT_DIM:]], axis=0)
        if kv_blk < 0:
            qt_ref[hb * HEAD_DIM:(hb + 1) * HEAD_DIM, :] = blk
        else:
            r = kv_blk % (2 * N_KV_HEADS)
            kv_refs[kv_blk // (2 * N_KV_HEADS)][r * HEAD_DIM:(r + 1) * HEAD_DIM, :] = blk


def _in_proj(x, wnorm, wtok, wfeat, gcol, cos_t, sin_t, tm, l, depth, kv_prev):
    n, t, _ = x.shape
    grid = (n, t // tm)
    tok = lambda w: pl.BlockSpec((None, tm, w), lambda i, j: (i, j, 0))
    feat = lambda w: pl.BlockSpec((None, w, tm), lambda i, j: (i, 0, j))
    kv_spec = pl.BlockSpec((None, None, D_ROWS, tm), lambda i, j: (l, i, 0, j))
    full = lambda a: pl.BlockSpec(a.shape, lambda i, j: (0,) * a.ndim)
    tok_shape = lambda w: jax.ShapeDtypeStruct((n, t, w), F32)
    kv_shape = jax.ShapeDtypeStruct((depth, n, D_ROWS, t), F32)
    n_in = 7
    kv_prev = () if kv_prev is None else tuple(kv_prev)
    return pl.pallas_call(
        _in_proj_kernel,
        grid=grid,
        in_specs=[tok(D_MODEL), full(wnorm), full(wtok), full(wfeat), full(gcol),
                  pl.BlockSpec((ROT_HALF, tm), lambda i, j: (0, j)),
                  pl.BlockSpec((ROT_HALF, tm), lambda i, j: (0, j))]
                 + [pl.BlockSpec(memory_space=pl.ANY)] * len(kv_prev),
        out_specs=[tok(D_CONV), tok(D_POOL), tok(D_CONV + D_POOL + D_ATTN), tok(GATE_PAD),
                   feat(D_ATTN), kv_spec, kv_spec, kv_spec],
        out_shape=[tok_shape(D_CONV), tok_shape(D_POOL), tok_shape(D_CONV + D_POOL + D_ATTN), tok_shape(GATE_PAD),
                   jax.ShapeDtypeStruct((n, D_ATTN, t), F32), kv_shape, kv_shape, kv_shape],
        input_output_aliases={n_in + k: 5 + k for k in range(len(kv_prev))},
        compiler_params=pltpu.CompilerParams(dimension_semantics=("arbitrary", "arbitrary"),
                                             vmem_limit_bytes=VMEM_LIMIT),
        name="in_proj",
    )(x, wnorm, wtok, wfeat, gcol, cos_t, sin_t, *kv_prev)


def _conv_tail(acc, sz_a, bdw, lng, lnb, wpw, bpw):
    y = acc + bdw
    mu = jnp.mean(y, axis=-1, keepdims=True)
    yc = y - mu
    var = jnp.mean(yc * yc, axis=-1, keepdims=True)
    y = yc * lax.rsqrt(var + EPS) * lng + lnb
    y = _dot(_silu(y).astype(BF16), wpw) + bpw
    return y * sz_a


def _pool_tail(total, cnt, xcur, sz_b, wpool, pscale):
    d = total / cnt - xcur
    y = _dot(d.astype(BF16), wpool) * pscale
    return y * sz_b


def _pool_window_of_lane(shape):
    lane = lax.broadcasted_iota(jnp.int32, shape, len(shape) - 1)
    w = jnp.full(shape, POOL_WINDOWS[0], jnp.int32)
    for gi in range(1, len(POOL_WINDOWS)):
        w = jnp.where(lane >= gi * POOL_GROUP, POOL_WINDOWS[gi], w)
    return w


def _pool_select(sums, shape):
    lane = lax.broadcasted_iota(jnp.int32, shape, len(shape) - 1)
    total = sums[0]
    for gi in range(1, len(POOL_WINDOWS)):
        total = jnp.where(lane >= gi * POOL_GROUP, sums[gi], total)
    return total


CONV_HALO = 32
POOL_HALO = 16


MIX_ROWS = 64
SUBLANES = 8


def _tap_rows(buf, w_ref, r0, rows, off, taps, lanes):
    y = None
    for c in range(SUBLANES):
        part = None
        for k in taps:
            if (k + off) % SUBLANES != c:
                continue
            base = r0 + k + off - c
            term = w_ref[k:k + 1, lanes] * buf[base:base + rows + SUBLANES, lanes]
            part = term if part is None else part + term
        if part is not None:
            y = part[c:c + rows] if y is None else y + part[c:c + rows]
    return y


def _mix_out_prompt_kernel(x_ref, glu_ref, gprev_ref, bin_ref, bprev_ref, sz_ref, oat_ref, wdw_ref, bdw_ref, lng_ref,
                           lnb_ref, wpw_ref, bpw_ref, wpool_ref, pscale_ref, wout_ref, o_ref,
                           cbuf, pbuf, ymix, *, t):
    i = pl.program_id(1)
    keep = (i > 0).astype(F32)
    cbuf[0:CONV_HALO, :] = gprev_ref[...] * keep
    cbuf[CONV_HALO:CONV_HALO + t, :] = glu_ref[...]
    cbuf[CONV_HALO + t:CONV_HALO + t + SUBLANES, :] = jnp.zeros((SUBLANES, D_CONV), F32)
    pbuf[0:POOL_HALO, :] = bprev_ref[...] * keep
    pbuf[POOL_HALO:POOL_HALO + t, :] = bin_ref[...]
    nab = D_CONV + D_POOL
    for r0 in range(0, t, MIX_ROWS):
        rows = slice(r0, r0 + MIX_ROWS)
        sz = sz_ref[rows, :]
        acc = _tap_rows(cbuf, wdw_ref, r0, MIX_ROWS, CONV_HALO - CONV_STATE, range(CONV_WIDTH), slice(0, D_CONV))
        y_a = _conv_tail(acc, sz[:, 0:D_CONV], bdw_ref[...], lng_ref[...], lnb_ref[...], wpw_ref[...], bpw_ref[...])
        ymix[rows, 0:D_CONV] = y_a.astype(BF16)

        sums = []
        run = jnp.zeros((MIX_ROWS, D_POOL), F32)
        j = 0
        for w in POOL_WINDOWS:
            while j < w:
                run = run + pbuf[pl.ds(POOL_HALO + r0 - j, MIX_ROWS), :]
                j += 1
            sums.append(run)
        total = _pool_select(sums, (MIX_ROWS, D_POOL))
        pos = i * t + r0 + lax.broadcasted_iota(jnp.int32, (MIX_ROWS, D_POOL), 0)
        cnt = jnp.minimum(_pool_window_of_lane((MIX_ROWS, D_POOL)), pos + 1).astype(F32)
        y_b = _pool_tail(total, cnt, bin_ref[rows, :], sz[:, D_CONV:nab], wpool_ref[...], pscale_ref[...])
        ymix[rows, D_CONV:nab] = y_b.astype(BF16)
        ymix[rows, nab:nab + D_ATTN] = (oat_ref[rows, :] * sz[:, nab:nab + D_ATTN]).astype(BF16)
    o_ref[...] = x_ref[...] + _dot(ymix[...], wout_ref[...])


def _mix_out_prompt(x, glu, b_in, sz, o_attn, wdw, bdw, lng, lnb, wpw, bpw, wpool, pscale, wout, t):
    n, s, _ = glu.shape
    assert t % MIX_ROWS == 0 and s % t == 0
    grid = (n, s // t)
    cur = lambda w: pl.BlockSpec((None, t, w), lambda i, j: (i, j, 0))
    prev = lambda rows, w: pl.BlockSpec((None, rows, w), lambda i, j: (i, jnp.maximum(j * (t // rows) - 1, 0), 0))
    full = lambda a: pl.BlockSpec(a.shape, lambda i, j: (0,) * a.ndim)
    d_mix = D_CONV + D_POOL + D_ATTN
    return pl.pallas_call(
        functools.partial(_mix_out_prompt_kernel, t=t),
        grid=grid,
        in_specs=[cur(D_MODEL), cur(D_CONV), prev(CONV_HALO, D_CONV), cur(D_POOL), prev(POOL_HALO, D_POOL),
                  cur(d_mix), cur(D_ATTN),
                  full(wdw), full(bdw), full(lng), full(lnb), full(wpw), full(bpw), full(wpool),
                  full(pscale), full(wout)],
        out_specs=cur(D_MODEL),
        out_shape=jax.ShapeDtypeStruct(x.shape, F32),
        scratch_shapes=[pltpu.VMEM((CONV_HALO + t + SUBLANES, D_CONV), F32),
                        pltpu.VMEM((POOL_HALO + t, D_POOL), F32),
                        pltpu.VMEM((t, d_mix), BF16)],
        compiler_params=pltpu.CompilerParams(dimension_semantics=("arbitrary", "arbitrary"),
                                             vmem_limit_bytes=VMEM_LIMIT),
        name="mix_out_prompt",
    )(x, glu, glu, b_in, b_in, sz, o_attn, wdw, bdw, lng, lnb, wpw, bpw, wpool, pscale, wout)


def _convpool_sample_kernel(sc_ref, sp_ref, glu_ref, bin_ref, sz_ref, wdw_ref, bdw_ref, lng_ref, lnb_ref,
                            wpw_ref, bpw_ref, wpool_ref, pscale_ref, y_ref, sc_out, sp_out, *, pos):
    glu = glu_ref[...]
    acc = wdw_ref[CONV_STATE:CONV_WIDTH, :] * glu
    for k in range(CONV_STATE):
        acc = acc + wdw_ref[k:k + 1, :] * sc_ref[k]
    sz = sz_ref[...]
    y_ref[:, 0:D_CONV] = _conv_tail(acc, sz[:, 0:D_CONV], bdw_ref[...], lng_ref[...], lnb_ref[...],
                                    wpw_ref[...], bpw_ref[...])
    for k in range(CONV_STATE - 1):
        sc_out[k] = sc_ref[k + 1]
    sc_out[CONV_STATE - 1] = glu

    xcur = bin_ref[...]
    shape = xcur.shape
    sums = []
    run = xcur
    j = 1
    for w in POOL_WINDOWS:
        while j < w:
            run = run + sp_ref[POOL_STATE - j]
            j += 1
        sums.append(run)
    total = _pool_select(sums, shape)
    cnt = jnp.minimum(_pool_window_of_lane(shape), pos + 1).astype(F32)
    y_ref[:, D_CONV:D_CONV + D_POOL] = _pool_tail(total, cnt, xcur, sz[:, D_CONV:D_CONV + D_POOL],
                                                  wpool_ref[...], pscale_ref[...])
    for k in range(POOL_STATE - 1):
        sp_out[k] = sp_ref[k + 1]
    sp_out[POOL_STATE - 1] = xcur


def _convpool_sample(sc, sp, glu, b_in, sz, wdw, bdw, lng, lnb, wpw, bpw, wpool, pscale, pos):
    nb = glu.shape[0]
    return pl.pallas_call(
        functools.partial(_convpool_sample_kernel, pos=pos),
        out_shape=[jax.ShapeDtypeStruct((nb, D_CONV + D_POOL), F32),
                   jax.ShapeDtypeStruct(sc.shape, F32), jax.ShapeDtypeStruct(sp.shape, F32)],
        compiler_params=pltpu.CompilerParams(vmem_limit_bytes=VMEM_LIMIT),
        name="convpool_sample",
    )(sc, sp, glu, b_in, sz[:, 0:D_CONV + D_POOL], wdw, bdw, lng, lnb, wpw, bpw, wpool, pscale)


def _out_proj_kernel(x_ref, yab_ref, oat_ref, szc_ref, w_ref, o_ref):
    ya = yab_ref[...].astype(BF16)
    yc = (oat_ref[...] * szc_ref[...]).astype(BF16)
    nab = D_CONV + D_POOL
    o_ref[...] = x_ref[...] + _dot(ya, w_ref[0:nab, :]) + _dot(yc, w_ref[nab:nab + D_ATTN, :])


def _out_proj(x, yab, o_attn, sz, w_out, tm):
    n, t, _ = x.shape
    nab = D_CONV + D_POOL
    assert nab == D_ATTN
    grid = (n, t // tm)
    tok = lambda w, cb=0: pl.BlockSpec((None, tm, w), lambda i, j: (i, j, cb))
    return pl.pallas_call(
        _out_proj_kernel,
        grid=grid,
        in_specs=[tok(D_MODEL), tok(nab), tok(D_ATTN), tok(D_ATTN, 1),
                  pl.BlockSpec(w_out.shape, lambda i, j: (0, 0))],
        out_specs=tok(D_MODEL),
        out_shape=jax.ShapeDtypeStruct(x.shape, F32),
        compiler_params=pltpu.CompilerParams(dimension_semantics=("arbitrary", "arbitrary"),
                                             vmem_limit_bytes=VMEM_LIMIT),
        name="out_proj",
    )(x, yab, o_attn, sz, w_out)


GD = N_KV_HEADS * HEAD_DIM
CMP_FLAT = CMP_STRIDE * GD


FLAT_SPAN = CMP_STRIDE * CMP_STRIDE


def _piece_perm():
    m = np.arange(FLAT_SPAN)
    src = (m % CMP_STRIDE) * CMP_STRIDE + m // CMP_STRIDE
    return jnp.asarray((src[:, None] == np.arange(FLAT_SPAN)[None, :]).astype(np.float32), BF16)


def _fill_flat(x_t, perm, flat_ref, piece0):
    y = _dot_nt(perm, x_t).astype(BF16)
    for kv in range(2):
        for r in range(CMP_STRIDE):
            flat_ref[kv, piece0:piece0 + CMP_STRIDE, r * GD:(r + 1) * GD] = (
                y[r * CMP_STRIDE:(r + 1) * CMP_STRIDE, kv * GD:(kv + 1) * GD])


def _compress_mlp(flat, w1ab, bias, w2):
    n_pc = flat.shape[0]
    hb = _dot(flat, w1ab)
    h = hb[:, 0:GD] + jnp.concatenate([hb[1:n_pc, GD:2 * GD], jnp.zeros((1, GD), F32)], axis=0) + bias
    out = _dot(_gelu_tanh(h).astype(BF16), w2)
    row = lax.broadcasted_iota(jnp.int32, out.shape, 0)
    return jnp.where(row < n_pc - 1, out, 0.0)


def _cmp_bias_kernel(pe_ref, w1_ref, o_ref):
    for kv in range(2):
        o_ref[kv] = _dot(pe_ref[kv].astype(BF16), w1_ref[kv])


def _cmp_bias(pe_flat, w1):
    return pl.pallas_call(_cmp_bias_kernel, out_shape=jax.ShapeDtypeStruct((2, 1, HEAD_DIM), F32),
                          name="cmp_bias")(pe_flat, w1)


def _compress_prompt_kernel(x_ref, perm_ref, w1ab_ref, bias_ref, w2_ref, o_ref, flat_ref, *, s):
    for t in range(s // FLAT_SPAN):
        _fill_flat(x_ref[:, t * FLAT_SPAN:(t + 1) * FLAT_SPAN].astype(BF16), perm_ref[...], flat_ref, t * CMP_STRIDE)
    for kv in range(2):
        o_ref[:, kv * GD:(kv + 1) * GD] = _compress_mlp(flat_ref[kv], w1ab_ref[kv], bias_ref[kv], w2_ref[kv])


def _compress_prompt(cmp_t, l, w1ab, bias, w2):
    _, n, _, s = cmp_t.shape
    assert s % FLAT_SPAN == 0
    n_pc = s // CMP_STRIDE
    perm = _piece_perm()
    full = lambda a: pl.BlockSpec(a.shape, lambda i: (0,) * a.ndim)
    return pl.pallas_call(
        functools.partial(_compress_prompt_kernel, s=s),
        grid=(n,),
        in_specs=[pl.BlockSpec((None, None, D_ROWS, s), lambda i: (l, i, 0, 0)),
                  full(perm), full(w1ab), full(bias), full(w2)],
        out_specs=pl.BlockSpec((None, n_pc, 2 * GD), lambda i: (i, 0, 0)),
        out_shape=jax.ShapeDtypeStruct((n, n_pc, 2 * GD), F32),
        scratch_shapes=[pltpu.VMEM((2, n_pc, CMP_FLAT), BF16)],
        compiler_params=pltpu.CompilerParams(dimension_semantics=("arbitrary",), vmem_limit_bytes=VMEM_LIMIT),
        name="compress_prompt",
    )(cmp_t, perm, w1ab, bias, w2)


def _select_blocks(imp_t, pos_row, n_blocks):
    shape = imp_t.shape
    assert n_blocks % 8 == 0
    j = lax.broadcasted_iota(jnp.int32, shape, 0)
    cur = pos_row // SLC_BLOCK
    forced = (j == 0) | (j == cur) | (j == cur - 1)
    score = jnp.where(j <= cur, jnp.where(forced, BIG, imp_t), -BIG)
    tiles = [score[8 * v:8 * v + 8] for v in range(n_blocks // 8)]
    j8 = lax.broadcasted_iota(jnp.int32, (8, shape[1]), 0)
    ranks = [jnp.zeros((8, shape[1]), F32) for _ in tiles]
    for jp in range(n_blocks):
        sj = score[jp:jp + 1, :]
        for v, tile in enumerate(tiles):
            if jp < 8 * v:
                beats = sj >= tile
            elif jp >= 8 * v + 8:
                beats = sj > tile
            else:
                beats = (sj > tile) | ((sj == tile) & (j8 > jp - 8 * v))
            ranks[v] = ranks[v] + jnp.where(beats, 1.0, 0.0)
    rank = jnp.concatenate(ranks, axis=0)
    return (rank < float(min(N_SEL, n_blocks))).astype(F32)


def _col_softmax(s):
    m = jnp.max(s, axis=0, keepdims=True)
    e = jnp.exp(s - m)
    inv = jnp.where(m > 0.5 * NEG, 1.0 / jnp.maximum(jnp.sum(e, axis=0, keepdims=True), 1e-30), 0.0)
    return e, inv


ONES_ROWS = 16


def _with_ones(v_t):
    return jnp.concatenate([v_t, jnp.ones((ONES_ROWS, v_t.shape[1]), BF16)], axis=0)


def _normalised(acc):
    return acc[0:HEAD_DIM] * (1.0 / jnp.maximum(acc[HEAD_DIM:HEAD_DIM + 1], 1e-30))


def _mask_heads(sc, ok, tq):
    return jnp.concatenate([jnp.where(ok, sc[:, r * tq:(r + 1) * tq], NEG) for r in range(N_REP)], axis=1)


def _attn_prompt_kernel(qt_ref, kcv_ref, slc_ref, win_ref, gates_ref, ovt_ref, o_ref,
                        ks_ref, kw_ref, s_ref, sel_ref, acc_ref, m_ref, *, s, n_blk):
    tq = Q_BLOCK
    cols = N_REP * tq
    groups = range(N_KV_HEADS)
    qc = pl.program_id(1)
    q0 = qc * tq
    pos_row = q0 + lax.broadcasted_iota(jnp.int32, (1, tq), 1)
    n_slc = s // SLC_BLOCK
    n_cmp = kcv_ref.shape[0]
    scale = HEAD_DIM ** -0.5
    v_row0 = lambda g: (N_KV_HEADS + g) * HEAD_DIM

    @pl.when(qc == 0)
    def _():
        for g in range(N_KV_HEADS):
            for t in range(s // 128):
                cols_t = slice(t * 128, (t + 1) * 128)
                ks_ref[g, cols_t, :] = slc_ref[g * HEAD_DIM:(g + 1) * HEAD_DIM, cols_t].T.astype(BF16)
                kw_ref[g, cols_t, :] = win_ref[g * HEAD_DIM:(g + 1) * HEAD_DIM, cols_t].T.astype(BF16)

    gates_t = gates_ref[...].T
    vc_t = kcv_ref[:, GD:2 * GD].T.astype(BF16)
    q_ts = []
    for g in groups:
        q_t = jnp.concatenate([qt_ref[(g * N_REP + r) * HEAD_DIM:(g * N_REP + r + 1) * HEAD_DIM, :]
                               for r in range(N_REP)], axis=1)
        q_ts.append((q_t * scale).astype(BF16))

    c_idx = lax.broadcasted_iota(jnp.int32, (n_cmp, 1), 0)
    valid_c = (c_idx * CMP_STRIDE + (CMP_BLOCK - 1) <= pos_row) & (c_idx < n_blk)
    ovt = ovt_ref[...]
    o_cs = []
    for g in groups:
        kc = kcv_ref[:, g * HEAD_DIM:(g + 1) * HEAD_DIM].astype(BF16)
        e_c, inv_c = _col_softmax(_mask_heads(_dot(kc, q_ts[g]), valid_c, tq))
        o_cs.append(_dot(vc_t[g * HEAD_DIM:(g + 1) * HEAD_DIM, :], e_c.astype(BF16)) * inv_c)
        pc_sum = e_c[:, 0:tq] * inv_c[:, 0:tq]
        for r in range(1, N_REP):
            pc_sum = pc_sum + e_c[:, r * tq:(r + 1) * tq] * inv_c[:, r * tq:(r + 1) * tq]
        pc_hi = pc_sum.astype(BF16)
        pc_lo = (pc_sum - pc_hi.astype(F32)).astype(BF16)
        imp_t = _dot(ovt, pc_hi) + _dot(ovt, pc_lo)
        sel_ref[g] = _select_blocks(imp_t, pos_row, n_slc)

    n_keys = q0 + tq
    rest = n_keys % SEL_KEYS
    half_tail = (rest > 0) & (rest <= SEL_KEYS // 2)
    n_steps = n_keys // SEL_KEYS + (rest > SEL_KEYS // 2).astype(jnp.int32)
    tail_k0 = pl.multiple_of((n_keys // SEL_KEYS) * SEL_KEYS, SEL_KEYS)

    def score_chunk(k0, size):
        key = k0 + lax.broadcasted_iota(jnp.int32, (size, 1), 0)
        causal = key <= pos_row
        for g in groups:
            sc = _dot(ks_ref[g, pl.ds(k0, size), :], q_ts[g])
            parts = []
            for jj in range(size // SLC_BLOCK):
                rows = slice(jj * SLC_BLOCK, (jj + 1) * SLC_BLOCK)
                chosen = sel_ref[g, pl.ds(k0 // SLC_BLOCK + jj, 1), :] > 0.5
                parts.append(_mask_heads(sc[rows], chosen & causal[rows], tq))
            sc = jnp.concatenate(parts, axis=0)
            s_ref[g, pl.ds(k0, size), :] = sc.astype(BF16)
            m_ref[g] = jnp.maximum(m_ref[g], jnp.max(sc, axis=0, keepdims=True))

    m_ref[...] = jnp.full(m_ref.shape, NEG, F32)

    def score_step(i, carry):
        score_chunk(pl.multiple_of(i * SEL_KEYS, SEL_KEYS), SEL_KEYS)
        return carry

    lax.fori_loop(0, n_steps, score_step, 0)

    @pl.when(half_tail)
    def _():
        score_chunk(tail_k0, SEL_KEYS // 2)

    m_sel = [m_ref[g].astype(BF16) for g in groups]

    acc_ref[...] = jnp.zeros(acc_ref.shape, F32)

    def value_chunk(k0, size):
        for g in groups:
            p = jnp.exp(s_ref[g, pl.ds(k0, size), :] - m_sel[g])
            v_t = slc_ref[v_row0(g):v_row0(g) + HEAD_DIM, pl.ds(k0, size)].astype(BF16)
            acc_ref[g] += _dot(_with_ones(v_t), p)

    def value_step(i, carry):
        value_chunk(pl.multiple_of(i * SEL_KEYS, SEL_KEYS), SEL_KEYS)
        return carry

    lax.fori_loop(0, n_steps, value_step, 0)

    @pl.when(half_tail)
    def _():
        value_chunk(tail_k0, SEL_KEYS // 2)

    n_prev = -(-(WINDOW - 1) // tq)
    k0s, oks = [], []
    for b in range(n_prev + 1):
        kb = qc - n_prev + b
        k0s.append(pl.multiple_of(jnp.maximum(kb, 0) * tq, tq))
        kp = kb * tq + lax.broadcasted_iota(jnp.int32, (tq, 1), 0)
        oks.append((kp <= pos_row) & (kp > pos_row - WINDOW) & (kp >= 0))

    out_blocks = []
    for g in groups:
        o_s = _normalised(acc_ref[g])
        s_w = jnp.concatenate([_mask_heads(_dot(kw_ref[g, pl.ds(k0s[b], tq), :], q_ts[g]), oks[b], tq)
                               for b in range(n_prev + 1)], axis=0)
        e_w = jnp.exp(s_w.astype(BF16) - jnp.max(s_w, axis=0, keepdims=True).astype(BF16))
        o_w = _dot(_with_ones(win_ref[v_row0(g):v_row0(g) + HEAD_DIM, pl.ds(k0s[0], tq)].astype(BF16)), e_w[0:tq])
        for b in range(1, n_prev + 1):
            o_w = o_w + _dot(_with_ones(win_ref[v_row0(g):v_row0(g) + HEAD_DIM, pl.ds(k0s[b], tq)].astype(BF16)),
                             e_w[b * tq:(b + 1) * tq])
        o_w = _normalised(o_w)
        for r in range(N_REP):
            h = g * N_REP + r
            c = slice(r * tq, (r + 1) * tq)
            out_blocks.append(gates_t[h:h + 1] * o_cs[g][:, c] + gates_t[N_HEADS + h:N_HEADS + h + 1] * o_s[:, c]
                              + gates_t[2 * N_HEADS + h:2 * N_HEADS + h + 1] * o_w[:, c])
    o_ref[...] = jnp.concatenate(out_blocks, axis=0).T


def _overlap_t(n_cmp_rows, n_blk, n_slc):
    c = np.arange(n_cmp_rows)
    start = c * CMP_STRIDE
    end = start + CMP_BLOCK - 1
    j0 = np.arange(n_slc) * SLC_BLOCK
    ov = (end[None, :] >= j0[:, None]) & (start[None, :] < j0[:, None] + SLC_BLOCK) & (c[None, :] < n_blk)
    return jnp.asarray(ov.astype(np.float32), BF16)


def _attn_prompt(q_t, kcv, slc_t, win_t, gates, l):
    n, _, s = q_t.shape
    n_cmp = kcv.shape[1]
    n_blk = s // CMP_STRIDE - CMP_BLOCK // CMP_STRIDE + 1
    n_slc = s // SLC_BLOCK
    ovt = _overlap_t(n_cmp, n_blk, n_slc)
    tq = Q_BLOCK
    cols = N_REP * tq
    assert s % SEL_KEYS == 0 and (SEL_KEYS // 2) % SLC_BLOCK == 0 and (SEL_KEYS // 2) % tq == 0
    return pl.pallas_call(
        functools.partial(_attn_prompt_kernel, s=s, n_blk=n_blk),
        grid=(n, s // tq),
        in_specs=[pl.BlockSpec((None, D_ATTN, tq), lambda i, j: (i, 0, j)),
                  pl.BlockSpec((None,) + kcv.shape[1:], lambda i, j: (i, 0, 0)),
                  pl.BlockSpec((None, None, D_ROWS, s), lambda i, j: (l, i, 0, 0)),
                  pl.BlockSpec((None, None, D_ROWS, s), lambda i, j: (l, i, 0, 0)),
                  pl.BlockSpec((None, tq, GATE_PAD), lambda i, j: (i, j, 0)),
                  pl.BlockSpec(ovt.shape, lambda i, j: (0, 0))],
        out_specs=pl.BlockSpec((None, tq, D_ATTN), lambda i, j: (i, j, 0)),
        out_shape=jax.ShapeDtypeStruct((n, s, D_ATTN), F32),
        scratch_shapes=[pltpu.VMEM((N_KV_HEADS, s, HEAD_DIM), BF16),
                        pltpu.VMEM((N_KV_HEADS, s, HEAD_DIM), BF16),
                        pltpu.VMEM((N_KV_HEADS, s, cols), BF16),
                        pltpu.VMEM((N_KV_HEADS, n_slc, tq), F32),
                        pltpu.VMEM((N_KV_HEADS, HEAD_DIM + ONES_ROWS, cols), F32),
                        pltpu.VMEM((N_KV_HEADS, 1, cols), F32)],
        compiler_params=pltpu.CompilerParams(dimension_semantics=("arbitrary", "arbitrary"),
                                             vmem_limit_bytes=VMEM_LIMIT),
        name="attn_prompt",
    )(q_t, kcv, slc_t, win_t, gates, ovt)


PAGE = 128
SAMPLE_SEQS = 4


def _sample_geometry(past):
    length = past + 1
    padded = -(-length // SLC_BLOCK) * SLC_BLOCK
    n_pc = padded // CMP_STRIDE
    n_pc_pad = -(-n_pc // CMP_STRIDE) * CMP_STRIDE
    return dict(n_pc=n_pc, n_pc_pad=n_pc_pad, n_blk=n_pc - CMP_BLOCK // CMP_STRIDE + 1, n_slc=padded // SLC_BLOCK)


def _attn_sample_kernel(pt_ref, q_ref, new_ref, gates_ref, wincol_ref, cwin_ref, ov_ref,
                        perm_ref, w1ab_ref, bias_ref, w2_ref, ccmp_hbm, cslc_hbm, *rest, l, past, n_pages, nseq):
    o_ref, wout_ref, flat_ref, cmp_buf, slc_buf, sem = rest[-6:]
    step = pl.program_id(0)
    slot = step % 2

    def page_copies(of_step, into_slot, lookup):
        copies = []
        for u in range(nseq):
            for j in range(n_pages):
                page = pt_ref[of_step * nseq + u, j] if lookup else 0
                k = u * n_pages + j
                copies.append(pltpu.make_async_copy(ccmp_hbm.at[l, page], cmp_buf.at[into_slot, k], sem.at[into_slot, 0]))
                copies.append(pltpu.make_async_copy(cslc_hbm.at[l, page], slc_buf.at[into_slot, k], sem.at[into_slot, 1]))
        return copies

    @pl.when(step == 0)
    def _():
        for c in page_copies(0, 0, True):
            c.start()

    @pl.when(step + 1 < pl.num_programs(0))
    def _():
        for c in page_copies(step + 1, 1 - slot, True):
            c.start()

    for c in page_copies(step, slot, False):
        c.wait()
    cmp_pages = [[cmp_buf.at[slot, u * n_pages + j] for j in range(n_pages)] for u in range(nseq)]
    slc_pages = [[slc_buf.at[slot, u * n_pages + j] for j in range(n_pages)] for u in range(nseq)]
    geo = _sample_geometry(past)
    n_pc_pad, n_blk, n_slc = geo["n_pc_pad"], geo["n_blk"], geo["n_slc"]
    pos = past
    lb = cwin_ref.shape[-1]
    scale = HEAD_DIM ** -0.5
    head = lax.broadcasted_iota(jnp.int32, (N_HEADS, 1), 0)
    lane = lax.broadcasted_iota(jnp.int32, (1, GD), 1)
    lane_h = lax.broadcasted_iota(jnp.int32, (N_HEADS, GD), 1)
    own = (lane_h // HEAD_DIM) == (head // N_REP)

    def new_row(u, branch, kv):
        o = (branch * 2 + kv) * GD
        return new_ref[u, :, o:o + GD]

    def rounded(x):
        return x.astype(BF16).astype(F32)

    kcv = []
    pages_per_span = FLAT_SPAN // PAGE
    past_pc = past // CMP_STRIDE
    for u in range(nseq):
        row0 = u * n_pc_pad
        for t in range(n_pages // pages_per_span):
            x_t = jnp.concatenate([cmp_pages[u][t * pages_per_span + w][...].reshape(2 * GD, PAGE)
                                   for w in range(pages_per_span)], axis=1)
            _fill_flat(x_t.astype(BF16), perm_ref[...], flat_ref, row0 + t * CMP_STRIDE)
        for kv in range(2):
            flat_ref[kv, row0 + past_pc:row0 + n_pc_pad, :] = jnp.zeros((n_pc_pad - past_pc, CMP_FLAT), BF16)
            flat_ref[kv, row0 + past_pc:row0 + past_pc + 1, 0:GD] = new_row(u, 0, kv).astype(BF16)
    for kv in range(2):
        kcv.append(_compress_mlp(flat_ref[kv], w1ab_ref[kv], bias_ref[kv], w2_ref[kv]).astype(BF16))
    for u in range(nseq):
        _attn_sample_one(u, kcv[0][u * n_pc_pad:(u + 1) * n_pc_pad], kcv[1][u * n_pc_pad:(u + 1) * n_pc_pad],
                         q_ref, new_row, rounded, gates_ref, wincol_ref, cwin_ref, ov_ref,
                         slc_pages[u], o_ref, wout_ref, geo=geo, past=past, consts=(head, lane, lane_h, own))


def _attn_sample_one(u, kc, vc, q_ref, new_row, rounded, gates_ref, wincol_ref, cwin_ref, ov_ref,
                     slc_pages, o_ref, wout_ref, *, geo, past, consts):
    head, lane, lane_h, own = consts
    n_pc_pad, n_blk, n_slc = geo["n_pc_pad"], geo["n_blk"], geo["n_slc"]
    n_pages = len(slc_pages)
    pos = past
    lb = cwin_ref.shape[-1]
    scale = HEAD_DIM ** -0.5
    q_rows = jnp.concatenate([q_ref[u, :, h * HEAD_DIM:(h + 1) * HEAD_DIM] for h in range(N_HEADS)], axis=0)
    qx = (jnp.where(own, jnp.concatenate([q_rows] * N_KV_HEADS, axis=1), 0.0) * scale).astype(BF16)
    qx32 = qx.astype(F32)
    c_idx = lax.broadcasted_iota(jnp.int32, (1, n_pc_pad), 1)
    valid = (c_idx * CMP_STRIDE + (CMP_BLOCK - 1) <= pos) & (c_idx < n_blk)
    p_c = _masked_softmax(_dot_nt(qx, kc), valid)
    o_c = _dot(p_c.astype(BF16), vc)

    cur = pos // SLC_BLOCK
    forced = (lane == 0) | (lane == cur) | (lane == cur - 1)
    jp = lax.broadcasted_iota(jnp.int32, (GD, GD), 0)
    jj = lax.broadcasted_iota(jnp.int32, (GD, GD), 1)
    sels = []
    for g in range(N_KV_HEADS):
        pc_sum = jnp.sum(p_c[g * N_REP:(g + 1) * N_REP], axis=0, keepdims=True)
        pc_hi = pc_sum.astype(BF16)
        pc_lo = (pc_sum - pc_hi.astype(F32)).astype(BF16)
        imp = _dot(pc_hi, ov_ref[...]) + _dot(pc_lo, ov_ref[...])
        score = jnp.where(lane <= cur, jnp.where(forced, BIG, imp), -BIG)
        score = jnp.where(lane < n_slc, score, -2.0 * BIG)
        score_b = jnp.broadcast_to(score, (GD, GD))
        score_a = score_b.T
        beats = (score_a > score_b) | ((score_a == score_b) & (jp < jj))
        rank = jnp.sum(beats.astype(F32), axis=0, keepdims=True)
        sel = ((rank < float(min(N_SEL, n_slc))) & (lane < n_slc)).astype(F32)
        sels.append(jnp.broadcast_to(sel, (N_REP, GD)))
    sel_h = jnp.concatenate(sels, axis=0)

    blocks_per_tile = GD // SLC_BLOCK
    chosen = []
    for t in range(past // GD):
        tile = sel_h[:, t * blocks_per_tile:t * blocks_per_tile + 1]
        for b in range(1, blocks_per_tile):
            tile = jnp.where(lane >= b * SLC_BLOCK, sel_h[:, t * blocks_per_tile + b:t * blocks_per_tile + b + 1], tile)
        chosen.append(jnp.broadcast_to(tile, (N_HEADS, GD)))
    chosen = jnp.concatenate(chosen, axis=-1) > 0.5
    s_past = jnp.concatenate([_dot(qx, slc_pages[j][0].astype(BF16)) for j in range(n_pages)], axis=-1)
    s_past = jnp.where(chosen, s_past, NEG)
    new_ok = jnp.sum(jnp.where(lane == pos // SLC_BLOCK, sel_h, 0.0), axis=-1, keepdims=True) > 0.5
    s_new = jnp.where(new_ok, jnp.sum(qx32 * rounded(new_row(u, 1, 0)), axis=-1, keepdims=True), NEG)
    m = jnp.maximum(jnp.max(s_past, axis=-1, keepdims=True), s_new)
    p_past = jnp.where(chosen, jnp.exp(s_past - m), 0.0)
    p_new = jnp.where(new_ok, jnp.exp(s_new - m), 0.0)
    denom = jnp.maximum(jnp.sum(p_past, axis=-1, keepdims=True) + p_new, 1e-30)
    o_s = rounded(p_new) * rounded(new_row(u, 1, 1))
    for j in range(n_pages):
        o_s = o_s + _dot_nt(p_past[:, j * PAGE:(j + 1) * PAGE].astype(BF16), slc_pages[j][1].astype(BF16))
    o_s = o_s / denom

    kpos = (past - lb) + lax.broadcasted_iota(jnp.int32, (1, lb), 1)
    w_ok = (kpos <= pos) & (kpos > pos - WINDOW)
    s_w = jnp.where(w_ok, _dot(qx, cwin_ref[u, 0].astype(BF16)), NEG)
    s_wn = jnp.sum(qx32 * rounded(new_row(u, 2, 0)), axis=-1, keepdims=True)
    m = jnp.maximum(jnp.max(s_w, axis=-1, keepdims=True), s_wn)
    p_w = jnp.where(w_ok, jnp.exp(s_w - m), 0.0)
    p_wn = jnp.exp(s_wn - m)
    denom = jnp.maximum(jnp.sum(p_w, axis=-1, keepdims=True) + p_wn, 1e-30)
    o_w = (_dot_nt(p_w.astype(BF16), cwin_ref[u, 1].astype(BF16))
           + rounded(p_wn) * rounded(new_row(u, 2, 1))) / denom

    gates = gates_ref[u]
    gate = lambda br: jnp.sum(jnp.where(lane_h == head + br * N_HEADS, gates, 0.0), axis=-1, keepdims=True)
    o = gate(0) * o_c + gate(1) * o_s + gate(2) * o_w
    for h in range(N_HEADS):
        g = h // N_REP
        o_ref[u, :, h * HEAD_DIM:(h + 1) * HEAD_DIM] = o[h:h + 1, g * HEAD_DIM:(g + 1) * HEAD_DIM]

    for kv in range(2):
        wout_ref[u, kv, :, 0:lb - 1] = cwin_ref[u, kv, :, 1:lb]
        wout_ref[u, kv, :, lb - 1:lb] = wincol_ref[u, kv * GD:(kv + 1) * GD, :]


def _attn_sample(l, q_tok, new_tok, gates, win_col, cwin_t, ccmp_t, cslc_t, page_table, w1ab, bias, w2, win_prev):
    nb = q_tok.shape[0]
    n_pages = page_table.shape[1]
    past = n_pages * PAGE
    geo = _sample_geometry(past)
    lb = cwin_t.shape[-1]
    assert GATE_PAD == GD and FLAT_SPAN % PAGE == 0 and n_pages % (FLAT_SPAN // PAGE) == 0
    perm = _piece_perm()
    ov = _overlap_t(geo["n_pc_pad"], geo["n_blk"], GD).T
    ov = jnp.where(jnp.arange(GD)[None, :] < geo["n_slc"], ov, 0).astype(BF16)
    assert GD % SLC_BLOCK == 0 and past % GD == 0
    nseq = SAMPLE_SEQS if nb % SAMPLE_SEQS == 0 else 1
    seq = lambda a: pl.BlockSpec((nseq,) + a.shape[1:], lambda i, pt: (i,) + (0,) * (a.ndim - 1))
    full = lambda a: pl.BlockSpec(a.shape, lambda i, pt: (0,) * a.ndim)
    operands = [page_table, q_tok, new_tok, gates, win_col, cwin_t, ov, perm, w1ab, bias, w2, ccmp_t, cslc_t]
    win_prev = () if win_prev is None else (win_prev,)
    page_buf = pltpu.VMEM((2, nseq * n_pages, 2, GD, PAGE), F32)
    grid_spec = pltpu.PrefetchScalarGridSpec(
        num_scalar_prefetch=1,
        grid=(nb // nseq,),
        in_specs=[seq(q_tok), seq(new_tok), seq(gates), seq(win_col),
                  pl.BlockSpec((None, nseq, 2, GD, lb), lambda i, pt: (l, i, 0, 0, 0)),
                  full(ov), full(perm), full(w1ab), full(bias), full(w2)]
                 + [pl.BlockSpec(memory_space=pl.ANY)] * (2 + len(win_prev)),
        out_specs=[pl.BlockSpec((nseq, 1, D_ATTN), lambda i, pt: (i, 0, 0)),
                   pl.BlockSpec((None, nseq, 2, GD, lb), lambda i, pt: (l, i, 0, 0, 0))],
        scratch_shapes=[pltpu.VMEM((2, nseq * geo["n_pc_pad"], CMP_FLAT), BF16), page_buf, page_buf,
                        pltpu.SemaphoreType.DMA((2, 2))],
    )
    return pl.pallas_call(
        functools.partial(_attn_sample_kernel, l=l, past=past, n_pages=n_pages, nseq=nseq),
        grid_spec=grid_spec,
        out_shape=[jax.ShapeDtypeStruct((nb, 1, D_ATTN), F32),
                   jax.ShapeDtypeStruct((cwin_t.shape[0], nb, 2, GD, lb), F32)],
        input_output_aliases={len(operands): 1} if win_prev else {},
        compiler_params=pltpu.CompilerParams(dimension_semantics=("arbitrary",), vmem_limit_bytes=VMEM_LIMIT),
        name="attn_sample",
    )(*operands, *win_prev)


def _prep_layer(l, w_norm, w_in, w_out, w_dw, b_dw, ln_g, ln_b, w_pw, b_pw, w_pool, pool_scale,
                g_q, g_k, cmp_pe, cmp_w1, cmp_w2):
    sizes = (D_CONV, D_CONV, D_CONV, D_POOL, D_POOL, D_ATTN, D_KV, N_BRANCH * N_HEADS, D_ATTN)
    offs = [0] + [int(v) for v in np.cumsum(sizes)]
    col = lambda k: w_in[l][:, offs[k]:offs[k + 1]]
    a_val, a_gate, z_a, b_in, z_b, q, kv, gate, z_c = (col(k) for k in range(9))
    pad = jnp.zeros((D_MODEL, GATE_PAD - N_BRANCH * N_HEADS), F32)
    wtok = jnp.concatenate([a_val, a_gate, b_in, z_a, z_b, z_c, gate, pad], axis=1).astype(BF16)
    wfeat = jnp.concatenate([q, kv], axis=1).T.astype(BF16)
    ones = jnp.ones((N_KV_HEADS * HEAD_DIM,), F32)
    gcol = jnp.concatenate([jnp.tile(g_q[l], N_HEADS)]
                           + [piece for br in range(N_BRANCH)
                              for piece in (jnp.tile(g_k[l, br], N_KV_HEADS), ones)])[:, None]
    wpool_bd = jnp.zeros((D_POOL, D_POOL), F32)
    for gi in range(len(POOL_WINDOWS)):
        wpool_bd = wpool_bd.at[gi * POOL_GROUP:(gi + 1) * POOL_GROUP, gi * POOL_GROUP:(gi + 1) * POOL_GROUP].set(w_pool[l, gi])
    row = lambda v: v[None, :]

    def both_groups(w):
        z = jnp.zeros_like(w)
        return jnp.stack([jnp.concatenate([w, z], axis=2), jnp.concatenate([z, w], axis=2)], axis=1).reshape(-1, GD)

    half = CMP_STRIDE * HEAD_DIM
    w1 = cmp_w1[l].astype(BF16)
    w1ab = jnp.stack([jnp.concatenate(
        [both_groups(w1[kv, :half].reshape(CMP_STRIDE, HEAD_DIM, HEAD_DIM)),
         both_groups(w1[kv, half:].reshape(CMP_STRIDE, HEAD_DIM, HEAD_DIM))], axis=1) for kv in range(2)])
    w2p = jnp.stack([both_groups(cmp_w2[l, kv].astype(BF16)[None]) for kv in range(2)])
    cbias = jnp.tile(_cmp_bias(cmp_pe[l].reshape(2, 1, CMP_BLOCK * HEAD_DIM), w1), (1, 1, N_KV_HEADS))
    return dict(
        wnorm=row(w_norm[l]), wtok=wtok, wfeat=wfeat, gcol=gcol,
        wdw=w_dw[l], bdw=row(b_dw[l]), lng=row(ln_g[l]), lnb=row(ln_b[l]),
        wpw=w_pw[l].astype(BF16), bpw=row(b_pw[l]),
        wpool=wpool_bd.astype(BF16), pscale=row(pool_scale[l]),
        w1ab=w1ab, cbias=cbias, w2p=w2p,
        wout=w_out[l].astype(BF16),
    )


def _rope_tables(pos):
    inv = ROPE_THETA ** (-jnp.arange(ROT_HALF, dtype=F32) * 2.0 / ROT_DIM)
    ang = pos.astype(F32)[:, None] * inv[None, :]
    return jnp.cos(ang).T, jnp.sin(ang).T


def _prompt_layer(l, depth, x, p, cos_t, sin_t, kv_prev):
    glu, b_in, sz, gates, q_t, *kv = _in_proj(
        x, p["wnorm"], p["wtok"], p["wfeat"], p["gcol"], cos_t, sin_t, min(512, x.shape[1]), l, depth, kv_prev)
    cmp_t, slc_t, win_t = kv
    kcv = _compress_prompt(cmp_t, l, p["w1ab"], p["cbias"], p["w2p"])
    o_attn = _attn_prompt(q_t, kcv, slc_t, win_t, gates, l)
    y = _mix_out_prompt(x, glu, b_in, sz, o_attn, p["wdw"], p["bdw"], p["lng"], p["lnb"], p["wpw"], p["bpw"],
                        p["wpool"], p["pscale"], p["wout"], t=min(512, x.shape[1]))
    return y, glu, b_in, kv


def _sample_layer(l, depth, x, p, cos_t, sin_t, sc, sp, cwin_t, ccmp_t, cslc_t, page_table, past, kv_prev, win_prev):
    nb = x.shape[0]
    glu, b_in, sz, gates, q_t, *kv = _in_proj(
        x[None], p["wnorm"], p["wtok"], p["wfeat"], p["gcol"], cos_t, sin_t, nb, l, depth, kv_prev)
    glu, b_in, sz, gates = glu[0], b_in[0], sz[0], gates[0]
    yab, sc_new, sp_new = _convpool_sample(sc, sp, glu, b_in, sz, p["wdw"], p["bdw"], p["lng"], p["lnb"],
                                           p["wpw"], p["bpw"], p["wpool"], p["pscale"], pos=past)
    new_t = jnp.concatenate([a[l, 0] for a in kv], axis=0)
    new_tok = new_t.T[:, None, :]
    q_tok = q_t[0].T[:, None, :]
    win_col = kv[2][l, 0].T[:, :, None]
    o_attn, win_next = _attn_sample(l, q_tok, new_tok, gates[:, None, :], win_col, cwin_t, ccmp_t, cslc_t,
                                    page_table, p["w1ab"], p["cbias"], p["w2p"], win_prev)
    y = _out_proj(x[None], yab[None], o_attn.reshape(1, nb, D_ATTN), sz[None], p["wout"], tm=nb)[0]
    return y, sc_new, sp_new, win_next, kv


def _rows_from_feat(a):
    lead = a.shape[:-2]
    a = a.reshape(lead + (2, N_KV_HEADS, HEAD_DIM, a.shape[-1]))
    nl = len(lead)
    return jnp.transpose(a, tuple(range(nl)) + (nl + 3, nl, nl + 1, nl + 2))


def kernel(x_prompt, x_sample, state_conv, state_pool, cache_win_kv, cache_cmp_kv, cache_slc_kv, page_table,
           w_norm, w_in, w_out, w_dw, b_dw, ln_g, ln_b, w_pw, b_pw, w_pool, pool_scale,
           g_q, g_k, cmp_pe, cmp_w1, cmp_w2):
    bp, s, _ = x_prompt.shape
    bs, t_new, _ = x_sample.shape
    depth = w_in.shape[0]
    assert t_new == 1 and cache_cmp_kv.shape[2] == PAGE and s % 512 == 0
    past = page_table.shape[1] * PAGE
    lb = cache_win_kv.shape[2]

    def to_feat(c):
        c = jnp.transpose(c, (0, 1, 3, 4, 5, 2))
        return c.reshape(c.shape[:3] + (GD, c.shape[-1]))
    cwin_t, ccmp_t, cslc_t = to_feat(cache_win_kv), to_feat(cache_cmp_kv), to_feat(cache_slc_kv)
    sc_all = jnp.transpose(state_conv, (0, 2, 1, 3))
    sp_all = jnp.transpose(state_pool, (0, 2, 1, 3))
    cos_p, sin_p = _rope_tables(jnp.arange(s))
    cos_s, sin_s = _rope_tables(jnp.full((bs,), past))

    xp, xs = x_prompt, x_sample[:, 0, :]
    conv_p, pool_p, conv_s, pool_s = [], [], [], []
    kv_p = kv_s = win_s = None
    for l in range(depth):
        p = _prep_layer(l, w_norm, w_in, w_out, w_dw, b_dw, ln_g, ln_b, w_pw, b_pw, w_pool, pool_scale,
                        g_q, g_k, cmp_pe, cmp_w1, cmp_w2)
        xp, glu, b_in, kv_p = _prompt_layer(l, depth, xp, p, cos_p, sin_p, kv_p)
        conv_p.append(glu[:, s - CONV_STATE:, :])
        pool_p.append(b_in[:, s - POOL_STATE:, :])
        xs, sc_new, sp_new, win_s, kv_s = _sample_layer(
            l, depth, xs, p, cos_s, sin_s, sc_all[l], sp_all[l], cwin_t, ccmp_t, cslc_t, page_table, past,
            kv_s, win_s)
        conv_s.append(jnp.transpose(sc_new, (1, 0, 2)))
        pool_s.append(jnp.transpose(sp_new, (1, 0, 2)))
    cmp_p, slc_p, win_p = kv_p
    cmp_s, slc_s, _ = kv_s
    new_rows = lambda a: _rows_from_feat(a[:, 0])[:, :, None]
    return (xp, xs[:, None, :], jnp.stack(conv_p), jnp.stack(pool_p),
            _rows_from_feat(win_p[:, :, :, s - min(WINDOW, s):]), _rows_from_feat(cmp_p), _rows_from_feat(slc_p),
            jnp.stack(conv_s), jnp.stack(pool_s), _rows_from_feat(win_s.reshape(depth, bs, D_ROWS, lb)),
            new_rows(cmp_s), new_rows(slc_s))
```

```python
import functools

import numpy as np
import jax
import jax.numpy as jnp
from jax import lax
from jax.experimental import pallas as pl
from jax.experimental.pallas import tpu as pltpu

D_MODEL = 1024
D_CONV = 256
D_POOL = 256
N_HEADS = 8
N_KV_HEADS = 2
HEAD_DIM = 64
D_ATTN = N_HEADS * HEAD_DIM
CONV_WIDTH = 31
CONV_STATE = CONV_WIDTH - 1
POOL_WINDOWS = (2, 4, 8, 16)
POOL_GROUP = D_POOL // len(POOL_WINDOWS)
POOL_STATE = max(POOL_WINDOWS) - 1
CMP_BLOCK = 32
CMP_STRIDE = 16
SLC_BLOCK = 64
N_SEL = 8
WINDOW = 256
Q_BLOCK = 256
ROT_DIM = HEAD_DIM // 4
ROT_HALF = ROT_DIM // 2
ROPE_THETA = 500000.0
N_BRANCH = 3
D_KV = N_BRANCH * 2 * N_KV_HEADS * HEAD_DIM
D_ROWS = 2 * N_KV_HEADS * HEAD_DIM
N_REP = N_HEADS // N_KV_HEADS
EPS = 1e-6
NEG = -1e30
BIG = 1e4

GATE_PAD = 128
D_TOK = 3 * D_CONV + 2 * D_POOL + D_ATTN + GATE_PAD
D_FEAT = D_ATTN + D_KV
N_FEAT_BLOCKS = D_FEAT // HEAD_DIM
SEL_KEYS = 512
VMEM_LIMIT = 56 * 1024 * 1024

F32 = jnp.float32
BF16 = jnp.bfloat16
NT_DIMS = (((1,), (1,)), ((), ()))


def _sigmoid(x):
    return 1.0 / (1.0 + jnp.exp(-x))


def _silu(x):
    return x * _sigmoid(x)


def _gelu_tanh(x):
    return 0.5 * x * (1.0 + jnp.tanh(np.sqrt(2.0 / np.pi).astype(np.float32) * (x + 0.044715 * (x * x * x))))


def _dot(a, b):
    return jnp.dot(a, b, preferred_element_type=F32)


def _dot_nt(a, b):
    return lax.dot_general(a, b, NT_DIMS, preferred_element_type=F32)


def _masked_softmax(s, mask):
    s = jnp.where(mask, s, NEG)
    m = jnp.max(s, axis=-1, keepdims=True)
    p = jnp.where(mask, jnp.exp(s - m), 0.0)
    return p / jnp.maximum(jnp.sum(p, axis=-1, keepdims=True), 1e-30)


def _in_proj_kernel(x_ref, wnorm_ref, wtok_ref, wfeat_ref, gcol_ref, cos_ref, sin_ref, *rest):
    glu_ref, bin_ref, sz_ref, gates_ref, qt_ref, cmp_ref, slc_ref, win_ref = rest[-8:]
    x = x_ref[...]
    ms = jnp.mean(x * x, axis=-1, keepdims=True)
    h = (x * lax.rsqrt(ms + EPS) * wnorm_ref[...]).astype(BF16)

    a = _dot(h, wtok_ref[...])
    o = 0
    glu_ref[...] = a[:, o:o + D_CONV] * _sigmoid(a[:, o + D_CONV:o + 2 * D_CONV])
    o += 2 * D_CONV
    bin_ref[...] = a[:, o:o + D_POOL]
    o += D_POOL
    z = a[:, o:o + D_CONV + D_POOL + D_ATTN]
    sz_ref[...] = _silu(z)
    o += D_CONV + D_POOL + D_ATTN
    gates_ref[...] = _sigmoid(a[:, o:o + GATE_PAD])

    f = _dot_nt(wfeat_ref[...], h)
    cos = cos_ref[...]
    sin = sin_ref[...]
    kv_refs = (cmp_ref, slc_ref, win_ref)
    for hb in range(N_FEAT_BLOCKS):
        blk = f[hb * HEAD_DIM:(hb + 1) * HEAD_DIM, :]
        kv_blk = hb - N_HEADS
        is_value = kv_blk >= 0 and (kv_blk % (2 * N_KV_HEADS)) >= N_KV_HEADS
        if not is_value:
            bms = jnp.mean(blk * blk, axis=0, keepdims=True)
            y = blk * lax.rsqrt(bms + EPS) * gcol_ref[hb * HEAD_DIM:(hb + 1) * HEAD_DIM, :]
            x1 = y[0:ROT_HALF]
            x2 = y[ROT_HALF:ROT_DIM]
            blk = jnp.concatenate([x1 * cos - x2 * sin, x2 * cos + x1 * sin, y[ROT_DIM:]], axis=0)
        if kv_blk < 0:
            qt_ref[hb * HEAD_DIM:(hb + 1) * HEAD_DIM, :] = blk
        else:
            r = kv_blk % (2 * N_KV_HEADS)
            kv_refs[kv_blk // (2 * N_KV_HEADS)][r * HEAD_DIM:(r + 1) * HEAD_DIM, :] = blk


def _in_proj(x, wnorm, wtok, wfeat, gcol, cos_t, sin_t, tm, l, depth, kv_prev):
    n, t, _ = x.shape
    grid = (n, t // tm)
    tok = lambda w: pl.BlockSpec((None, tm, w), lambda i, j: (i, j, 0))
    feat = lambda w: pl.BlockSpec((None, w, tm), lambda i, j: (i, 0, j))
    kv_spec = pl.BlockSpec((None, None, D_ROWS, tm), lambda i, j: (l, i, 0, j))
    full = lambda a: pl.BlockSpec(a.shape, lambda i, j: (0,) * a.ndim)
    tok_shape = lambda w: jax.ShapeDtypeStruct((n, t, w), F32)
    kv_shape = jax.ShapeDtypeStruct((depth, n, D_ROWS, t), F32)
    n_in = 7
    kv_prev = () if kv_prev is None else tuple(kv_prev)
    return pl.pallas_call(
        _in_proj_kernel,
        grid=grid,
        in_specs=[tok(D_MODEL), full(wnorm), full(wtok), full(wfeat), full(gcol),
                  pl.BlockSpec((ROT_HALF, tm), lambda i, j: (0, j)),
                  pl.BlockSpec((ROT_HALF, tm), lambda i, j: (0, j))]
                 + [pl.BlockSpec(memory_space=pl.ANY)] * len(kv_prev),
        out_specs=[tok(D_CONV), tok(D_POOL), tok(D_CONV + D_POOL + D_ATTN), tok(GATE_PAD),
                   feat(D_ATTN), kv_spec, kv_spec, kv_spec],
        out_shape=[tok_shape(D_CONV), tok_shape(D_POOL), tok_shape(D_CONV + D_POOL + D_ATTN), tok_shape(GATE_PAD),
                   jax.ShapeDtypeStruct((n, D_ATTN, t), F32), kv_shape, kv_shape, kv_shape],
        input_output_aliases={n_in + k: 5 + k for k in range(len(kv_prev))},
        compiler_params=pltpu.CompilerParams(dimension_semantics=("arbitrary", "arbitrary"),
                                             vmem_limit_bytes=VMEM_LIMIT),
        name="in_proj",
    )(x, wnorm, wtok, wfeat, gcol, cos_t, sin_t, *kv_prev)


def _conv_tail(acc, sz_a, bdw, lng, lnb, wpw, bpw):
    y = acc + bdw
    mu = jnp.mean(y, axis=-1, keepdims=True)
    yc = y - mu
    var = jnp.mean(yc * yc, axis=-1, keepdims=True)
    y = yc * lax.rsqrt(var + EPS) * lng + lnb
    y = _dot(_silu(y).astype(BF16), wpw) + bpw
    return y * sz_a


def _pool_tail(total, cnt, xcur, sz_b, wpool, pscale):
    d = total / cnt - xcur
    y = _dot(d.astype(BF16), wpool) * pscale
    return y * sz_b


def _pool_window_of_lane(shape):
    lane = lax.broadcasted_iota(jnp.int32, shape, len(shape) - 1)
    w = jnp.full(shape, POOL_WINDOWS[0], jnp.int32)
    for gi in range(1, len(POOL_WINDOWS)):
        w = jnp.where(lane >= gi * POOL_GROUP, POOL_WINDOWS[gi], w)
    return w


def _pool_select(sums, shape):
    lane = lax.broadcasted_iota(jnp.int32, shape, len(shape) - 1)
    total = sums[0]
    for gi in range(1, len(POOL_WINDOWS)):
        total = jnp.where(lane >= gi * POOL_GROUP, sums[gi], total)
    return total


CONV_HALO = 32
POOL_HALO = 16


MIX_ROWS = 64
SUBLANES = 8


def _tap_rows(buf, w_ref, r0, rows, off, taps, lanes):
    y = None
    for c in range(SUBLANES):
        part = None
        for k in taps:
            if (k + off) % SUBLANES != c:
                continue
            base = r0 + k + off - c
            term = w_ref[k:k + 1, lanes] * buf[base:base + rows + SUBLANES, lanes]
            part = term if part is None else part + term
        if part is not None:
            y = part[c:c + rows] if y is None else y + part[c:c + rows]
    return y


def _mix_out_prompt_kernel(x_ref, glu_ref, gprev_ref, bin_ref, bprev_ref, sz_ref, oat_ref, wdw_ref, bdw_ref, lng_ref,
                           lnb_ref, wpw_ref, bpw_ref, wpool_ref, pscale_ref, wout_ref, o_ref,
                           cbuf, pbuf, ymix, *, t):
    i = pl.program_id(1)
    keep = (i > 0).astype(F32)
    cbuf[0:CONV_HALO, :] = gprev_ref[...] * keep
    cbuf[CONV_HALO:CONV_HALO + t, :] = glu_ref[...]
    cbuf[CONV_HALO + t:CONV_HALO + t + SUBLANES, :] = jnp.zeros((SUBLANES, D_CONV), F32)
    pbuf[0:POOL_HALO, :] = bprev_ref[...] * keep
    pbuf[POOL_HALO:POOL_HALO + t, :] = bin_ref[...]
    nab = D_CONV + D_POOL
    for r0 in range(0, t, MIX_ROWS):
        rows = slice(r0, r0 + MIX_ROWS)
        sz = sz_ref[rows, :]
        acc = _tap_rows(cbuf, wdw_ref, r0, MIX_ROWS, CONV_HALO - CONV_STATE, range(CONV_WIDTH), slice(0, D_CONV))
        y_a = _conv_tail(acc, sz[:, 0:D_CONV], bdw_ref[...], lng_ref[...], lnb_ref[...], wpw_ref[...], bpw_ref[...])
        ymix[rows, 0:D_CONV] = y_a.astype(BF16)

        sums = []
        run = jnp.zeros((MIX_ROWS, D_POOL), F32)
        j = 0
        for w in POOL_WINDOWS:
            while j < w:
                run = run + pbuf[pl.ds(POOL_HALO + r0 - j, MIX_ROWS), :]
                j += 1
            sums.append(run)
        total = _pool_select(sums, (MIX_ROWS, D_POOL))
        pos = i * t + r0 + lax.broadcasted_iota(jnp.int32, (MIX_ROWS, D_POOL), 0)
        cnt = jnp.minimum(_pool_window_of_lane((MIX_ROWS, D_POOL)), pos + 1).astype(F32)
        y_b = _pool_tail(total, cnt, bin_ref[rows, :], sz[:, D_CONV:nab], wpool_ref[...], pscale_ref[...])
        ymix[rows, D_CONV:nab] = y_b.astype(BF16)
        ymix[rows, nab:nab + D_ATTN] = (oat_ref[rows, :] * sz[:, nab:nab + D_ATTN]).astype(BF16)
    o_ref[...] = x_ref[...] + _dot(ymix[...], wout_ref[...])


def _mix_out_prompt(x, glu, b_in, sz, o_attn, wdw, bdw, lng, lnb, wpw, bpw, wpool, pscale, wout, t):
    n, s, _ = glu.shape
    assert t % MIX_ROWS == 0 and s % t == 0
    grid = (n, s // t)
    cur = lambda w: pl.BlockSpec((None, t, w), lambda i, j: (i, j, 0))
    prev = lambda rows, w: pl.BlockSpec((None, rows, w), lambda i, j: (i, jnp.maximum(j * (t // rows) - 1, 0), 0))
    full = lambda a: pl.BlockSpec(a.shape, lambda i, j: (0,) * a.ndim)
    d_mix = D_CONV + D_POOL + D_ATTN
    return pl.pallas_call(
        functools.partial(_mix_out_prompt_kernel, t=t),
        grid=grid,
        in_specs=[cur(D_MODEL), cur(D_CONV), prev(CONV_HALO, D_CONV), cur(D_POOL), prev(POOL_HALO, D_POOL),
                  cur(d_mix), cur(D_ATTN),
                  full(wdw), full(bdw), full(lng), full(lnb), full(wpw), full(bpw), full(wpool),
                  full(pscale), full(wout)],
        out_specs=cur(D_MODEL),
        out_shape=jax.ShapeDtypeStruct(x.shape, F32),
        scratch_shapes=[pltpu.VMEM((CONV_HALO + t + SUBLANES, D_CONV), F32),
                        pltpu.VMEM((POOL_HALO + t, D_POOL), F32),
                        pltpu.VMEM((t, d_mix), BF16)],
        compiler_params=pltpu.CompilerParams(dimension_semantics=("arbitrary", "arbitrary"),
                                             vmem_limit_bytes=VMEM_LIMIT),
        name="mix_out_prompt",
    )(x, glu, glu, b_in, b_in, sz, o_attn, wdw, bdw, lng, lnb, wpw, bpw, wpool, pscale, wout)


def _convpool_sample_kernel(sc_ref, sp_ref, glu_ref, bin_ref, sz_ref, wdw_ref, bdw_ref, lng_ref, lnb_ref,
                            wpw_ref, bpw_ref, wpool_ref, pscale_ref, y_ref, sc_out, sp_out, *, pos):
    glu = glu_ref[...]
    acc = wdw_ref[CONV_STATE:CONV_WIDTH, :] * glu
    for k in range(CONV_STATE):
        acc = acc + wdw_ref[k:k + 1, :] * sc_ref[k]
    sz = sz_ref[...]
    y_ref[:, 0:D_CONV] = _conv_tail(acc, sz[:, 0:D_CONV], bdw_ref[...], lng_ref[...], lnb_ref[...],
                                    wpw_ref[...], bpw_ref[...])
    for k in range(CONV_STATE - 1):
        sc_out[k] = sc_ref[k + 1]
    sc_out[CONV_STATE - 1] = glu

    xcur = bin_ref[...]
    shape = xcur.shape
    sums = []
    run = xcur
    j = 1
    for w in POOL_WINDOWS:
        while j < w:
            run = run + sp_ref[POOL_STATE - j]
            j += 1
        sums.append(run)
    total = _pool_select(sums, shape)
    cnt = jnp.minimum(_pool_window_of_lane(shape), pos + 1).astype(F32)
    y_ref[:, D_CONV:D_CONV + D_POOL] = _pool_tail(total, cnt, xcur, sz[:, D_CONV:D_CONV + D_POOL],
                                                  wpool_ref[...], pscale_ref[...])
    for k in range(POOL_STATE - 1):
        sp_out[k] = sp_ref[k + 1]
    sp_out[POOL_STATE - 1] = xcur


def _convpool_sample(sc, sp, glu, b_in, sz, wdw, bdw, lng, lnb, wpw, bpw, wpool, pscale, pos):
    nb = glu.shape[0]
    return pl.pallas_call(
        functools.partial(_convpool_sample_kernel, pos=pos),
        out_shape=[jax.ShapeDtypeStruct((nb, D_CONV + D_POOL), F32),
                   jax.ShapeDtypeStruct(sc.shape, F32), jax.ShapeDtypeStruct(sp.shape, F32)],
        compiler_params=pltpu.CompilerParams(vmem_limit_bytes=VMEM_LIMIT),
        name="convpool_sample",
    )(sc, sp, glu, b_in, sz[:, 0:D_CONV + D_POOL], wdw, bdw, lng, lnb, wpw, bpw, wpool, pscale)


def _out_proj_kernel(x_ref, yab_ref, oat_ref, szc_ref, w_ref, o_ref):
    ya = yab_ref[...].astype(BF16)
    yc = (oat_ref[...] * szc_ref[...]).astype(BF16)
    nab = D_CONV + D_POOL
    o_ref[...] = x_ref[...] + _dot(ya, w_ref[0:nab, :]) + _dot(yc, w_ref[nab:nab + D_ATTN, :])


def _out_proj(x, yab, o_attn, sz, w_out, tm):
    n, t, _ = x.shape
    nab = D_CONV + D_POOL
    assert nab == D_ATTN
    grid = (n, t // tm)
    tok = lambda w, cb=0: pl.BlockSpec((None, tm, w), lambda i, j: (i, j, cb))
    return pl.pallas_call(
        _out_proj_kernel,
        grid=grid,
        in_specs=[tok(D_MODEL), tok(nab), tok(D_ATTN), tok(D_ATTN, 1),
                  pl.BlockSpec(w_out.shape, lambda i, j: (0, 0))],
        out_specs=tok(D_MODEL),
        out_shape=jax.ShapeDtypeStruct(x.shape, F32),
        compiler_params=pltpu.CompilerParams(dimension_semantics=("arbitrary", "arbitrary"),
                                             vmem_limit_bytes=VMEM_LIMIT),
        name="out_proj",
    )(x, yab, o_attn, sz, w_out)


GD = N_KV_HEADS * HEAD_DIM
CMP_FLAT = CMP_STRIDE * GD


FLAT_SPAN = CMP_STRIDE * CMP_STRIDE


def _piece_perm():
    m = np.arange(FLAT_SPAN)
    src = (m % CMP_STRIDE) * CMP_STRIDE + m // CMP_STRIDE
    return jnp.asarray((src[:, None] == np.arange(FLAT_SPAN)[None, :]).astype(np.float32), BF16)


def _fill_flat(x_t, perm, flat_ref, piece0):
    y = _dot_nt(perm, x_t).astype(BF16)
    for kv in range(2):
        for r in range(CMP_STRIDE):
            flat_ref[kv, piece0:piece0 + CMP_STRIDE, r * GD:(r + 1) * GD] = (
                y[r * CMP_STRIDE:(r + 1) * CMP_STRIDE, kv * GD:(kv + 1) * GD])


def _compress_mlp(flat, w1ab, bias, w2):
    n_pc = flat.shape[0]
    hb = _dot(flat, w1ab)
    h = hb[:, 0:GD] + jnp.concatenate([hb[1:n_pc, GD:2 * GD], jnp.zeros((1, GD), F32)], axis=0) + bias
    out = _dot(_gelu_tanh(h).astype(BF16), w2)
    row = lax.broadcasted_iota(jnp.int32, out.shape, 0)
    return jnp.where(row < n_pc - 1, out, 0.0)


def _cmp_bias_kernel(pe_ref, w1_ref, o_ref):
    for kv in range(2):
        o_ref[kv] = _dot(pe_ref[kv].astype(BF16), w1_ref[kv])


def _cmp_bias(pe_flat, w1):
    return pl.pallas_call(_cmp_bias_kernel, out_shape=jax.ShapeDtypeStruct((2, 1, HEAD_DIM), F32),
                          name="cmp_bias")(pe_flat, w1)


def _compress_prompt_kernel(x_ref, perm_ref, w1ab_ref, bias_ref, w2_ref, o_ref, flat_ref, *, s):
    for t in range(s // FLAT_SPAN):
        _fill_flat(x_ref[:, t * FLAT_SPAN:(t + 1) * FLAT_SPAN].astype(BF16), perm_ref[...], flat_ref, t * CMP_STRIDE)
    for kv in range(2):
        o_ref[:, kv * GD:(kv + 1) * GD] = _compress_mlp(flat_ref[kv], w1ab_ref[kv], bias_ref[kv], w2_ref[kv])


def _compress_prompt(cmp_t, l, w1ab, bias, w2):
    _, n, _, s = cmp_t.shape
    assert s % FLAT_SPAN == 0
    n_pc = s // CMP_STRIDE
    perm = _piece_perm()
    full = lambda a: pl.BlockSpec(a.shape, lambda i: (0,) * a.ndim)
    return pl.pallas_call(
        functools.partial(_compress_prompt_kernel, s=s),
        grid=(n,),
        in_specs=[pl.BlockSpec((None, None, D_ROWS, s), lambda i: (l, i, 0, 0)),
                  full(perm), full(w1ab), full(bias), full(w2)],
        out_specs=pl.BlockSpec((None, n_pc, 2 * GD), lambda i: (i, 0, 0)),
        out_shape=jax.ShapeDtypeStruct((n, n_pc, 2 * GD), F32),
        scratch_shapes=[pltpu.VMEM((2, n_pc, CMP_FLAT), BF16)],
        compiler_params=pltpu.CompilerParams(dimension_semantics=("arbitrary",), vmem_limit_bytes=VMEM_LIMIT),
        name="compress_prompt",
    )(cmp_t, perm, w1ab, bias, w2)


def _select_blocks(imp_t, pos_row, n_blocks):
    shape = imp_t.shape
    assert n_blocks % 8 == 0
    j = lax.broadcasted_iota(jnp.int32, shape, 0)
    cur = pos_row // SLC_BLOCK
    forced = (j == 0) | (j == cur) | (j == cur - 1)
    score = jnp.where(j <= cur, jnp.where(forced, BIG, imp_t), -BIG)
    tiles = [score[8 * v:8 * v + 8] for v in range(n_blocks // 8)]
    j8 = lax.broadcasted_iota(jnp.int32, (8, shape[1]), 0)
    ranks = [jnp.zeros((8, shape[1]), F32) for _ in tiles]
    for jp in range(n_blocks):
        sj = score[jp:jp + 1, :]
        for v, tile in enumerate(tiles):
            if jp < 8 * v:
                beats = sj >= tile
            elif jp >= 8 * v + 8:
                beats = sj > tile
            else:
                beats = (sj > tile) | ((sj == tile) & (j8 > jp - 8 * v))
            ranks[v] = ranks[v] + jnp.where(beats, 1.0, 0.0)
    rank = jnp.concatenate(ranks, axis=0)
    return (rank < float(min(N_SEL, n_blocks))).astype(F32)


def _col_softmax(s):
    m = jnp.max(s, axis=0, keepdims=True)
    e = jnp.exp(s - m)
    inv = jnp.where(m > 0.5 * NEG, 1.0 / jnp.maximum(jnp.sum(e, axis=0, keepdims=True), 1e-30), 0.0)
    return e, inv


ONES_ROWS = 16


def _with_ones(v_t):
    return jnp.concatenate([v_t, jnp.ones((ONES_ROWS, v_t.shape[1]), BF16)], axis=0)


def _normalised(acc):
    return acc[0:HEAD_DIM] * (1.0 / jnp.maximum(acc[HEAD_DIM:HEAD_DIM + 1], 1e-30))


def _mask_heads(sc, ok, tq):
    return jnp.concatenate([jnp.where(ok, sc[:, r * tq:(r + 1) * tq], NEG) for r in range(N_REP)], axis=1)


def _attn_prompt_kernel(qt_ref, kcv_ref, slc_ref, win_ref, gates_ref, ovt_ref, o_ref,
                        ks_ref, kw_ref, s_ref, sel_ref, acc_ref, m_ref, *, s, n_blk):
    tq = Q_BLOCK
    cols = N_REP * tq
    groups = range(N_KV_HEADS)
    qc = pl.program_id(1)
    q0 = qc * tq
    pos_row = q0 + lax.broadcasted_iota(jnp.int32, (1, tq), 1)
    n_slc = s // SLC_BLOCK
    n_cmp = kcv_ref.shape[0]
    scale = HEAD_DIM ** -0.5
    v_row0 = lambda g: (N_KV_HEADS + g) * HEAD_DIM

    @pl.when(qc == 0)
    def _():
        for g in range(N_KV_HEADS):
            for t in range(s // 128):
                cols_t = slice(t * 128, (t + 1) * 128)
                ks_ref[g, cols_t, :] = slc_ref[g * HEAD_DIM:(g + 1) * HEAD_DIM, cols_t].T.astype(BF16)
                kw_ref[g, cols_t, :] = win_ref[g * HEAD_DIM:(g + 1) * HEAD_DIM, cols_t].T.astype(BF16)

    gates_t = gates_ref[...].T
    vc_t = kcv_ref[:, GD:2 * GD].T.astype(BF16)
    q_ts = []
    for g in groups:
        q_t = jnp.concatenate([qt_ref[(g * N_REP + r) * HEAD_DIM:(g * N_REP + r + 1) * HEAD_DIM, :]
                               for r in range(N_REP)], axis=1)
        q_ts.append((q_t * scale).astype(BF16))

    c_idx = lax.broadcasted_iota(jnp.int32, (n_cmp, 1), 0)
    valid_c = (c_idx * CMP_STRIDE + (CMP_BLOCK - 1) <= pos_row) & (c_idx < n_blk)
    ovt = ovt_ref[...]
    o_cs = []
    for g in groups:
        kc = kcv_ref[:, g * HEAD_DIM:(g + 1) * HEAD_DIM].astype(BF16)
        e_c, inv_c = _col_softmax(_mask_heads(_dot(kc, q_ts[g]), valid_c, tq))
        o_cs.append(_dot(vc_t[g * HEAD_DIM:(g + 1) * HEAD_DIM, :], e_c.astype(BF16)) * inv_c)
        pc_sum = e_c[:, 0:tq] * inv_c[:, 0:tq]
        for r in range(1, N_REP):
            pc_sum = pc_sum + e_c[:, r * tq:(r + 1) * tq] * inv_c[:, r * tq:(r + 1) * tq]
        pc_hi = pc_sum.astype(BF16)
        pc_lo = (pc_sum - pc_hi.astype(F32)).astype(BF16)
        imp_t = _dot(ovt, pc_hi) + _dot(ovt, pc_lo)
        sel_ref[g] = _select_blocks(imp_t, pos_row, n_slc)

    n_keys = q0 + tq
    rest = n_keys % SEL_KEYS
    half_tail = (rest > 0) & (rest <= SEL_KEYS // 2)
    n_steps = n_keys // SEL_KEYS + (rest > SEL_KEYS // 2).astype(jnp.int32)
    tail_k0 = pl.multiple_of((n_keys // SEL_KEYS) * SEL_KEYS, SEL_KEYS)

    def score_chunk(k0, size):
        key = k0 + lax.broadcasted_iota(jnp.int32, (size, 1), 0)
        causal = key <= pos_row
        for g in groups:
            sc = _dot(ks_ref[g, pl.ds(k0, size), :], q_ts[g])
            parts = []
            for jj in range(size // SLC_BLOCK):
                rows = slice(jj * SLC_BLOCK, (jj + 1) * SLC_BLOCK)
                chosen = sel_ref[g, pl.ds(k0 // SLC_BLOCK + jj, 1), :] > 0.5
                parts.append(_mask_heads(sc[rows], chosen & causal[rows], tq))
            sc = jnp.concatenate(parts, axis=0)
            s_ref[g, pl.ds(k0, size), :] = sc.astype(BF16)
            m_ref[g] = jnp.maximum(m_ref[g], jnp.max(sc, axis=0, keepdims=True))

    m_ref[...] = jnp.full(m_ref.shape, NEG, F32)

    def score_step(i, carry):
        score_chunk(pl.multiple_of(i * SEL_KEYS, SEL_KEYS), SEL_KEYS)
        return carry

    lax.fori_loop(0, n_steps, score_step, 0)

    @pl.when(half_tail)
    def _():
        score_chunk(tail_k0, SEL_KEYS // 2)

    m_sel = [m_ref[g].astype(BF16) for g in groups]

    acc_ref[...] = jnp.zeros(acc_ref.shape, F32)

    def value_chunk(k0, size):
        for g in groups:
            p = jnp.exp(s_ref[g, pl.ds(k0, size), :] - m_sel[g])
            v_t = slc_ref[v_row0(g):v_row0(g) + HEAD_DIM, pl.ds(k0, size)].astype(BF16)
            acc_ref[g] += _dot(_with_ones(v_t), p)

    def value_step(i, carry):
        value_chunk(pl.multiple_of(i * SEL_KEYS, SEL_KEYS), SEL_KEYS)
        return carry

    lax.fori_loop(0, n_steps, value_step, 0)

    @pl.when(half_tail)
    def _():
        value_chunk(tail_k0, SEL_KEYS // 2)

    n_prev = -(-(WINDOW - 1) // tq)
    k0s, oks = [], []
    for b in range(n_prev + 1):
        kb = qc - n_prev + b
        k0s.append(pl.multiple_of(jnp.maximum(kb, 0) * tq, tq))
        kp = kb * tq + lax.broadcasted_iota(jnp.int32, (tq, 1), 0)
        oks.append((kp <= pos_row) & (kp > pos_row - WINDOW) & (kp >= 0))

    out_blocks = []
    for g in groups:
        o_s = _normalised(acc_ref[g])
        s_w = jnp.concatenate([_mask_heads(_dot(kw_ref[g, pl.ds(k0s[b], tq), :], q_ts[g]), oks[b], tq)
                               for b in range(n_prev + 1)], axis=0)
        e_w = jnp.exp(s_w.astype(BF16) - jnp.max(s_w, axis=0, keepdims=True).astype(BF16))
        o_w = _dot(_with_ones(win_ref[v_row0(g):v_row0(g) + HEAD_DIM, pl.ds(k0s[0], tq)].astype(BF16)), e_w[0:tq])
        for b in range(1, n_prev + 1):
            o_w = o_w + _dot(_with_ones(win_ref[v_row0(g):v_row0(g) + HEAD_DIM, pl.ds(k0s[b], tq)].astype(BF16)),
                             e_w[b * tq:(b + 1) * tq])
        o_w = _normalised(o_w)
        for r in range(N_REP):
            h = g * N_REP + r
            c = slice(r * tq, (r + 1) * tq)
            out_blocks.append(gates_t[h:h + 1] * o_cs[g][:, c] + gates_t[N_HEADS + h:N_HEADS + h + 1] * o_s[:, c]
                              + gates_t[2 * N_HEADS + h:2 * N_HEADS + h + 1] * o_w[:, c])
    o_ref[...] = jnp.concatenate(out_blocks, axis=0).T


def _overlap_t(n_cmp_rows, n_blk, n_slc):
    c = np.arange(n_cmp_rows)
    start = c * CMP_STRIDE
    end = start + CMP_BLOCK - 1
    j0 = np.arange(n_slc) * SLC_BLOCK
    ov = (end[None, :] >= j0[:, None]) & (start[None, :] < j0[:, None] + SLC_BLOCK) & (c[None, :] < n_blk)
    return jnp.asarray(ov.astype(np.float32), BF16)


def _attn_prompt(q_t, kcv, slc_t, win_t, gates, l):
    n, _, s = q_t.shape
    n_cmp = kcv.shape[1]
    n_blk = s // CMP_STRIDE - CMP_BLOCK // CMP_STRIDE + 1
    n_slc = s // SLC_BLOCK
    ovt = _overlap_t(n_cmp, n_blk, n_slc)
    tq = Q_BLOCK
    cols = N_REP * tq
    assert s % SEL_KEYS == 0 and (SEL_KEYS // 2) % SLC_BLOCK == 0
    assert (SEL_KEYS // 2) % tq == 0 or tq % SEL_KEYS == 0
    return pl.pallas_call(
        functools.partial(_attn_prompt_kernel, s=s, n_blk=n_blk),
        grid=(n, s // tq),
        in_specs=[pl.BlockSpec((None, D_ATTN, tq), lambda i, j: (i, 0, j)),
                  pl.BlockSpec((None,) + kcv.shape[1:], lambda i, j: (i, 0, 0)),
                  pl.BlockSpec((None, None, D_ROWS, s), lambda i, j: (l, i, 0, 0)),
                  pl.BlockSpec((None, None, D_ROWS, s), lambda i, j: (l, i, 0, 0)),
                  pl.BlockSpec((None, tq, GATE_PAD), lambda i, j: (i, j, 0)),
                  pl.BlockSpec(ovt.shape, lambda i, j: (0, 0))],
        out_specs=pl.BlockSpec((None, tq, D_ATTN), lambda i, j: (i, j, 0)),
        out_shape=jax.ShapeDtypeStruct((n, s, D_ATTN), F32),
        scratch_shapes=[pltpu.VMEM((N_KV_HEADS, s, HEAD_DIM), BF16),
                        pltpu.VMEM((N_KV_HEADS, s, HEAD_DIM), BF16),
                        pltpu.VMEM((N_KV_HEADS, s, cols), BF16),
                        pltpu.VMEM((N_KV_HEADS, n_slc, tq), F32),
                        pltpu.VMEM((N_KV_HEADS, HEAD_DIM + ONES_ROWS, cols), F32),
                        pltpu.VMEM((N_KV_HEADS, 1, cols), F32)],
        compiler_params=pltpu.CompilerParams(dimension_semantics=("arbitrary", "arbitrary"),
                                             vmem_limit_bytes=VMEM_LIMIT),
        name="attn_prompt",
    )(q_t, kcv, slc_t, win_t, gates, ovt)


PAGE = 128
SAMPLE_SEQS = 4


def _sample_geometry(past):
    length = past + 1
    padded = -(-length // SLC_BLOCK) * SLC_BLOCK
    n_pc = padded // CMP_STRIDE
    n_pc_pad = -(-n_pc // CMP_STRIDE) * CMP_STRIDE
    return dict(n_pc=n_pc, n_pc_pad=n_pc_pad, n_blk=n_pc - CMP_BLOCK // CMP_STRIDE + 1, n_slc=padded // SLC_BLOCK)


def _attn_sample_kernel(pt_ref, q_ref, new_ref, gates_ref, wincol_ref, cwin_ref, ov_ref,
                        perm_ref, w1ab_ref, bias_ref, w2_ref, ccmp_hbm, cslc_hbm, *rest, l, past, n_pages, nseq):
    o_ref, wout_ref, flat_ref, cmp_buf, slc_buf, sem = rest[-6:]
    step = pl.program_id(0)
    slot = step % 2

    def page_copies(of_step, into_slot, lookup):
        copies = []
        for u in range(nseq):
            for j in range(n_pages):
                page = pt_ref[of_step * nseq + u, j] if lookup else 0
                k = u * n_pages + j
                copies.append(pltpu.make_async_copy(ccmp_hbm.at[l, page], cmp_buf.at[into_slot, k], sem.at[into_slot, 0]))
                copies.append(pltpu.make_async_copy(cslc_hbm.at[l, page], slc_buf.at[into_slot, k], sem.at[into_slot, 1]))
        return copies

    @pl.when(step == 0)
    def _():
        for c in page_copies(0, 0, True):
            c.start()

    @pl.when(step + 1 < pl.num_programs(0))
    def _():
        for c in page_copies(step + 1, 1 - slot, True):
            c.start()

    for c in page_copies(step, slot, False):
        c.wait()
    cmp_pages = [[cmp_buf.at[slot, u * n_pages + j] for j in range(n_pages)] for u in range(nseq)]
    slc_pages = [[slc_buf.at[slot, u * n_pages + j] for j in range(n_pages)] for u in range(nseq)]
    geo = _sample_geometry(past)
    n_pc_pad, n_blk, n_slc = geo["n_pc_pad"], geo["n_blk"], geo["n_slc"]
    pos = past
    lb = cwin_ref.shape[-1]
    scale = HEAD_DIM ** -0.5
    head = lax.broadcasted_iota(jnp.int32, (N_HEADS, 1), 0)
    lane = lax.broadcasted_iota(jnp.int32, (1, GD), 1)
    lane_h = lax.broadcasted_iota(jnp.int32, (N_HEADS, GD), 1)
    own = (lane_h // HEAD_DIM) == (head // N_REP)

    def new_row(u, branch, kv):
        o = (branch * 2 + kv) * GD
        return new_ref[u, :, o:o + GD]

    def rounded(x):
        return x.astype(BF16).astype(F32)

    kcv = []
    pages_per_span = FLAT_SPAN // PAGE
    past_pc = past // CMP_STRIDE
    for u in range(nseq):
        row0 = u * n_pc_pad
        for t in range(n_pages // pages_per_span):
            x_t = jnp.concatenate([cmp_pages[u][t * pages_per_span + w][...].reshape(2 * GD, PAGE)
                                   for w in range(pages_per_span)], axis=1)
            _fill_flat(x_t.astype(BF16), perm_ref[...], flat_ref, row0 + t * CMP_STRIDE)
        for kv in range(2):
            flat_ref[kv, row0 + past_pc:row0 + n_pc_pad, :] = jnp.zeros((n_pc_pad - past_pc, CMP_FLAT), BF16)
            flat_ref[kv, row0 + past_pc:row0 + past_pc + 1, 0:GD] = new_row(u, 0, kv).astype(BF16)
    for kv in range(2):
        kcv.append(_compress_mlp(flat_ref[kv], w1ab_ref[kv], bias_ref[kv], w2_ref[kv]).astype(BF16))
    for u in range(nseq):
        _attn_sample_one(u, kcv[0][u * n_pc_pad:(u + 1) * n_pc_pad], kcv[1][u * n_pc_pad:(u + 1) * n_pc_pad],
                         q_ref, new_row, rounded, gates_ref, wincol_ref, cwin_ref, ov_ref,
                         slc_pages[u], o_ref, wout_ref, geo=geo, past=past, consts=(head, lane, lane_h, own))


def _attn_sample_one(u, kc, vc, q_ref, new_row, rounded, gates_ref, wincol_ref, cwin_ref, ov_ref,
                     slc_pages, o_ref, wout_ref, *, geo, past, consts):
    head, lane, lane_h, own = consts
    n_pc_pad, n_blk, n_slc = geo["n_pc_pad"], geo["n_blk"], geo["n_slc"]
    n_pages = len(slc_pages)
    pos = past
    lb = cwin_ref.shape[-1]
    scale = HEAD_DIM ** -0.5
    q_rows = jnp.concatenate([q_ref[u, :, h * HEAD_DIM:(h + 1) * HEAD_DIM] for h in range(N_HEADS)], axis=0)
    qx = (jnp.where(own, jnp.concatenate([q_rows] * N_KV_HEADS, axis=1), 0.0) * scale).astype(BF16)
    qx32 = qx.astype(F32)
    c_idx = lax.broadcasted_iota(jnp.int32, (1, n_pc_pad), 1)
    valid = (c_idx * CMP_STRIDE + (CMP_BLOCK - 1) <= pos) & (c_idx < n_blk)
    p_c = _masked_softmax(_dot_nt(qx, kc), valid)
    o_c = _dot(p_c.astype(BF16), vc)

    cur = pos // SLC_BLOCK
    forced = (lane == 0) | (lane == cur) | (lane == cur - 1)
    jp = lax.broadcasted_iota(jnp.int32, (GD, GD), 0)
    jj = lax.broadcasted_iota(jnp.int32, (GD, GD), 1)
    sels = []
    for g in range(N_KV_HEADS):
        pc_sum = jnp.sum(p_c[g * N_REP:(g + 1) * N_REP], axis=0, keepdims=True)
        pc_hi = pc_sum.astype(BF16)
        pc_lo = (pc_sum - pc_hi.astype(F32)).astype(BF16)
        imp = _dot(pc_hi, ov_ref[...]) + _dot(pc_lo, ov_ref[...])
        score = jnp.where(lane <= cur, jnp.where(forced, BIG, imp), -BIG)
        score = jnp.where(lane < n_slc, score, -2.0 * BIG)
        score_b = jnp.broadcast_to(score, (GD, GD))
        score_a = score_b.T
        beats = (score_a > score_b) | ((score_a == score_b) & (jp < jj))
        rank = jnp.sum(beats.astype(F32), axis=0, keepdims=True)
        sel = ((rank < float(min(N_SEL, n_slc))) & (lane < n_slc)).astype(F32)
        sels.append(jnp.broadcast_to(sel, (N_REP, GD)))
    sel_h = jnp.concatenate(sels, axis=0)

    blocks_per_tile = GD // SLC_BLOCK
    chosen = []
    for t in range(past // GD):
        tile = sel_h[:, t * blocks_per_tile:t * blocks_per_tile + 1]
        for b in range(1, blocks_per_tile):
            tile = jnp.where(lane >= b * SLC_BLOCK, sel_h[:, t * blocks_per_tile + b:t * blocks_per_tile + b + 1], tile)
        chosen.append(jnp.broadcast_to(tile, (N_HEADS, GD)))
    chosen = jnp.concatenate(chosen, axis=-1) > 0.5
    s_past = jnp.concatenate([_dot(qx, slc_pages[j][0].astype(BF16)) for j in range(n_pages)], axis=-1)
    s_past = jnp.where(chosen, s_past, NEG)
    new_ok = jnp.sum(jnp.where(lane == pos // SLC_BLOCK, sel_h, 0.0), axis=-1, keepdims=True) > 0.5
    s_new = jnp.where(new_ok, jnp.sum(qx32 * rounded(new_row(u, 1, 0)), axis=-1, keepdims=True), NEG)
    m = jnp.maximum(jnp.max(s_past, axis=-1, keepdims=True), s_new)
    p_past = jnp.where(chosen, jnp.exp(s_past - m), 0.0)
    p_new = jnp.where(new_ok, jnp.exp(s_new - m), 0.0)
    denom = jnp.maximum(jnp.sum(p_past, axis=-1, keepdims=True) + p_new, 1e-30)
    o_s = rounded(p_new) * rounded(new_row(u, 1, 1))
    for j in range(n_pages):
        o_s = o_s + _dot_nt(p_past[:, j * PAGE:(j + 1) * PAGE].astype(BF16), slc_pages[j][1].astype(BF16))
    o_s = o_s / denom

    kpos = (past - lb) + lax.broadcasted_iota(jnp.int32, (1, lb), 1)
    w_ok = (kpos <= pos) & (kpos > pos - WINDOW)
    s_w = jnp.where(w_ok, _dot(qx, cwin_ref[u, 0].astype(BF16)), NEG)
    s_wn = jnp.sum(qx32 * rounded(new_row(u, 2, 0)), axis=-1, keepdims=True)
    m = jnp.maximum(jnp.max(s_w, axis=-1, keepdims=True), s_wn)
    p_w = jnp.where(w_ok, jnp.exp(s_w - m), 0.0)
    p_wn = jnp.exp(s_wn - m)
    denom = jnp.maximum(jnp.sum(p_w, axis=-1, keepdims=True) + p_wn, 1e-30)
    o_w = (_dot_nt(p_w.astype(BF16), cwin_ref[u, 1].astype(BF16))
           + rounded(p_wn) * rounded(new_row(u, 2, 1))) / denom

    gates = gates_ref[u]
    gate = lambda br: jnp.sum(jnp.where(lane_h == head + br * N_HEADS, gates, 0.0), axis=-1, keepdims=True)
    o = gate(0) * o_c + gate(1) * o_s + gate(2) * o_w
    for h in range(N_HEADS):
        g = h // N_REP
        o_ref[u, :, h * HEAD_DIM:(h + 1) * HEAD_DIM] = o[h:h + 1, g * HEAD_DIM:(g + 1) * HEAD_DIM]

    for kv in range(2):
        wout_ref[u, kv, :, 0:lb - 1] = cwin_ref[u, kv, :, 1:lb]
        wout_ref[u, kv, :, lb - 1:lb] = wincol_ref[u, kv * GD:(kv + 1) * GD, :]


def _attn_sample(l, q_tok, new_tok, gates, win_col, cwin_t, ccmp_t, cslc_t, page_table, w1ab, bias, w2, win_prev):
    nb = q_tok.shape[0]
    n_pages = page_table.shape[1]
    past = n_pages * PAGE
    geo = _sample_geometry(past)
    lb = cwin_t.shape[-1]
    assert GATE_PAD == GD and FLAT_SPAN % PAGE == 0 and n_pages % (FLAT_SPAN // PAGE) == 0
    perm = _piece_perm()
    ov = _overlap_t(geo["n_pc_pad"], geo["n_blk"], GD).T
    ov = jnp.where(jnp.arange(GD)[None, :] < geo["n_slc"], ov, 0).astype(BF16)
    assert GD % SLC_BLOCK == 0 and past % GD == 0
    nseq = SAMPLE_SEQS if nb % SAMPLE_SEQS == 0 else 1
    seq = lambda a: pl.BlockSpec((nseq,) + a.shape[1:], lambda i, pt: (i,) + (0,) * (a.ndim - 1))
    full = lambda a: pl.BlockSpec(a.shape, lambda i, pt: (0,) * a.ndim)
    operands = [page_table, q_tok, new_tok, gates, win_col, cwin_t, ov, perm, w1ab, bias, w2, ccmp_t, cslc_t]
    win_prev = () if win_prev is None else (win_prev,)
    page_buf = pltpu.VMEM((2, nseq * n_pages, 2, GD, PAGE), F32)
    grid_spec = pltpu.PrefetchScalarGridSpec(
        num_scalar_prefetch=1,
        grid=(nb // nseq,),
        in_specs=[seq(q_tok), seq(new_tok), seq(gates), seq(win_col),
                  pl.BlockSpec((None, nseq, 2, GD, lb), lambda i, pt: (l, i, 0, 0, 0)),
                  full(ov), full(perm), full(w1ab), full(bias), full(w2)]
                 + [pl.BlockSpec(memory_space=pl.ANY)] * (2 + len(win_prev)),
        out_specs=[pl.BlockSpec((nseq, 1, D_ATTN), lambda i, pt: (i, 0, 0)),
                   pl.BlockSpec((None, nseq, 2, GD, lb), lambda i, pt: (l, i, 0, 0, 0))],
        scratch_shapes=[pltpu.VMEM((2, nseq * geo["n_pc_pad"], CMP_FLAT), BF16), page_buf, page_buf,
                        pltpu.SemaphoreType.DMA((2, 2))],
    )
    return pl.pallas_call(
        functools.partial(_attn_sample_kernel, l=l, past=past, n_pages=n_pages, nseq=nseq),
        grid_spec=grid_spec,
        out_shape=[jax.ShapeDtypeStruct((nb, 1, D_ATTN), F32),
                   jax.ShapeDtypeStruct((cwin_t.shape[0], nb, 2, GD, lb), F32)],
        input_output_aliases={len(operands): 1} if win_prev else {},
        compiler_params=pltpu.CompilerParams(dimension_semantics=("arbitrary",), vmem_limit_bytes=VMEM_LIMIT),
        name="attn_sample",
    )(*operands, *win_prev)


def _prep_layer(l, w_norm, w_in, w_out, w_dw, b_dw, ln_g, ln_b, w_pw, b_pw, w_pool, pool_scale,
                g_q, g_k, cmp_pe, cmp_w1, cmp_w2):
    sizes = (D_CONV, D_CONV, D_CONV, D_POOL, D_POOL, D_ATTN, D_KV, N_BRANCH * N_HEADS, D_ATTN)
    offs = [0] + [int(v) for v in np.cumsum(sizes)]
    col = lambda k: w_in[l][:, offs[k]:offs[k + 1]]
    a_val, a_gate, z_a, b_in, z_b, q, kv, gate, z_c = (col(k) for k in range(9))
    pad = jnp.zeros((D_MODEL, GATE_PAD - N_BRANCH * N_HEADS), F32)
    wtok = jnp.concatenate([a_val, a_gate, b_in, z_a, z_b, z_c, gate, pad], axis=1).astype(BF16)
    wfeat = jnp.concatenate([q, kv], axis=1).T.astype(BF16)
    ones = jnp.ones((N_KV_HEADS * HEAD_DIM,), F32)
    gcol = jnp.concatenate([jnp.tile(g_q[l], N_HEADS)]
                           + [piece for br in range(N_BRANCH)
                              for piece in (jnp.tile(g_k[l, br], N_KV_HEADS), ones)])[:, None]
    wpool_bd = jnp.zeros((D_POOL, D_POOL), F32)
    for gi in range(len(POOL_WINDOWS)):
        wpool_bd = wpool_bd.at[gi * POOL_GROUP:(gi + 1) * POOL_GROUP, gi * POOL_GROUP:(gi + 1) * POOL_GROUP].set(w_pool[l, gi])
    row = lambda v: v[None, :]

    def both_groups(w):
        z = jnp.zeros_like(w)
        return jnp.stack([jnp.concatenate([w, z], axis=2), jnp.concatenate([z, w], axis=2)], axis=1).reshape(-1, GD)

    half = CMP_STRIDE * HEAD_DIM
    w1 = cmp_w1[l].astype(BF16)
    w1ab = jnp.stack([jnp.concatenate(
        [both_groups(w1[kv, :half].reshape(CMP_STRIDE, HEAD_DIM, HEAD_DIM)),
         both_groups(w1[kv, half:].reshape(CMP_STRIDE, HEAD_DIM, HEAD_DIM))], axis=1) for kv in range(2)])
    w2p = jnp.stack([both_groups(cmp_w2[l, kv].astype(BF16)[None]) for kv in range(2)])
    cbias = jnp.tile(_cmp_bias(cmp_pe[l].reshape(2, 1, CMP_BLOCK * HEAD_DIM), w1), (1, 1, N_KV_HEADS))
    return dict(
        wnorm=row(w_norm[l]), wtok=wtok, wfeat=wfeat, gcol=gcol,
        wdw=w_dw[l], bdw=row(b_dw[l]), lng=row(ln_g[l]), lnb=row(ln_b[l]),
        wpw=w_pw[l].astype(BF16), bpw=row(b_pw[l]),
        wpool=wpool_bd.astype(BF16), pscale=row(pool_scale[l]),
        w1ab=w1ab, cbias=cbias, w2p=w2p,
        wout=w_out[l].astype(BF16),
    )


def _rope_tables(pos):
    inv = ROPE_THETA ** (-jnp.arange(ROT_HALF, dtype=F32) * 2.0 / ROT_DIM)
    ang = pos.astype(F32)[:, None] * inv[None, :]
    return jnp.cos(ang).T, jnp.sin(ang).T


def _prompt_layer(l, depth, x, p, cos_t, sin_t, kv_prev):
    glu, b_in, sz, gates, q_t, *kv = _in_proj(
        x, p["wnorm"], p["wtok"], p["wfeat"], p["gcol"], cos_t, sin_t, min(512, x.shape[1]), l, depth, kv_prev)
    cmp_t, slc_t, win_t = kv
    kcv = _compress_prompt(cmp_t, l, p["w1ab"], p["cbias"], p["w2p"])
    o_attn = _attn_prompt(q_t, kcv, slc_t, win_t, gates, l)
    y = _mix_out_prompt(x, glu, b_in, sz, o_attn, p["wdw"], p["bdw"], p["lng"], p["lnb"], p["wpw"], p["bpw"],
                        p["wpool"], p["pscale"], p["wout"], t=min(512, x.shape[1]))
    return y, glu, b_in, kv


def _sample_layer(l, depth, x, p, cos_t, sin_t, sc, sp, cwin_t, ccmp_t, cslc_t, page_table, past, kv_prev, win_prev):
    nb = x.shape[0]
    glu, b_in, sz, gates, q_t, *kv = _in_proj(
        x[None], p["wnorm"], p["wtok"], p["wfeat"], p["gcol"], cos_t, sin_t, nb, l, depth, kv_prev)
    glu, b_in, sz, gates = glu[0], b_in[0], sz[0], gates[0]
    yab, sc_new, sp_new = _convpool_sample(sc, sp, glu, b_in, sz, p["wdw"], p["bdw"], p["lng"], p["lnb"],
                                           p["wpw"], p["bpw"], p["wpool"], p["pscale"], pos=past)
    new_t = jnp.concatenate([a[l, 0] for a in kv], axis=0)
    new_tok = new_t.T[:, None, :]
    q_tok = q_t[0].T[:, None, :]
    win_col = kv[2][l, 0].T[:, :, None]
    o_attn, win_next = _attn_sample(l, q_tok, new_tok, gates[:, None, :], win_col, cwin_t, ccmp_t, cslc_t,
                                    page_table, p["w1ab"], p["cbias"], p["w2p"], win_prev)
    y = _out_proj(x[None], yab[None], o_attn.reshape(1, nb, D_ATTN), sz[None], p["wout"], tm=nb)[0]
    return y, sc_new, sp_new, win_next, kv


def _rows_from_feat(a):
    lead = a.shape[:-2]
    a = a.reshape(lead + (2, N_KV_HEADS, HEAD_DIM, a.shape[-1]))
    nl = len(lead)
    return jnp.transpose(a, tuple(range(nl)) + (nl + 3, nl, nl + 1, nl + 2))


def kernel(x_prompt, x_sample, state_conv, state_pool, cache_win_kv, cache_cmp_kv, cache_slc_kv, page_table,
           w_norm, w_in, w_out, w_dw, b_dw, ln_g, ln_b, w_pw, b_pw, w_pool, pool_scale,
           g_q, g_k, cmp_pe, cmp_w1, cmp_w2):
    bp, s, _ = x_prompt.shape
    bs, t_new, _ = x_sample.shape
    depth = w_in.shape[0]
    assert t_new == 1 and cache_cmp_kv.shape[2] == PAGE and s % 512 == 0
    past = page_table.shape[1] * PAGE
    lb = cache_win_kv.shape[2]

    def to_feat(c):
        c = jnp.transpose(c, (0, 1, 3, 4, 5, 2))
        return c.reshape(c.shape[:3] + (GD, c.shape[-1]))
    cwin_t, ccmp_t, cslc_t = to_feat(cache_win_kv), to_feat(cache_cmp_kv), to_feat(cache_slc_kv)
    sc_all = jnp.transpose(state_conv, (0, 2, 1, 3))
    sp_all = jnp.transpose(state_pool, (0, 2, 1, 3))
    cos_p, sin_p = _rope_tables(jnp.arange(s))
    cos_s, sin_s = _rope_tables(jnp.full((bs,), past))

    xp, xs = x_prompt, x_sample[:, 0, :]
    conv_p, pool_p, conv_s, pool_s = [], [], [], []
    kv_p = kv_s = win_s = None
    for l in range(depth):
        p = _prep_layer(l, w_norm, w_in, w_out, w_dw, b_dw, ln_g, ln_b, w_pw, b_pw, w_pool, pool_scale,
                        g_q, g_k, cmp_pe, cmp_w1, cmp_w2)
        xp, glu, b_in, kv_p = _prompt_layer(l, depth, xp, p, cos_p, sin_p, kv_p)
        conv_p.append(glu[:, s - CONV_STATE:, :])
        pool_p.append(b_in[:, s - POOL_STATE:, :])
        xs, sc_new, sp_new, win_s, kv_s = _sample_layer(
            l, depth, xs, p, cos_s, sin_s, sc_all[l], sp_all[l], cwin_t, ccmp_t, cslc_t, page_table, past,
            kv_s, win_s)
        conv_s.append(jnp.transpose(sc_new, (1, 0, 2)))
        pool_s.append(jnp.transpose(sp_new, (1, 0, 2)))
    cmp_p, slc_p, win_p = kv_p
    cmp_s, slc_s, _ = kv_s
    new_rows = lambda a: _rows_from_feat(a[:, 0])[:, :, None]
    return (xp, xs[:, None, :], jnp.stack(conv_p), jnp.stack(pool_p),
            _rows_from_feat(win_p[:, :, :, s - min(WINDOW, s):]), _rows_from_feat(cmp_p), _rows_from_feat(slc_p),
            jnp.stack(conv_s), jnp.stack(pool_s), _rows_from_feat(win_s.reshape(depth, bs, D_ROWS, lb)),
            new_rows(cmp_s), new_rows(slc_s))
```

```python
import functools

import numpy as np
import jax
import jax.numpy as jnp
from jax import lax
from jax.experimental import pallas as pl
from jax.experimental.pallas import tpu as pltpu

D_MODEL = 1024
D_CONV = 256
D_POOL = 256
N_HEADS = 8
N_KV_HEADS = 2
HEAD_DIM = 64
D_ATTN = N_HEADS * HEAD_DIM
CONV_WIDTH = 31
CONV_STATE = CONV_WIDTH - 1
POOL_WINDOWS = (2, 4, 8, 16)
POOL_GROUP = D_POOL // len(POOL_WINDOWS)
POOL_STATE = max(POOL_WINDOWS) - 1
CMP_BLOCK = 32
CMP_STRIDE = 16
SLC_BLOCK = 64
N_SEL = 8
WINDOW = 256
Q_BLOCK = 256
ROT_DIM = HEAD_DIM // 4
ROT_HALF = ROT_DIM // 2
ROPE_THETA = 500000.0
N_BRANCH = 3
D_KV = N_BRANCH * 2 * N_KV_HEADS * HEAD_DIM
D_ROWS = 2 * N_KV_HEADS * HEAD_DIM
N_REP = N_HEADS // N_KV_HEADS
EPS = 1e-6
NEG = -1e30
BIG = 1e4

GATE_ROWS = 32
D_TOK = 3 * D_CONV + 2 * D_POOL + D_ATTN
D_QKV = D_ATTN + D_KV
D_FEAT = D_QKV + GATE_ROWS
N_FEAT_BLOCKS = D_QKV // HEAD_DIM
SEL_KEYS = 512
VMEM_LIMIT = 56 * 1024 * 1024

F32 = jnp.float32
BF16 = jnp.bfloat16
NT_DIMS = (((1,), (1,)), ((), ()))


def _sigmoid(x):
    return 1.0 / (1.0 + jnp.exp(-x))


def _silu(x):
    return x * _sigmoid(x)


def _gelu_tanh(x):
    return 0.5 * x * (1.0 + jnp.tanh(np.sqrt(2.0 / np.pi).astype(np.float32) * (x + 0.044715 * (x * x * x))))


def _dot(a, b):
    return jnp.dot(a, b, preferred_element_type=F32)


def _dot_nt(a, b):
    return lax.dot_general(a, b, NT_DIMS, preferred_element_type=F32)


def _masked_softmax(s, mask):
    s = jnp.where(mask, s, NEG)
    m = jnp.max(s, axis=-1, keepdims=True)
    p = jnp.where(mask, jnp.exp(s - m), 0.0)
    return p / jnp.maximum(jnp.sum(p, axis=-1, keepdims=True), 1e-30)


def _in_proj_kernel(x_ref, wnorm_ref, wtok_ref, wfeat_ref, gcol_ref, cos_ref, sin_ref, *rest):
    glu_ref, bin_ref, sz_ref, gates_ref, qt_ref, cmp_ref, slc_ref, win_ref = rest[-8:]
    x = x_ref[...]
    ms = jnp.mean(x * x, axis=-1, keepdims=True)
    h = (x * lax.rsqrt(ms + EPS) * wnorm_ref[...]).astype(BF16)

    a = _dot(h, wtok_ref[...])
    o = 0
    glu_ref[...] = a[:, o:o + D_CONV] * _sigmoid(a[:, o + D_CONV:o + 2 * D_CONV])
    o += 2 * D_CONV
    bin_ref[...] = a[:, o:o + D_POOL]
    o += D_POOL
    z = a[:, o:o + D_CONV + D_POOL + D_ATTN]
    sz_ref[...] = _silu(z)

    f = _dot_nt(wfeat_ref[...], h)
    gates_ref[...] = _sigmoid(f[D_QKV:D_FEAT, :])
    cos = cos_ref[...]
    sin = sin_ref[...]
    kv_refs = (cmp_ref, slc_ref, win_ref)
    for hb in range(N_FEAT_BLOCKS):
        blk = f[hb * HEAD_DIM:(hb + 1) * HEAD_DIM, :]
        kv_blk = hb - N_HEADS
        is_value = kv_blk >= 0 and (kv_blk % (2 * N_KV_HEADS)) >= N_KV_HEADS
        if not is_value:
            bms = jnp.mean(blk * blk, axis=0, keepdims=True)
            y = blk * lax.rsqrt(bms + EPS) * gcol_ref[hb * HEAD_DIM:(hb + 1) * HEAD_DIM, :]
            x1 = y[0:ROT_HALF]
            x2 = y[ROT_HALF:ROT_DIM]
            blk = jnp.concatenate([x1 * cos - x2 * sin, x2 * cos + x1 * sin, y[ROT_DIM:]], axis=0)
        if kv_blk < 0:
            qt_ref[hb * HEAD_DIM:(hb + 1) * HEAD_DIM, :] = blk
        else:
            r = kv_blk % (2 * N_KV_HEADS)
            kv_refs[kv_blk // (2 * N_KV_HEADS)][r * HEAD_DIM:(r + 1) * HEAD_DIM, :] = blk


def _in_proj(x, wnorm, wtok, wfeat, gcol, cos_t, sin_t, tm, l, depth, kv_prev):
    n, t, _ = x.shape
    grid = (n, t // tm)
    tok = lambda w: pl.BlockSpec((None, tm, w), lambda i, j: (i, j, 0))
    feat = lambda w: pl.BlockSpec((None, w, tm), lambda i, j: (i, 0, j))
    kv_spec = pl.BlockSpec((None, None, D_ROWS, tm), lambda i, j: (l, i, 0, j))
    full = lambda a: pl.BlockSpec(a.shape, lambda i, j: (0,) * a.ndim)
    tok_shape = lambda w: jax.ShapeDtypeStruct((n, t, w), F32)
    kv_shape = jax.ShapeDtypeStruct((depth, n, D_ROWS, t), F32)
    n_in = 7
    kv_prev = () if kv_prev is None else tuple(kv_prev)
    return pl.pallas_call(
        _in_proj_kernel,
        grid=grid,
        in_specs=[tok(D_MODEL), full(wnorm), full(wtok), full(wfeat), full(gcol),
                  pl.BlockSpec((ROT_HALF, tm), lambda i, j: (0, j)),
                  pl.BlockSpec((ROT_HALF, tm), lambda i, j: (0, j))]
                 + [pl.BlockSpec(memory_space=pl.ANY)] * len(kv_prev),
        out_specs=[tok(D_CONV), tok(D_POOL), tok(D_CONV + D_POOL + D_ATTN), feat(GATE_ROWS),
                   feat(D_ATTN), kv_spec, kv_spec, kv_spec],
        out_shape=[tok_shape(D_CONV), tok_shape(D_POOL), tok_shape(D_CONV + D_POOL + D_ATTN),
                   jax.ShapeDtypeStruct((n, GATE_ROWS, t), F32),
                   jax.ShapeDtypeStruct((n, D_ATTN, t), F32), kv_shape, kv_shape, kv_shape],
        input_output_aliases={n_in + k: 5 + k for k in range(len(kv_prev))},
        compiler_params=pltpu.CompilerParams(dimension_semantics=("arbitrary", "arbitrary"),
                                             vmem_limit_bytes=VMEM_LIMIT),
        name="in_proj",
    )(x, wnorm, wtok, wfeat, gcol, cos_t, sin_t, *kv_prev)


def _conv_tail(acc, sz_a, bdw, lng, lnb, wpw, bpw):
    y = acc + bdw
    mu = jnp.mean(y, axis=-1, keepdims=True)
    yc = y - mu
    var = jnp.mean(yc * yc, axis=-1, keepdims=True)
    y = yc * lax.rsqrt(var + EPS) * lng + lnb
    y = _dot(_silu(y).astype(BF16), wpw) + bpw
    return y * sz_a


def _pool_tail(total, cnt, xcur, sz_b, wpool, pscale):
    d = total / cnt - xcur
    y = _dot(d.astype(BF16), wpool) * pscale
    return y * sz_b


def _pool_window_of_lane(shape):
    lane = lax.broadcasted_iota(jnp.int32, shape, len(shape) - 1)
    w = jnp.full(shape, POOL_WINDOWS[0], jnp.int32)
    for gi in range(1, len(POOL_WINDOWS)):
        w = jnp.where(lane >= gi * POOL_GROUP, POOL_WINDOWS[gi], w)
    return w


def _pool_select(sums, shape):
    lane = lax.broadcasted_iota(jnp.int32, shape, len(shape) - 1)
    total = sums[0]
    for gi in range(1, len(POOL_WINDOWS)):
        total = jnp.where(lane >= gi * POOL_GROUP, sums[gi], total)
    return total


CONV_HALO = 32
POOL_HALO = 16


MIX_ROWS = 256
SUBLANES = 8


def _tap_rows(buf, w_ref, r0, rows, off, taps, lanes):
    y = None
    for c in range(SUBLANES):
        part = None
        for k in taps:
            if (k + off) % SUBLANES != c:
                continue
            base = r0 + k + off - c
            term = w_ref[k:k + 1, lanes] * buf[base:base + rows + SUBLANES, lanes]
            part = term if part is None else part + term
        if part is not None:
            y = part[c:c + rows] if y is None else y + part[c:c + rows]
    return y


def _mix_out_prompt_kernel(x_ref, glu_ref, gprev_ref, bin_ref, bprev_ref, sz_ref, oat_ref, wdw_ref, bdw_ref, lng_ref,
                           lnb_ref, wpw_ref, bpw_ref, wpool_ref, pscale_ref, wout_ref, o_ref,
                           cbuf, pbuf, ymix, *, t):
    i = pl.program_id(1)
    keep = (i > 0).astype(F32)
    cbuf[0:CONV_HALO, :] = gprev_ref[...] * keep
    cbuf[CONV_HALO:CONV_HALO + t, :] = glu_ref[...]
    cbuf[CONV_HALO + t:CONV_HALO + t + SUBLANES, :] = jnp.zeros((SUBLANES, D_CONV), F32)
    pbuf[0:POOL_HALO, :] = bprev_ref[...] * keep
    pbuf[POOL_HALO:POOL_HALO + t, :] = bin_ref[...]
    nab = D_CONV + D_POOL
    for r0 in range(0, t, MIX_ROWS):
        rows = slice(r0, r0 + MIX_ROWS)
        sz = sz_ref[rows, :]
        acc = _tap_rows(cbuf, wdw_ref, r0, MIX_ROWS, CONV_HALO - CONV_STATE, range(CONV_WIDTH), slice(0, D_CONV))
        y_a = _conv_tail(acc, sz[:, 0:D_CONV], bdw_ref[...], lng_ref[...], lnb_ref[...], wpw_ref[...], bpw_ref[...])
        ymix[rows, 0:D_CONV] = y_a.astype(BF16)

        sums = []
        run = jnp.zeros((MIX_ROWS, D_POOL), F32)
        j = 0
        for w in POOL_WINDOWS:
            while j < w:
                run = run + pbuf[pl.ds(POOL_HALO + r0 - j, MIX_ROWS), :]
                j += 1
            sums.append(run)
        total = _pool_select(sums, (MIX_ROWS, D_POOL))
        pos = i * t + r0 + lax.broadcasted_iota(jnp.int32, (MIX_ROWS, D_POOL), 0)
        cnt = jnp.minimum(_pool_window_of_lane((MIX_ROWS, D_POOL)), pos + 1).astype(F32)
        y_b = _pool_tail(total, cnt, bin_ref[rows, :], sz[:, D_CONV:nab], wpool_ref[...], pscale_ref[...])
        ymix[rows, D_CONV:nab] = y_b.astype(BF16)
        ymix[rows, nab:nab + D_ATTN] = (oat_ref[rows, :] * sz[:, nab:nab + D_ATTN]).astype(BF16)
    o_ref[...] = x_ref[...] + _dot(ymix[...], wout_ref[...])


def _mix_out_prompt(x, glu, b_in, sz, o_attn, wdw, bdw, lng, lnb, wpw, bpw, wpool, pscale, wout, t):
    n, s, _ = glu.shape
    assert t % MIX_ROWS == 0 and s % t == 0
    grid = (n, s // t)
    cur = lambda w: pl.BlockSpec((None, t, w), lambda i, j: (i, j, 0))
    prev = lambda rows, w: pl.BlockSpec((None, rows, w), lambda i, j: (i, jnp.maximum(j * (t // rows) - 1, 0), 0))
    full = lambda a: pl.BlockSpec(a.shape, lambda i, j: (0,) * a.ndim)
    d_mix = D_CONV + D_POOL + D_ATTN
    return pl.pallas_call(
        functools.partial(_mix_out_prompt_kernel, t=t),
        grid=grid,
        in_specs=[cur(D_MODEL), cur(D_CONV), prev(CONV_HALO, D_CONV), cur(D_POOL), prev(POOL_HALO, D_POOL),
                  cur(d_mix), cur(D_ATTN),
                  full(wdw), full(bdw), full(lng), full(lnb), full(wpw), full(bpw), full(wpool),
                  full(pscale), full(wout)],
        out_specs=cur(D_MODEL),
        out_shape=jax.ShapeDtypeStruct(x.shape, F32),
        scratch_shapes=[pltpu.VMEM((CONV_HALO + t + SUBLANES, D_CONV), F32),
                        pltpu.VMEM((POOL_HALO + t, D_POOL), F32),
                        pltpu.VMEM((t, d_mix), BF16)],
        compiler_params=pltpu.CompilerParams(dimension_semantics=("arbitrary", "arbitrary"),
                                             vmem_limit_bytes=VMEM_LIMIT),
        name="mix_out_prompt",
    )(x, glu, glu, b_in, b_in, sz, o_attn, wdw, bdw, lng, lnb, wpw, bpw, wpool, pscale, wout)


def _convpool_sample_kernel(sc_ref, sp_ref, glu_ref, bin_ref, sz_ref, wdw_ref, bdw_ref, lng_ref, lnb_ref,
                            wpw_ref, bpw_ref, wpool_ref, pscale_ref, y_ref, sc_out, sp_out, *, pos):
    glu = glu_ref[...]
    acc = wdw_ref[CONV_STATE:CONV_WIDTH, :] * glu
    for k in range(CONV_STATE):
        acc = acc + wdw_ref[k:k + 1, :] * sc_ref[k]
    sz = sz_ref[...]
    y_ref[:, 0:D_CONV] = _conv_tail(acc, sz[:, 0:D_CONV], bdw_ref[...], lng_ref[...], lnb_ref[...],
                                    wpw_ref[...], bpw_ref[...])
    for k in range(CONV_STATE - 1):
        sc_out[k] = sc_ref[k + 1]
    sc_out[CONV_STATE - 1] = glu

    xcur = bin_ref[...]
    shape = xcur.shape
    sums = []
    run = xcur
    j = 1
    for w in POOL_WINDOWS:
        while j < w:
            run = run + sp_ref[POOL_STATE - j]
            j += 1
        sums.append(run)
    total = _pool_select(sums, shape)
    cnt = jnp.minimum(_pool_window_of_lane(shape), pos + 1).astype(F32)
    y_ref[:, D_CONV:D_CONV + D_POOL] = _pool_tail(total, cnt, xcur, sz[:, D_CONV:D_CONV + D_POOL],
                                                  wpool_ref[...], pscale_ref[...])
    for k in range(POOL_STATE - 1):
        sp_out[k] = sp_ref[k + 1]
    sp_out[POOL_STATE - 1] = xcur


def _convpool_sample(sc, sp, glu, b_in, sz, wdw, bdw, lng, lnb, wpw, bpw, wpool, pscale, pos):
    nb = glu.shape[0]
    return pl.pallas_call(
        functools.partial(_convpool_sample_kernel, pos=pos),
        out_shape=[jax.ShapeDtypeStruct((nb, D_CONV + D_POOL), F32),
                   jax.ShapeDtypeStruct(sc.shape, F32), jax.ShapeDtypeStruct(sp.shape, F32)],
        compiler_params=pltpu.CompilerParams(vmem_limit_bytes=VMEM_LIMIT),
        name="convpool_sample",
    )(sc, sp, glu, b_in, sz[:, 0:D_CONV + D_POOL], wdw, bdw, lng, lnb, wpw, bpw, wpool, pscale)


def _out_proj_kernel(x_ref, yab_ref, oat_ref, szc_ref, w_ref, o_ref):
    ya = yab_ref[...].astype(BF16)
    yc = (oat_ref[...] * szc_ref[...]).astype(BF16)
    nab = D_CONV + D_POOL
    o_ref[...] = x_ref[...] + _dot(ya, w_ref[0:nab, :]) + _dot(yc, w_ref[nab:nab + D_ATTN, :])


def _out_proj(x, yab, o_attn, sz, w_out, tm):
    n, t, _ = x.shape
    nab = D_CONV + D_POOL
    assert nab == D_ATTN
    grid = (n, t // tm)
    tok = lambda w, cb=0: pl.BlockSpec((None, tm, w), lambda i, j: (i, j, cb))
    return pl.pallas_call(
        _out_proj_kernel,
        grid=grid,
        in_specs=[tok(D_MODEL), tok(nab), tok(D_ATTN), tok(D_ATTN, 1),
                  pl.BlockSpec(w_out.shape, lambda i, j: (0, 0))],
        out_specs=tok(D_MODEL),
        out_shape=jax.ShapeDtypeStruct(x.shape, F32),
        compiler_params=pltpu.CompilerParams(dimension_semantics=("arbitrary", "arbitrary"),
                                             vmem_limit_bytes=VMEM_LIMIT),
        name="out_proj",
    )(x, yab, o_attn, sz, w_out)


GD = N_KV_HEADS * HEAD_DIM
CMP_FLAT = CMP_STRIDE * GD


FLAT_SPAN = CMP_STRIDE * CMP_STRIDE


def _piece_perm():
    m = np.arange(FLAT_SPAN)
    src = (m % CMP_STRIDE) * CMP_STRIDE + m // CMP_STRIDE
    return jnp.asarray((src[:, None] == np.arange(FLAT_SPAN)[None, :]).astype(np.float32), BF16)


def _fill_flat(x_t, perm, flat_ref, piece0):
    y = _dot_nt(perm, x_t).astype(BF16)
    for kv in range(2):
        for r in range(CMP_STRIDE):
            flat_ref[kv, piece0:piece0 + CMP_STRIDE, r * GD:(r + 1) * GD] = (
                y[r * CMP_STRIDE:(r + 1) * CMP_STRIDE, kv * GD:(kv + 1) * GD])


def _compress_mlp(flat, w1ab, bias, w2):
    n_pc = flat.shape[0]
    hb = _dot(flat, w1ab)
    h = hb[:, 0:GD] + jnp.concatenate([hb[1:n_pc, GD:2 * GD], jnp.zeros((1, GD), F32)], axis=0) + bias
    out = _dot(_gelu_tanh(h).astype(BF16), w2)
    row = lax.broadcasted_iota(jnp.int32, out.shape, 0)
    return jnp.where(row < n_pc - 1, out, 0.0)


def _cmp_bias_kernel(pe_ref, w1_ref, o_ref):
    for kv in range(2):
        o_ref[kv] = _dot(pe_ref[kv].astype(BF16), w1_ref[kv])


def _cmp_bias(pe_flat, w1):
    return pl.pallas_call(_cmp_bias_kernel, out_shape=jax.ShapeDtypeStruct((2, 1, HEAD_DIM), F32),
                          name="cmp_bias")(pe_flat, w1)


def _compress_prompt_kernel(x_ref, perm_ref, w1ab_ref, bias_ref, w2_ref, o_ref, flat_ref, *, s):
    for t in range(s // FLAT_SPAN):
        _fill_flat(x_ref[:, t * FLAT_SPAN:(t + 1) * FLAT_SPAN].astype(BF16), perm_ref[...], flat_ref, t * CMP_STRIDE)
    for kv in range(2):
        o_ref[:, kv * GD:(kv + 1) * GD] = _compress_mlp(flat_ref[kv], w1ab_ref[kv], bias_ref[kv], w2_ref[kv])


def _compress_prompt(cmp_t, l, w1ab, bias, w2):
    _, n, _, s = cmp_t.shape
    assert s % FLAT_SPAN == 0
    n_pc = s // CMP_STRIDE
    perm = _piece_perm()
    full = lambda a: pl.BlockSpec(a.shape, lambda i: (0,) * a.ndim)
    return pl.pallas_call(
        functools.partial(_compress_prompt_kernel, s=s),
        grid=(n,),
        in_specs=[pl.BlockSpec((None, None, D_ROWS, s), lambda i: (l, i, 0, 0)),
                  full(perm), full(w1ab), full(bias), full(w2)],
        out_specs=pl.BlockSpec((None, n_pc, 2 * GD), lambda i: (i, 0, 0)),
        out_shape=jax.ShapeDtypeStruct((n, n_pc, 2 * GD), F32),
        scratch_shapes=[pltpu.VMEM((2, n_pc, CMP_FLAT), BF16)],
        compiler_params=pltpu.CompilerParams(dimension_semantics=("arbitrary",), vmem_limit_bytes=VMEM_LIMIT),
        name="compress_prompt",
    )(cmp_t, perm, w1ab, bias, w2)


def _select_blocks(imp_t, pos_row, n_blocks):
    shape = imp_t.shape
    assert n_blocks % 8 == 0
    j = lax.broadcasted_iota(jnp.int32, shape, 0)
    cur = pos_row // SLC_BLOCK
    forced = (j == 0) | (j == cur) | (j == cur - 1)
    score = jnp.where(j <= cur, jnp.where(forced, BIG, imp_t), -BIG)
    tiles = [score[8 * v:8 * v + 8] for v in range(n_blocks // 8)]
    j8 = lax.broadcasted_iota(jnp.int32, (8, shape[1]), 0)
    ranks = [jnp.zeros((8, shape[1]), F32) for _ in tiles]
    for jp in range(n_blocks):
        sj = score[jp:jp + 1, :]
        for v, tile in enumerate(tiles):
            if jp < 8 * v:
                beats = sj >= tile
            elif jp >= 8 * v + 8:
                beats = sj > tile
            else:
                beats = (sj > tile) | ((sj == tile) & (j8 > jp - 8 * v))
            ranks[v] = ranks[v] + jnp.where(beats, 1.0, 0.0)
    rank = jnp.concatenate(ranks, axis=0)
    return (rank < float(min(N_SEL, n_blocks))).astype(F32)


def _col_softmax(s):
    m = jnp.max(s, axis=0, keepdims=True)
    e = jnp.exp(s - m)
    inv = jnp.where(m > 0.5 * NEG, 1.0 / jnp.maximum(jnp.sum(e, axis=0, keepdims=True), 1e-30), 0.0)
    return e, inv


ONES_ROWS = 16


def _with_ones(v_t):
    return jnp.concatenate([v_t, jnp.ones((ONES_ROWS, v_t.shape[1]), BF16)], axis=0)


def _normalised(acc):
    return acc[0:HEAD_DIM] * (1.0 / jnp.maximum(acc[HEAD_DIM:HEAD_DIM + 1], 1e-30))


def _mask_heads(sc, ok, tq):
    return jnp.concatenate([jnp.where(ok, sc[:, r * tq:(r + 1) * tq], NEG) for r in range(N_REP)], axis=1)


def _attn_prompt_kernel(qt_ref, kcv_ref, slc_ref, win_ref, gates_ref, ovt_ref, o_ref,
                        ks_ref, kw_ref, s_ref, sel_ref, acc_ref, m_ref, *, s, n_blk):
    tq = Q_BLOCK
    cols = N_REP * tq
    groups = range(N_KV_HEADS)
    qc = pl.program_id(1)
    q0 = qc * tq
    pos_row = q0 + lax.broadcasted_iota(jnp.int32, (1, tq), 1)
    n_slc = s // SLC_BLOCK
    n_cmp = kcv_ref.shape[0]
    scale = HEAD_DIM ** -0.5
    v_row0 = lambda g: (N_KV_HEADS + g) * HEAD_DIM

    @pl.when(qc == 0)
    def _():
        for g in range(N_KV_HEADS):
            for t in range(s // 128):
                cols_t = slice(t * 128, (t + 1) * 128)
                ks_ref[g, cols_t, :] = slc_ref[g * HEAD_DIM:(g + 1) * HEAD_DIM, cols_t].T.astype(BF16)
                kw_ref[g, cols_t, :] = win_ref[g * HEAD_DIM:(g + 1) * HEAD_DIM, cols_t].T.astype(BF16)

    gates_t = gates_ref[...]
    vc_t = kcv_ref[:, GD:2 * GD].T.astype(BF16)
    q_ts = []
    for g in groups:
        q_t = jnp.concatenate([qt_ref[(g * N_REP + r) * HEAD_DIM:(g * N_REP + r + 1) * HEAD_DIM, :]
                               for r in range(N_REP)], axis=1)
        q_ts.append((q_t * scale).astype(BF16))

    c_idx = lax.broadcasted_iota(jnp.int32, (n_cmp, 1), 0)
    valid_c = (c_idx * CMP_STRIDE + (CMP_BLOCK - 1) <= pos_row) & (c_idx < n_blk)
    ovt = ovt_ref[...]
    o_cs = []
    for g in groups:
        kc = kcv_ref[:, g * HEAD_DIM:(g + 1) * HEAD_DIM].astype(BF16)
        e_c, inv_c = _col_softmax(_mask_heads(_dot(kc, q_ts[g]), valid_c, tq))
        o_cs.append(_dot(vc_t[g * HEAD_DIM:(g + 1) * HEAD_DIM, :], e_c.astype(BF16)) * inv_c)
        pc_sum = e_c[:, 0:tq] * inv_c[:, 0:tq]
        for r in range(1, N_REP):
            pc_sum = pc_sum + e_c[:, r * tq:(r + 1) * tq] * inv_c[:, r * tq:(r + 1) * tq]
        pc_hi = pc_sum.astype(BF16)
        pc_lo = (pc_sum - pc_hi.astype(F32)).astype(BF16)
        imp_t = _dot(ovt, pc_hi) + _dot(ovt, pc_lo)
        sel_ref[g] = _select_blocks(imp_t, pos_row, n_slc)

    n_keys = q0 + tq
    rest = n_keys % SEL_KEYS
    half_tail = (rest > 0) & (rest <= SEL_KEYS // 2)
    n_steps = n_keys // SEL_KEYS + (rest > SEL_KEYS // 2).astype(jnp.int32)
    tail_k0 = pl.multiple_of((n_keys // SEL_KEYS) * SEL_KEYS, SEL_KEYS)

    def score_chunk(k0, size):
        key = k0 + lax.broadcasted_iota(jnp.int32, (size, 1), 0)
        causal = key <= pos_row
        for g in groups:
            sc = _dot(ks_ref[g, pl.ds(k0, size), :], q_ts[g])
            parts = []
            for jj in range(size // SLC_BLOCK):
                rows = slice(jj * SLC_BLOCK, (jj + 1) * SLC_BLOCK)
                chosen = sel_ref[g, pl.ds(k0 // SLC_BLOCK + jj, 1), :] > 0.5
                parts.append(_mask_heads(sc[rows], chosen & causal[rows], tq))
            sc = jnp.concatenate(parts, axis=0)
            s_ref[g, pl.ds(k0, size), :] = sc.astype(BF16)
            m_ref[g] = jnp.maximum(m_ref[g], jnp.max(sc, axis=0, keepdims=True))

    m_ref[...] = jnp.full(m_ref.shape, NEG, F32)

    def score_step(i, carry):
        score_chunk(pl.multiple_of(i * SEL_KEYS, SEL_KEYS), SEL_KEYS)
        return carry

    lax.fori_loop(0, n_steps, score_step, 0)

    @pl.when(half_tail)
    def _():
        score_chunk(tail_k0, SEL_KEYS // 2)

    m_sel = [m_ref[g].astype(BF16) for g in groups]

    acc_ref[...] = jnp.zeros(acc_ref.shape, F32)

    def value_chunk(k0, size):
        for g in groups:
            p = jnp.exp(s_ref[g, pl.ds(k0, size), :] - m_sel[g])
            v_t = slc_ref[v_row0(g):v_row0(g) + HEAD_DIM, pl.ds(k0, size)].astype(BF16)
            acc_ref[g] += _dot(_with_ones(v_t), p)

    def value_step(i, carry):
        value_chunk(pl.multiple_of(i * SEL_KEYS, SEL_KEYS), SEL_KEYS)
        return carry

    lax.fori_loop(0, n_steps, value_step, 0)

    @pl.when(half_tail)
    def _():
        value_chunk(tail_k0, SEL_KEYS // 2)

    n_prev = -(-(WINDOW - 1) // tq)
    k0s, oks = [], []
    for b in range(n_prev + 1):
        kb = qc - n_prev + b
        k0s.append(pl.multiple_of(jnp.maximum(kb, 0) * tq, tq))
        kp = kb * tq + lax.broadcasted_iota(jnp.int32, (tq, 1), 0)
        oks.append((kp <= pos_row) & (kp > pos_row - WINDOW) & (kp >= 0))

    out_blocks = []
    for g in groups:
        o_s = _normalised(acc_ref[g])
        s_w = jnp.concatenate([_mask_heads(_dot(kw_ref[g, pl.ds(k0s[b], tq), :], q_ts[g]), oks[b], tq)
                               for b in range(n_prev + 1)], axis=0)
        e_w = jnp.exp(s_w.astype(BF16) - jnp.max(s_w, axis=0, keepdims=True).astype(BF16))
        o_w = _dot(_with_ones(win_ref[v_row0(g):v_row0(g) + HEAD_DIM, pl.ds(k0s[0], tq)].astype(BF16)), e_w[0:tq])
        for b in range(1, n_prev + 1):
            o_w = o_w + _dot(_with_ones(win_ref[v_row0(g):v_row0(g) + HEAD_DIM, pl.ds(k0s[b], tq)].astype(BF16)),
                             e_w[b * tq:(b + 1) * tq])
        o_w = _normalised(o_w)
        for r in range(N_REP):
            h = g * N_REP + r
            c = slice(r * tq, (r + 1) * tq)
            out_blocks.append(gates_t[h:h + 1] * o_cs[g][:, c] + gates_t[N_HEADS + h:N_HEADS + h + 1] * o_s[:, c]
                              + gates_t[2 * N_HEADS + h:2 * N_HEADS + h + 1] * o_w[:, c])
    o_ref[...] = jnp.concatenate(out_blocks, axis=0).T


def _overlap_t(n_cmp_rows, n_blk, n_slc):
    c = np.arange(n_cmp_rows)
    start = c * CMP_STRIDE
    end = start + CMP_BLOCK - 1
    j0 = np.arange(n_slc) * SLC_BLOCK
    ov = (end[None, :] >= j0[:, None]) & (start[None, :] < j0[:, None] + SLC_BLOCK) & (c[None, :] < n_blk)
    return jnp.asarray(ov.astype(np.float32), BF16)


def _attn_prompt(q_t, kcv, slc_t, win_t, gates, l):
    n, _, s = q_t.shape
    n_cmp = kcv.shape[1]
    n_blk = s // CMP_STRIDE - CMP_BLOCK // CMP_STRIDE + 1
    n_slc = s // SLC_BLOCK
    ovt = _overlap_t(n_cmp, n_blk, n_slc)
    tq = Q_BLOCK
    cols = N_REP * tq
    assert s % SEL_KEYS == 0 and (SEL_KEYS // 2) % SLC_BLOCK == 0
    assert (SEL_KEYS // 2) % tq == 0 or tq % SEL_KEYS == 0
    return pl.pallas_call(
        functools.partial(_attn_prompt_kernel, s=s, n_blk=n_blk),
        grid=(n, s // tq),
        in_specs=[pl.BlockSpec((None, D_ATTN, tq), lambda i, j: (i, 0, j)),
                  pl.BlockSpec((None,) + kcv.shape[1:], lambda i, j: (i, 0, 0)),
                  pl.BlockSpec((None, None, D_ROWS, s), lambda i, j: (l, i, 0, 0)),
                  pl.BlockSpec((None, None, D_ROWS, s), lambda i, j: (l, i, 0, 0)),
                  pl.BlockSpec((None, GATE_ROWS, tq), lambda i, j: (i, 0, j)),
                  pl.BlockSpec(ovt.shape, lambda i, j: (0, 0))],
        out_specs=pl.BlockSpec((None, tq, D_ATTN), lambda i, j: (i, j, 0)),
        out_shape=jax.ShapeDtypeStruct((n, s, D_ATTN), F32),
        scratch_shapes=[pltpu.VMEM((N_KV_HEADS, s, HEAD_DIM), BF16),
                        pltpu.VMEM((N_KV_HEADS, s, HEAD_DIM), BF16),
                        pltpu.VMEM((N_KV_HEADS, s, cols), BF16),
                        pltpu.VMEM((N_KV_HEADS, n_slc, tq), F32),
                        pltpu.VMEM((N_KV_HEADS, HEAD_DIM + ONES_ROWS, cols), F32),
                        pltpu.VMEM((N_KV_HEADS, 1, cols), F32)],
        compiler_params=pltpu.CompilerParams(dimension_semantics=("arbitrary", "arbitrary"),
                                             vmem_limit_bytes=VMEM_LIMIT),
        name="attn_prompt",
    )(q_t, kcv, slc_t, win_t, gates, ovt)


PAGE = 128
SAMPLE_SEQS = 4


def _sample_geometry(past):
    length = past + 1
    padded = -(-length // SLC_BLOCK) * SLC_BLOCK
    n_pc = padded // CMP_STRIDE
    n_pc_pad = -(-n_pc // CMP_STRIDE) * CMP_STRIDE
    return dict(n_pc=n_pc, n_pc_pad=n_pc_pad, n_blk=n_pc - CMP_BLOCK // CMP_STRIDE + 1, n_slc=padded // SLC_BLOCK)


def _attn_sample_kernel(pt_ref, q_ref, new_ref, gates_ref, wincol_ref, cwin_ref, ov_ref,
                        perm_ref, w1ab_ref, bias_ref, w2_ref, ccmp_hbm, cslc_hbm, *rest, l, past, n_pages, nseq):
    o_ref, wout_ref, flat_ref, cmp_buf, slc_buf, sem = rest[-6:]
    step = pl.program_id(0)
    slot = step % 2

    def page_copies(of_step, into_slot, lookup):
        copies = []
        for u in range(nseq):
            for j in range(n_pages):
                page = pt_ref[of_step * nseq + u, j] if lookup else 0
                k = u * n_pages + j
                copies.append(pltpu.make_async_copy(ccmp_hbm.at[l, page], cmp_buf.at[into_slot, k], sem.at[into_slot, 0]))
                copies.append(pltpu.make_async_copy(cslc_hbm.at[l, page], slc_buf.at[into_slot, k], sem.at[into_slot, 1]))
        return copies

    @pl.when(step == 0)
    def _():
        for c in page_copies(0, 0, True):
            c.start()

    @pl.when(step + 1 < pl.num_programs(0))
    def _():
        for c in page_copies(step + 1, 1 - slot, True):
            c.start()

    for c in page_copies(step, slot, False):
        c.wait()
    cmp_pages = [[cmp_buf.at[slot, u * n_pages + j] for j in range(n_pages)] for u in range(nseq)]
    slc_pages = [[slc_buf.at[slot, u * n_pages + j] for j in range(n_pages)] for u in range(nseq)]
    geo = _sample_geometry(past)
    n_pc_pad, n_blk, n_slc = geo["n_pc_pad"], geo["n_blk"], geo["n_slc"]
    pos = past
    lb = cwin_ref.shape[-1]
    scale = HEAD_DIM ** -0.5
    head = lax.broadcasted_iota(jnp.int32, (N_HEADS, 1), 0)
    lane = lax.broadcasted_iota(jnp.int32, (1, GD), 1)
    lane_h = lax.broadcasted_iota(jnp.int32, (N_HEADS, GD), 1)
    own = (lane_h // HEAD_DIM) == (head // N_REP)

    def new_row(u, branch, kv):
        o = (branch * 2 + kv) * GD
        return new_ref[u, :, o:o + GD]

    def rounded(x):
        return x.astype(BF16).astype(F32)

    kcv = []
    pages_per_span = FLAT_SPAN // PAGE
    past_pc = past // CMP_STRIDE
    for u in range(nseq):
        row0 = u * n_pc_pad
        for t in range(n_pages // pages_per_span):
            x_t = jnp.concatenate([cmp_pages[u][t * pages_per_span + w][...].reshape(2 * GD, PAGE)
                                   for w in range(pages_per_span)], axis=1)
            _fill_flat(x_t.astype(BF16), perm_ref[...], flat_ref, row0 + t * CMP_STRIDE)
        for kv in range(2):
            flat_ref[kv, row0 + past_pc:row0 + n_pc_pad, :] = jnp.zeros((n_pc_pad - past_pc, CMP_FLAT), BF16)
            flat_ref[kv, row0 + past_pc:row0 + past_pc + 1, 0:GD] = new_row(u, 0, kv).astype(BF16)
    for kv in range(2):
        kcv.append(_compress_mlp(flat_ref[kv], w1ab_ref[kv], bias_ref[kv], w2_ref[kv]).astype(BF16))
    for u in range(nseq):
        _attn_sample_one(u, kcv[0][u * n_pc_pad:(u + 1) * n_pc_pad], kcv[1][u * n_pc_pad:(u + 1) * n_pc_pad],
                         q_ref, new_row, rounded, gates_ref, wincol_ref, cwin_ref, ov_ref,
                         slc_pages[u], o_ref, wout_ref, geo=geo, past=past, consts=(head, lane, lane_h, own))


def _attn_sample_one(u, kc, vc, q_ref, new_row, rounded, gates_ref, wincol_ref, cwin_ref, ov_ref,
                     slc_pages, o_ref, wout_ref, *, geo, past, consts):
    head, lane, lane_h, own = consts
    n_pc_pad, n_blk, n_slc = geo["n_pc_pad"], geo["n_blk"], geo["n_slc"]
    n_pages = len(slc_pages)
    pos = past
    lb = cwin_ref.shape[-1]
    scale = HEAD_DIM ** -0.5
    q_rows = jnp.concatenate([q_ref[u, :, h * HEAD_DIM:(h + 1) * HEAD_DIM] for h in range(N_HEADS)], axis=0)
    qx = (jnp.where(own, jnp.concatenate([q_rows] * N_KV_HEADS, axis=1), 0.0) * scale).astype(BF16)
    qx32 = qx.astype(F32)
    c_idx = lax.broadcasted_iota(jnp.int32, (1, n_pc_pad), 1)
    valid = (c_idx * CMP_STRIDE + (CMP_BLOCK - 1) <= pos) & (c_idx < n_blk)
    p_c = _masked_softmax(_dot_nt(qx, kc), valid)
    o_c = _dot(p_c.astype(BF16), vc)

    cur = pos // SLC_BLOCK
    forced = (lane == 0) | (lane == cur) | (lane == cur - 1)
    jp = lax.broadcasted_iota(jnp.int32, (GD, GD), 0)
    jj = lax.broadcasted_iota(jnp.int32, (GD, GD), 1)
    sels = []
    for g in range(N_KV_HEADS):
        pc_sum = jnp.sum(p_c[g * N_REP:(g + 1) * N_REP], axis=0, keepdims=True)
        pc_hi = pc_sum.astype(BF16)
        pc_lo = (pc_sum - pc_hi.astype(F32)).astype(BF16)
        imp = _dot(pc_hi, ov_ref[...]) + _dot(pc_lo, ov_ref[...])
        score = jnp.where(lane <= cur, jnp.where(forced, BIG, imp), -BIG)
        score = jnp.where(lane < n_slc, score, -2.0 * BIG)
        score_b = jnp.broadcast_to(score, (GD, GD))
        score_a = score_b.T
        beats = (score_a > score_b) | ((score_a == score_b) & (jp < jj))
        rank = jnp.sum(beats.astype(F32), axis=0, keepdims=True)
        sel = ((rank < float(min(N_SEL, n_slc))) & (lane < n_slc)).astype(F32)
        sels.append(jnp.broadcast_to(sel, (N_REP, GD)))
    sel_h = jnp.concatenate(sels, axis=0)

    blocks_per_tile = GD // SLC_BLOCK
    chosen = []
    for t in range(past // GD):
        tile = sel_h[:, t * blocks_per_tile:t * blocks_per_tile + 1]
        for b in range(1, blocks_per_tile):
            tile = jnp.where(lane >= b * SLC_BLOCK, sel_h[:, t * blocks_per_tile + b:t * blocks_per_tile + b + 1], tile)
        chosen.append(jnp.broadcast_to(tile, (N_HEADS, GD)))
    chosen = jnp.concatenate(chosen, axis=-1) > 0.5
    s_past = jnp.concatenate([_dot(qx, slc_pages[j][0].astype(BF16)) for j in range(n_pages)], axis=-1)
    s_past = jnp.where(chosen, s_past, NEG)
    new_ok = jnp.sum(jnp.where(lane == pos // SLC_BLOCK, sel_h, 0.0), axis=-1, keepdims=True) > 0.5
    s_new = jnp.where(new_ok, jnp.sum(qx32 * rounded(new_row(u, 1, 0)), axis=-1, keepdims=True), NEG)
    m = jnp.maximum(jnp.max(s_past, axis=-1, keepdims=True), s_new)
    p_past = jnp.where(chosen, jnp.exp(s_past - m), 0.0)
    p_new = jnp.where(new_ok, jnp.exp(s_new - m), 0.0)
    denom = jnp.maximum(jnp.sum(p_past, axis=-1, keepdims=True) + p_new, 1e-30)
    o_s = rounded(p_new) * rounded(new_row(u, 1, 1))
    for j in range(n_pages):
        o_s = o_s + _dot_nt(p_past[:, j * PAGE:(j + 1) * PAGE].astype(BF16), slc_pages[j][1].astype(BF16))
    o_s = o_s / denom

    kpos = (past - lb) + lax.broadcasted_iota(jnp.int32, (1, lb), 1)
    w_ok = (kpos <= pos) & (kpos > pos - WINDOW)
    s_w = jnp.where(w_ok, _dot(qx, cwin_ref[u, 0].astype(BF16)), NEG)
    s_wn = jnp.sum(qx32 * rounded(new_row(u, 2, 0)), axis=-1, keepdims=True)
    m = jnp.maximum(jnp.max(s_w, axis=-1, keepdims=True), s_wn)
    p_w = jnp.where(w_ok, jnp.exp(s_w - m), 0.0)
    p_wn = jnp.exp(s_wn - m)
    denom = jnp.maximum(jnp.sum(p_w, axis=-1, keepdims=True) + p_wn, 1e-30)
    o_w = (_dot_nt(p_w.astype(BF16), cwin_ref[u, 1].astype(BF16))
           + rounded(p_wn) * rounded(new_row(u, 2, 1))) / denom

    gates = gates_ref[u]
    gate_id = lax.broadcasted_iota(jnp.int32, (N_HEADS, GATE_ROWS), 1)
    gate = lambda br: jnp.sum(jnp.where(gate_id == head + br * N_HEADS, gates, 0.0), axis=-1, keepdims=True)
    o = gate(0) * o_c + gate(1) * o_s + gate(2) * o_w
    for h in range(N_HEADS):
        g = h // N_REP
        o_ref[u, :, h * HEAD_DIM:(h + 1) * HEAD_DIM] = o[h:h + 1, g * HEAD_DIM:(g + 1) * HEAD_DIM]

    for kv in range(2):
        wout_ref[u, kv, :, 0:lb - 1] = cwin_ref[u, kv, :, 1:lb]
        wout_ref[u, kv, :, lb - 1:lb] = wincol_ref[u, kv * GD:(kv + 1) * GD, :]


def _attn_sample(l, q_tok, new_tok, gates, win_col, cwin_t, ccmp_t, cslc_t, page_table, w1ab, bias, w2, win_prev):
    nb = q_tok.shape[0]
    n_pages = page_table.shape[1]
    past = n_pages * PAGE
    geo = _sample_geometry(past)
    lb = cwin_t.shape[-1]
    assert FLAT_SPAN % PAGE == 0 and n_pages % (FLAT_SPAN // PAGE) == 0
    perm = _piece_perm()
    ov = _overlap_t(geo["n_pc_pad"], geo["n_blk"], GD).T
    ov = jnp.where(jnp.arange(GD)[None, :] < geo["n_slc"], ov, 0).astype(BF16)
    assert GD % SLC_BLOCK == 0 and past % GD == 0
    nseq = SAMPLE_SEQS if nb % SAMPLE_SEQS == 0 else 1
    seq = lambda a: pl.BlockSpec((nseq,) + a.shape[1:], lambda i, pt: (i,) + (0,) * (a.ndim - 1))
    full = lambda a: pl.BlockSpec(a.shape, lambda i, pt: (0,) * a.ndim)
    operands = [page_table, q_tok, new_tok, gates, win_col, cwin_t, ov, perm, w1ab, bias, w2, ccmp_t, cslc_t]
    win_prev = () if win_prev is None else (win_prev,)
    page_buf = pltpu.VMEM((2, nseq * n_pages, 2, GD, PAGE), F32)
    grid_spec = pltpu.PrefetchScalarGridSpec(
        num_scalar_prefetch=1,
        grid=(nb // nseq,),
        in_specs=[seq(q_tok), seq(new_tok), seq(gates), seq(win_col),
                  pl.BlockSpec((None, nseq, 2, GD, lb), lambda i, pt: (l, i, 0, 0, 0)),
                  full(ov), full(perm), full(w1ab), full(bias), full(w2)]
                 + [pl.BlockSpec(memory_space=pl.ANY)] * (2 + len(win_prev)),
        out_specs=[pl.BlockSpec((nseq, 1, D_ATTN), lambda i, pt: (i, 0, 0)),
                   pl.BlockSpec((None, nseq, 2, GD, lb), lambda i, pt: (l, i, 0, 0, 0))],
        scratch_shapes=[pltpu.VMEM((2, nseq * geo["n_pc_pad"], CMP_FLAT), BF16), page_buf, page_buf,
                        pltpu.SemaphoreType.DMA((2, 2))],
    )
    return pl.pallas_call(
        functools.partial(_attn_sample_kernel, l=l, past=past, n_pages=n_pages, nseq=nseq),
        grid_spec=grid_spec,
        out_shape=[jax.ShapeDtypeStruct((nb, 1, D_ATTN), F32),
                   jax.ShapeDtypeStruct((cwin_t.shape[0], nb, 2, GD, lb), F32)],
        input_output_aliases={len(operands): 1} if win_prev else {},
        compiler_params=pltpu.CompilerParams(dimension_semantics=("arbitrary",), vmem_limit_bytes=VMEM_LIMIT),
        name="attn_sample",
    )(*operands, *win_prev)


def _prep_layer(l, w_norm, w_in, w_out, w_dw, b_dw, ln_g, ln_b, w_pw, b_pw, w_pool, pool_scale,
                g_q, g_k, cmp_pe, cmp_w1, cmp_w2):
    sizes = (D_CONV, D_CONV, D_CONV, D_POOL, D_POOL, D_ATTN, D_KV, N_BRANCH * N_HEADS, D_ATTN)
    offs = [0] + [int(v) for v in np.cumsum(sizes)]
    col = lambda k: w_in[l][:, offs[k]:offs[k + 1]]
    a_val, a_gate, z_a, b_in, z_b, q, kv, gate, z_c = (col(k) for k in range(9))
    pad = jnp.zeros((D_MODEL, GATE_ROWS - N_BRANCH * N_HEADS), F32)
    wtok = jnp.concatenate([a_val, a_gate, b_in, z_a, z_b, z_c], axis=1).astype(BF16)
    wfeat = jnp.concatenate([q, kv, gate, pad], axis=1).T.astype(BF16)
    ones = jnp.ones((N_KV_HEADS * HEAD_DIM,), F32)
    gcol = jnp.concatenate([jnp.tile(g_q[l], N_HEADS)]
                           + [piece for br in range(N_BRANCH)
                              for piece in (jnp.tile(g_k[l, br], N_KV_HEADS), ones)])[:, None]
    wpool_bd = jnp.zeros((D_POOL, D_POOL), F32)
    for gi in range(len(POOL_WINDOWS)):
        wpool_bd = wpool_bd.at[gi * POOL_GROUP:(gi + 1) * POOL_GROUP, gi * POOL_GROUP:(gi + 1) * POOL_GROUP].set(w_pool[l, gi])
    row = lambda v: v[None, :]

    def both_groups(w):
        z = jnp.zeros_like(w)
        return jnp.stack([jnp.concatenate([w, z], axis=2), jnp.concatenate([z, w], axis=2)], axis=1).reshape(-1, GD)

    half = CMP_STRIDE * HEAD_DIM
    w1 = cmp_w1[l].astype(BF16)
    w1ab = jnp.stack([jnp.concatenate(
        [both_groups(w1[kv, :half].reshape(CMP_STRIDE, HEAD_DIM, HEAD_DIM)),
         both_groups(w1[kv, half:].reshape(CMP_STRIDE, HEAD_DIM, HEAD_DIM))], axis=1) for kv in range(2)])
    w2p = jnp.stack([both_groups(cmp_w2[l, kv].astype(BF16)[None]) for kv in range(2)])
    cbias = jnp.tile(_cmp_bias(cmp_pe[l].reshape(2, 1, CMP_BLOCK * HEAD_DIM), w1), (1, 1, N_KV_HEADS))
    return dict(
        wnorm=row(w_norm[l]), wtok=wtok, wfeat=wfeat, gcol=gcol,
        wdw=w_dw[l], bdw=row(b_dw[l]), lng=row(ln_g[l]), lnb=row(ln_b[l]),
        wpw=w_pw[l].astype(BF16), bpw=row(b_pw[l]),
        wpool=wpool_bd.astype(BF16), pscale=row(pool_scale[l]),
        w1ab=w1ab, cbias=cbias, w2p=w2p,
        wout=w_out[l].astype(BF16),
    )


def _rope_tables(pos):
    inv = ROPE_THETA ** (-jnp.arange(ROT_HALF, dtype=F32) * 2.0 / ROT_DIM)
    ang = pos.astype(F32)[:, None] * inv[None, :]
    return jnp.cos(ang).T, jnp.sin(ang).T


def _prompt_layer(l, depth, x, p, cos_t, sin_t, kv_prev):
    glu, b_in, sz, gates, q_t, *kv = _in_proj(
        x, p["wnorm"], p["wtok"], p["wfeat"], p["gcol"], cos_t, sin_t, min(1024, x.shape[1]), l, depth, kv_prev)
    cmp_t, slc_t, win_t = kv
    kcv = _compress_prompt(cmp_t, l, p["w1ab"], p["cbias"], p["w2p"])
    o_attn = _attn_prompt(q_t, kcv, slc_t, win_t, gates, l)
    y = _mix_out_prompt(x, glu, b_in, sz, o_attn, p["wdw"], p["bdw"], p["lng"], p["lnb"], p["wpw"], p["bpw"],
                        p["wpool"], p["pscale"], p["wout"], t=min(512, x.shape[1]))
    return y, glu, b_in, kv


def _sample_layer(l, depth, x, p, cos_t, sin_t, sc, sp, cwin_t, ccmp_t, cslc_t, page_table, past, kv_prev, win_prev):
    nb = x.shape[0]
    glu, b_in, sz, gates, q_t, *kv = _in_proj(
        x[None], p["wnorm"], p["wtok"], p["wfeat"], p["gcol"], cos_t, sin_t, nb, l, depth, kv_prev)
    glu, b_in, sz, gates = glu[0], b_in[0], sz[0], gates[0]
    yab, sc_new, sp_new = _convpool_sample(sc, sp, glu, b_in, sz, p["wdw"], p["bdw"], p["lng"], p["lnb"],
                                           p["wpw"], p["bpw"], p["wpool"], p["pscale"], pos=past)
    new_t = jnp.concatenate([a[l, 0] for a in kv], axis=0)
    new_tok = new_t.T[:, None, :]
    q_tok = q_t[0].T[:, None, :]
    win_col = kv[2][l, 0].T[:, :, None]
    o_attn, win_next = _attn_sample(l, q_tok, new_tok, gates.T[:, None, :], win_col, cwin_t, ccmp_t, cslc_t,
                                    page_table, p["w1ab"], p["cbias"], p["w2p"], win_prev)
    y = _out_proj(x[None], yab[None], o_attn.reshape(1, nb, D_ATTN), sz[None], p["wout"], tm=nb)[0]
    return y, sc_new, sp_new, win_next, kv


def _rows_from_feat(a):
    lead = a.shape[:-2]
    a = a.reshape(lead + (2, N_KV_HEADS, HEAD_DIM, a.shape[-1]))
    nl = len(lead)
    return jnp.transpose(a, tuple(range(nl)) + (nl + 3, nl, nl + 1, nl + 2))


def kernel(x_prompt, x_sample, state_conv, state_pool, cache_win_kv, cache_cmp_kv, cache_slc_kv, page_table,
           w_norm, w_in, w_out, w_dw, b_dw, ln_g, ln_b, w_pw, b_pw, w_pool, pool_scale,
           g_q, g_k, cmp_pe, cmp_w1, cmp_w2):
    bp, s, _ = x_prompt.shape
    bs, t_new, _ = x_sample.shape
    depth = w_in.shape[0]
    assert t_new == 1 and cache_cmp_kv.shape[2] == PAGE and s % 512 == 0
    past = page_table.shape[1] * PAGE
    lb = cache_win_kv.shape[2]

    def to_feat(c):
        c = jnp.transpose(c, (0, 1, 3, 4, 5, 2))
        return c.reshape(c.shape[:3] + (GD, c.shape[-1]))
    cwin_t, ccmp_t, cslc_t = to_feat(cache_win_kv), to_feat(cache_cmp_kv), to_feat(cache_slc_kv)
    sc_all = jnp.transpose(state_conv, (0, 2, 1, 3))
    sp_all = jnp.transpose(state_pool, (0, 2, 1, 3))
    cos_p, sin_p = _rope_tables(jnp.arange(s))
    cos_s, sin_s = _rope_tables(jnp.full((bs,), past))

    xp, xs = x_prompt, x_sample[:, 0, :]
    conv_p, pool_p, conv_s, pool_s = [], [], [], []
    kv_p = kv_s = win_s = None
    for l in range(depth):
        p = _prep_layer(l, w_norm, w_in, w_out, w_dw, b_dw, ln_g, ln_b, w_pw, b_pw, w_pool, pool_scale,
                        g_q, g_k, cmp_pe, cmp_w1, cmp_w2)
        xp, glu, b_in, kv_p = _prompt_layer(l, depth, xp, p, cos_p, sin_p, kv_p)
        conv_p.append(glu[:, s - CONV_STATE:, :])
        pool_p.append(b_in[:, s - POOL_STATE:, :])
        xs, sc_new, sp_new, win_s, kv_s = _sample_layer(
            l, depth, xs, p, cos_s, sin_s, sc_all[l], sp_all[l], cwin_t, ccmp_t, cslc_t, page_table, past,
            kv_s, win_s)
        conv_s.append(jnp.transpose(sc_new, (1, 0, 2)))
        pool_s.append(jnp.transpose(sp_new, (1, 0, 2)))
    cmp_p, slc_p, win_p = kv_p
    cmp_s, slc_s, _ = kv_s
    new_rows = lambda a: _rows_from_feat(a[:, 0])[:, :, None]
    return (xp, xs[:, None, :], jnp.stack(conv_p), jnp.stack(pool_p),
            _rows_from_feat(win_p[:, :, :, s - min(WINDOW, s):]), _rows_from_feat(cmp_p), _rows_from_feat(slc_p),
            jnp.stack(conv_s), jnp.stack(pool_s), _rows_from_feat(win_s.reshape(depth, bs, D_ROWS, lb)),
            new_rows(cmp_s), new_rows(slc_s))
```

```python
import functools

import numpy as np
import jax
import jax.numpy as jnp
from jax import lax
from jax.experimental import pallas as pl
from jax.experimental.pallas import tpu as pltpu

D_MODEL = 1024
D_CONV = 256
D_POOL = 256
N_HEADS = 8
N_KV_HEADS = 2
HEAD_DIM = 64
D_ATTN = N_HEADS * HEAD_DIM
CONV_WIDTH = 31
CONV_STATE = CONV_WIDTH - 1
POOL_WINDOWS = (2, 4, 8, 16)
POOL_GROUP = D_POOL // len(POOL_WINDOWS)
POOL_STATE = max(POOL_WINDOWS) - 1
CMP_BLOCK = 32
CMP_STRIDE = 16
SLC_BLOCK = 64
N_SEL = 8
WINDOW = 256
Q_BLOCK = 256
ROT_DIM = HEAD_DIM // 4
ROT_HALF = ROT_DIM // 2
ROPE_THETA = 500000.0
N_BRANCH = 3
D_KV = N_BRANCH * 2 * N_KV_HEADS * HEAD_DIM
D_ROWS = 2 * N_KV_HEADS * HEAD_DIM
N_REP = N_HEADS // N_KV_HEADS
EPS = 1e-6
NEG = -1e30
BIG = 1e4

GATE_ROWS = 32
D_TOK = 3 * D_CONV + 2 * D_POOL + D_ATTN
D_QKV = D_ATTN + D_KV
D_FEAT = D_QKV + GATE_ROWS
N_FEAT_BLOCKS = D_QKV // HEAD_DIM
SEL_KEYS = 512
VMEM_LIMIT = 56 * 1024 * 1024

F32 = jnp.float32
BF16 = jnp.bfloat16
NT_DIMS = (((1,), (1,)), ((), ()))


def _sigmoid(x):
    return 1.0 / (1.0 + jnp.exp(-x))


def _silu(x):
    return x * _sigmoid(x)


def _gelu_tanh(x):
    return 0.5 * x * (1.0 + jnp.tanh(np.sqrt(2.0 / np.pi).astype(np.float32) * (x + 0.044715 * (x * x * x))))


def _dot(a, b):
    return jnp.dot(a, b, preferred_element_type=F32)


def _dot_nt(a, b):
    return lax.dot_general(a, b, NT_DIMS, preferred_element_type=F32)


def _masked_softmax(s, mask):
    s = jnp.where(mask, s, NEG)
    m = jnp.max(s, axis=-1, keepdims=True)
    p = jnp.where(mask, jnp.exp(s - m), 0.0)
    return p / jnp.maximum(jnp.sum(p, axis=-1, keepdims=True), 1e-30)


def _in_proj_kernel(x_ref, wnorm_ref, wtok_ref, wfeat_ref, gcol_ref, cos_ref, sin_ref, *rest):
    glu_ref, bin_ref, sz_ref, gates_ref, qt_ref, cmp_ref, slc_ref, win_ref = rest[-8:]
    x = x_ref[...]
    ms = jnp.mean(x * x, axis=-1, keepdims=True)
    h = (x * lax.rsqrt(ms + EPS) * wnorm_ref[...]).astype(BF16)

    a = _dot(h, wtok_ref[...])
    o = 0
    glu_ref[...] = a[:, o:o + D_CONV] * _sigmoid(a[:, o + D_CONV:o + 2 * D_CONV])
    o += 2 * D_CONV
    bin_ref[...] = a[:, o:o + D_POOL]
    o += D_POOL
    z = a[:, o:o + D_CONV + D_POOL + D_ATTN]
    sz_ref[...] = _silu(z)

    f = _dot_nt(wfeat_ref[...], h)
    gates_ref[...] = _sigmoid(f[D_QKV:D_FEAT, :])
    cos = cos_ref[...]
    sin = sin_ref[...]
    kv_refs = (cmp_ref, slc_ref, win_ref)
    for hb in range(N_FEAT_BLOCKS):
        blk = f[hb * HEAD_DIM:(hb + 1) * HEAD_DIM, :]
        kv_blk = hb - N_HEADS
        is_value = kv_blk >= 0 and (kv_blk % (2 * N_KV_HEADS)) >= N_KV_HEADS
        if not is_value:
            bms = jnp.mean(blk * blk, axis=0, keepdims=True)
            y = blk * lax.rsqrt(bms + EPS) * gcol_ref[hb * HEAD_DIM:(hb + 1) * HEAD_DIM, :]
            x1 = y[0:ROT_HALF]
            x2 = y[ROT_HALF:ROT_DIM]
            blk = jnp.concatenate([x1 * cos - x2 * sin, x2 * cos + x1 * sin, y[ROT_DIM:]], axis=0)
        if kv_blk < 0:
            qt_ref[hb * HEAD_DIM:(hb + 1) * HEAD_DIM, :] = blk
        else:
            r = kv_blk % (2 * N_KV_HEADS)
            kv_refs[kv_blk // (2 * N_KV_HEADS)][r * HEAD_DIM:(r + 1) * HEAD_DIM, :] = blk


def _in_proj(x, wnorm, wtok, wfeat, gcol, cos_t, sin_t, tm, l, depth, kv_prev):
    n, t, _ = x.shape
    grid = (n, t // tm)
    tok = lambda w: pl.BlockSpec((None, tm, w), lambda i, j: (i, j, 0))
    feat = lambda w: pl.BlockSpec((None, w, tm), lambda i, j: (i, 0, j))
    kv_spec = pl.BlockSpec((None, None, D_ROWS, tm), lambda i, j: (l, i, 0, j))
    full = lambda a: pl.BlockSpec(a.shape, lambda i, j: (0,) * a.ndim)
    tok_shape = lambda w: jax.ShapeDtypeStruct((n, t, w), F32)
    kv_shape = jax.ShapeDtypeStruct((depth, n, D_ROWS, t), F32)
    n_in = 7
    kv_prev = () if kv_prev is None else tuple(kv_prev)
    return pl.pallas_call(
        _in_proj_kernel,
        grid=grid,
        in_specs=[tok(D_MODEL), full(wnorm), full(wtok), full(wfeat), full(gcol),
                  pl.BlockSpec((ROT_HALF, tm), lambda i, j: (0, j)),
                  pl.BlockSpec((ROT_HALF, tm), lambda i, j: (0, j))]
                 + [pl.BlockSpec(memory_space=pl.ANY)] * len(kv_prev),
        out_specs=[tok(D_CONV), tok(D_POOL), tok(D_CONV + D_POOL + D_ATTN), feat(GATE_ROWS),
                   feat(D_ATTN), kv_spec, kv_spec, kv_spec],
        out_shape=[tok_shape(D_CONV), tok_shape(D_POOL), tok_shape(D_CONV + D_POOL + D_ATTN),
                   jax.ShapeDtypeStruct((n, GATE_ROWS, t), F32),
                   jax.ShapeDtypeStruct((n, D_ATTN, t), F32), kv_shape, kv_shape, kv_shape],
        input_output_aliases={n_in + k: 5 + k for k in range(len(kv_prev))},
        compiler_params=pltpu.CompilerParams(dimension_semantics=("arbitrary", "arbitrary"),
                                             vmem_limit_bytes=VMEM_LIMIT),
        name="in_proj",
    )(x, wnorm, wtok, wfeat, gcol, cos_t, sin_t, *kv_prev)


def _conv_tail(acc, sz_a, bdw, lng, lnb, wpw, bpw):
    y = acc + bdw
    mu = jnp.mean(y, axis=-1, keepdims=True)
    yc = y - mu
    var = jnp.mean(yc * yc, axis=-1, keepdims=True)
    y = yc * lax.rsqrt(var + EPS) * lng + lnb
    y = _dot(_silu(y).astype(BF16), wpw) + bpw
    return y * sz_a


def _pool_tail(total, cnt, xcur, sz_b, wpool, pscale):
    d = total / cnt - xcur
    y = _dot(d.astype(BF16), wpool) * pscale
    return y * sz_b


def _pool_window_of_lane(shape):
    lane = lax.broadcasted_iota(jnp.int32, shape, len(shape) - 1)
    w = jnp.full(shape, POOL_WINDOWS[0], jnp.int32)
    for gi in range(1, len(POOL_WINDOWS)):
        w = jnp.where(lane >= gi * POOL_GROUP, POOL_WINDOWS[gi], w)
    return w


def _pool_select(sums, shape):
    lane = lax.broadcasted_iota(jnp.int32, shape, len(shape) - 1)
    total = sums[0]
    for gi in range(1, len(POOL_WINDOWS)):
        total = jnp.where(lane >= gi * POOL_GROUP, sums[gi], total)
    return total


CONV_HALO = 32
POOL_HALO = 16


MIX_ROWS = 256
SUBLANES = 8


def _tap_rows(buf, w_ref, r0, rows, off, taps, lanes):
    y = None
    for c in range(SUBLANES):
        part = None
        for k in taps:
            if (k + off) % SUBLANES != c:
                continue
            base = r0 + k + off - c
            term = w_ref[k:k + 1, lanes] * buf[base:base + rows + SUBLANES, lanes]
            part = term if part is None else part + term
        if part is not None:
            y = part[c:c + rows] if y is None else y + part[c:c + rows]
    return y


def _mix_out_prompt_kernel(x_ref, glu_ref, gprev_ref, bin_ref, bprev_ref, sz_ref, oat_ref, wdw_ref, bdw_ref, lng_ref,
                           lnb_ref, wpw_ref, bpw_ref, wpool_ref, pscale_ref, wout_ref, o_ref,
                           cbuf, pbuf, ymix, *, t):
    i = pl.program_id(1)
    keep = (i > 0).astype(F32)
    cbuf[0:CONV_HALO, :] = gprev_ref[...] * keep
    cbuf[CONV_HALO:CONV_HALO + t, :] = glu_ref[...]
    cbuf[CONV_HALO + t:CONV_HALO + t + SUBLANES, :] = jnp.zeros((SUBLANES, D_CONV), F32)
    pbuf[0:POOL_HALO, :] = bprev_ref[...] * keep
    pbuf[POOL_HALO:POOL_HALO + t, :] = bin_ref[...]
    nab = D_CONV + D_POOL
    for r0 in range(0, t, MIX_ROWS):
        rows = slice(r0, r0 + MIX_ROWS)
        sz = sz_ref[rows, :]
        acc = _tap_rows(cbuf, wdw_ref, r0, MIX_ROWS, CONV_HALO - CONV_STATE, range(CONV_WIDTH), slice(0, D_CONV))
        y_a = _conv_tail(acc, sz[:, 0:D_CONV], bdw_ref[...], lng_ref[...], lnb_ref[...], wpw_ref[...], bpw_ref[...])
        ymix[rows, 0:D_CONV] = y_a.astype(BF16)

        sums = []
        run = jnp.zeros((MIX_ROWS, D_POOL), F32)
        j = 0
        for w in POOL_WINDOWS:
            while j < w:
                run = run + pbuf[pl.ds(POOL_HALO + r0 - j, MIX_ROWS), :]
                j += 1
            sums.append(run)
        total = _pool_select(sums, (MIX_ROWS, D_POOL))
        pos = i * t + r0 + lax.broadcasted_iota(jnp.int32, (MIX_ROWS, D_POOL), 0)
        cnt = jnp.minimum(_pool_window_of_lane((MIX_ROWS, D_POOL)), pos + 1).astype(F32)
        y_b = _pool_tail(total, cnt, bin_ref[rows, :], sz[:, D_CONV:nab], wpool_ref[...], pscale_ref[...])
        ymix[rows, D_CONV:nab] = y_b.astype(BF16)
        ymix[rows, nab:nab + D_ATTN] = (oat_ref[rows, :] * sz[:, nab:nab + D_ATTN]).astype(BF16)
    o_ref[...] = x_ref[...] + _dot(ymix[...], wout_ref[...])


def _mix_out_prompt(x, glu, b_in, sz, o_attn, wdw, bdw, lng, lnb, wpw, bpw, wpool, pscale, wout, t):
    n, s, _ = glu.shape
    assert t % MIX_ROWS == 0 and s % t == 0
    grid = (n, s // t)
    cur = lambda w: pl.BlockSpec((None, t, w), lambda i, j: (i, j, 0))
    prev = lambda rows, w: pl.BlockSpec((None, rows, w), lambda i, j: (i, jnp.maximum(j * (t // rows) - 1, 0), 0))
    full = lambda a: pl.BlockSpec(a.shape, lambda i, j: (0,) * a.ndim)
    d_mix = D_CONV + D_POOL + D_ATTN
    return pl.pallas_call(
        functools.partial(_mix_out_prompt_kernel, t=t),
        grid=grid,
        in_specs=[cur(D_MODEL), cur(D_CONV), prev(CONV_HALO, D_CONV), cur(D_POOL), prev(POOL_HALO, D_POOL),
                  cur(d_mix), cur(D_ATTN),
                  full(wdw), full(bdw), full(lng), full(lnb), full(wpw), full(bpw), full(wpool),
                  full(pscale), full(wout)],
        out_specs=cur(D_MODEL),
        out_shape=jax.ShapeDtypeStruct(x.shape, F32),
        scratch_shapes=[pltpu.VMEM((CONV_HALO + t + SUBLANES, D_CONV), F32),
                        pltpu.VMEM((POOL_HALO + t, D_POOL), F32),
                        pltpu.VMEM((t, d_mix), BF16)],
        compiler_params=pltpu.CompilerParams(dimension_semantics=("arbitrary", "arbitrary"),
                                             vmem_limit_bytes=VMEM_LIMIT),
        name="mix_out_prompt",
    )(x, glu, glu, b_in, b_in, sz, o_attn, wdw, bdw, lng, lnb, wpw, bpw, wpool, pscale, wout)


def _convpool_sample_kernel(sc_ref, sp_ref, glu_ref, bin_ref, sz_ref, wdw_ref, bdw_ref, lng_ref, lnb_ref,
                            wpw_ref, bpw_ref, wpool_ref, pscale_ref, y_ref, sc_out, sp_out, *, pos):
    glu = glu_ref[...]
    acc = wdw_ref[CONV_STATE:CONV_WIDTH, :] * glu
    for k in range(CONV_STATE):
        acc = acc + wdw_ref[k:k + 1, :] * sc_ref[k]
    sz = sz_ref[...]
    y_ref[:, 0:D_CONV] = _conv_tail(acc, sz[:, 0:D_CONV], bdw_ref[...], lng_ref[...], lnb_ref[...],
                                    wpw_ref[...], bpw_ref[...])
    for k in range(CONV_STATE - 1):
        sc_out[k] = sc_ref[k + 1]
    sc_out[CONV_STATE - 1] = glu

    xcur = bin_ref[...]
    shape = xcur.shape
    sums = []
    run = xcur
    j = 1
    for w in POOL_WINDOWS:
        while j < w:
            run = run + sp_ref[POOL_STATE - j]
            j += 1
        sums.append(run)
    total = _pool_select(sums, shape)
    cnt = jnp.minimum(_pool_window_of_lane(shape), pos + 1).astype(F32)
    y_ref[:, D_CONV:D_CONV + D_POOL] = _pool_tail(total, cnt, xcur, sz[:, D_CONV:D_CONV + D_POOL],
                                                  wpool_ref[...], pscale_ref[...])
    for k in range(POOL_STATE - 1):
        sp_out[k] = sp_ref[k + 1]
    sp_out[POOL_STATE - 1] = xcur


def _convpool_sample(sc, sp, glu, b_in, sz, wdw, bdw, lng, lnb, wpw, bpw, wpool, pscale, pos):
    nb = glu.shape[0]
    return pl.pallas_call(
        functools.partial(_convpool_sample_kernel, pos=pos),
        out_shape=[jax.ShapeDtypeStruct((nb, D_CONV + D_POOL), F32),
                   jax.ShapeDtypeStruct(sc.shape, F32), jax.ShapeDtypeStruct(sp.shape, F32)],
        compiler_params=pltpu.CompilerParams(vmem_limit_bytes=VMEM_LIMIT),
        name="convpool_sample",
    )(sc, sp, glu, b_in, sz[:, 0:D_CONV + D_POOL], wdw, bdw, lng, lnb, wpw, bpw, wpool, pscale)


def _out_proj_kernel(x_ref, yab_ref, oat_ref, szc_ref, w_ref, o_ref):
    ya = yab_ref[...].astype(BF16)
    yc = (oat_ref[...] * szc_ref[...]).astype(BF16)
    nab = D_CONV + D_POOL
    o_ref[...] = x_ref[...] + _dot(ya, w_ref[0:nab, :]) + _dot(yc, w_ref[nab:nab + D_ATTN, :])


def _out_proj(x, yab, o_attn, sz, w_out, tm):
    n, t, _ = x.shape
    nab = D_CONV + D_POOL
    assert nab == D_ATTN
    grid = (n, t // tm)
    tok = lambda w, cb=0: pl.BlockSpec((None, tm, w), lambda i, j: (i, j, cb))
    return pl.pallas_call(
        _out_proj_kernel,
        grid=grid,
        in_specs=[tok(D_MODEL), tok(nab), tok(D_ATTN), tok(D_ATTN, 1),
                  pl.BlockSpec(w_out.shape, lambda i, j: (0, 0))],
        out_specs=tok(D_MODEL),
        out_shape=jax.ShapeDtypeStruct(x.shape, F32),
        compiler_params=pltpu.CompilerParams(dimension_semantics=("arbitrary", "arbitrary"),
                                             vmem_limit_bytes=VMEM_LIMIT),
        name="out_proj",
    )(x, yab, o_attn, sz, w_out)


GD = N_KV_HEADS * HEAD_DIM
CMP_FLAT = CMP_STRIDE * GD


FLAT_SPAN = CMP_STRIDE * CMP_STRIDE


def _piece_perm():
    m = np.arange(FLAT_SPAN)
    src = (m % CMP_STRIDE) * CMP_STRIDE + m // CMP_STRIDE
    return jnp.asarray((src[:, None] == np.arange(FLAT_SPAN)[None, :]).astype(np.float32), BF16)


def _fill_flat(x_t, perm, flat_ref, piece0):
    y = _dot_nt(perm, x_t).astype(BF16)
    for kv in range(2):
        for r in range(CMP_STRIDE):
            flat_ref[kv, piece0:piece0 + CMP_STRIDE, r * GD:(r + 1) * GD] = (
                y[r * CMP_STRIDE:(r + 1) * CMP_STRIDE, kv * GD:(kv + 1) * GD])


def _compress_mlp(flat, w1ab, bias, w2):
    n_pc = flat.shape[0]
    hb = _dot(flat, w1ab)
    h = hb[:, 0:GD] + jnp.concatenate([hb[1:n_pc, GD:2 * GD], jnp.zeros((1, GD), F32)], axis=0) + bias
    out = _dot(_gelu_tanh(h).astype(BF16), w2)
    row = lax.broadcasted_iota(jnp.int32, out.shape, 0)
    return jnp.where(row < n_pc - 1, out, 0.0)


def _cmp_bias_kernel(pe_ref, w1_ref, o_ref):
    for kv in range(2):
        o_ref[kv] = _dot(pe_ref[kv].astype(BF16), w1_ref[kv])


def _cmp_bias(pe_flat, w1):
    return pl.pallas_call(_cmp_bias_kernel, out_shape=jax.ShapeDtypeStruct((2, 1, HEAD_DIM), F32),
                          name="cmp_bias")(pe_flat, w1)


def _compress_prompt_kernel(x_ref, perm_ref, w1ab_ref, bias_ref, w2_ref, o_ref, flat_ref, *, s):
    for t in range(s // FLAT_SPAN):
        _fill_flat(x_ref[:, t * FLAT_SPAN:(t + 1) * FLAT_SPAN].astype(BF16), perm_ref[...], flat_ref, t * CMP_STRIDE)
    for kv in range(2):
        o_ref[:, kv * GD:(kv + 1) * GD] = _compress_mlp(flat_ref[kv], w1ab_ref[kv], bias_ref[kv], w2_ref[kv])


def _compress_prompt(cmp_t, l, w1ab, bias, w2):
    _, n, _, s = cmp_t.shape
    assert s % FLAT_SPAN == 0
    n_pc = s // CMP_STRIDE
    perm = _piece_perm()
    full = lambda a: pl.BlockSpec(a.shape, lambda i: (0,) * a.ndim)
    return pl.pallas_call(
        functools.partial(_compress_prompt_kernel, s=s),
        grid=(n,),
        in_specs=[pl.BlockSpec((None, None, D_ROWS, s), lambda i: (l, i, 0, 0)),
                  full(perm), full(w1ab), full(bias), full(w2)],
        out_specs=pl.BlockSpec((None, n_pc, 2 * GD), lambda i: (i, 0, 0)),
        out_shape=jax.ShapeDtypeStruct((n, n_pc, 2 * GD), F32),
        scratch_shapes=[pltpu.VMEM((2, n_pc, CMP_FLAT), BF16)],
        compiler_params=pltpu.CompilerParams(dimension_semantics=("arbitrary",), vmem_limit_bytes=VMEM_LIMIT),
        name="compress_prompt",
    )(cmp_t, perm, w1ab, bias, w2)


def _select_blocks(imp_t, pos_row, n_blocks):
    shape = imp_t.shape
    assert n_blocks % 8 == 0
    j = lax.broadcasted_iota(jnp.int32, shape, 0)
    cur = pos_row // SLC_BLOCK
    forced = (j == 0) | (j == cur) | (j == cur - 1)
    score = jnp.where(j <= cur, jnp.where(forced, BIG, imp_t), -BIG)
    tiles = [score[8 * v:8 * v + 8] for v in range(n_blocks // 8)]
    j8 = lax.broadcasted_iota(jnp.int32, (8, shape[1]), 0)
    ranks = [jnp.zeros((8, shape[1]), F32) for _ in tiles]
    for jp in range(n_blocks):
        sj = score[jp:jp + 1, :]
        for v, tile in enumerate(tiles):
            if jp < 8 * v:
                beats = sj >= tile
            elif jp >= 8 * v + 8:
                beats = sj > tile
            else:
                beats = (sj > tile) | ((sj == tile) & (j8 > jp - 8 * v))
            ranks[v] = ranks[v] + jnp.where(beats, 1.0, 0.0)
    rank = jnp.concatenate(ranks, axis=0)
    return (rank < float(min(N_SEL, n_blocks))).astype(F32)


def _col_softmax(s):
    m = jnp.max(s, axis=0, keepdims=True)
    e = jnp.exp(s - m)
    inv = jnp.where(m > 0.5 * NEG, 1.0 / jnp.maximum(jnp.sum(e, axis=0, keepdims=True), 1e-30), 0.0)
    return e, inv


ONES_ROWS = 16
BLOCK_CODE = 64


def _with_ones(v_t):
    return jnp.concatenate([v_t, jnp.ones((ONES_ROWS, v_t.shape[1]), BF16)], axis=0)


def _normalised(acc):
    return acc[0:HEAD_DIM] * (1.0 / jnp.maximum(acc[HEAD_DIM:HEAD_DIM + 1], 1e-30))


def _mask_heads(sc, ok, tq):
    return jnp.concatenate([jnp.where(ok, sc[:, r * tq:(r + 1) * tq], NEG) for r in range(N_REP)], axis=1)


def _attn_prompt_kernel(qt_ref, kcv_ref, slc_ref, win_ref, gates_ref, ovt_ref, o_ref,
                        ks_ref, kw_ref, s_ref, acc_ref, m_ref, *, s, n_blk):
    tq = Q_BLOCK
    cols = N_REP * tq
    groups = range(N_KV_HEADS)
    qc = pl.program_id(1)
    q0 = qc * tq
    pos_row = q0 + lax.broadcasted_iota(jnp.int32, (1, tq), 1)
    n_slc = s // SLC_BLOCK
    n_cmp = kcv_ref.shape[0]
    scale = HEAD_DIM ** -0.5
    v_row0 = lambda g: (N_KV_HEADS + g) * HEAD_DIM

    @pl.when(qc == 0)
    def _():
        for g in range(N_KV_HEADS):
            for t in range(s // 128):
                cols_t = slice(t * 128, (t + 1) * 128)
                ks_ref[g, cols_t, 0:HEAD_DIM] = slc_ref[g * HEAD_DIM:(g + 1) * HEAD_DIM, cols_t].T.astype(BF16)
                block_of_key = (t * 128 + lax.broadcasted_iota(jnp.int32, (128, BLOCK_CODE), 0)) // SLC_BLOCK
                code = block_of_key == lax.broadcasted_iota(jnp.int32, (128, BLOCK_CODE), 1)
                ks_ref[g, cols_t, HEAD_DIM:HEAD_DIM + BLOCK_CODE] = jnp.where(code, 1.0, 0.0).astype(BF16)
                kw_ref[g, cols_t, :] = win_ref[g * HEAD_DIM:(g + 1) * HEAD_DIM, cols_t].T.astype(BF16)

    gates_t = gates_ref[...]
    vc_t = kcv_ref[:, GD:2 * GD].T.astype(BF16)
    q_ts = []
    for g in groups:
        q_t = jnp.concatenate([qt_ref[(g * N_REP + r) * HEAD_DIM:(g * N_REP + r + 1) * HEAD_DIM, :]
                               for r in range(N_REP)], axis=1)
        q_ts.append((q_t * scale).astype(BF16))

    c_idx = lax.broadcasted_iota(jnp.int32, (n_cmp, 1), 0)
    valid_c = (c_idx * CMP_STRIDE + (CMP_BLOCK - 1) <= pos_row) & (c_idx < n_blk)
    ovt = ovt_ref[...]
    o_cs, q_codes = [], []
    for g in groups:
        kc = kcv_ref[:, g * HEAD_DIM:(g + 1) * HEAD_DIM].astype(BF16)
        e_c, inv_c = _col_softmax(_mask_heads(_dot(kc, q_ts[g]), valid_c, tq))
        o_cs.append(_dot(vc_t[g * HEAD_DIM:(g + 1) * HEAD_DIM, :], e_c.astype(BF16)) * inv_c)
        pc_sum = e_c[:, 0:tq] * inv_c[:, 0:tq]
        for r in range(1, N_REP):
            pc_sum = pc_sum + e_c[:, r * tq:(r + 1) * tq] * inv_c[:, r * tq:(r + 1) * tq]
        pc_hi = pc_sum.astype(BF16)
        pc_lo = (pc_sum - pc_hi.astype(F32)).astype(BF16)
        imp_t = _dot(ovt, pc_hi) + _dot(ovt, pc_lo)
        sel_t = _select_blocks(imp_t, pos_row, n_slc)
        bias = jnp.concatenate([jnp.where(sel_t > 0.5, 0.0, NEG), jnp.zeros((BLOCK_CODE - n_slc, tq), F32)], axis=0)
        q_codes.append(jnp.concatenate([q_ts[g], jnp.concatenate([bias] * N_REP, axis=1).astype(BF16)], axis=0))

    n_keys = q0 + tq
    rest = n_keys % SEL_KEYS
    half_tail = (rest > 0) & (rest <= SEL_KEYS // 2)
    n_steps = n_keys // SEL_KEYS + (rest > SEL_KEYS // 2).astype(jnp.int32)
    tail_k0 = pl.multiple_of((n_keys // SEL_KEYS) * SEL_KEYS, SEL_KEYS)
    last_full = jnp.logical_not(half_tail)

    def score_chunk(k0, size, diagonal):
        key = k0 + lax.broadcasted_iota(jnp.int32, (size, 1), 0)
        for g in groups:
            sc = _dot(ks_ref[g, pl.ds(k0, size), :], q_codes[g])
            if diagonal:
                sc = _mask_heads(sc, key <= pos_row, tq)
            s_ref[g, pl.ds(k0, size), :] = sc.astype(BF16)
            m_ref[g] = jnp.maximum(m_ref[g], jnp.max(sc, axis=0, keepdims=True))

    m_ref[...] = jnp.full(m_ref.shape, NEG, F32)

    def score_step(i, carry):
        score_chunk(pl.multiple_of(i * SEL_KEYS, SEL_KEYS), SEL_KEYS, False)
        return carry

    lax.fori_loop(0, n_steps - last_full.astype(jnp.int32), score_step, 0)

    @pl.when(last_full)
    def _():
        score_chunk(pl.multiple_of((n_steps - 1) * SEL_KEYS, SEL_KEYS), SEL_KEYS, True)

    @pl.when(half_tail)
    def _():
        score_chunk(tail_k0, SEL_KEYS // 2, True)

    m_sel = [m_ref[g].astype(BF16) for g in groups]

    acc_ref[...] = jnp.zeros(acc_ref.shape, F32)

    def value_chunk(k0, size):
        for g in groups:
            p = jnp.exp(s_ref[g, pl.ds(k0, size), :] - m_sel[g])
            v_t = slc_ref[v_row0(g):v_row0(g) + HEAD_DIM, pl.ds(k0, size)].astype(BF16)
            acc_ref[g] += _dot(_with_ones(v_t), p)

    def value_step(i, carry):
        value_chunk(pl.multiple_of(i * SEL_KEYS, SEL_KEYS), SEL_KEYS)
        return carry

    lax.fori_loop(0, n_steps, value_step, 0)

    @pl.when(half_tail)
    def _():
        value_chunk(tail_k0, SEL_KEYS // 2)

    n_prev = -(-(WINDOW - 1) // tq)
    k0s, oks = [], []
    for b in range(n_prev + 1):
        kb = qc - n_prev + b
        k0s.append(pl.multiple_of(jnp.maximum(kb, 0) * tq, tq))
        kp = kb * tq + lax.broadcasted_iota(jnp.int32, (tq, 1), 0)
        oks.append((kp <= pos_row) & (kp > pos_row - WINDOW) & (kp >= 0))

    out_blocks = []
    for g in groups:
        o_s = _normalised(acc_ref[g])
        s_w = jnp.concatenate([_mask_heads(_dot(kw_ref[g, pl.ds(k0s[b], tq), :], q_ts[g]), oks[b], tq)
                               for b in range(n_prev + 1)], axis=0)
        e_w = jnp.exp(s_w.astype(BF16) - jnp.max(s_w, axis=0, keepdims=True).astype(BF16))
        o_w = _dot(_with_ones(win_ref[v_row0(g):v_row0(g) + HEAD_DIM, pl.ds(k0s[0], tq)].astype(BF16)), e_w[0:tq])
        for b in range(1, n_prev + 1):
            o_w = o_w + _dot(_with_ones(win_ref[v_row0(g):v_row0(g) + HEAD_DIM, pl.ds(k0s[b], tq)].astype(BF16)),
                             e_w[b * tq:(b + 1) * tq])
        o_w = _normalised(o_w)
        for r in range(N_REP):
            h = g * N_REP + r
            c = slice(r * tq, (r + 1) * tq)
            out_blocks.append(gates_t[h:h + 1] * o_cs[g][:, c] + gates_t[N_HEADS + h:N_HEADS + h + 1] * o_s[:, c]
                              + gates_t[2 * N_HEADS + h:2 * N_HEADS + h + 1] * o_w[:, c])
    o_ref[...] = jnp.concatenate(out_blocks, axis=0).T


def _overlap_t(n_cmp_rows, n_blk, n_slc):
    c = np.arange(n_cmp_rows)
    start = c * CMP_STRIDE
    end = start + CMP_BLOCK - 1
    j0 = np.arange(n_slc) * SLC_BLOCK
    ov = (end[None, :] >= j0[:, None]) & (start[None, :] < j0[:, None] + SLC_BLOCK) & (c[None, :] < n_blk)
    return jnp.asarray(ov.astype(np.float32), BF16)


def _attn_prompt(q_t, kcv, slc_t, win_t, gates, l):
    n, _, s = q_t.shape
    n_cmp = kcv.shape[1]
    n_blk = s // CMP_STRIDE - CMP_BLOCK // CMP_STRIDE + 1
    n_slc = s // SLC_BLOCK
    ovt = _overlap_t(n_cmp, n_blk, n_slc)
    tq = Q_BLOCK
    cols = N_REP * tq
    assert s % SEL_KEYS == 0 and (SEL_KEYS // 2) % tq == 0 and n_slc <= BLOCK_CODE
    return pl.pallas_call(
        functools.partial(_attn_prompt_kernel, s=s, n_blk=n_blk),
        grid=(n, s // tq),
        in_specs=[pl.BlockSpec((None, D_ATTN, tq), lambda i, j: (i, 0, j)),
                  pl.BlockSpec((None,) + kcv.shape[1:], lambda i, j: (i, 0, 0)),
                  pl.BlockSpec((None, None, D_ROWS, s), lambda i, j: (l, i, 0, 0)),
                  pl.BlockSpec((None, None, D_ROWS, s), lambda i, j: (l, i, 0, 0)),
                  pl.BlockSpec((None, GATE_ROWS, tq), lambda i, j: (i, 0, j)),
                  pl.BlockSpec(ovt.shape, lambda i, j: (0, 0))],
        out_specs=pl.BlockSpec((None, tq, D_ATTN), lambda i, j: (i, j, 0)),
        out_shape=jax.ShapeDtypeStruct((n, s, D_ATTN), F32),
        scratch_shapes=[pltpu.VMEM((N_KV_HEADS, s, HEAD_DIM + BLOCK_CODE), BF16),
                        pltpu.VMEM((N_KV_HEADS, s, HEAD_DIM), BF16),
                        pltpu.VMEM((N_KV_HEADS, s, cols), BF16),
                        pltpu.VMEM((N_KV_HEADS, HEAD_DIM + ONES_ROWS, cols), F32),
                        pltpu.VMEM((N_KV_HEADS, 1, cols), F32)],
        compiler_params=pltpu.CompilerParams(dimension_semantics=("arbitrary", "arbitrary"),
                                             vmem_limit_bytes=VMEM_LIMIT),
        name="attn_prompt",
    )(q_t, kcv, slc_t, win_t, gates, ovt)


PAGE = 128
SAMPLE_SEQS = 4


def _sample_geometry(past):
    length = past + 1
    padded = -(-length // SLC_BLOCK) * SLC_BLOCK
    n_pc = padded // CMP_STRIDE
    n_pc_pad = -(-n_pc // CMP_STRIDE) * CMP_STRIDE
    return dict(n_pc=n_pc, n_pc_pad=n_pc_pad, n_blk=n_pc - CMP_BLOCK // CMP_STRIDE + 1, n_slc=padded // SLC_BLOCK)


def _attn_sample_kernel(pt_ref, q_ref, new_ref, gates_ref, wincol_ref, cwin_ref, ov_ref,
                        perm_ref, w1ab_ref, bias_ref, w2_ref, ccmp_hbm, cslc_hbm, *rest, l, past, n_pages, nseq):
    o_ref, wout_ref, flat_ref, cmp_buf, slc_buf, sem = rest[-6:]
    step = pl.program_id(0)
    slot = step % 2

    def page_copies(of_step, into_slot, lookup):
        copies = []
        for u in range(nseq):
            for j in range(n_pages):
                page = pt_ref[of_step * nseq + u, j] if lookup else 0
                k = u * n_pages + j
                copies.append(pltpu.make_async_copy(ccmp_hbm.at[l, page], cmp_buf.at[into_slot, k], sem.at[into_slot, 0]))
                copies.append(pltpu.make_async_copy(cslc_hbm.at[l, page], slc_buf.at[into_slot, k], sem.at[into_slot, 1]))
        return copies

    @pl.when(step == 0)
    def _():
        for c in page_copies(0, 0, True):
            c.start()

    @pl.when(step + 1 < pl.num_programs(0))
    def _():
        for c in page_copies(step + 1, 1 - slot, True):
            c.start()

    for c in page_copies(step, slot, False):
        c.wait()
    cmp_pages = [[cmp_buf.at[slot, u * n_pages + j] for j in range(n_pages)] for u in range(nseq)]
    slc_pages = [[slc_buf.at[slot, u * n_pages + j] for j in range(n_pages)] for u in range(nseq)]
    geo = _sample_geometry(past)
    n_pc_pad, n_blk, n_slc = geo["n_pc_pad"], geo["n_blk"], geo["n_slc"]
    pos = past
    lb = cwin_ref.shape[-1]
    scale = HEAD_DIM ** -0.5
    head = lax.broadcasted_iota(jnp.int32, (N_HEADS, 1), 0)
    lane = lax.broadcasted_iota(jnp.int32, (1, GD), 1)
    lane_h = lax.broadcasted_iota(jnp.int32, (N_HEADS, GD), 1)
    own = (lane_h // HEAD_DIM) == (head // N_REP)

    def new_row(u, branch, kv):
        o = (branch * 2 + kv) * GD
        return new_ref[u, :, o:o + GD]

    def rounded(x):
        return x.astype(BF16).astype(F32)

    kcv = []
    pages_per_span = FLAT_SPAN // PAGE
    past_pc = past // CMP_STRIDE
    for u in range(nseq):
        row0 = u * n_pc_pad
        for t in range(n_pages // pages_per_span):
            x_t = jnp.concatenate([cmp_pages[u][t * pages_per_span + w][...].reshape(2 * GD, PAGE)
                                   for w in range(pages_per_span)], axis=1)
            _fill_flat(x_t.astype(BF16), perm_ref[...], flat_ref, row0 + t * CMP_STRIDE)
        for kv in range(2):
            flat_ref[kv, row0 + past_pc:row0 + n_pc_pad, :] = jnp.zeros((n_pc_pad - past_pc, CMP_FLAT), BF16)
            flat_ref[kv, row0 + past_pc:row0 + past_pc + 1, 0:GD] = new_row(u, 0, kv).astype(BF16)
    for kv in range(2):
        kcv.append(_compress_mlp(flat_ref[kv], w1ab_ref[kv], bias_ref[kv], w2_ref[kv]).astype(BF16))
    for u in range(nseq):
        _attn_sample_one(u, kcv[0][u * n_pc_pad:(u + 1) * n_pc_pad], kcv[1][u * n_pc_pad:(u + 1) * n_pc_pad],
                         q_ref, new_row, rounded, gates_ref, wincol_ref, cwin_ref, ov_ref,
                         slc_pages[u], o_ref, wout_ref, geo=geo, past=past, consts=(head, lane, lane_h, own))


def _attn_sample_one(u, kc, vc, q_ref, new_row, rounded, gates_ref, wincol_ref, cwin_ref, ov_ref,
                     slc_pages, o_ref, wout_ref, *, geo, past, consts):
    head, lane, lane_h, own = consts
    n_pc_pad, n_blk, n_slc = geo["n_pc_pad"], geo["n_blk"], geo["n_slc"]
    n_pages = len(slc_pages)
    pos = past
    lb = cwin_ref.shape[-1]
    scale = HEAD_DIM ** -0.5
    q_rows = jnp.concatenate([q_ref[u, :, h * HEAD_DIM:(h + 1) * HEAD_DIM] for h in range(N_HEADS)], axis=0)
    qx = (jnp.where(own, jnp.concatenate([q_rows] * N_KV_HEADS, axis=1), 0.0) * scale).astype(BF16)
    qx32 = qx.astype(F32)
    c_idx = lax.broadcasted_iota(jnp.int32, (1, n_pc_pad), 1)
    valid = (c_idx * CMP_STRIDE + (CMP_BLOCK - 1) <= pos) & (c_idx < n_blk)
    p_c = _masked_softmax(_dot_nt(qx, kc), valid)
    o_c = _dot(p_c.astype(BF16), vc)

    cur = pos // SLC_BLOCK
    forced = (lane == 0) | (lane == cur) | (lane == cur - 1)
    jp = lax.broadcasted_iota(jnp.int32, (GD, GD), 0)
    jj = lax.broadcasted_iota(jnp.int32, (GD, GD), 1)
    sels = []
    for g in range(N_KV_HEADS):
        pc_sum = jnp.sum(p_c[g * N_REP:(g + 1) * N_REP], axis=0, keepdims=True)
        pc_hi = pc_sum.astype(BF16)
        pc_lo = (pc_sum - pc_hi.astype(F32)).astype(BF16)
        imp = _dot(pc_hi, ov_ref[...]) + _dot(pc_lo, ov_ref[...])
        score = jnp.where(lane <= cur, jnp.where(forced, BIG, imp), -BIG)
        score = jnp.where(lane < n_slc, score, -2.0 * BIG)
        score_b = jnp.broadcast_to(score, (GD, GD))
        score_a = score_b.T
        beats = (score_a > score_b) | ((score_a == score_b) & (jp < jj))
        rank = jnp.sum(beats.astype(F32), axis=0, keepdims=True)
        sel = ((rank < float(min(N_SEL, n_slc))) & (lane < n_slc)).astype(F32)
        sels.append(jnp.broadcast_to(sel, (N_REP, GD)))
    sel_h = jnp.concatenate(sels, axis=0)

    blocks_per_tile = GD // SLC_BLOCK
    chosen = []
    for t in range(past // GD):
        tile = sel_h[:, t * blocks_per_tile:t * blocks_per_tile + 1]
        for b in range(1, blocks_per_tile):
            tile = jnp.where(lane >= b * SLC_BLOCK, sel_h[:, t * blocks_per_tile + b:t * blocks_per_tile + b + 1], tile)
        chosen.append(jnp.broadcast_to(tile, (N_HEADS, GD)))
    chosen = jnp.concatenate(chosen, axis=-1) > 0.5
    s_past = jnp.concatenate([_dot(qx, slc_pages[j][0].astype(BF16)) for j in range(n_pages)], axis=-1)
    s_past = jnp.where(chosen, s_past, NEG)
    new_ok = jnp.sum(jnp.where(lane == pos // SLC_BLOCK, sel_h, 0.0), axis=-1, keepdims=True) > 0.5
    s_new = jnp.where(new_ok, jnp.sum(qx32 * rounded(new_row(u, 1, 0)), axis=-1, keepdims=True), NEG)
    m = jnp.maximum(jnp.max(s_past, axis=-1, keepdims=True), s_new)
    p_past = jnp.where(chosen, jnp.exp(s_past - m), 0.0)
    p_new = jnp.where(new_ok, jnp.exp(s_new - m), 0.0)
    denom = jnp.maximum(jnp.sum(p_past, axis=-1, keepdims=True) + p_new, 1e-30)
    o_s = rounded(p_new) * rounded(new_row(u, 1, 1))
    for j in range(n_pages):
        o_s = o_s + _dot_nt(p_past[:, j * PAGE:(j + 1) * PAGE].astype(BF16), slc_pages[j][1].astype(BF16))
    o_s = o_s / denom

    kpos = (past - lb) + lax.broadcasted_iota(jnp.int32, (1, lb), 1)
    w_ok = (kpos <= pos) & (kpos > pos - WINDOW)
    s_w = jnp.where(w_ok, _dot(qx, cwin_ref[u, 0].astype(BF16)), NEG)
    s_wn = jnp.sum(qx32 * rounded(new_row(u, 2, 0)), axis=-1, keepdims=True)
    m = jnp.maximum(jnp.max(s_w, axis=-1, keepdims=True), s_wn)
    p_w = jnp.where(w_ok, jnp.exp(s_w - m), 0.0)
    p_wn = jnp.exp(s_wn - m)
    denom = jnp.maximum(jnp.sum(p_w, axis=-1, keepdims=True) + p_wn, 1e-30)
    o_w = (_dot_nt(p_w.astype(BF16), cwin_ref[u, 1].astype(BF16))
           + rounded(p_wn) * rounded(new_row(u, 2, 1))) / denom

    gates = gates_ref[u]
    gate_id = lax.broadcasted_iota(jnp.int32, (N_HEADS, GATE_ROWS), 1)
    gate = lambda br: jnp.sum(jnp.where(gate_id == head + br * N_HEADS, gates, 0.0), axis=-1, keepdims=True)
    o = gate(0) * o_c + gate(1) * o_s + gate(2) * o_w
    for h in range(N_HEADS):
        g = h // N_REP
        o_ref[u, :, h * HEAD_DIM:(h + 1) * HEAD_DIM] = o[h:h + 1, g * HEAD_DIM:(g + 1) * HEAD_DIM]

    for kv in range(2):
        wout_ref[u, kv, :, 0:lb - 1] = cwin_ref[u, kv, :, 1:lb]
        wout_ref[u, kv, :, lb - 1:lb] = wincol_ref[u, kv * GD:(kv + 1) * GD, :]


def _attn_sample(l, q_tok, new_tok, gates, win_col, cwin_t, ccmp_t, cslc_t, page_table, w1ab, bias, w2, win_prev):
    nb = q_tok.shape[0]
    n_pages = page_table.shape[1]
    past = n_pages * PAGE
    geo = _sample_geometry(past)
    lb = cwin_t.shape[-1]
    assert FLAT_SPAN % PAGE == 0 and n_pages % (FLAT_SPAN // PAGE) == 0
    perm = _piece_perm()
    ov = _overlap_t(geo["n_pc_pad"], geo["n_blk"], GD).T
    ov = jnp.where(jnp.arange(GD)[None, :] < geo["n_slc"], ov, 0).astype(BF16)
    assert GD % SLC_BLOCK == 0 and past % GD == 0
    nseq = SAMPLE_SEQS if nb % SAMPLE_SEQS == 0 else 1
    seq = lambda a: pl.BlockSpec((nseq,) + a.shape[1:], lambda i, pt: (i,) + (0,) * (a.ndim - 1))
    full = lambda a: pl.BlockSpec(a.shape, lambda i, pt: (0,) * a.ndim)
    operands = [page_table, q_tok, new_tok, gates, win_col, cwin_t, ov, perm, w1ab, bias, w2, ccmp_t, cslc_t]
    win_prev = () if win_prev is None else (win_prev,)
    page_buf = pltpu.VMEM((2, nseq * n_pages, 2, GD, PAGE), F32)
    grid_spec = pltpu.PrefetchScalarGridSpec(
        num_scalar_prefetch=1,
        grid=(nb // nseq,),
        in_specs=[seq(q_tok), seq(new_tok), seq(gates), seq(win_col),
                  pl.BlockSpec((None, nseq, 2, GD, lb), lambda i, pt: (l, i, 0, 0, 0)),
                  full(ov), full(perm), full(w1ab), full(bias), full(w2)]
                 + [pl.BlockSpec(memory_space=pl.ANY)] * (2 + len(win_prev)),
        out_specs=[pl.BlockSpec((nseq, 1, D_ATTN), lambda i, pt: (i, 0, 0)),
                   pl.BlockSpec((None, nseq, 2, GD, lb), lambda i, pt: (l, i, 0, 0, 0))],
        scratch_shapes=[pltpu.VMEM((2, nseq * geo["n_pc_pad"], CMP_FLAT), BF16), page_buf, page_buf,
                        pltpu.SemaphoreType.DMA((2, 2))],
    )
    return pl.pallas_call(
        functools.partial(_attn_sample_kernel, l=l, past=past, n_pages=n_pages, nseq=nseq),
        grid_spec=grid_spec,
        out_shape=[jax.ShapeDtypeStruct((nb, 1, D_ATTN), F32),
                   jax.ShapeDtypeStruct((cwin_t.shape[0], nb, 2, GD, lb), F32)],
        input_output_aliases={len(operands): 1} if win_prev else {},
        compiler_params=pltpu.CompilerParams(dimension_semantics=("arbitrary",), vmem_limit_bytes=VMEM_LIMIT),
        name="attn_sample",
    )(*operands, *win_prev)


def _prep_layer(l, w_norm, w_in, w_out, w_dw, b_dw, ln_g, ln_b, w_pw, b_pw, w_pool, pool_scale,
                g_q, g_k, cmp_pe, cmp_w1, cmp_w2):
    sizes = (D_CONV, D_CONV, D_CONV, D_POOL, D_POOL, D_ATTN, D_KV, N_BRANCH * N_HEADS, D_ATTN)
    offs = [0] + [int(v) for v in np.cumsum(sizes)]
    col = lambda k: w_in[l][:, offs[k]:offs[k + 1]]
    a_val, a_gate, z_a, b_in, z_b, q, kv, gate, z_c = (col(k) for k in range(9))
    pad = jnp.zeros((D_MODEL, GATE_ROWS - N_BRANCH * N_HEADS), F32)
    wtok = jnp.concatenate([a_val, a_gate, b_in, z_a, z_b, z_c], axis=1).astype(BF16)
    wfeat = jnp.concatenate([q, kv, gate, pad], axis=1).T.astype(BF16)
    ones = jnp.ones((N_KV_HEADS * HEAD_DIM,), F32)
    gcol = jnp.concatenate([jnp.tile(g_q[l], N_HEADS)]
                           + [piece for br in range(N_BRANCH)
                              for piece in (jnp.tile(g_k[l, br], N_KV_HEADS), ones)])[:, None]
    wpool_bd = jnp.zeros((D_POOL, D_POOL), F32)
    for gi in range(len(POOL_WINDOWS)):
        wpool_bd = wpool_bd.at[gi * POOL_GROUP:(gi + 1) * POOL_GROUP, gi * POOL_GROUP:(gi + 1) * POOL_GROUP].set(w_pool[l, gi])
    row = lambda v: v[None, :]

    def both_groups(w):
        z = jnp.zeros_like(w)
        return jnp.stack([jnp.concatenate([w, z], axis=2), jnp.concatenate([z, w], axis=2)], axis=1).reshape(-1, GD)

    half = CMP_STRIDE * HEAD_DIM
    w1 = cmp_w1[l].astype(BF16)
    w1ab = jnp.stack([jnp.concatenate(
        [both_groups(w1[kv, :half].reshape(CMP_STRIDE, HEAD_DIM, HEAD_DIM)),
         both_groups(w1[kv, half:].reshape(CMP_STRIDE, HEAD_DIM, HEAD_DIM))], axis=1) for kv in range(2)])
    w2p = jnp.stack([both_groups(cmp_w2[l, kv].astype(BF16)[None]) for kv in range(2)])
    cbias = jnp.tile(_cmp_bias(cmp_pe[l].reshape(2, 1, CMP_BLOCK * HEAD_DIM), w1), (1, 1, N_KV_HEADS))
    return dict(
        wnorm=row(w_norm[l]), wtok=wtok, wfeat=wfeat, gcol=gcol,
        wdw=w_dw[l], bdw=row(b_dw[l]), lng=row(ln_g[l]), lnb=row(ln_b[l]),
        wpw=w_pw[l].astype(BF16), bpw=row(b_pw[l]),
        wpool=wpool_bd.astype(BF16), pscale=row(pool_scale[l]),
        w1ab=w1ab, cbias=cbias, w2p=w2p,
        wout=w_out[l].astype(BF16),
    )


def _rope_tables(pos):
    inv = ROPE_THETA ** (-jnp.arange(ROT_HALF, dtype=F32) * 2.0 / ROT_DIM)
    ang = pos.astype(F32)[:, None] * inv[None, :]
    return jnp.cos(ang).T, jnp.sin(ang).T


def _prompt_layer(l, depth, x, p, cos_t, sin_t, kv_prev):
    glu, b_in, sz, gates, q_t, *kv = _in_proj(
        x, p["wnorm"], p["wtok"], p["wfeat"], p["gcol"], cos_t, sin_t, min(1024, x.shape[1]), l, depth, kv_prev)
    cmp_t, slc_t, win_t = kv
    kcv = _compress_prompt(cmp_t, l, p["w1ab"], p["cbias"], p["w2p"])
    o_attn = _attn_prompt(q_t, kcv, slc_t, win_t, gates, l)
    y = _mix_out_prompt(x, glu, b_in, sz, o_attn, p["wdw"], p["bdw"], p["lng"], p["lnb"], p["wpw"], p["bpw"],
                        p["wpool"], p["pscale"], p["wout"], t=min(512, x.shape[1]))
    return y, glu, b_in, kv


def _sample_layer(l, depth, x, p, cos_t, sin_t, sc, sp, cwin_t, ccmp_t, cslc_t, page_table, past, kv_prev, win_prev):
    nb = x.shape[0]
    glu, b_in, sz, gates, q_t, *kv = _in_proj(
        x[None], p["wnorm"], p["wtok"], p["wfeat"], p["gcol"], cos_t, sin_t, nb, l, depth, kv_prev)
    glu, b_in, sz, gates = glu[0], b_in[0], sz[0], gates[0]
    yab, sc_new, sp_new = _convpool_sample(sc, sp, glu, b_in, sz, p["wdw"], p["bdw"], p["lng"], p["lnb"],
                                           p["wpw"], p["bpw"], p["wpool"], p["pscale"], pos=past)
    new_t = jnp.concatenate([a[l, 0] for a in kv], axis=0)
    new_tok = new_t.T[:, None, :]
    q_tok = q_t[0].T[:, None, :]
    win_col = kv[2][l, 0].T[:, :, None]
    o_attn, win_next = _attn_sample(l, q_tok, new_tok, gates.T[:, None, :], win_col, cwin_t, ccmp_t, cslc_t,
                                    page_table, p["w1ab"], p["cbias"], p["w2p"], win_prev)
    y = _out_proj(x[None], yab[None], o_attn.reshape(1, nb, D_ATTN), sz[None], p["wout"], tm=nb)[0]
    return y, sc_new, sp_new, win_next, kv


def _rows_from_feat(a):
    lead = a.shape[:-2]
    a = a.reshape(lead + (2, N_KV_HEADS, HEAD_DIM, a.shape[-1]))
    nl = len(lead)
    return jnp.transpose(a, tuple(range(nl)) + (nl + 3, nl, nl + 1, nl + 2))


def kernel(x_prompt, x_sample, state_conv, state_pool, cache_win_kv, cache_cmp_kv, cache_slc_kv, page_table,
           w_norm, w_in, w_out, w_dw, b_dw, ln_g, ln_b, w_pw, b_pw, w_pool, pool_scale,
           g_q, g_k, cmp_pe, cmp_w1, cmp_w2):
    bp, s, _ = x_prompt.shape
    bs, t_new, _ = x_sample.shape
    depth = w_in.shape[0]
    assert t_new == 1 and cache_cmp_kv.shape[2] == PAGE and s % 512 == 0
    past = page_table.shape[1] * PAGE
    lb = cache_win_kv.shape[2]

    def to_feat(c):
        c = jnp.transpose(c, (0, 1, 3, 4, 5, 2))
        return c.reshape(c.shape[:3] + (GD, c.shape[-1]))
    cwin_t, ccmp_t, cslc_t = to_feat(cache_win_kv), to_feat(cache_cmp_kv), to_feat(cache_slc_kv)
    sc_all = jnp.transpose(state_conv, (0, 2, 1, 3))
    sp_all = jnp.transpose(state_pool, (0, 2, 1, 3))
    cos_p, sin_p = _rope_tables(jnp.arange(s))
    cos_s, sin_s = _rope_tables(jnp.full((bs,), past))

    xp, xs = x_prompt, x_sample[:, 0, :]
    conv_p, pool_p, conv_s, pool_s = [], [], [], []
    kv_p = kv_s = win_s = None
    for l in range(depth):
        p = _prep_layer(l, w_norm, w_in, w_out, w_dw, b_dw, ln_g, ln_b, w_pw, b_pw, w_pool, pool_scale,
                        g_q, g_k, cmp_pe, cmp_w1, cmp_w2)
        xp, glu, b_in, kv_p = _prompt_layer(l, depth, xp, p, cos_p, sin_p, kv_p)
        conv_p.append(glu[:, s - CONV_STATE:, :])
        pool_p.append(b_in[:, s - POOL_STATE:, :])
        xs, sc_new, sp_new, win_s, kv_s = _sample_layer(
            l, depth, xs, p, cos_s, sin_s, sc_all[l], sp_all[l], cwin_t, ccmp_t, cslc_t, page_table, past,
            kv_s, win_s)
        conv_s.append(jnp.transpose(sc_new, (1, 0, 2)))
        pool_s.append(jnp.transpose(sp_new, (1, 0, 2)))
    cmp_p, slc_p, win_p = kv_p
    cmp_s, slc_s, _ = kv_s
    new_rows = lambda a: _rows_from_feat(a[:, 0])[:, :, None]
    return (xp, xs[:, None, :], jnp.stack(conv_p), jnp.stack(pool_p),
            _rows_from_feat(win_p[:, :, :, s - min(WINDOW, s):]), _rows_from_feat(cmp_p), _rows_from_feat(slc_p),
            jnp.stack(conv_s), jnp.stack(pool_s), _rows_from_feat(win_s.reshape(depth, bs, D_ROWS, lb)),
            new_rows(cmp_s), new_rows(slc_s))
```

```python
import functools

import numpy as np
import jax
import jax.numpy as jnp
from jax import lax
from jax.experimental import pallas as pl
from jax.experimental.pallas import tpu as pltpu

D_MODEL = 1024
D_CONV = 256
D_POOL = 256
N_HEADS = 8
N_KV_HEADS = 2
HEAD_DIM = 64
D_ATTN = N_HEADS * HEAD_DIM
CONV_WIDTH = 31
CONV_STATE = CONV_WIDTH - 1
POOL_WINDOWS = (2, 4, 8, 16)
POOL_GROUP = D_POOL // len(POOL_WINDOWS)
POOL_STATE = max(POOL_WINDOWS) - 1
CMP_BLOCK = 32
CMP_STRIDE = 16
SLC_BLOCK = 64
N_SEL = 8
WINDOW = 256
Q_BLOCK = 256
ROT_DIM = HEAD_DIM // 4
ROT_HALF = ROT_DIM // 2
ROPE_THETA = 500000.0
N_BRANCH = 3
D_KV = N_BRANCH * 2 * N_KV_HEADS * HEAD_DIM
D_ROWS = 2 * N_KV_HEADS * HEAD_DIM
N_REP = N_HEADS // N_KV_HEADS
EPS = 1e-6
NEG = -1e30
BIG = 1e4

GATE_ROWS = 32
D_TOK = 3 * D_CONV + 2 * D_POOL + D_ATTN
D_QKV = D_ATTN + D_KV
D_FEAT = D_QKV + GATE_ROWS
N_FEAT_BLOCKS = D_QKV // HEAD_DIM
SEL_KEYS = 512
VMEM_LIMIT = 56 * 1024 * 1024

F32 = jnp.float32
BF16 = jnp.bfloat16
NT_DIMS = (((1,), (1,)), ((), ()))


def _sigmoid(x):
    return 1.0 / (1.0 + jnp.exp(-x))


def _silu(x):
    return x * _sigmoid(x)


def _gelu_tanh(x):
    return 0.5 * x * (1.0 + jnp.tanh(np.sqrt(2.0 / np.pi).astype(np.float32) * (x + 0.044715 * (x * x * x))))


def _dot(a, b):
    return jnp.dot(a, b, preferred_element_type=F32)


def _dot_nt(a, b):
    return lax.dot_general(a, b, NT_DIMS, preferred_element_type=F32)


def _masked_softmax(s, mask):
    s = jnp.where(mask, s, NEG)
    m = jnp.max(s, axis=-1, keepdims=True)
    p = jnp.where(mask, jnp.exp(s - m), 0.0)
    return p / jnp.maximum(jnp.sum(p, axis=-1, keepdims=True), 1e-30)


def _in_proj_kernel(x_ref, wnorm_ref, wtok_ref, wfeat_ref, gcol_ref, cos_ref, sin_ref, *rest):
    glu_ref, bin_ref, sz_ref, gates_ref, qt_ref, cmp_ref, slc_ref, win_ref = rest[-8:]
    x = x_ref[...]
    ms = jnp.mean(x * x, axis=-1, keepdims=True)
    h = (x * lax.rsqrt(ms + EPS) * wnorm_ref[...]).astype(BF16)

    a = _dot(h, wtok_ref[...])
    o = 0
    glu_ref[...] = a[:, o:o + D_CONV] * _sigmoid(a[:, o + D_CONV:o + 2 * D_CONV])
    o += 2 * D_CONV
    bin_ref[...] = a[:, o:o + D_POOL]
    o += D_POOL
    z = a[:, o:o + D_CONV + D_POOL + D_ATTN]
    sz_ref[...] = _silu(z)

    f = _dot_nt(wfeat_ref[...], h)
    gates_ref[...] = _sigmoid(f[D_QKV:D_FEAT, :])
    cos = cos_ref[...]
    sin = sin_ref[...]
    kv_refs = (cmp_ref, slc_ref, win_ref)
    for hb in range(N_FEAT_BLOCKS):
        blk = f[hb * HEAD_DIM:(hb + 1) * HEAD_DIM, :]
        kv_blk = hb - N_HEADS
        is_value = kv_blk >= 0 and (kv_blk % (2 * N_KV_HEADS)) >= N_KV_HEADS
        if not is_value:
            bms = jnp.mean(blk * blk, axis=0, keepdims=True)
            y = blk * lax.rsqrt(bms + EPS) * gcol_ref[hb * HEAD_DIM:(hb + 1) * HEAD_DIM, :]
            x1 = y[0:ROT_HALF]
            x2 = y[ROT_HALF:ROT_DIM]
            blk = jnp.concatenate([x1 * cos - x2 * sin, x2 * cos + x1 * sin, y[ROT_DIM:]], axis=0)
        if kv_blk < 0:
            qt_ref[hb * HEAD_DIM:(hb + 1) * HEAD_DIM, :] = blk
        else:
            r = kv_blk % (2 * N_KV_HEADS)
            kv_refs[kv_blk // (2 * N_KV_HEADS)][r * HEAD_DIM:(r + 1) * HEAD_DIM, :] = blk


def _in_proj(x, wnorm, wtok, wfeat, gcol, cos_t, sin_t, tm, l, depth, kv_prev):
    n, t, _ = x.shape
    grid = (n, t // tm)
    tok = lambda w: pl.BlockSpec((None, tm, w), lambda i, j: (i, j, 0))
    feat = lambda w: pl.BlockSpec((None, w, tm), lambda i, j: (i, 0, j))
    kv_spec = pl.BlockSpec((None, None, D_ROWS, tm), lambda i, j: (l, i, 0, j))
    full = lambda a: pl.BlockSpec(a.shape, lambda i, j: (0,) * a.ndim)
    tok_shape = lambda w: jax.ShapeDtypeStruct((n, t, w), F32)
    kv_shape = jax.ShapeDtypeStruct((depth, n, D_ROWS, t), F32)
    n_in = 7
    kv_prev = () if kv_prev is None else tuple(kv_prev)
    return pl.pallas_call(
        _in_proj_kernel,
        grid=grid,
        in_specs=[tok(D_MODEL), full(wnorm), full(wtok), full(wfeat), full(gcol),
                  pl.BlockSpec((ROT_HALF, tm), lambda i, j: (0, j)),
                  pl.BlockSpec((ROT_HALF, tm), lambda i, j: (0, j))]
                 + [pl.BlockSpec(memory_space=pl.ANY)] * len(kv_prev),
        out_specs=[tok(D_CONV), tok(D_POOL), tok(D_CONV + D_POOL + D_ATTN), feat(GATE_ROWS),
                   feat(D_ATTN), kv_spec, kv_spec, kv_spec],
        out_shape=[tok_shape(D_CONV), tok_shape(D_POOL), tok_shape(D_CONV + D_POOL + D_ATTN),
                   jax.ShapeDtypeStruct((n, GATE_ROWS, t), F32),
                   jax.ShapeDtypeStruct((n, D_ATTN, t), F32), kv_shape, kv_shape, kv_shape],
        input_output_aliases={n_in + k: 5 + k for k in range(len(kv_prev))},
        compiler_params=pltpu.CompilerParams(dimension_semantics=("arbitrary", "arbitrary"),
                                             vmem_limit_bytes=VMEM_LIMIT),
        name="in_proj",
    )(x, wnorm, wtok, wfeat, gcol, cos_t, sin_t, *kv_prev)


def _conv_tail(acc, sz_a, bdw, lng, lnb, wpw, bpw):
    y = acc + bdw
    mu = jnp.mean(y, axis=-1, keepdims=True)
    yc = y - mu
    var = jnp.mean(yc * yc, axis=-1, keepdims=True)
    y = yc * lax.rsqrt(var + EPS) * lng + lnb
    y = _dot(_silu(y).astype(BF16), wpw) + bpw
    return y * sz_a


def _pool_tail(total, cnt, xcur, sz_b, wpool, pscale):
    d = total / cnt - xcur
    y = _dot(d.astype(BF16), wpool) * pscale
    return y * sz_b


def _pool_window_of_lane(shape):
    lane = lax.broadcasted_iota(jnp.int32, shape, len(shape) - 1)
    w = jnp.full(shape, POOL_WINDOWS[0], jnp.int32)
    for gi in range(1, len(POOL_WINDOWS)):
        w = jnp.where(lane >= gi * POOL_GROUP, POOL_WINDOWS[gi], w)
    return w


def _pool_select(sums, shape):
    lane = lax.broadcasted_iota(jnp.int32, shape, len(shape) - 1)
    total = sums[0]
    for gi in range(1, len(POOL_WINDOWS)):
        total = jnp.where(lane >= gi * POOL_GROUP, sums[gi], total)
    return total


CONV_HALO = 32
POOL_HALO = 16


MIX_ROWS = 256
SUBLANES = 8


def _tap_rows(buf, w_ref, r0, rows, off, taps, lanes):
    y = None
    for c in range(SUBLANES):
        part = None
        for k in taps:
            if (k + off) % SUBLANES != c:
                continue
            base = r0 + k + off - c
            term = w_ref[k:k + 1, lanes] * buf[base:base + rows + SUBLANES, lanes]
            part = term if part is None else part + term
        if part is not None:
            y = part[c:c + rows] if y is None else y + part[c:c + rows]
    return y


def _mix_out_prompt_kernel(x_ref, glu_ref, gprev_ref, bin_ref, bprev_ref, sz_ref, oat_ref, wdw_ref, bdw_ref, lng_ref,
                           lnb_ref, wpw_ref, bpw_ref, wpool_ref, pscale_ref, wout_ref, o_ref,
                           cbuf, pbuf, ymix, *, t):
    i = pl.program_id(1)
    keep = (i > 0).astype(F32)
    cbuf[0:CONV_HALO, :] = gprev_ref[...] * keep
    cbuf[CONV_HALO:CONV_HALO + t, :] = glu_ref[...]
    cbuf[CONV_HALO + t:CONV_HALO + t + SUBLANES, :] = jnp.zeros((SUBLANES, D_CONV), F32)
    pbuf[0:POOL_HALO, :] = bprev_ref[...] * keep
    pbuf[POOL_HALO:POOL_HALO + t, :] = bin_ref[...]
    nab = D_CONV + D_POOL
    for r0 in range(0, t, MIX_ROWS):
        rows = slice(r0, r0 + MIX_ROWS)
        sz = sz_ref[rows, :]
        acc = _tap_rows(cbuf, wdw_ref, r0, MIX_ROWS, CONV_HALO - CONV_STATE, range(CONV_WIDTH), slice(0, D_CONV))
        y_a = _conv_tail(acc, sz[:, 0:D_CONV], bdw_ref[...], lng_ref[...], lnb_ref[...], wpw_ref[...], bpw_ref[...])
        ymix[rows, 0:D_CONV] = y_a.astype(BF16)

        sums = []
        run = jnp.zeros((MIX_ROWS, D_POOL), F32)
        j = 0
        for w in POOL_WINDOWS:
            while j < w:
                run = run + pbuf[pl.ds(POOL_HALO + r0 - j, MIX_ROWS), :]
                j += 1
            sums.append(run)
        total = _pool_select(sums, (MIX_ROWS, D_POOL))
        pos = i * t + r0 + lax.broadcasted_iota(jnp.int32, (MIX_ROWS, D_POOL), 0)
        cnt = jnp.minimum(_pool_window_of_lane((MIX_ROWS, D_POOL)), pos + 1).astype(F32)
        y_b = _pool_tail(total, cnt, bin_ref[rows, :], sz[:, D_CONV:nab], wpool_ref[...], pscale_ref[...])
        ymix[rows, D_CONV:nab] = y_b.astype(BF16)
        ymix[rows, nab:nab + D_ATTN] = (oat_ref[rows, :] * sz[:, nab:nab + D_ATTN]).astype(BF16)
    o_ref[...] = x_ref[...] + _dot(ymix[...], wout_ref[...])


def _mix_out_prompt(x, glu, b_in, sz, o_attn, wdw, bdw, lng, lnb, wpw, bpw, wpool, pscale, wout, t):
    n, s, _ = glu.shape
    assert t % MIX_ROWS == 0 and s % t == 0
    grid = (n, s // t)
    cur = lambda w: pl.BlockSpec((None, t, w), lambda i, j: (i, j, 0))
    prev = lambda rows, w: pl.BlockSpec((None, rows, w), lambda i, j: (i, jnp.maximum(j * (t // rows) - 1, 0), 0))
    full = lambda a: pl.BlockSpec(a.shape, lambda i, j: (0,) * a.ndim)
    d_mix = D_CONV + D_POOL + D_ATTN
    return pl.pallas_call(
        functools.partial(_mix_out_prompt_kernel, t=t),
        grid=grid,
        in_specs=[cur(D_MODEL), cur(D_CONV), prev(CONV_HALO, D_CONV), cur(D_POOL), prev(POOL_HALO, D_POOL),
                  cur(d_mix), cur(D_ATTN),
                  full(wdw), full(bdw), full(lng), full(lnb), full(wpw), full(bpw), full(wpool),
                  full(pscale), full(wout)],
        out_specs=cur(D_MODEL),
        out_shape=jax.ShapeDtypeStruct(x.shape, F32),
        scratch_shapes=[pltpu.VMEM((CONV_HALO + t + SUBLANES, D_CONV), F32),
                        pltpu.VMEM((POOL_HALO + t, D_POOL), F32),
                        pltpu.VMEM((t, d_mix), BF16)],
        compiler_params=pltpu.CompilerParams(dimension_semantics=("arbitrary", "arbitrary"),
                                             vmem_limit_bytes=VMEM_LIMIT),
        name="mix_out_prompt",
    )(x, glu, glu, b_in, b_in, sz, o_attn, wdw, bdw, lng, lnb, wpw, bpw, wpool, pscale, wout)


def _convpool_sample_kernel(sc_ref, sp_ref, glu_ref, bin_ref, sz_ref, wdw_ref, bdw_ref, lng_ref, lnb_ref,
                            wpw_ref, bpw_ref, wpool_ref, pscale_ref, y_ref, sc_out, sp_out, *, pos):
    glu = glu_ref[...]
    acc = wdw_ref[CONV_STATE:CONV_WIDTH, :] * glu
    for k in range(CONV_STATE):
        acc = acc + wdw_ref[k:k + 1, :] * sc_ref[k]
    sz = sz_ref[...]
    y_ref[:, 0:D_CONV] = _conv_tail(acc, sz[:, 0:D_CONV], bdw_ref[...], lng_ref[...], lnb_ref[...],
                                    wpw_ref[...], bpw_ref[...])
    for k in range(CONV_STATE - 1):
        sc_out[k] = sc_ref[k + 1]
    sc_out[CONV_STATE - 1] = glu

    xcur = bin_ref[...]
    shape = xcur.shape
    sums = []
    run = xcur
    j = 1
    for w in POOL_WINDOWS:
        while j < w:
            run = run + sp_ref[POOL_STATE - j]
            j += 1
        sums.append(run)
    total = _pool_select(sums, shape)
    cnt = jnp.minimum(_pool_window_of_lane(shape), pos + 1).astype(F32)
    y_ref[:, D_CONV:D_CONV + D_POOL] = _pool_tail(total, cnt, xcur, sz[:, D_CONV:D_CONV + D_POOL],
                                                  wpool_ref[...], pscale_ref[...])
    for k in range(POOL_STATE - 1):
        sp_out[k] = sp_ref[k + 1]
    sp_out[POOL_STATE - 1] = xcur


def _convpool_sample(sc, sp, glu, b_in, sz, wdw, bdw, lng, lnb, wpw, bpw, wpool, pscale, pos):
    nb = glu.shape[0]
    return pl.pallas_call(
        functools.partial(_convpool_sample_kernel, pos=pos),
        out_shape=[jax.ShapeDtypeStruct((nb, D_CONV + D_POOL), F32),
                   jax.ShapeDtypeStruct(sc.shape, F32), jax.ShapeDtypeStruct(sp.shape, F32)],
        compiler_params=pltpu.CompilerParams(vmem_limit_bytes=VMEM_LIMIT),
        name="convpool_sample",
    )(sc, sp, glu, b_in, sz[:, 0:D_CONV + D_POOL], wdw, bdw, lng, lnb, wpw, bpw, wpool, pscale)


def _out_proj_kernel(x_ref, yab_ref, oat_ref, szc_ref, w_ref, o_ref):
    ya = yab_ref[...].astype(BF16)
    yc = (oat_ref[...] * szc_ref[...]).astype(BF16)
    nab = D_CONV + D_POOL
    o_ref[...] = x_ref[...] + _dot(ya, w_ref[0:nab, :]) + _dot(yc, w_ref[nab:nab + D_ATTN, :])


def _out_proj(x, yab, o_attn, sz, w_out, tm):
    n, t, _ = x.shape
    nab = D_CONV + D_POOL
    assert nab == D_ATTN
    grid = (n, t // tm)
    tok = lambda w, cb=0: pl.BlockSpec((None, tm, w), lambda i, j: (i, j, cb))
    return pl.pallas_call(
        _out_proj_kernel,
        grid=grid,
        in_specs=[tok(D_MODEL), tok(nab), tok(D_ATTN), tok(D_ATTN, 1),
                  pl.BlockSpec(w_out.shape, lambda i, j: (0, 0))],
        out_specs=tok(D_MODEL),
        out_shape=jax.ShapeDtypeStruct(x.shape, F32),
        compiler_params=pltpu.CompilerParams(dimension_semantics=("arbitrary", "arbitrary"),
                                             vmem_limit_bytes=VMEM_LIMIT),
        name="out_proj",
    )(x, yab, o_attn, sz, w_out)


GD = N_KV_HEADS * HEAD_DIM
CMP_FLAT = CMP_STRIDE * GD


FLAT_SPAN = CMP_STRIDE * CMP_STRIDE


def _piece_perm():
    m = np.arange(FLAT_SPAN)
    src = (m % CMP_STRIDE) * CMP_STRIDE + m // CMP_STRIDE
    return jnp.asarray((src[:, None] == np.arange(FLAT_SPAN)[None, :]).astype(np.float32), BF16)


def _fill_flat(x_t, perm, flat_ref, piece0):
    y = _dot_nt(perm, x_t).astype(BF16)
    for kv in range(2):
        for r in range(CMP_STRIDE):
            flat_ref[kv, piece0:piece0 + CMP_STRIDE, r * GD:(r + 1) * GD] = (
                y[r * CMP_STRIDE:(r + 1) * CMP_STRIDE, kv * GD:(kv + 1) * GD])


def _compress_mlp(flat, w1ab, bias, w2):
    n_pc = flat.shape[0]
    hb = _dot(flat, w1ab)
    h = hb[:, 0:GD] + jnp.concatenate([hb[1:n_pc, GD:2 * GD], jnp.zeros((1, GD), F32)], axis=0) + bias
    out = _dot(_gelu_tanh(h).astype(BF16), w2)
    row = lax.broadcasted_iota(jnp.int32, out.shape, 0)
    return jnp.where(row < n_pc - 1, out, 0.0)


def _cmp_bias_kernel(pe_ref, w1_ref, o_ref):
    for kv in range(2):
        o_ref[kv] = _dot(pe_ref[kv].astype(BF16), w1_ref[kv])


def _cmp_bias(pe_flat, w1):
    return pl.pallas_call(_cmp_bias_kernel, out_shape=jax.ShapeDtypeStruct((2, 1, HEAD_DIM), F32),
                          name="cmp_bias")(pe_flat, w1)


def _compress_prompt_kernel(x_ref, perm_ref, w1ab_ref, bias_ref, w2_ref, o_ref, flat_ref, *, s):
    for t in range(s // FLAT_SPAN):
        _fill_flat(x_ref[:, t * FLAT_SPAN:(t + 1) * FLAT_SPAN].astype(BF16), perm_ref[...], flat_ref, t * CMP_STRIDE)
    for kv in range(2):
        o_ref[:, kv * GD:(kv + 1) * GD] = _compress_mlp(flat_ref[kv], w1ab_ref[kv], bias_ref[kv], w2_ref[kv])


def _compress_prompt(cmp_t, l, w1ab, bias, w2):
    _, n, _, s = cmp_t.shape
    assert s % FLAT_SPAN == 0
    n_pc = s // CMP_STRIDE
    perm = _piece_perm()
    full = lambda a: pl.BlockSpec(a.shape, lambda i: (0,) * a.ndim)
    return pl.pallas_call(
        functools.partial(_compress_prompt_kernel, s=s),
        grid=(n,),
        in_specs=[pl.BlockSpec((None, None, D_ROWS, s), lambda i: (l, i, 0, 0)),
                  full(perm), full(w1ab), full(bias), full(w2)],
        out_specs=pl.BlockSpec((None, n_pc, 2 * GD), lambda i: (i, 0, 0)),
        out_shape=jax.ShapeDtypeStruct((n, n_pc, 2 * GD), F32),
        scratch_shapes=[pltpu.VMEM((2, n_pc, CMP_FLAT), BF16)],
        compiler_params=pltpu.CompilerParams(dimension_semantics=("arbitrary",), vmem_limit_bytes=VMEM_LIMIT),
        name="compress_prompt",
    )(cmp_t, perm, w1ab, bias, w2)


def _select_blocks(imp_t, pos_row, n_blocks):
    shape = imp_t.shape
    assert n_blocks % 8 == 0
    j = lax.broadcasted_iota(jnp.int32, shape, 0)
    cur = pos_row // SLC_BLOCK
    forced = (j == 0) | (j == cur) | (j == cur - 1)
    score = jnp.where(j <= cur, jnp.where(forced, BIG, imp_t), -BIG)
    tiles = [score[8 * v:8 * v + 8] for v in range(n_blocks // 8)]
    j8 = lax.broadcasted_iota(jnp.int32, (8, shape[1]), 0)
    ranks = [jnp.zeros((8, shape[1]), F32) for _ in tiles]
    for jp in range(n_blocks):
        sj = score[jp:jp + 1, :]
        for v, tile in enumerate(tiles):
            if jp < 8 * v:
                beats = sj >= tile
            elif jp >= 8 * v + 8:
                beats = sj > tile
            else:
                beats = (sj > tile) | ((sj == tile) & (j8 > jp - 8 * v))
            ranks[v] = ranks[v] + jnp.where(beats, 1.0, 0.0)
    rank = jnp.concatenate(ranks, axis=0)
    return (rank < float(min(N_SEL, n_blocks))).astype(F32)


def _col_softmax(s):
    m = jnp.max(s, axis=0, keepdims=True)
    e = jnp.exp(s - m)
    inv = jnp.where(m > 0.5 * NEG, 1.0 / jnp.maximum(jnp.sum(e, axis=0, keepdims=True), 1e-30), 0.0)
    return e, inv


ONES_ROWS = 16
WIN_TILE = 128
BLOCK_CODE = 64


def _with_ones(v_t):
    return jnp.concatenate([v_t, jnp.ones((ONES_ROWS, v_t.shape[1]), BF16)], axis=0)


def _normalised(acc):
    return acc[0:HEAD_DIM] * (1.0 / jnp.maximum(acc[HEAD_DIM:HEAD_DIM + 1], 1e-30))


def _mask_heads(sc, ok, tq):
    return jnp.concatenate([jnp.where(ok, sc[:, r * tq:(r + 1) * tq], NEG) for r in range(N_REP)], axis=1)


def _attn_prompt_kernel(qt_ref, kcv_ref, slc_ref, win_ref, gates_ref, ovt_ref, o_ref,
                        ks_ref, kw_ref, s_ref, acc_ref, m_ref, *, s, n_blk):
    tq = Q_BLOCK
    cols = N_REP * tq
    groups = range(N_KV_HEADS)
    qc = pl.program_id(1)
    q0 = qc * tq
    pos_row = q0 + lax.broadcasted_iota(jnp.int32, (1, tq), 1)
    n_slc = s // SLC_BLOCK
    n_cmp = kcv_ref.shape[0]
    scale = HEAD_DIM ** -0.5
    v_row0 = lambda g: (N_KV_HEADS + g) * HEAD_DIM

    @pl.when(qc == 0)
    def _():
        for g in range(N_KV_HEADS):
            for t in range(s // 128):
                cols_t = slice(t * 128, (t + 1) * 128)
                ks_ref[g, cols_t, 0:HEAD_DIM] = slc_ref[g * HEAD_DIM:(g + 1) * HEAD_DIM, cols_t].T.astype(BF16)
                block_of_key = (t * 128 + lax.broadcasted_iota(jnp.int32, (128, BLOCK_CODE), 0)) // SLC_BLOCK
                code = block_of_key == lax.broadcasted_iota(jnp.int32, (128, BLOCK_CODE), 1)
                ks_ref[g, cols_t, HEAD_DIM:HEAD_DIM + BLOCK_CODE] = jnp.where(code, 1.0, 0.0).astype(BF16)
                kw_ref[g, cols_t, :] = win_ref[g * HEAD_DIM:(g + 1) * HEAD_DIM, cols_t].T.astype(BF16)

    gates_t = gates_ref[...]
    vc_t = kcv_ref[:, GD:2 * GD].T.astype(BF16)
    q_ts = []
    for g in groups:
        q_t = jnp.concatenate([qt_ref[(g * N_REP + r) * HEAD_DIM:(g * N_REP + r + 1) * HEAD_DIM, :]
                               for r in range(N_REP)], axis=1)
        q_ts.append((q_t * scale).astype(BF16))

    c_idx = lax.broadcasted_iota(jnp.int32, (n_cmp, 1), 0)
    valid_c = (c_idx * CMP_STRIDE + (CMP_BLOCK - 1) <= pos_row) & (c_idx < n_blk)
    ovt = ovt_ref[...]
    o_cs, q_codes = [], []
    for g in groups:
        kc = kcv_ref[:, g * HEAD_DIM:(g + 1) * HEAD_DIM].astype(BF16)
        e_c, inv_c = _col_softmax(_mask_heads(_dot(kc, q_ts[g]), valid_c, tq))
        o_cs.append(_dot(vc_t[g * HEAD_DIM:(g + 1) * HEAD_DIM, :], e_c.astype(BF16)) * inv_c)
        pc_sum = e_c[:, 0:tq] * inv_c[:, 0:tq]
        for r in range(1, N_REP):
            pc_sum = pc_sum + e_c[:, r * tq:(r + 1) * tq] * inv_c[:, r * tq:(r + 1) * tq]
        pc_hi = pc_sum.astype(BF16)
        pc_lo = (pc_sum - pc_hi.astype(F32)).astype(BF16)
        imp_t = _dot(ovt, pc_hi) + _dot(ovt, pc_lo)
        sel_t = _select_blocks(imp_t, pos_row, n_slc)
        bias = jnp.concatenate([jnp.where(sel_t > 0.5, 0.0, NEG), jnp.zeros((BLOCK_CODE - n_slc, tq), F32)], axis=0)
        q_codes.append(jnp.concatenate([q_ts[g], jnp.concatenate([bias] * N_REP, axis=1).astype(BF16)], axis=0))

    n_keys = q0 + tq
    rest = n_keys % SEL_KEYS
    half_tail = (rest > 0) & (rest <= SEL_KEYS // 2)
    n_steps = n_keys // SEL_KEYS + (rest > SEL_KEYS // 2).astype(jnp.int32)
    tail_k0 = pl.multiple_of((n_keys // SEL_KEYS) * SEL_KEYS, SEL_KEYS)
    last_full = jnp.logical_not(half_tail)

    def score_chunk(k0, size, diagonal):
        key = k0 + lax.broadcasted_iota(jnp.int32, (size, 1), 0)
        for g in groups:
            sc = _dot(ks_ref[g, pl.ds(k0, size), :], q_codes[g])
            if diagonal:
                sc = _mask_heads(sc, key <= pos_row, tq)
            s_ref[g, pl.ds(k0, size), :] = sc.astype(BF16)
            m_ref[g] = jnp.maximum(m_ref[g], jnp.max(sc, axis=0, keepdims=True))

    m_ref[...] = jnp.full(m_ref.shape, NEG, F32)

    def score_step(i, carry):
        score_chunk(pl.multiple_of(i * SEL_KEYS, SEL_KEYS), SEL_KEYS, False)
        return carry

    lax.fori_loop(0, n_steps - last_full.astype(jnp.int32), score_step, 0)

    @pl.when(last_full)
    def _():
        score_chunk(pl.multiple_of((n_steps - 1) * SEL_KEYS, SEL_KEYS), SEL_KEYS, True)

    @pl.when(half_tail)
    def _():
        score_chunk(tail_k0, SEL_KEYS // 2, True)

    m_sel = [m_ref[g].astype(BF16) for g in groups]

    acc_ref[...] = jnp.zeros(acc_ref.shape, F32)

    def value_chunk(k0, size):
        for g in groups:
            p = jnp.exp(s_ref[g, pl.ds(k0, size), :] - m_sel[g])
            v_t = slc_ref[v_row0(g):v_row0(g) + HEAD_DIM, pl.ds(k0, size)].astype(BF16)
            acc_ref[g] += _dot(_with_ones(v_t), p)

    def value_step(i, carry):
        value_chunk(pl.multiple_of(i * SEL_KEYS, SEL_KEYS), SEL_KEYS)
        return carry

    lax.fori_loop(0, n_steps, value_step, 0)

    @pl.when(half_tail)
    def _():
        value_chunk(tail_k0, SEL_KEYS // 2)

    n_prev = -(-(WINDOW - 1) // WIN_TILE)
    sub_tiles = tq // WIN_TILE
    win_k0, win_ok = {}, {}
    for u in range(sub_tiles):
        pos_u = pos_row[:, u * WIN_TILE:(u + 1) * WIN_TILE]
        for b in range(n_prev + 1):
            kb = q0 // WIN_TILE + u - n_prev + b
            win_k0[u, b] = pl.multiple_of(jnp.maximum(kb, 0) * WIN_TILE, WIN_TILE)
            kp = kb * WIN_TILE + lax.broadcasted_iota(jnp.int32, (WIN_TILE, 1), 0)
            win_ok[u, b] = (kp <= pos_u) & (kp > pos_u - WINDOW) & (kp >= 0)

    out_blocks = []
    for g in groups:
        o_s = _normalised(acc_ref[g])
        o_ws = []
        for u in range(sub_tiles):
            q_u = jnp.concatenate([q_ts[g][:, r * tq + u * WIN_TILE:r * tq + (u + 1) * WIN_TILE]
                                   for r in range(N_REP)], axis=1)
            s_w = jnp.concatenate([_mask_heads(_dot(kw_ref[g, pl.ds(win_k0[u, b], WIN_TILE), :], q_u),
                                               win_ok[u, b], WIN_TILE) for b in range(n_prev + 1)], axis=0)
            e_w = jnp.exp(s_w.astype(BF16) - jnp.max(s_w, axis=0, keepdims=True).astype(BF16))
            o_w = None
            for b in range(n_prev + 1):
                v_t = win_ref[v_row0(g):v_row0(g) + HEAD_DIM, pl.ds(win_k0[u, b], WIN_TILE)].astype(BF16)
                part = _dot(_with_ones(v_t), e_w[b * WIN_TILE:(b + 1) * WIN_TILE])
                o_w = part if o_w is None else o_w + part
            o_ws.append(_normalised(o_w))
        for r in range(N_REP):
            h = g * N_REP + r
            c = slice(r * tq, (r + 1) * tq)
            o_w = jnp.concatenate([o_ws[u][:, r * WIN_TILE:(r + 1) * WIN_TILE] for u in range(sub_tiles)], axis=1)
            out_blocks.append(gates_t[h:h + 1] * o_cs[g][:, c] + gates_t[N_HEADS + h:N_HEADS + h + 1] * o_s[:, c]
                              + gates_t[2 * N_HEADS + h:2 * N_HEADS + h + 1] * o_w)
    o_ref[...] = jnp.concatenate(out_blocks, axis=0).T


def _overlap_t(n_cmp_rows, n_blk, n_slc):
    c = np.arange(n_cmp_rows)
    start = c * CMP_STRIDE
    end = start + CMP_BLOCK - 1
    j0 = np.arange(n_slc) * SLC_BLOCK
    ov = (end[None, :] >= j0[:, None]) & (start[None, :] < j0[:, None] + SLC_BLOCK) & (c[None, :] < n_blk)
    return jnp.asarray(ov.astype(np.float32), BF16)


def _attn_prompt(q_t, kcv, slc_t, win_t, gates, l):
    n, _, s = q_t.shape
    n_cmp = kcv.shape[1]
    n_blk = s // CMP_STRIDE - CMP_BLOCK // CMP_STRIDE + 1
    n_slc = s // SLC_BLOCK
    ovt = _overlap_t(n_cmp, n_blk, n_slc)
    tq = Q_BLOCK
    cols = N_REP * tq
    assert s % SEL_KEYS == 0 and (SEL_KEYS // 2) % tq == 0 and n_slc <= BLOCK_CODE
    return pl.pallas_call(
        functools.partial(_attn_prompt_kernel, s=s, n_blk=n_blk),
        grid=(n, s // tq),
        in_specs=[pl.BlockSpec((None, D_ATTN, tq), lambda i, j: (i, 0, j)),
                  pl.BlockSpec((None,) + kcv.shape[1:], lambda i, j: (i, 0, 0)),
                  pl.BlockSpec((None, None, D_ROWS, s), lambda i, j: (l, i, 0, 0)),
                  pl.BlockSpec((None, None, D_ROWS, s), lambda i, j: (l, i, 0, 0)),
                  pl.BlockSpec((None, GATE_ROWS, tq), lambda i, j: (i, 0, j)),
                  pl.BlockSpec(ovt.shape, lambda i, j: (0, 0))],
        out_specs=pl.BlockSpec((None, tq, D_ATTN), lambda i, j: (i, j, 0)),
        out_shape=jax.ShapeDtypeStruct((n, s, D_ATTN), F32),
        scratch_shapes=[pltpu.VMEM((N_KV_HEADS, s, HEAD_DIM + BLOCK_CODE), BF16),
                        pltpu.VMEM((N_KV_HEADS, s, HEAD_DIM), BF16),
                        pltpu.VMEM((N_KV_HEADS, s, cols), BF16),
                        pltpu.VMEM((N_KV_HEADS, HEAD_DIM + ONES_ROWS, cols), F32),
                        pltpu.VMEM((N_KV_HEADS, 1, cols), F32)],
        compiler_params=pltpu.CompilerParams(dimension_semantics=("arbitrary", "arbitrary"),
                                             vmem_limit_bytes=VMEM_LIMIT),
        name="attn_prompt",
    )(q_t, kcv, slc_t, win_t, gates, ovt)


PAGE = 128
SAMPLE_SEQS = 4


def _sample_geometry(past):
    length = past + 1
    padded = -(-length // SLC_BLOCK) * SLC_BLOCK
    n_pc = padded // CMP_STRIDE
    n_pc_pad = -(-n_pc // CMP_STRIDE) * CMP_STRIDE
    return dict(n_pc=n_pc, n_pc_pad=n_pc_pad, n_blk=n_pc - CMP_BLOCK // CMP_STRIDE + 1, n_slc=padded // SLC_BLOCK)


def _attn_sample_kernel(pt_ref, q_ref, new_ref, gates_ref, wincol_ref, cwin_ref, ov_ref,
                        perm_ref, w1ab_ref, bias_ref, w2_ref, ccmp_hbm, cslc_hbm, *rest, l, past, n_pages, nseq):
    o_ref, wout_ref, flat_ref, cmp_buf, slc_buf, sem = rest[-6:]
    step = pl.program_id(0)
    slot = step % 2

    def page_copies(of_step, into_slot, lookup):
        copies = []
        for u in range(nseq):
            for j in range(n_pages):
                page = pt_ref[of_step * nseq + u, j] if lookup else 0
                k = u * n_pages + j
                copies.append(pltpu.make_async_copy(ccmp_hbm.at[l, page], cmp_buf.at[into_slot, k], sem.at[into_slot, 0]))
                copies.append(pltpu.make_async_copy(cslc_hbm.at[l, page], slc_buf.at[into_slot, k], sem.at[into_slot, 1]))
        return copies

    @pl.when(step == 0)
    def _():
        for c in page_copies(0, 0, True):
            c.start()

    @pl.when(step + 1 < pl.num_programs(0))
    def _():
        for c in page_copies(step + 1, 1 - slot, True):
            c.start()

    for c in page_copies(step, slot, False):
        c.wait()
    cmp_pages = [[cmp_buf.at[slot, u * n_pages + j] for j in range(n_pages)] for u in range(nseq)]
    slc_pages = [[slc_buf.at[slot, u * n_pages + j] for j in range(n_pages)] for u in range(nseq)]
    geo = _sample_geometry(past)
    n_pc_pad, n_blk, n_slc = geo["n_pc_pad"], geo["n_blk"], geo["n_slc"]
    pos = past
    lb = cwin_ref.shape[-1]
    scale = HEAD_DIM ** -0.5
    head = lax.broadcasted_iota(jnp.int32, (N_HEADS, 1), 0)
    lane = lax.broadcasted_iota(jnp.int32, (1, GD), 1)
    lane_h = lax.broadcasted_iota(jnp.int32, (N_HEADS, GD), 1)
    own = (lane_h // HEAD_DIM) == (head // N_REP)

    def new_row(u, branch, kv):
        o = (branch * 2 + kv) * GD
        return new_ref[u, :, o:o + GD]

    def rounded(x):
        return x.astype(BF16).astype(F32)

    kcv = []
    pages_per_span = FLAT_SPAN // PAGE
    past_pc = past // CMP_STRIDE
    for u in range(nseq):
        row0 = u * n_pc_pad
        for t in range(n_pages // pages_per_span):
            x_t = jnp.concatenate([cmp_pages[u][t * pages_per_span + w][...].reshape(2 * GD, PAGE)
                                   for w in range(pages_per_span)], axis=1)
            _fill_flat(x_t.astype(BF16), perm_ref[...], flat_ref, row0 + t * CMP_STRIDE)
        for kv in range(2):
            flat_ref[kv, row0 + past_pc:row0 + n_pc_pad, :] = jnp.zeros((n_pc_pad - past_pc, CMP_FLAT), BF16)
            flat_ref[kv, row0 + past_pc:row0 + past_pc + 1, 0:GD] = new_row(u, 0, kv).astype(BF16)
    for kv in range(2):
        kcv.append(_compress_mlp(flat_ref[kv], w1ab_ref[kv], bias_ref[kv], w2_ref[kv]).astype(BF16))
    for u in range(nseq):
        _attn_sample_one(u, kcv[0][u * n_pc_pad:(u + 1) * n_pc_pad], kcv[1][u * n_pc_pad:(u + 1) * n_pc_pad],
                         q_ref, new_row, rounded, gates_ref, wincol_ref, cwin_ref, ov_ref,
                         slc_pages[u], o_ref, wout_ref, geo=geo, past=past, consts=(head, lane, lane_h, own))


def _attn_sample_one(u, kc, vc, q_ref, new_row, rounded, gates_ref, wincol_ref, cwin_ref, ov_ref,
                     slc_pages, o_ref, wout_ref, *, geo, past, consts):
    head, lane, lane_h, own = consts
    n_pc_pad, n_blk, n_slc = geo["n_pc_pad"], geo["n_blk"], geo["n_slc"]
    n_pages = len(slc_pages)
    pos = past
    lb = cwin_ref.shape[-1]
    scale = HEAD_DIM ** -0.5
    q_rows = jnp.concatenate([q_ref[u, :, h * HEAD_DIM:(h + 1) * HEAD_DIM] for h in range(N_HEADS)], axis=0)
    qx = (jnp.where(own, jnp.concatenate([q_rows] * N_KV_HEADS, axis=1), 0.0) * scale).astype(BF16)
    qx32 = qx.astype(F32)
    c_idx = lax.broadcasted_iota(jnp.int32, (1, n_pc_pad), 1)
    valid = (c_idx * CMP_STRIDE + (CMP_BLOCK - 1) <= pos) & (c_idx < n_blk)
    p_c = _masked_softmax(_dot_nt(qx, kc), valid)
    o_c = _dot(p_c.astype(BF16), vc)

    cur = pos // SLC_BLOCK
    forced = (lane == 0) | (lane == cur) | (lane == cur - 1)
    jp = lax.broadcasted_iota(jnp.int32, (GD, GD), 0)
    jj = lax.broadcasted_iota(jnp.int32, (GD, GD), 1)
    sels = []
    for g in range(N_KV_HEADS):
        pc_sum = jnp.sum(p_c[g * N_REP:(g + 1) * N_REP], axis=0, keepdims=True)
        pc_hi = pc_sum.astype(BF16)
        pc_lo = (pc_sum - pc_hi.astype(F32)).astype(BF16)
        imp = _dot(pc_hi, ov_ref[...]) + _dot(pc_lo, ov_ref[...])
        score = jnp.where(lane <= cur, jnp.where(forced, BIG, imp), -BIG)
        score = jnp.where(lane < n_slc, score, -2.0 * BIG)
        score_b = jnp.broadcast_to(score, (GD, GD))
        score_a = score_b.T
        beats = (score_a > score_b) | ((score_a == score_b) & (jp < jj))
        rank = jnp.sum(beats.astype(F32), axis=0, keepdims=True)
        sel = ((rank < float(min(N_SEL, n_slc))) & (lane < n_slc)).astype(F32)
        sels.append(jnp.broadcast_to(sel, (N_REP, GD)))
    sel_h = jnp.concatenate(sels, axis=0)

    blocks_per_tile = GD // SLC_BLOCK
    chosen = []
    for t in range(past // GD):
        tile = sel_h[:, t * blocks_per_tile:t * blocks_per_tile + 1]
        for b in range(1, blocks_per_tile):
            tile = jnp.where(lane >= b * SLC_BLOCK, sel_h[:, t * blocks_per_tile + b:t * blocks_per_tile + b + 1], tile)
        chosen.append(jnp.broadcast_to(tile, (N_HEADS, GD)))
    chosen = jnp.concatenate(chosen, axis=-1) > 0.5
    s_past = jnp.concatenate([_dot(qx, slc_pages[j][0].astype(BF16)) for j in range(n_pages)], axis=-1)
    s_past = jnp.where(chosen, s_past, NEG)
    new_ok = jnp.sum(jnp.where(lane == pos // SLC_BLOCK, sel_h, 0.0), axis=-1, keepdims=True) > 0.5
    s_new = jnp.where(new_ok, jnp.sum(qx32 * rounded(new_row(u, 1, 0)), axis=-1, keepdims=True), NEG)
    m = jnp.maximum(jnp.max(s_past, axis=-1, keepdims=True), s_new)
    p_past = jnp.where(chosen, jnp.exp(s_past - m), 0.0)
    p_new = jnp.where(new_ok, jnp.exp(s_new - m), 0.0)
    denom = jnp.maximum(jnp.sum(p_past, axis=-1, keepdims=True) + p_new, 1e-30)
    o_s = rounded(p_new) * rounded(new_row(u, 1, 1))
    for j in range(n_pages):
        o_s = o_s + _dot_nt(p_past[:, j * PAGE:(j + 1) * PAGE].astype(BF16), slc_pages[j][1].astype(BF16))
    o_s = o_s / denom

    kpos = (past - lb) + lax.broadcasted_iota(jnp.int32, (1, lb), 1)
    w_ok = (kpos <= pos) & (kpos > pos - WINDOW)
    s_w = jnp.where(w_ok, _dot(qx, cwin_ref[u, 0].astype(BF16)), NEG)
    s_wn = jnp.sum(qx32 * rounded(new_row(u, 2, 0)), axis=-1, keepdims=True)
    m = jnp.maximum(jnp.max(s_w, axis=-1, keepdims=True), s_wn)
    p_w = jnp.where(w_ok, jnp.exp(s_w - m), 0.0)
    p_wn = jnp.exp(s_wn - m)
    denom = jnp.maximum(jnp.sum(p_w, axis=-1, keepdims=True) + p_wn, 1e-30)
    o_w = (_dot_nt(p_w.astype(BF16), cwin_ref[u, 1].astype(BF16))
           + rounded(p_wn) * rounded(new_row(u, 2, 1))) / denom

    gates = gates_ref[u]
    gate_id = lax.broadcasted_iota(jnp.int32, (N_HEADS, GATE_ROWS), 1)
    gate = lambda br: jnp.sum(jnp.where(gate_id == head + br * N_HEADS, gates, 0.0), axis=-1, keepdims=True)
    o = gate(0) * o_c + gate(1) * o_s + gate(2) * o_w
    for h in range(N_HEADS):
        g = h // N_REP
        o_ref[u, :, h * HEAD_DIM:(h + 1) * HEAD_DIM] = o[h:h + 1, g * HEAD_DIM:(g + 1) * HEAD_DIM]

    for kv in range(2):
        wout_ref[u, kv, :, 0:lb - 1] = cwin_ref[u, kv, :, 1:lb]
        wout_ref[u, kv, :, lb - 1:lb] = wincol_ref[u, kv * GD:(kv + 1) * GD, :]


def _attn_sample(l, q_tok, new_tok, gates, win_col, cwin_t, ccmp_t, cslc_t, page_table, w1ab, bias, w2, win_prev):
    nb = q_tok.shape[0]
    n_pages = page_table.shape[1]
    past = n_pages * PAGE
    geo = _sample_geometry(past)
    lb = cwin_t.shape[-1]
    assert FLAT_SPAN % PAGE == 0 and n_pages % (FLAT_SPAN // PAGE) == 0
    perm = _piece_perm()
    ov = _overlap_t(geo["n_pc_pad"], geo["n_blk"], GD).T
    ov = jnp.where(jnp.arange(GD)[None, :] < geo["n_slc"], ov, 0).astype(BF16)
    assert GD % SLC_BLOCK == 0 and past % GD == 0
    nseq = SAMPLE_SEQS if nb % SAMPLE_SEQS == 0 else 1
    seq = lambda a: pl.BlockSpec((nseq,) + a.shape[1:], lambda i, pt: (i,) + (0,) * (a.ndim - 1))
    full = lambda a: pl.BlockSpec(a.shape, lambda i, pt: (0,) * a.ndim)
    operands = [page_table, q_tok, new_tok, gates, win_col, cwin_t, ov, perm, w1ab, bias, w2, ccmp_t, cslc_t]
    win_prev = () if win_prev is None else (win_prev,)
    page_buf = pltpu.VMEM((2, nseq * n_pages, 2, GD, PAGE), F32)
    grid_spec = pltpu.PrefetchScalarGridSpec(
        num_scalar_prefetch=1,
        grid=(nb // nseq,),
        in_specs=[seq(q_tok), seq(new_tok), seq(gates), seq(win_col),
                  pl.BlockSpec((None, nseq, 2, GD, lb), lambda i, pt: (l, i, 0, 0, 0)),
                  full(ov), full(perm), full(w1ab), full(bias), full(w2)]
                 + [pl.BlockSpec(memory_space=pl.ANY)] * (2 + len(win_prev)),
        out_specs=[pl.BlockSpec((nseq, 1, D_ATTN), lambda i, pt: (i, 0, 0)),
                   pl.BlockSpec((None, nseq, 2, GD, lb), lambda i, pt: (l, i, 0, 0, 0))],
        scratch_shapes=[pltpu.VMEM((2, nseq * geo["n_pc_pad"], CMP_FLAT), BF16), page_buf, page_buf,
                        pltpu.SemaphoreType.DMA((2, 2))],
    )
    return pl.pallas_call(
        functools.partial(_attn_sample_kernel, l=l, past=past, n_pages=n_pages, nseq=nseq),
        grid_spec=grid_spec,
        out_shape=[jax.ShapeDtypeStruct((nb, 1, D_ATTN), F32),
                   jax.ShapeDtypeStruct((cwin_t.shape[0], nb, 2, GD, lb), F32)],
        input_output_aliases={len(operands): 1} if win_prev else {},
        compiler_params=pltpu.CompilerParams(dimension_semantics=("arbitrary",), vmem_limit_bytes=VMEM_LIMIT),
        name="attn_sample",
    )(*operands, *win_prev)


def _prep_layer(l, w_norm, w_in, w_out, w_dw, b_dw, ln_g, ln_b, w_pw, b_pw, w_pool, pool_scale,
                g_q, g_k, cmp_pe, cmp_w1, cmp_w2):
    sizes = (D_CONV, D_CONV, D_CONV, D_POOL, D_POOL, D_ATTN, D_KV, N_BRANCH * N_HEADS, D_ATTN)
    offs = [0] + [int(v) for v in np.cumsum(sizes)]
    col = lambda k: w_in[l][:, offs[k]:offs[k + 1]]
    a_val, a_gate, z_a, b_in, z_b, q, kv, gate, z_c = (col(k) for k in range(9))
    pad = jnp.zeros((D_MODEL, GATE_ROWS - N_BRANCH * N_HEADS), F32)
    wtok = jnp.concatenate([a_val, a_gate, b_in, z_a, z_b, z_c], axis=1).astype(BF16)
    wfeat = jnp.concatenate([q, kv, gate, pad], axis=1).T.astype(BF16)
    ones = jnp.ones((N_KV_HEADS * HEAD_DIM,), F32)
    gcol = jnp.concatenate([jnp.tile(g_q[l], N_HEADS)]
                           + [piece for br in range(N_BRANCH)
                              for piece in (jnp.tile(g_k[l, br], N_KV_HEADS), ones)])[:, None]
    wpool_bd = jnp.zeros((D_POOL, D_POOL), F32)
    for gi in range(len(POOL_WINDOWS)):
        wpool_bd = wpool_bd.at[gi * POOL_GROUP:(gi + 1) * POOL_GROUP, gi * POOL_GROUP:(gi + 1) * POOL_GROUP].set(w_pool[l, gi])
    row = lambda v: v[None, :]

    def both_groups(w):
        z = jnp.zeros_like(w)
        return jnp.stack([jnp.concatenate([w, z], axis=2), jnp.concatenate([z, w], axis=2)], axis=1).reshape(-1, GD)

    half = CMP_STRIDE * HEAD_DIM
    w1 = cmp_w1[l].astype(BF16)
    w1ab = jnp.stack([jnp.concatenate(
        [both_groups(w1[kv, :half].reshape(CMP_STRIDE, HEAD_DIM, HEAD_DIM)),
         both_groups(w1[kv, half:].reshape(CMP_STRIDE, HEAD_DIM, HEAD_DIM))], axis=1) for kv in range(2)])
    w2p = jnp.stack([both_groups(cmp_w2[l, kv].astype(BF16)[None]) for kv in range(2)])
    cbias = jnp.tile(_cmp_bias(cmp_pe[l].reshape(2, 1, CMP_BLOCK * HEAD_DIM), w1), (1, 1, N_KV_HEADS))
    return dict(
        wnorm=row(w_norm[l]), wtok=wtok, wfeat=wfeat, gcol=gcol,
        wdw=w_dw[l], bdw=row(b_dw[l]), lng=row(ln_g[l]), lnb=row(ln_b[l]),
        wpw=w_pw[l].astype(BF16), bpw=row(b_pw[l]),
        wpool=wpool_bd.astype(BF16), pscale=row(pool_scale[l]),
        w1ab=w1ab, cbias=cbias, w2p=w2p,
        wout=w_out[l].astype(BF16),
    )


def _rope_tables(pos):
    inv = ROPE_THETA ** (-jnp.arange(ROT_HALF, dtype=F32) * 2.0 / ROT_DIM)
    ang = pos.astype(F32)[:, None] * inv[None, :]
    return jnp.cos(ang).T, jnp.sin(ang).T


def _prompt_layer(l, depth, x, p, cos_t, sin_t, kv_prev):
    glu, b_in, sz, gates, q_t, *kv = _in_proj(
        x, p["wnorm"], p["wtok"], p["wfeat"], p["gcol"], cos_t, sin_t, min(1024, x.shape[1]), l, depth, kv_prev)
    cmp_t, slc_t, win_t = kv
    kcv = _compress_prompt(cmp_t, l, p["w1ab"], p["cbias"], p["w2p"])
    o_attn = _attn_prompt(q_t, kcv, slc_t, win_t, gates, l)
    y = _mix_out_prompt(x, glu, b_in, sz, o_attn, p["wdw"], p["bdw"], p["lng"], p["lnb"], p["wpw"], p["bpw"],
                        p["wpool"], p["pscale"], p["wout"], t=min(512, x.shape[1]))
    return y, glu, b_in, kv


def _sample_layer(l, depth, x, p, cos_t, sin_t, sc, sp, cwin_t, ccmp_t, cslc_t, page_table, past, kv_prev, win_prev):
    nb = x.shape[0]
    glu, b_in, sz, gates, q_t, *kv = _in_proj(
        x[None], p["wnorm"], p["wtok"], p["wfeat"], p["gcol"], cos_t, sin_t, nb, l, depth, kv_prev)
    glu, b_in, sz, gates = glu[0], b_in[0], sz[0], gates[0]
    yab, sc_new, sp_new = _convpool_sample(sc, sp, glu, b_in, sz, p["wdw"], p["bdw"], p["lng"], p["lnb"],
                                           p["wpw"], p["bpw"], p["wpool"], p["pscale"], pos=past)
    new_t = jnp.concatenate([a[l, 0] for a in kv], axis=0)
    new_tok = new_t.T[:, None, :]
    q_tok = q_t[0].T[:, None, :]
    win_col = kv[2][l, 0].T[:, :, None]
    o_attn, win_next = _attn_sample(l, q_tok, new_tok, gates.T[:, None, :], win_col, cwin_t, ccmp_t, cslc_t,
                                    page_table, p["w1ab"], p["cbias"], p["w2p"], win_prev)
    y = _out_proj(x[None], yab[None], o_attn.reshape(1, nb, D_ATTN), sz[None], p["wout"], tm=nb)[0]
    return y, sc_new, sp_new, win_next, kv


def _rows_from_feat(a):
    lead = a.shape[:-2]
    a = a.reshape(lead + (2, N_KV_HEADS, HEAD_DIM, a.shape[-1]))
    nl = len(lead)
    return jnp.transpose(a, tuple(range(nl)) + (nl + 3, nl, nl + 1, nl + 2))


def kernel(x_prompt, x_sample, state_conv, state_pool, cache_win_kv, cache_cmp_kv, cache_slc_kv, page_table,
           w_norm, w_in, w_out, w_dw, b_dw, ln_g, ln_b, w_pw, b_pw, w_pool, pool_scale,
           g_q, g_k, cmp_pe, cmp_w1, cmp_w2):
    bp, s, _ = x_prompt.shape
    bs, t_new, _ = x_sample.shape
    depth = w_in.shape[0]
    assert t_new == 1 and cache_cmp_kv.shape[2] == PAGE and s % 512 == 0
    past = page_table.shape[1] * PAGE
    lb = cache_win_kv.shape[2]

    def to_feat(c):
        c = jnp.transpose(c, (0, 1, 3, 4, 5, 2))
        return c.reshape(c.shape[:3] + (GD, c.shape[-1]))
    cwin_t, ccmp_t, cslc_t = to_feat(cache_win_kv), to_feat(cache_cmp_kv), to_feat(cache_slc_kv)
    sc_all = jnp.transpose(state_conv, (0, 2, 1, 3))
    sp_all = jnp.transpose(state_pool, (0, 2, 1, 3))
    cos_p, sin_p = _rope_tables(jnp.arange(s))
    cos_s, sin_s = _rope_tables(jnp.full((bs,), past))

    xp, xs = x_prompt, x_sample[:, 0, :]
    conv_p, pool_p, conv_s, pool_s = [], [], [], []
    kv_p = kv_s = win_s = None
    for l in range(depth):
        p = _prep_layer(l, w_norm, w_in, w_out, w_dw, b_dw, ln_g, ln_b, w_pw, b_pw, w_pool, pool_scale,
                        g_q, g_k, cmp_pe, cmp_w1, cmp_w2)
        xp, glu, b_in, kv_p = _prompt_layer(l, depth, xp, p, cos_p, sin_p, kv_p)
        conv_p.append(glu[:, s - CONV_STATE:, :])
        pool_p.append(b_in[:, s - POOL_STATE:, :])
        xs, sc_new, sp_new, win_s, kv_s = _sample_layer(
            l, depth, xs, p, cos_s, sin_s, sc_all[l], sp_all[l], cwin_t, ccmp_t, cslc_t, page_table, past,
            kv_s, win_s)
        conv_s.append(jnp.transpose(sc_new, (1, 0, 2)))
        pool_s.append(jnp.transpose(sp_new, (1, 0, 2)))
    cmp_p, slc_p, win_p = kv_p
    cmp_s, slc_s, _ = kv_s
    new_rows = lambda a: _rows_from_feat(a[:, 0])[:, :, None]
    return (xp, xs[:, None, :], jnp.stack(conv_p), jnp.stack(pool_p),
            _rows_from_feat(win_p[:, :, :, s - min(WINDOW, s):]), _rows_from_feat(cmp_p), _rows_from_feat(slc_p),
            jnp.stack(conv_s), jnp.stack(pool_s), _rows_from_feat(win_s.reshape(depth, bs, D_ROWS, lb)),
            new_rows(cmp_s), new_rows(slc_s))
```

```python
import functools

import numpy as np
import jax
import jax.numpy as jnp
from jax import lax
from jax.experimental import pallas as pl
from jax.experimental.pallas import tpu as pltpu

D_MODEL = 1024
D_CONV = 256
D_POOL = 256
N_HEADS = 8
N_KV_HEADS = 2
HEAD_DIM = 64
D_ATTN = N_HEADS * HEAD_DIM
CONV_WIDTH = 31
CONV_STATE = CONV_WIDTH - 1
POOL_WINDOWS = (2, 4, 8, 16)
POOL_GROUP = D_POOL // len(POOL_WINDOWS)
POOL_STATE = max(POOL_WINDOWS) - 1
CMP_BLOCK = 32
CMP_STRIDE = 16
SLC_BLOCK = 64
N_SEL = 8
WINDOW = 256
Q_BLOCK = 256
ROT_DIM = HEAD_DIM // 4
ROT_HALF = ROT_DIM // 2
ROPE_THETA = 500000.0
N_BRANCH = 3
D_KV = N_BRANCH * 2 * N_KV_HEADS * HEAD_DIM
D_ROWS = 2 * N_KV_HEADS * HEAD_DIM
N_REP = N_HEADS // N_KV_HEADS
EPS = 1e-6
NEG = -1e30
BIG = 1e4

GATE_ROWS = 32
D_TOK = 3 * D_CONV + 2 * D_POOL + D_ATTN
D_QKV = D_ATTN + D_KV
D_FEAT = D_QKV + GATE_ROWS
N_FEAT_BLOCKS = D_QKV // HEAD_DIM
SEL_KEYS = 512
IN_PROJ_ROWS = 1024
MIX_OUT_ROWS = 512
VMEM_V7X = 64 * 1024 * 1024
VMEM_LIMIT = VMEM_V7X * 7 // 8

F32 = jnp.float32
BF16 = jnp.bfloat16
NT_DIMS = (((1,), (1,)), ((), ()))


def _sigmoid(x):
    return 1.0 / (1.0 + jnp.exp(-x))


def _silu(x):
    return x * _sigmoid(x)


def _gelu_tanh(x):
    return 0.5 * x * (1.0 + jnp.tanh(np.sqrt(2.0 / np.pi).astype(np.float32) * (x + 0.044715 * (x * x * x))))


def _dot(a, b):
    return jnp.dot(a, b, preferred_element_type=F32)


def _dot_nt(a, b):
    return lax.dot_general(a, b, NT_DIMS, preferred_element_type=F32)


def _masked_softmax(s, mask):
    s = jnp.where(mask, s, NEG)
    m = jnp.max(s, axis=-1, keepdims=True)
    p = jnp.where(mask, jnp.exp(s - m), 0.0)
    return p / jnp.maximum(jnp.sum(p, axis=-1, keepdims=True), 1e-30)


def _in_proj_kernel(x_ref, wnorm_ref, wtok_ref, wfeat_ref, gcol_ref, cos_ref, sin_ref, *rest):
    glu_ref, bin_ref, sz_ref, gates_ref, qt_ref, cmp_ref, slc_ref, win_ref = rest[-8:]
    x = x_ref[...]
    ms = jnp.mean(x * x, axis=-1, keepdims=True)
    h = (x * lax.rsqrt(ms + EPS) * wnorm_ref[...]).astype(BF16)

    a = _dot(h, wtok_ref[...])
    o = 0
    glu_ref[...] = a[:, o:o + D_CONV] * _sigmoid(a[:, o + D_CONV:o + 2 * D_CONV])
    o += 2 * D_CONV
    bin_ref[...] = a[:, o:o + D_POOL]
    o += D_POOL
    z = a[:, o:o + D_CONV + D_POOL + D_ATTN]
    sz_ref[...] = _silu(z)

    f = _dot_nt(wfeat_ref[...], h)
    gates_ref[...] = _sigmoid(f[D_QKV:D_FEAT, :])
    cos = cos_ref[...]
    sin = sin_ref[...]
    kv_refs = (cmp_ref, slc_ref, win_ref)
    for hb in range(N_FEAT_BLOCKS):
        blk = f[hb * HEAD_DIM:(hb + 1) * HEAD_DIM, :]
        kv_blk = hb - N_HEADS
        is_value = kv_blk >= 0 and (kv_blk % (2 * N_KV_HEADS)) >= N_KV_HEADS
        if not is_value:
            bms = jnp.mean(blk * blk, axis=0, keepdims=True)
            y = blk * lax.rsqrt(bms + EPS) * gcol_ref[hb * HEAD_DIM:(hb + 1) * HEAD_DIM, :]
            x1 = y[0:ROT_HALF]
            x2 = y[ROT_HALF:ROT_DIM]
            blk = jnp.concatenate([x1 * cos - x2 * sin, x2 * cos + x1 * sin, y[ROT_DIM:]], axis=0)
        if kv_blk < 0:
            qt_ref[hb * HEAD_DIM:(hb + 1) * HEAD_DIM, :] = blk
        else:
            r = kv_blk % (2 * N_KV_HEADS)
            kv_refs[kv_blk // (2 * N_KV_HEADS)][r * HEAD_DIM:(r + 1) * HEAD_DIM, :] = blk


def _in_proj(x, wnorm, wtok, wfeat, gcol, cos_t, sin_t, tm, l, depth, kv_prev):
    n, t, _ = x.shape
    assert t % tm == 0
    grid = (n, t // tm)
    tok = lambda w: pl.BlockSpec((None, tm, w), lambda i, j: (i, j, 0))
    feat = lambda w: pl.BlockSpec((None, w, tm), lambda i, j: (i, 0, j))
    kv_spec = pl.BlockSpec((None, None, D_ROWS, tm), lambda i, j: (l, i, 0, j))
    full = lambda a: pl.BlockSpec(a.shape, lambda i, j: (0,) * a.ndim)
    tok_shape = lambda w: jax.ShapeDtypeStruct((n, t, w), F32)
    kv_shape = jax.ShapeDtypeStruct((depth, n, D_ROWS, t), F32)
    n_in = 7
    kv_prev = () if kv_prev is None else tuple(kv_prev)
    return pl.pallas_call(
        _in_proj_kernel,
        grid=grid,
        in_specs=[tok(D_MODEL), full(wnorm), full(wtok), full(wfeat), full(gcol),
                  pl.BlockSpec((ROT_HALF, tm), lambda i, j: (0, j)),
                  pl.BlockSpec((ROT_HALF, tm), lambda i, j: (0, j))]
                 + [pl.BlockSpec(memory_space=pl.ANY)] * len(kv_prev),
        out_specs=[tok(D_CONV), tok(D_POOL), tok(D_CONV + D_POOL + D_ATTN), feat(GATE_ROWS),
                   feat(D_ATTN), kv_spec, kv_spec, kv_spec],
        out_shape=[tok_shape(D_CONV), tok_shape(D_POOL), tok_shape(D_CONV + D_POOL + D_ATTN),
                   jax.ShapeDtypeStruct((n, GATE_ROWS, t), F32),
                   jax.ShapeDtypeStruct((n, D_ATTN, t), F32), kv_shape, kv_shape, kv_shape],
        input_output_aliases={n_in + k: 5 + k for k in range(len(kv_prev))},
        compiler_params=pltpu.CompilerParams(dimension_semantics=("arbitrary", "arbitrary"),
                                             vmem_limit_bytes=VMEM_LIMIT),
        name="in_proj",
    )(x, wnorm, wtok, wfeat, gcol, cos_t, sin_t, *kv_prev)


def _conv_tail(acc, sz_a, bdw, lng, lnb, wpw, bpw):
    y = acc + bdw
    mu = jnp.mean(y, axis=-1, keepdims=True)
    yc = y - mu
    var = jnp.mean(yc * yc, axis=-1, keepdims=True)
    y = yc * lax.rsqrt(var + EPS) * lng + lnb
    y = _dot(_silu(y).astype(BF16), wpw) + bpw
    return y * sz_a


def _pool_tail(total, cnt, xcur, sz_b, wpool, pscale):
    d = total / cnt - xcur
    y = _dot(d.astype(BF16), wpool) * pscale
    return y * sz_b


def _pool_window_of_lane(shape):
    lane = lax.broadcasted_iota(jnp.int32, shape, len(shape) - 1)
    w = jnp.full(shape, POOL_WINDOWS[0], jnp.int32)
    for gi in range(1, len(POOL_WINDOWS)):
        w = jnp.where(lane >= gi * POOL_GROUP, POOL_WINDOWS[gi], w)
    return w


def _pool_select(sums, shape):
    lane = lax.broadcasted_iota(jnp.int32, shape, len(shape) - 1)
    total = sums[0]
    for gi in range(1, len(POOL_WINDOWS)):
        total = jnp.where(lane >= gi * POOL_GROUP, sums[gi], total)
    return total


CONV_HALO = 32
POOL_HALO = 16


MIX_ROWS = 256
SUBLANES = 8


def _tap_rows(buf, w_ref, r0, rows, off, taps, lanes):
    y = None
    for c in range(SUBLANES):
        part = None
        for k in taps:
            if (k + off) % SUBLANES != c:
                continue
            base = r0 + k + off - c
            term = w_ref[k:k + 1, lanes] * buf[base:base + rows + SUBLANES, lanes]
            part = term if part is None else part + term
        if part is not None:
            y = part[c:c + rows] if y is None else y + part[c:c + rows]
    return y


def _mix_out_prompt_kernel(x_ref, glu_ref, gprev_ref, bin_ref, bprev_ref, sz_ref, oat_ref, wdw_ref, bdw_ref, lng_ref,
                           lnb_ref, wpw_ref, bpw_ref, wpool_ref, pscale_ref, wout_ref, o_ref,
                           cbuf, pbuf, ymix, *, t):
    i = pl.program_id(1)
    keep = (i > 0).astype(F32)
    cbuf[0:CONV_HALO, :] = gprev_ref[...] * keep
    cbuf[CONV_HALO:CONV_HALO + t, :] = glu_ref[...]
    cbuf[CONV_HALO + t:CONV_HALO + t + SUBLANES, :] = jnp.zeros((SUBLANES, D_CONV), F32)
    pbuf[0:POOL_HALO, :] = bprev_ref[...] * keep
    pbuf[POOL_HALO:POOL_HALO + t, :] = bin_ref[...]
    nab = D_CONV + D_POOL
    for r0 in range(0, t, MIX_ROWS):
        rows = slice(r0, r0 + MIX_ROWS)
        sz = sz_ref[rows, :]
        acc = _tap_rows(cbuf, wdw_ref, r0, MIX_ROWS, CONV_HALO - CONV_STATE, range(CONV_WIDTH), slice(0, D_CONV))
        y_a = _conv_tail(acc, sz[:, 0:D_CONV], bdw_ref[...], lng_ref[...], lnb_ref[...], wpw_ref[...], bpw_ref[...])
        ymix[rows, 0:D_CONV] = y_a.astype(BF16)

        sums = []
        run = jnp.zeros((MIX_ROWS, D_POOL), F32)
        j = 0
        for w in POOL_WINDOWS:
            while j < w:
                run = run + pbuf[pl.ds(POOL_HALO + r0 - j, MIX_ROWS), :]
                j += 1
            sums.append(run)
        total = _pool_select(sums, (MIX_ROWS, D_POOL))
        pos = i * t + r0 + lax.broadcasted_iota(jnp.int32, (MIX_ROWS, D_POOL), 0)
        cnt = jnp.minimum(_pool_window_of_lane((MIX_ROWS, D_POOL)), pos + 1).astype(F32)
        y_b = _pool_tail(total, cnt, bin_ref[rows, :], sz[:, D_CONV:nab], wpool_ref[...], pscale_ref[...])
        ymix[rows, D_CONV:nab] = y_b.astype(BF16)
        ymix[rows, nab:nab + D_ATTN] = (oat_ref[rows, :] * sz[:, nab:nab + D_ATTN]).astype(BF16)
    o_ref[...] = x_ref[...] + _dot(ymix[...], wout_ref[...])


def _mix_out_prompt(x, glu, b_in, sz, o_attn, wdw, bdw, lng, lnb, wpw, bpw, wpool, pscale, wout, t):
    n, s, _ = glu.shape
    assert t % MIX_ROWS == 0 and s % t == 0
    grid = (n, s // t)
    cur = lambda w: pl.BlockSpec((None, t, w), lambda i, j: (i, j, 0))
    prev = lambda rows, w: pl.BlockSpec((None, rows, w), lambda i, j: (i, jnp.maximum(j * (t // rows) - 1, 0), 0))
    full = lambda a: pl.BlockSpec(a.shape, lambda i, j: (0,) * a.ndim)
    d_mix = D_CONV + D_POOL + D_ATTN
    return pl.pallas_call(
        functools.partial(_mix_out_prompt_kernel, t=t),
        grid=grid,
        in_specs=[cur(D_MODEL), cur(D_CONV), prev(CONV_HALO, D_CONV), cur(D_POOL), prev(POOL_HALO, D_POOL),
                  cur(d_mix), cur(D_ATTN),
                  full(wdw), full(bdw), full(lng), full(lnb), full(wpw), full(bpw), full(wpool),
                  full(pscale), full(wout)],
        out_specs=cur(D_MODEL),
        out_shape=jax.ShapeDtypeStruct(x.shape, F32),
        scratch_shapes=[pltpu.VMEM((CONV_HALO + t + SUBLANES, D_CONV), F32),
                        pltpu.VMEM((POOL_HALO + t, D_POOL), F32),
                        pltpu.VMEM((t, d_mix), BF16)],
        compiler_params=pltpu.CompilerParams(dimension_semantics=("arbitrary", "arbitrary"),
                                             vmem_limit_bytes=VMEM_LIMIT),
        name="mix_out_prompt",
    )(x, glu, glu, b_in, b_in, sz, o_attn, wdw, bdw, lng, lnb, wpw, bpw, wpool, pscale, wout)


def _convpool_sample_kernel(sc_ref, sp_ref, glu_ref, bin_ref, sz_ref, wdw_ref, bdw_ref, lng_ref, lnb_ref,
                            wpw_ref, bpw_ref, wpool_ref, pscale_ref, y_ref, sc_out, sp_out, *, pos):
    glu = glu_ref[...]
    acc = wdw_ref[CONV_STATE:CONV_WIDTH, :] * glu
    for k in range(CONV_STATE):
        acc = acc + wdw_ref[k:k + 1, :] * sc_ref[k]
    sz = sz_ref[...]
    y_ref[:, 0:D_CONV] = _conv_tail(acc, sz[:, 0:D_CONV], bdw_ref[...], lng_ref[...], lnb_ref[...],
                                    wpw_ref[...], bpw_ref[...])
    for k in range(CONV_STATE - 1):
        sc_out[k] = sc_ref[k + 1]
    sc_out[CONV_STATE - 1] = glu

    xcur = bin_ref[...]
    shape = xcur.shape
    sums = []
    run = xcur
    j = 1
    for w in POOL_WINDOWS:
        while j < w:
            run = run + sp_ref[POOL_STATE - j]
            j += 1
        sums.append(run)
    total = _pool_select(sums, shape)
    cnt = jnp.minimum(_pool_window_of_lane(shape), pos + 1).astype(F32)
    y_ref[:, D_CONV:D_CONV + D_POOL] = _pool_tail(total, cnt, xcur, sz[:, D_CONV:D_CONV + D_POOL],
                                                  wpool_ref[...], pscale_ref[...])
    for k in range(POOL_STATE - 1):
        sp_out[k] = sp_ref[k + 1]
    sp_out[POOL_STATE - 1] = xcur


def _convpool_sample(sc, sp, glu, b_in, sz, wdw, bdw, lng, lnb, wpw, bpw, wpool, pscale, pos):
    nb = glu.shape[0]
    return pl.pallas_call(
        functools.partial(_convpool_sample_kernel, pos=pos),
        out_shape=[jax.ShapeDtypeStruct((nb, D_CONV + D_POOL), F32),
                   jax.ShapeDtypeStruct(sc.shape, F32), jax.ShapeDtypeStruct(sp.shape, F32)],
        compiler_params=pltpu.CompilerParams(vmem_limit_bytes=VMEM_LIMIT),
        name="convpool_sample",
    )(sc, sp, glu, b_in, sz[:, 0:D_CONV + D_POOL], wdw, bdw, lng, lnb, wpw, bpw, wpool, pscale)


def _out_proj_kernel(x_ref, yab_ref, oat_ref, szc_ref, w_ref, o_ref):
    ya = yab_ref[...].astype(BF16)
    yc = (oat_ref[...] * szc_ref[...]).astype(BF16)
    nab = D_CONV + D_POOL
    o_ref[...] = x_ref[...] + _dot(ya, w_ref[0:nab, :]) + _dot(yc, w_ref[nab:nab + D_ATTN, :])


def _out_proj(x, yab, o_attn, sz, w_out, tm):
    n, t, _ = x.shape
    nab = D_CONV + D_POOL
    assert nab == D_ATTN
    grid = (n, t // tm)
    tok = lambda w, cb=0: pl.BlockSpec((None, tm, w), lambda i, j: (i, j, cb))
    return pl.pallas_call(
        _out_proj_kernel,
        grid=grid,
        in_specs=[tok(D_MODEL), tok(nab), tok(D_ATTN), tok(D_ATTN, 1),
                  pl.BlockSpec(w_out.shape, lambda i, j: (0, 0))],
        out_specs=tok(D_MODEL),
        out_shape=jax.ShapeDtypeStruct(x.shape, F32),
        compiler_params=pltpu.CompilerParams(dimension_semantics=("arbitrary", "arbitrary"),
                                             vmem_limit_bytes=VMEM_LIMIT),
        name="out_proj",
    )(x, yab, o_attn, sz, w_out)


GD = N_KV_HEADS * HEAD_DIM
CMP_FLAT = CMP_STRIDE * GD


FLAT_SPAN = CMP_STRIDE * CMP_STRIDE


def _piece_perm():
    m = np.arange(FLAT_SPAN)
    src = (m % CMP_STRIDE) * CMP_STRIDE + m // CMP_STRIDE
    return jnp.asarray((src[:, None] == np.arange(FLAT_SPAN)[None, :]).astype(np.float32), BF16)


def _fill_flat(x_t, perm, flat_ref, piece0):
    y = _dot_nt(perm, x_t).astype(BF16)
    for kv in range(2):
        for r in range(CMP_STRIDE):
            flat_ref[kv, piece0:piece0 + CMP_STRIDE, r * GD:(r + 1) * GD] = (
                y[r * CMP_STRIDE:(r + 1) * CMP_STRIDE, kv * GD:(kv + 1) * GD])


def _compress_mlp(flat, w1ab, bias, w2):
    n_pc = flat.shape[0]
    hb = _dot(flat, w1ab)
    h = hb[:, 0:GD] + jnp.concatenate([hb[1:n_pc, GD:2 * GD], jnp.zeros((1, GD), F32)], axis=0) + bias
    out = _dot(_gelu_tanh(h).astype(BF16), w2)
    row = lax.broadcasted_iota(jnp.int32, out.shape, 0)
    return jnp.where(row < n_pc - 1, out, 0.0)


def _cmp_bias_kernel(pe_ref, w1_ref, o_ref):
    for kv in range(2):
        o_ref[kv] = _dot(pe_ref[kv].astype(BF16), w1_ref[kv])


def _cmp_bias(pe_flat, w1):
    return pl.pallas_call(_cmp_bias_kernel, out_shape=jax.ShapeDtypeStruct((2, 1, HEAD_DIM), F32),
                          name="cmp_bias")(pe_flat, w1)


def _compress_prompt_kernel(x_ref, perm_ref, w1ab_ref, bias_ref, w2_ref, o_ref, flat_ref, *, s):
    for t in range(s // FLAT_SPAN):
        _fill_flat(x_ref[:, t * FLAT_SPAN:(t + 1) * FLAT_SPAN].astype(BF16), perm_ref[...], flat_ref, t * CMP_STRIDE)
    for kv in range(2):
        o_ref[:, kv * GD:(kv + 1) * GD] = _compress_mlp(flat_ref[kv], w1ab_ref[kv], bias_ref[kv], w2_ref[kv])


def _compress_prompt(cmp_t, l, w1ab, bias, w2):
    _, n, _, s = cmp_t.shape
    assert s % FLAT_SPAN == 0
    n_pc = s // CMP_STRIDE
    perm = _piece_perm()
    full = lambda a: pl.BlockSpec(a.shape, lambda i: (0,) * a.ndim)
    return pl.pallas_call(
        functools.partial(_compress_prompt_kernel, s=s),
        grid=(n,),
        in_specs=[pl.BlockSpec((None, None, D_ROWS, s), lambda i: (l, i, 0, 0)),
                  full(perm), full(w1ab), full(bias), full(w2)],
        out_specs=pl.BlockSpec((None, n_pc, 2 * GD), lambda i: (i, 0, 0)),
        out_shape=jax.ShapeDtypeStruct((n, n_pc, 2 * GD), F32),
        scratch_shapes=[pltpu.VMEM((2, n_pc, CMP_FLAT), BF16)],
        compiler_params=pltpu.CompilerParams(dimension_semantics=("arbitrary",), vmem_limit_bytes=VMEM_LIMIT),
        name="compress_prompt",
    )(cmp_t, perm, w1ab, bias, w2)


def _select_blocks(imp_t, pos_row, n_blocks):
    shape = imp_t.shape
    assert n_blocks % 8 == 0
    j = lax.broadcasted_iota(jnp.int32, shape, 0)
    cur = pos_row // SLC_BLOCK
    forced = (j == 0) | (j == cur) | (j == cur - 1)
    score = jnp.where(j <= cur, jnp.where(forced, BIG, imp_t), -BIG)
    tiles = [score[8 * v:8 * v + 8] for v in range(n_blocks // 8)]
    j8 = lax.broadcasted_iota(jnp.int32, (8, shape[1]), 0)
    ranks = [jnp.zeros((8, shape[1]), F32) for _ in tiles]
    for jp in range(n_blocks):
        sj = score[jp:jp + 1, :]
        for v, tile in enumerate(tiles):
            if jp < 8 * v:
                beats = sj >= tile
            elif jp >= 8 * v + 8:
                beats = sj > tile
            else:
                beats = (sj > tile) | ((sj == tile) & (j8 > jp - 8 * v))
            ranks[v] = ranks[v] + jnp.where(beats, 1.0, 0.0)
    rank = jnp.concatenate(ranks, axis=0)
    return (rank < float(min(N_SEL, n_blocks))).astype(F32)


def _col_softmax(s):
    m = jnp.max(s, axis=0, keepdims=True)
    e = jnp.exp(s - m)
    inv = jnp.where(m > 0.5 * NEG, 1.0 / jnp.maximum(jnp.sum(e, axis=0, keepdims=True), 1e-30), 0.0)
    return e, inv


ONES_ROWS = 16
WIN_TILE = 128
BLOCK_CODE = 64


def _with_ones(v_t):
    return jnp.concatenate([v_t, jnp.ones((ONES_ROWS, v_t.shape[1]), BF16)], axis=0)


def _normalised(acc):
    return acc[0:HEAD_DIM] * (1.0 / jnp.maximum(acc[HEAD_DIM:HEAD_DIM + 1], 1e-30))


def _mask_heads(sc, ok, tq):
    return jnp.concatenate([jnp.where(ok, sc[:, r * tq:(r + 1) * tq], NEG) for r in range(N_REP)], axis=1)


def _attn_prompt_kernel(qt_ref, kcv_ref, slc_ref, win_ref, gates_ref, ovt_ref, o_ref,
                        ks_ref, kw_ref, s_ref, acc_ref, m_ref, *, s, n_blk):
    tq = Q_BLOCK
    cols = N_REP * tq
    groups = range(N_KV_HEADS)
    qc = pl.program_id(1)
    q0 = qc * tq
    pos_row = q0 + lax.broadcasted_iota(jnp.int32, (1, tq), 1)
    n_slc = s // SLC_BLOCK
    n_cmp = kcv_ref.shape[0]
    scale = HEAD_DIM ** -0.5
    v_row0 = lambda g: (N_KV_HEADS + g) * HEAD_DIM

    @pl.when(qc == 0)
    def _():
        for g in range(N_KV_HEADS):
            for t in range(s // 128):
                cols_t = slice(t * 128, (t + 1) * 128)
                ks_ref[g, cols_t, 0:HEAD_DIM] = slc_ref[g * HEAD_DIM:(g + 1) * HEAD_DIM, cols_t].T.astype(BF16)
                block_of_key = (t * 128 + lax.broadcasted_iota(jnp.int32, (128, BLOCK_CODE), 0)) // SLC_BLOCK
                code = block_of_key == lax.broadcasted_iota(jnp.int32, (128, BLOCK_CODE), 1)
                ks_ref[g, cols_t, HEAD_DIM:HEAD_DIM + BLOCK_CODE] = jnp.where(code, 1.0, 0.0).astype(BF16)
                kw_ref[g, cols_t, :] = win_ref[g * HEAD_DIM:(g + 1) * HEAD_DIM, cols_t].T.astype(BF16)

    gates_t = gates_ref[...]
    vc_t = kcv_ref[:, GD:2 * GD].T.astype(BF16)
    q_ts = []
    for g in groups:
        q_t = jnp.concatenate([qt_ref[(g * N_REP + r) * HEAD_DIM:(g * N_REP + r + 1) * HEAD_DIM, :]
                               for r in range(N_REP)], axis=1)
        q_ts.append((q_t * scale).astype(BF16))

    c_idx = lax.broadcasted_iota(jnp.int32, (n_cmp, 1), 0)
    valid_c = (c_idx * CMP_STRIDE + (CMP_BLOCK - 1) <= pos_row) & (c_idx < n_blk)
    ovt = ovt_ref[...]
    o_cs, q_codes = [], []
    for g in groups:
        kc = kcv_ref[:, g * HEAD_DIM:(g + 1) * HEAD_DIM].astype(BF16)
        e_c, inv_c = _col_softmax(_mask_heads(_dot(kc, q_ts[g]), valid_c, tq))
        o_cs.append(_dot(vc_t[g * HEAD_DIM:(g + 1) * HEAD_DIM, :], e_c.astype(BF16)) * inv_c)
        pc_sum = e_c[:, 0:tq] * inv_c[:, 0:tq]
        for r in range(1, N_REP):
            pc_sum = pc_sum + e_c[:, r * tq:(r + 1) * tq] * inv_c[:, r * tq:(r + 1) * tq]
        pc_hi = pc_sum.astype(BF16)
        pc_lo = (pc_sum - pc_hi.astype(F32)).astype(BF16)
        imp_t = _dot(ovt, pc_hi) + _dot(ovt, pc_lo)
        sel_t = _select_blocks(imp_t, pos_row, n_slc)
        bias = jnp.concatenate([jnp.where(sel_t > 0.5, 0.0, NEG), jnp.zeros((BLOCK_CODE - n_slc, tq), F32)], axis=0)
        q_codes.append(jnp.concatenate([q_ts[g], jnp.concatenate([bias] * N_REP, axis=1).astype(BF16)], axis=0))

    n_keys = q0 + tq
    rest = n_keys % SEL_KEYS
    half_tail = (rest > 0) & (rest <= SEL_KEYS // 2)
    n_steps = n_keys // SEL_KEYS + (rest > SEL_KEYS // 2).astype(jnp.int32)
    tail_k0 = pl.multiple_of((n_keys // SEL_KEYS) * SEL_KEYS, SEL_KEYS)
    last_full = jnp.logical_not(half_tail)

    def score_chunk(k0, size, diagonal):
        key = k0 + lax.broadcasted_iota(jnp.int32, (size, 1), 0)
        for g in groups:
            sc = _dot(ks_ref[g, pl.ds(k0, size), :], q_codes[g])
            if diagonal:
                sc = _mask_heads(sc, key <= pos_row, tq)
            s_ref[g, pl.ds(k0, size), :] = sc.astype(BF16)
            m_ref[g] = jnp.maximum(m_ref[g], jnp.max(sc, axis=0, keepdims=True))

    m_ref[...] = jnp.full(m_ref.shape, NEG, F32)

    def score_step(i, carry):
        score_chunk(pl.multiple_of(i * SEL_KEYS, SEL_KEYS), SEL_KEYS, False)
        return carry

    lax.fori_loop(0, n_steps - last_full.astype(jnp.int32), score_step, 0)

    @pl.when(last_full)
    def _():
        score_chunk(pl.multiple_of((n_steps - 1) * SEL_KEYS, SEL_KEYS), SEL_KEYS, True)

    @pl.when(half_tail)
    def _():
        score_chunk(tail_k0, SEL_KEYS // 2, True)

    m_sel = [m_ref[g].astype(BF16) for g in groups]

    acc_ref[...] = jnp.zeros(acc_ref.shape, F32)

    def value_chunk(k0, size):
        for g in groups:
            p = jnp.exp(s_ref[g, pl.ds(k0, size), :] - m_sel[g])
            v_t = slc_ref[v_row0(g):v_row0(g) + HEAD_DIM, pl.ds(k0, size)].astype(BF16)
            acc_ref[g] += _dot(_with_ones(v_t), p)

    def value_step(i, carry):
        value_chunk(pl.multiple_of(i * SEL_KEYS, SEL_KEYS), SEL_KEYS)
        return carry

    lax.fori_loop(0, n_steps, value_step, 0)

    @pl.when(half_tail)
    def _():
        value_chunk(tail_k0, SEL_KEYS // 2)

    n_prev = -(-(WINDOW - 1) // WIN_TILE)
    sub_tiles = tq // WIN_TILE
    win_k0, win_ok = {}, {}
    for u in range(sub_tiles):
        pos_u = pos_row[:, u * WIN_TILE:(u + 1) * WIN_TILE]
        for b in range(n_prev + 1):
            kb = q0 // WIN_TILE + u - n_prev + b
            win_k0[u, b] = pl.multiple_of(jnp.maximum(kb, 0) * WIN_TILE, WIN_TILE)
            kp = kb * WIN_TILE + lax.broadcasted_iota(jnp.int32, (WIN_TILE, 1), 0)
            win_ok[u, b] = (kp <= pos_u) & (kp > pos_u - WINDOW) & (kp >= 0)

    out_blocks = []
    for g in groups:
        o_s = _normalised(acc_ref[g])
        o_ws = []
        for u in range(sub_tiles):
            q_u = jnp.concatenate([q_ts[g][:, r * tq + u * WIN_TILE:r * tq + (u + 1) * WIN_TILE]
                                   for r in range(N_REP)], axis=1)
            s_w = jnp.concatenate([_mask_heads(_dot(kw_ref[g, pl.ds(win_k0[u, b], WIN_TILE), :], q_u),
                                               win_ok[u, b], WIN_TILE) for b in range(n_prev + 1)], axis=0)
            e_w = jnp.exp(s_w.astype(BF16) - jnp.max(s_w, axis=0, keepdims=True).astype(BF16))
            o_w = None
            for b in range(n_prev + 1):
                v_t = win_ref[v_row0(g):v_row0(g) + HEAD_DIM, pl.ds(win_k0[u, b], WIN_TILE)].astype(BF16)
                part = _dot(_with_ones(v_t), e_w[b * WIN_TILE:(b + 1) * WIN_TILE])
                o_w = part if o_w is None else o_w + part
            o_ws.append(_normalised(o_w))
        for r in range(N_REP):
            h = g * N_REP + r
            c = slice(r * tq, (r + 1) * tq)
            o_w = jnp.concatenate([o_ws[u][:, r * WIN_TILE:(r + 1) * WIN_TILE] for u in range(sub_tiles)], axis=1)
            out_blocks.append(gates_t[h:h + 1] * o_cs[g][:, c] + gates_t[N_HEADS + h:N_HEADS + h + 1] * o_s[:, c]
                              + gates_t[2 * N_HEADS + h:2 * N_HEADS + h + 1] * o_w)
    o_ref[...] = jnp.concatenate(out_blocks, axis=0).T


def _overlap_t(n_cmp_rows, n_blk, n_slc):
    c = np.arange(n_cmp_rows)
    start = c * CMP_STRIDE
    end = start + CMP_BLOCK - 1
    j0 = np.arange(n_slc) * SLC_BLOCK
    ov = (end[None, :] >= j0[:, None]) & (start[None, :] < j0[:, None] + SLC_BLOCK) & (c[None, :] < n_blk)
    return jnp.asarray(ov.astype(np.float32), BF16)


def _attn_prompt(q_t, kcv, slc_t, win_t, gates, l):
    n, _, s = q_t.shape
    n_cmp = kcv.shape[1]
    n_blk = s // CMP_STRIDE - CMP_BLOCK // CMP_STRIDE + 1
    n_slc = s // SLC_BLOCK
    ovt = _overlap_t(n_cmp, n_blk, n_slc)
    tq = Q_BLOCK
    cols = N_REP * tq
    assert s % SEL_KEYS == 0 and (SEL_KEYS // 2) % tq == 0 and n_slc <= BLOCK_CODE
    return pl.pallas_call(
        functools.partial(_attn_prompt_kernel, s=s, n_blk=n_blk),
        grid=(n, s // tq),
        in_specs=[pl.BlockSpec((None, D_ATTN, tq), lambda i, j: (i, 0, j)),
                  pl.BlockSpec((None,) + kcv.shape[1:], lambda i, j: (i, 0, 0)),
                  pl.BlockSpec((None, None, D_ROWS, s), lambda i, j: (l, i, 0, 0)),
                  pl.BlockSpec((None, None, D_ROWS, s), lambda i, j: (l, i, 0, 0)),
                  pl.BlockSpec((None, GATE_ROWS, tq), lambda i, j: (i, 0, j)),
                  pl.BlockSpec(ovt.shape, lambda i, j: (0, 0))],
        out_specs=pl.BlockSpec((None, tq, D_ATTN), lambda i, j: (i, j, 0)),
        out_shape=jax.ShapeDtypeStruct((n, s, D_ATTN), F32),
        scratch_shapes=[pltpu.VMEM((N_KV_HEADS, s, HEAD_DIM + BLOCK_CODE), BF16),
                        pltpu.VMEM((N_KV_HEADS, s, HEAD_DIM), BF16),
                        pltpu.VMEM((N_KV_HEADS, s, cols), BF16),
                        pltpu.VMEM((N_KV_HEADS, HEAD_DIM + ONES_ROWS, cols), F32),
                        pltpu.VMEM((N_KV_HEADS, 1, cols), F32)],
        compiler_params=pltpu.CompilerParams(dimension_semantics=("arbitrary", "arbitrary"),
                                             vmem_limit_bytes=VMEM_LIMIT),
        name="attn_prompt",
    )(q_t, kcv, slc_t, win_t, gates, ovt)


PAGE = 128
SAMPLE_SEQS = 4


def _sample_geometry(past):
    length = past + 1
    padded = -(-length // SLC_BLOCK) * SLC_BLOCK
    n_pc = padded // CMP_STRIDE
    n_pc_pad = -(-n_pc // CMP_STRIDE) * CMP_STRIDE
    return dict(n_pc=n_pc, n_pc_pad=n_pc_pad, n_blk=n_pc - CMP_BLOCK // CMP_STRIDE + 1, n_slc=padded // SLC_BLOCK)


def _attn_sample_kernel(pt_ref, q_ref, new_ref, gates_ref, wincol_ref, cwin_ref, ov_ref,
                        perm_ref, w1ab_ref, bias_ref, w2_ref, ccmp_hbm, cslc_hbm, *rest, l, past, n_pages, nseq):
    o_ref, wout_ref, flat_ref, cmp_buf, slc_buf, sem = rest[-6:]
    step = pl.program_id(0)
    slot = step % 2

    def page_copies(of_step, into_slot, lookup):
        copies = []
        for u in range(nseq):
            for j in range(n_pages):
                page = pt_ref[of_step * nseq + u, j] if lookup else 0
                k = u * n_pages + j
                copies.append(pltpu.make_async_copy(ccmp_hbm.at[l, page], cmp_buf.at[into_slot, k], sem.at[into_slot, 0]))
                copies.append(pltpu.make_async_copy(cslc_hbm.at[l, page], slc_buf.at[into_slot, k], sem.at[into_slot, 1]))
        return copies

    @pl.when(step == 0)
    def _():
        for c in page_copies(0, 0, True):
            c.start()

    @pl.when(step + 1 < pl.num_programs(0))
    def _():
        for c in page_copies(step + 1, 1 - slot, True):
            c.start()

    for c in page_copies(step, slot, False):
        c.wait()
    cmp_pages = [[cmp_buf.at[slot, u * n_pages + j] for j in range(n_pages)] for u in range(nseq)]
    slc_pages = [[slc_buf.at[slot, u * n_pages + j] for j in range(n_pages)] for u in range(nseq)]
    geo = _sample_geometry(past)
    n_pc_pad, n_blk, n_slc = geo["n_pc_pad"], geo["n_blk"], geo["n_slc"]
    pos = past
    lb = cwin_ref.shape[-1]
    scale = HEAD_DIM ** -0.5
    head = lax.broadcasted_iota(jnp.int32, (N_HEADS, 1), 0)
    lane = lax.broadcasted_iota(jnp.int32, (1, GD), 1)
    lane_h = lax.broadcasted_iota(jnp.int32, (N_HEADS, GD), 1)
    own = (lane_h // HEAD_DIM) == (head // N_REP)

    def new_row(u, branch, kv):
        o = (branch * 2 + kv) * GD
        return new_ref[u, :, o:o + GD]

    def rounded(x):
        return x.astype(BF16).astype(F32)

    kcv = []
    pages_per_span = FLAT_SPAN // PAGE
    past_pc = past // CMP_STRIDE
    for u in range(nseq):
        row0 = u * n_pc_pad
        for t in range(n_pages // pages_per_span):
            x_t = jnp.concatenate([cmp_pages[u][t * pages_per_span + w][...].reshape(2 * GD, PAGE)
                                   for w in range(pages_per_span)], axis=1)
            _fill_flat(x_t.astype(BF16), perm_ref[...], flat_ref, row0 + t * CMP_STRIDE)
        for kv in range(2):
            flat_ref[kv, row0 + past_pc:row0 + n_pc_pad, :] = jnp.zeros((n_pc_pad - past_pc, CMP_FLAT), BF16)
            flat_ref[kv, row0 + past_pc:row0 + past_pc + 1, 0:GD] = new_row(u, 0, kv).astype(BF16)
    for kv in range(2):
        kcv.append(_compress_mlp(flat_ref[kv], w1ab_ref[kv], bias_ref[kv], w2_ref[kv]).astype(BF16))
    for u in range(nseq):
        _attn_sample_one(u, kcv[0][u * n_pc_pad:(u + 1) * n_pc_pad], kcv[1][u * n_pc_pad:(u + 1) * n_pc_pad],
                         q_ref, new_row, rounded, gates_ref, wincol_ref, cwin_ref, ov_ref,
                         slc_pages[u], o_ref, wout_ref, geo=geo, past=past, consts=(head, lane, lane_h, own))


def _attn_sample_one(u, kc, vc, q_ref, new_row, rounded, gates_ref, wincol_ref, cwin_ref, ov_ref,
                     slc_pages, o_ref, wout_ref, *, geo, past, consts):
    head, lane, lane_h, own = consts
    n_pc_pad, n_blk, n_slc = geo["n_pc_pad"], geo["n_blk"], geo["n_slc"]
    n_pages = len(slc_pages)
    pos = past
    lb = cwin_ref.shape[-1]
    scale = HEAD_DIM ** -0.5
    q_rows = jnp.concatenate([q_ref[u, :, h * HEAD_DIM:(h + 1) * HEAD_DIM] for h in range(N_HEADS)], axis=0)
    qx = (jnp.where(own, jnp.concatenate([q_rows] * N_KV_HEADS, axis=1), 0.0) * scale).astype(BF16)
    qx32 = qx.astype(F32)
    c_idx = lax.broadcasted_iota(jnp.int32, (1, n_pc_pad), 1)
    valid = (c_idx * CMP_STRIDE + (CMP_BLOCK - 1) <= pos) & (c_idx < n_blk)
    p_c = _masked_softmax(_dot_nt(qx, kc), valid)
    o_c = _dot(p_c.astype(BF16), vc)

    cur = pos // SLC_BLOCK
    forced = (lane == 0) | (lane == cur) | (lane == cur - 1)
    jp = lax.broadcasted_iota(jnp.int32, (GD, GD), 0)
    jj = lax.broadcasted_iota(jnp.int32, (GD, GD), 1)
    sels = []
    for g in range(N_KV_HEADS):
        pc_sum = jnp.sum(p_c[g * N_REP:(g + 1) * N_REP], axis=0, keepdims=True)
        pc_hi = pc_sum.astype(BF16)
        pc_lo = (pc_sum - pc_hi.astype(F32)).astype(BF16)
        imp = _dot(pc_hi, ov_ref[...]) + _dot(pc_lo, ov_ref[...])
        score = jnp.where(lane <= cur, jnp.where(forced, BIG, imp), -BIG)
        score = jnp.where(lane < n_slc, score, -2.0 * BIG)
        score_b = jnp.broadcast_to(score, (GD, GD))
        score_a = score_b.T
        beats = (score_a > score_b) | ((score_a == score_b) & (jp < jj))
        rank = jnp.sum(beats.astype(F32), axis=0, keepdims=True)
        sel = ((rank < float(min(N_SEL, n_slc))) & (lane < n_slc)).astype(F32)
        sels.append(jnp.broadcast_to(sel, (N_REP, GD)))
    sel_h = jnp.concatenate(sels, axis=0)

    blocks_per_tile = GD // SLC_BLOCK
    chosen = []
    for t in range(past // GD):
        tile = sel_h[:, t * blocks_per_tile:t * blocks_per_tile + 1]
        for b in range(1, blocks_per_tile):
            tile = jnp.where(lane >= b * SLC_BLOCK, sel_h[:, t * blocks_per_tile + b:t * blocks_per_tile + b + 1], tile)
        chosen.append(jnp.broadcast_to(tile, (N_HEADS, GD)))
    chosen = jnp.concatenate(chosen, axis=-1) > 0.5
    s_past = jnp.concatenate([_dot(qx, slc_pages[j][0].astype(BF16)) for j in range(n_pages)], axis=-1)
    s_past = jnp.where(chosen, s_past, NEG)
    new_ok = jnp.sum(jnp.where(lane == pos // SLC_BLOCK, sel_h, 0.0), axis=-1, keepdims=True) > 0.5
    s_new = jnp.where(new_ok, jnp.sum(qx32 * rounded(new_row(u, 1, 0)), axis=-1, keepdims=True), NEG)
    m = jnp.maximum(jnp.max(s_past, axis=-1, keepdims=True), s_new)
    p_past = jnp.where(chosen, jnp.exp(s_past - m), 0.0)
    p_new = jnp.where(new_ok, jnp.exp(s_new - m), 0.0)
    denom = jnp.maximum(jnp.sum(p_past, axis=-1, keepdims=True) + p_new, 1e-30)
    o_s = rounded(p_new) * rounded(new_row(u, 1, 1))
    for j in range(n_pages):
        o_s = o_s + _dot_nt(p_past[:, j * PAGE:(j + 1) * PAGE].astype(BF16), slc_pages[j][1].astype(BF16))
    o_s = o_s / denom

    kpos = (past - lb) + lax.broadcasted_iota(jnp.int32, (1, lb), 1)
    w_ok = (kpos <= pos) & (kpos > pos - WINDOW)
    s_w = jnp.where(w_ok, _dot(qx, cwin_ref[u, 0].astype(BF16)), NEG)
    s_wn = jnp.sum(qx32 * rounded(new_row(u, 2, 0)), axis=-1, keepdims=True)
    m = jnp.maximum(jnp.max(s_w, axis=-1, keepdims=True), s_wn)
    p_w = jnp.where(w_ok, jnp.exp(s_w - m), 0.0)
    p_wn = jnp.exp(s_wn - m)
    denom = jnp.maximum(jnp.sum(p_w, axis=-1, keepdims=True) + p_wn, 1e-30)
    o_w = (_dot_nt(p_w.astype(BF16), cwin_ref[u, 1].astype(BF16))
           + rounded(p_wn) * rounded(new_row(u, 2, 1))) / denom

    gates = gates_ref[u]
    gate_id = lax.broadcasted_iota(jnp.int32, (N_HEADS, GATE_ROWS), 1)
    gate = lambda br: jnp.sum(jnp.where(gate_id == head + br * N_HEADS, gates, 0.0), axis=-1, keepdims=True)
    o = gate(0) * o_c + gate(1) * o_s + gate(2) * o_w
    for h in range(N_HEADS):
        g = h // N_REP
        o_ref[u, :, h * HEAD_DIM:(h + 1) * HEAD_DIM] = o[h:h + 1, g * HEAD_DIM:(g + 1) * HEAD_DIM]

    for kv in range(2):
        wout_ref[u, kv, :, 0:lb - 1] = cwin_ref[u, kv, :, 1:lb]
        wout_ref[u, kv, :, lb - 1:lb] = wincol_ref[u, kv * GD:(kv + 1) * GD, :]


def _attn_sample(l, q_tok, new_tok, gates, win_col, cwin_t, ccmp_t, cslc_t, page_table, w1ab, bias, w2, win_prev):
    nb = q_tok.shape[0]
    n_pages = page_table.shape[1]
    past = n_pages * PAGE
    geo = _sample_geometry(past)
    lb = cwin_t.shape[-1]
    assert FLAT_SPAN % PAGE == 0 and n_pages % (FLAT_SPAN // PAGE) == 0
    perm = _piece_perm()
    ov = _overlap_t(geo["n_pc_pad"], geo["n_blk"], GD).T
    ov = jnp.where(jnp.arange(GD)[None, :] < geo["n_slc"], ov, 0).astype(BF16)
    assert GD % SLC_BLOCK == 0 and past % GD == 0
    nseq = SAMPLE_SEQS if nb % SAMPLE_SEQS == 0 else 1
    seq = lambda a: pl.BlockSpec((nseq,) + a.shape[1:], lambda i, pt: (i,) + (0,) * (a.ndim - 1))
    full = lambda a: pl.BlockSpec(a.shape, lambda i, pt: (0,) * a.ndim)
    operands = [page_table, q_tok, new_tok, gates, win_col, cwin_t, ov, perm, w1ab, bias, w2, ccmp_t, cslc_t]
    win_prev = () if win_prev is None else (win_prev,)
    page_buf = pltpu.VMEM((2, nseq * n_pages, 2, GD, PAGE), F32)
    grid_spec = pltpu.PrefetchScalarGridSpec(
        num_scalar_prefetch=1,
        grid=(nb // nseq,),
        in_specs=[seq(q_tok), seq(new_tok), seq(gates), seq(win_col),
                  pl.BlockSpec((None, nseq, 2, GD, lb), lambda i, pt: (l, i, 0, 0, 0)),
                  full(ov), full(perm), full(w1ab), full(bias), full(w2)]
                 + [pl.BlockSpec(memory_space=pl.ANY)] * (2 + len(win_prev)),
        out_specs=[pl.BlockSpec((nseq, 1, D_ATTN), lambda i, pt: (i, 0, 0)),
                   pl.BlockSpec((None, nseq, 2, GD, lb), lambda i, pt: (l, i, 0, 0, 0))],
        scratch_shapes=[pltpu.VMEM((2, nseq * geo["n_pc_pad"], CMP_FLAT), BF16), page_buf, page_buf,
                        pltpu.SemaphoreType.DMA((2, 2))],
    )
    return pl.pallas_call(
        functools.partial(_attn_sample_kernel, l=l, past=past, n_pages=n_pages, nseq=nseq),
        grid_spec=grid_spec,
        out_shape=[jax.ShapeDtypeStruct((nb, 1, D_ATTN), F32),
                   jax.ShapeDtypeStruct((cwin_t.shape[0], nb, 2, GD, lb), F32)],
        input_output_aliases={len(operands): 1} if win_prev else {},
        compiler_params=pltpu.CompilerParams(dimension_semantics=("arbitrary",), vmem_limit_bytes=VMEM_LIMIT),
        name="attn_sample",
    )(*operands, *win_prev)


def _prep_layer(l, w_norm, w_in, w_out, w_dw, b_dw, ln_g, ln_b, w_pw, b_pw, w_pool, pool_scale,
                g_q, g_k, cmp_pe, cmp_w1, cmp_w2):
    sizes = (D_CONV, D_CONV, D_CONV, D_POOL, D_POOL, D_ATTN, D_KV, N_BRANCH * N_HEADS, D_ATTN)
    offs = [0] + [int(v) for v in np.cumsum(sizes)]
    col = lambda k: w_in[l][:, offs[k]:offs[k + 1]]
    a_val, a_gate, z_a, b_in, z_b, q, kv, gate, z_c = (col(k) for k in range(9))
    pad = jnp.zeros((D_MODEL, GATE_ROWS - N_BRANCH * N_HEADS), F32)
    wtok = jnp.concatenate([a_val, a_gate, b_in, z_a, z_b, z_c], axis=1).astype(BF16)
    wfeat = jnp.concatenate([q, kv, gate, pad], axis=1).T.astype(BF16)
    ones = jnp.ones((N_KV_HEADS * HEAD_DIM,), F32)
    gcol = jnp.concatenate([jnp.tile(g_q[l], N_HEADS)]
                           + [piece for br in range(N_BRANCH)
                              for piece in (jnp.tile(g_k[l, br], N_KV_HEADS), ones)])[:, None]
    wpool_bd = jnp.zeros((D_POOL, D_POOL), F32)
    for gi in range(len(POOL_WINDOWS)):
        wpool_bd = wpool_bd.at[gi * POOL_GROUP:(gi + 1) * POOL_GROUP, gi * POOL_GROUP:(gi + 1) * POOL_GROUP].set(w_pool[l, gi])
    row = lambda v: v[None, :]

    def both_groups(w):
        z = jnp.zeros_like(w)
        return jnp.stack([jnp.concatenate([w, z], axis=2), jnp.concatenate([z, w], axis=2)], axis=1).reshape(-1, GD)

    half = CMP_STRIDE * HEAD_DIM
    w1 = cmp_w1[l].astype(BF16)
    w1ab = jnp.stack([jnp.concatenate(
        [both_groups(w1[kv, :half].reshape(CMP_STRIDE, HEAD_DIM, HEAD_DIM)),
         both_groups(w1[kv, half:].reshape(CMP_STRIDE, HEAD_DIM, HEAD_DIM))], axis=1) for kv in range(2)])
    w2p = jnp.stack([both_groups(cmp_w2[l, kv].astype(BF16)[None]) for kv in range(2)])
    cbias = jnp.tile(_cmp_bias(cmp_pe[l].reshape(2, 1, CMP_BLOCK * HEAD_DIM), w1), (1, 1, N_KV_HEADS))
    return dict(
        wnorm=row(w_norm[l]), wtok=wtok, wfeat=wfeat, gcol=gcol,
        wdw=w_dw[l], bdw=row(b_dw[l]), lng=row(ln_g[l]), lnb=row(ln_b[l]),
        wpw=w_pw[l].astype(BF16), bpw=row(b_pw[l]),
        wpool=wpool_bd.astype(BF16), pscale=row(pool_scale[l]),
        w1ab=w1ab, cbias=cbias, w2p=w2p,
        wout=w_out[l].astype(BF16),
    )


def _rope_tables(pos):
    inv = ROPE_THETA ** (-jnp.arange(ROT_HALF, dtype=F32) * 2.0 / ROT_DIM)
    ang = pos.astype(F32)[:, None] * inv[None, :]
    return jnp.cos(ang).T, jnp.sin(ang).T


def _prompt_layer(l, depth, x, p, cos_t, sin_t, kv_prev):
    glu, b_in, sz, gates, q_t, *kv = _in_proj(
        x, p["wnorm"], p["wtok"], p["wfeat"], p["gcol"], cos_t, sin_t, min(IN_PROJ_ROWS, x.shape[1]), l, depth, kv_prev)
    cmp_t, slc_t, win_t = kv
    kcv = _compress_prompt(cmp_t, l, p["w1ab"], p["cbias"], p["w2p"])
    o_attn = _attn_prompt(q_t, kcv, slc_t, win_t, gates, l)
    y = _mix_out_prompt(x, glu, b_in, sz, o_attn, p["wdw"], p["bdw"], p["lng"], p["lnb"], p["wpw"], p["bpw"],
                        p["wpool"], p["pscale"], p["wout"], t=min(MIX_OUT_ROWS, x.shape[1]))
    return y, glu, b_in, kv


def _sample_layer(l, depth, x, p, cos_t, sin_t, sc, sp, cwin_t, ccmp_t, cslc_t, page_table, past, kv_prev, win_prev):
    nb = x.shape[0]
    glu, b_in, sz, gates, q_t, *kv = _in_proj(
        x[None], p["wnorm"], p["wtok"], p["wfeat"], p["gcol"], cos_t, sin_t, nb, l, depth, kv_prev)
    glu, b_in, sz, gates = glu[0], b_in[0], sz[0], gates[0]
    yab, sc_new, sp_new = _convpool_sample(sc, sp, glu, b_in, sz, p["wdw"], p["bdw"], p["lng"], p["lnb"],
                                           p["wpw"], p["bpw"], p["wpool"], p["pscale"], pos=past)
    new_t = jnp.concatenate([a[l, 0] for a in kv], axis=0)
    new_tok = new_t.T[:, None, :]
    q_tok = q_t[0].T[:, None, :]
    win_col = kv[2][l, 0].T[:, :, None]
    o_attn, win_next = _attn_sample(l, q_tok, new_tok, gates.T[:, None, :], win_col, cwin_t, ccmp_t, cslc_t,
                                    page_table, p["w1ab"], p["cbias"], p["w2p"], win_prev)
    y = _out_proj(x[None], yab[None], o_attn.reshape(1, nb, D_ATTN), sz[None], p["wout"], tm=nb)[0]
    return y, sc_new, sp_new, win_next, kv


def _rows_from_feat(a):
    lead = a.shape[:-2]
    a = a.reshape(lead + (2, N_KV_HEADS, HEAD_DIM, a.shape[-1]))
    nl = len(lead)
    return jnp.transpose(a, tuple(range(nl)) + (nl + 3, nl, nl + 1, nl + 2))


def kernel(x_prompt, x_sample, state_conv, state_pool, cache_win_kv, cache_cmp_kv, cache_slc_kv, page_table,
           w_norm, w_in, w_out, w_dw, b_dw, ln_g, ln_b, w_pw, b_pw, w_pool, pool_scale,
           g_q, g_k, cmp_pe, cmp_w1, cmp_w2):
    bp, s, _ = x_prompt.shape
    bs, t_new, _ = x_sample.shape
    depth = w_in.shape[0]
    assert t_new == 1 and cache_cmp_kv.shape[2] == PAGE and s % SEL_KEYS == 0 and s % MIX_OUT_ROWS == 0
    past = page_table.shape[1] * PAGE
    lb = cache_win_kv.shape[2]

    def to_feat(c):
        c = jnp.transpose(c, (0, 1, 3, 4, 5, 2))
        return c.reshape(c.shape[:3] + (GD, c.shape[-1]))
    cwin_t, ccmp_t, cslc_t = to_feat(cache_win_kv), to_feat(cache_cmp_kv), to_feat(cache_slc_kv)
    sc_all = jnp.transpose(state_conv, (0, 2, 1, 3))
    sp_all = jnp.transpose(state_pool, (0, 2, 1, 3))
    cos_p, sin_p = _rope_tables(jnp.arange(s))
    cos_s, sin_s = _rope_tables(jnp.full((bs,), past))

    xp, xs = x_prompt, x_sample[:, 0, :]
    conv_p, pool_p, conv_s, pool_s = [], [], [], []
    kv_p = kv_s = win_s = None
    for l in range(depth):
        p = _prep_layer(l, w_norm, w_in, w_out, w_dw, b_dw, ln_g, ln_b, w_pw, b_pw, w_pool, pool_scale,
                        g_q, g_k, cmp_pe, cmp_w1, cmp_w2)
        xp, glu, b_in, kv_p = _prompt_layer(l, depth, xp, p, cos_p, sin_p, kv_p)
        conv_p.append(glu[:, s - CONV_STATE:, :])
        pool_p.append(b_in[:, s - POOL_STATE:, :])
        xs, sc_new, sp_new, win_s, kv_s = _sample_layer(
            l, depth, xs, p, cos_s, sin_s, sc_all[l], sp_all[l], cwin_t, ccmp_t, cslc_t, page_table, past,
            kv_s, win_s)
        conv_s.append(jnp.transpose(sc_new, (1, 0, 2)))
        pool_s.append(jnp.transpose(sp_new, (1, 0, 2)))
    cmp_p, slc_p, win_p = kv_p
    cmp_s, slc_s, _ = kv_s
    new_rows = lambda a: _rows_from_feat(a[:, 0])[:, :, None]
    return (xp, xs[:, None, :], jnp.stack(conv_p), jnp.stack(pool_p),
            _rows_from_feat(win_p[:, :, :, s - min(WINDOW, s):]), _rows_from_feat(cmp_p), _rows_from_feat(slc_p),
            jnp.stack(conv_s), jnp.stack(pool_s), _rows_from_feat(win_s.reshape(depth, bs, D_ROWS, lb)),
            new_rows(cmp_s), new_rows(slc_s))
```

```python
import functools

import numpy as np
import jax
import jax.numpy as jnp
from jax import lax
from jax.experimental import pallas as pl
from jax.experimental.pallas import tpu as pltpu

D_MODEL = 1024
D_CONV = 256
D_POOL = 256
N_HEADS = 8
N_KV_HEADS = 2
HEAD_DIM = 64
D_ATTN = N_HEADS * HEAD_DIM
CONV_WIDTH = 31
CONV_STATE = CONV_WIDTH - 1
POOL_WINDOWS = (2, 4, 8, 16)
POOL_GROUP = D_POOL // len(POOL_WINDOWS)
POOL_STATE = max(POOL_WINDOWS) - 1
CMP_BLOCK = 32
CMP_STRIDE = 16
SLC_BLOCK = 64
N_SEL = 8
WINDOW = 256
Q_BLOCK = 256
ROT_DIM = HEAD_DIM // 4
ROT_HALF = ROT_DIM // 2
ROPE_THETA = 500000.0
N_BRANCH = 3
D_KV = N_BRANCH * 2 * N_KV_HEADS * HEAD_DIM
D_ROWS = 2 * N_KV_HEADS * HEAD_DIM
N_REP = N_HEADS // N_KV_HEADS
EPS = 1e-6
NEG = -1e30
BIG = 1e4

GATE_ROWS = 32
D_TOK = 3 * D_CONV + 2 * D_POOL + D_ATTN
D_QKV = D_ATTN + D_KV
D_FEAT = D_QKV + GATE_ROWS
N_FEAT_BLOCKS = D_QKV // HEAD_DIM
SEL_KEYS = 512
IN_PROJ_ROWS = 1024
MIX_OUT_ROWS = 512
VMEM_V7X = 64 * 1024 * 1024
VMEM_LIMIT = VMEM_V7X * 7 // 8

F32 = jnp.float32
BF16 = jnp.bfloat16
NT_DIMS = (((1,), (1,)), ((), ()))


def _sigmoid(x):
    return 1.0 / (1.0 + jnp.exp(-x))


def _silu(x):
    return x * _sigmoid(x)


def _gelu_tanh(x):
    return 0.5 * x * (1.0 + jnp.tanh(np.sqrt(2.0 / np.pi).astype(np.float32) * (x + 0.044715 * (x * x * x))))


def _dot(a, b):
    return jnp.dot(a, b, preferred_element_type=F32)


def _dot_nt(a, b):
    return lax.dot_general(a, b, NT_DIMS, preferred_element_type=F32)


def _masked_softmax(s, mask):
    s = jnp.where(mask, s, NEG)
    m = jnp.max(s, axis=-1, keepdims=True)
    p = jnp.where(mask, jnp.exp(s - m), 0.0)
    return p / jnp.maximum(jnp.sum(p, axis=-1, keepdims=True), 1e-30)


def _in_proj_kernel(x_ref, wnorm_ref, wtok_ref, wfeat_ref, gcol_ref, cos_ref, sin_ref, *rest):
    glu_ref, bin_ref, sz_ref, gates_ref, qt_ref, cmp_ref, slc_ref, win_ref = rest[-8:]
    x = x_ref[...]
    ms = jnp.mean(x * x, axis=-1, keepdims=True)
    h = (x * lax.rsqrt(ms + EPS) * wnorm_ref[...]).astype(BF16)

    a = _dot(h, wtok_ref[...])
    o = 0
    glu_ref[...] = a[:, o:o + D_CONV] * _sigmoid(a[:, o + D_CONV:o + 2 * D_CONV])
    o += 2 * D_CONV
    bin_ref[...] = a[:, o:o + D_POOL]
    o += D_POOL
    z = a[:, o:o + D_CONV + D_POOL + D_ATTN]
    sz_ref[...] = _silu(z)

    f = _dot_nt(wfeat_ref[...], h)
    gates_ref[...] = _sigmoid(f[D_QKV:D_FEAT, :])
    cos = cos_ref[...]
    sin = sin_ref[...]
    kv_refs = (cmp_ref, slc_ref, win_ref)
    for hb in range(N_FEAT_BLOCKS):
        blk = f[hb * HEAD_DIM:(hb + 1) * HEAD_DIM, :]
        kv_blk = hb - N_HEADS
        is_value = kv_blk >= 0 and (kv_blk % (2 * N_KV_HEADS)) >= N_KV_HEADS
        if not is_value:
            bms = jnp.mean(blk * blk, axis=0, keepdims=True)
            y = blk * lax.rsqrt(bms + EPS) * gcol_ref[hb * HEAD_DIM:(hb + 1) * HEAD_DIM, :]
            x1 = y[0:ROT_HALF]
            x2 = y[ROT_HALF:ROT_DIM]
            blk = jnp.concatenate([x1 * cos - x2 * sin, x2 * cos + x1 * sin, y[ROT_DIM:]], axis=0)
        if kv_blk < 0:
            qt_ref[hb * HEAD_DIM:(hb + 1) * HEAD_DIM, :] = blk
        else:
            r = kv_blk % (2 * N_KV_HEADS)
            kv_refs[kv_blk // (2 * N_KV_HEADS)][r * HEAD_DIM:(r + 1) * HEAD_DIM, :] = blk


def _in_proj(x, wnorm, wtok, wfeat, gcol, cos_t, sin_t, tm, l, depth, kv_prev):
    n, t, _ = x.shape
    assert t % tm == 0
    grid = (n, t // tm)
    tok = lambda w: pl.BlockSpec((None, tm, w), lambda i, j: (i, j, 0))
    feat = lambda w: pl.BlockSpec((None, w, tm), lambda i, j: (i, 0, j))
    kv_spec = pl.BlockSpec((None, None, D_ROWS, tm), lambda i, j: (l, i, 0, j))
    full = lambda a: pl.BlockSpec(a.shape, lambda i, j: (0,) * a.ndim)
    tok_shape = lambda w: jax.ShapeDtypeStruct((n, t, w), F32)
    kv_shape = jax.ShapeDtypeStruct((depth, n, D_ROWS, t), F32)
    n_in = 7
    kv_prev = () if kv_prev is None else tuple(kv_prev)
    return pl.pallas_call(
        _in_proj_kernel,
        grid=grid,
        in_specs=[tok(D_MODEL), full(wnorm), full(wtok), full(wfeat), full(gcol),
                  pl.BlockSpec((ROT_HALF, tm), lambda i, j: (0, j)),
                  pl.BlockSpec((ROT_HALF, tm), lambda i, j: (0, j))]
                 + [pl.BlockSpec(memory_space=pl.ANY)] * len(kv_prev),
        out_specs=[tok(D_CONV), tok(D_POOL), tok(D_CONV + D_POOL + D_ATTN), feat(GATE_ROWS),
                   feat(D_ATTN), kv_spec, kv_spec, kv_spec],
        out_shape=[tok_shape(D_CONV), tok_shape(D_POOL), tok_shape(D_CONV + D_POOL + D_ATTN),
                   jax.ShapeDtypeStruct((n, GATE_ROWS, t), F32),
                   jax.ShapeDtypeStruct((n, D_ATTN, t), F32), kv_shape, kv_shape, kv_shape],
        input_output_aliases={n_in + k: 5 + k for k in range(len(kv_prev))},
        compiler_params=pltpu.CompilerParams(dimension_semantics=("arbitrary", "arbitrary"),
                                             vmem_limit_bytes=VMEM_LIMIT),
        name="in_proj",
    )(x, wnorm, wtok, wfeat, gcol, cos_t, sin_t, *kv_prev)


def _conv_tail(acc, sz_a, bdw, lng, lnb, wpw, bpw):
    y = acc + bdw
    mu = jnp.mean(y, axis=-1, keepdims=True)
    yc = y - mu
    var = jnp.mean(yc * yc, axis=-1, keepdims=True)
    y = yc * lax.rsqrt(var + EPS) * lng + lnb
    y = _dot(_silu(y).astype(BF16), wpw) + bpw
    return y * sz_a


def _pool_tail(total, cnt, xcur, sz_b, wpool, pscale):
    d = total / cnt - xcur
    y = _dot(d.astype(BF16), wpool) * pscale
    return y * sz_b


def _pool_window_of_lane(shape):
    lane = lax.broadcasted_iota(jnp.int32, shape, len(shape) - 1)
    w = jnp.full(shape, POOL_WINDOWS[0], jnp.int32)
    for gi in range(1, len(POOL_WINDOWS)):
        w = jnp.where(lane >= gi * POOL_GROUP, POOL_WINDOWS[gi], w)
    return w


def _pool_select(sums, shape):
    lane = lax.broadcasted_iota(jnp.int32, shape, len(shape) - 1)
    total = sums[0]
    for gi in range(1, len(POOL_WINDOWS)):
        total = jnp.where(lane >= gi * POOL_GROUP, sums[gi], total)
    return total


CONV_HALO = 32
POOL_HALO = 16


MIX_ROWS = 256
SUBLANES = 8


def _tap_rows(buf, w_ref, r0, rows, off, taps, lanes):
    y = None
    for c in range(SUBLANES):
        part = None
        for k in taps:
            if (k + off) % SUBLANES != c:
                continue
            base = r0 + k + off - c
            term = w_ref[k:k + 1, lanes] * buf[base:base + rows + SUBLANES, lanes]
            part = term if part is None else part + term
        if part is not None:
            y = part[c:c + rows] if y is None else y + part[c:c + rows]
    return y


def _mix_out_prompt_kernel(x_ref, glu_ref, gprev_ref, bin_ref, bprev_ref, sz_ref, oat_ref, wdw_ref, bdw_ref, lng_ref,
                           lnb_ref, wpw_ref, bpw_ref, wpool_ref, pscale_ref, wout_ref, o_ref,
                           cbuf, pbuf, ymix, *, t):
    i = pl.program_id(1)
    keep = (i > 0).astype(F32)
    cbuf[0:CONV_HALO, :] = gprev_ref[...] * keep
    cbuf[CONV_HALO:CONV_HALO + t, :] = glu_ref[...]
    cbuf[CONV_HALO + t:CONV_HALO + t + SUBLANES, :] = jnp.zeros((SUBLANES, D_CONV), F32)
    pbuf[0:POOL_HALO, :] = bprev_ref[...] * keep
    pbuf[POOL_HALO:POOL_HALO + t, :] = bin_ref[...]
    nab = D_CONV + D_POOL
    for r0 in range(0, t, MIX_ROWS):
        rows = slice(r0, r0 + MIX_ROWS)
        sz = sz_ref[rows, :]
        acc = _tap_rows(cbuf, wdw_ref, r0, MIX_ROWS, CONV_HALO - CONV_STATE, range(CONV_WIDTH), slice(0, D_CONV))
        y_a = _conv_tail(acc, sz[:, 0:D_CONV], bdw_ref[...], lng_ref[...], lnb_ref[...], wpw_ref[...], bpw_ref[...])
        ymix[rows, 0:D_CONV] = y_a.astype(BF16)

        sums = []
        run = jnp.zeros((MIX_ROWS, D_POOL), F32)
        j = 0
        for w in POOL_WINDOWS:
            while j < w:
                run = run + pbuf[pl.ds(POOL_HALO + r0 - j, MIX_ROWS), :]
                j += 1
            sums.append(run)
        total = _pool_select(sums, (MIX_ROWS, D_POOL))
        pos = i * t + r0 + lax.broadcasted_iota(jnp.int32, (MIX_ROWS, D_POOL), 0)
        cnt = jnp.minimum(_pool_window_of_lane((MIX_ROWS, D_POOL)), pos + 1).astype(F32)
        y_b = _pool_tail(total, cnt, bin_ref[rows, :], sz[:, D_CONV:nab], wpool_ref[...], pscale_ref[...])
        ymix[rows, D_CONV:nab] = y_b.astype(BF16)
        ymix[rows, nab:nab + D_ATTN] = (oat_ref[rows, :] * sz[:, nab:nab + D_ATTN]).astype(BF16)
    o_ref[...] = x_ref[...] + _dot(ymix[...], wout_ref[...])


def _mix_out_prompt(x, glu, b_in, sz, o_attn, wdw, bdw, lng, lnb, wpw, bpw, wpool, pscale, wout, t):
    n, s, _ = glu.shape
    assert t % MIX_ROWS == 0 and s % t == 0
    grid = (n, s // t)
    cur = lambda w: pl.BlockSpec((None, t, w), lambda i, j: (i, j, 0))
    prev = lambda rows, w: pl.BlockSpec((None, rows, w), lambda i, j: (i, jnp.maximum(j * (t // rows) - 1, 0), 0))
    full = lambda a: pl.BlockSpec(a.shape, lambda i, j: (0,) * a.ndim)
    d_mix = D_CONV + D_POOL + D_ATTN
    return pl.pallas_call(
        functools.partial(_mix_out_prompt_kernel, t=t),
        grid=grid,
        in_specs=[cur(D_MODEL), cur(D_CONV), prev(CONV_HALO, D_CONV), cur(D_POOL), prev(POOL_HALO, D_POOL),
                  cur(d_mix), cur(D_ATTN),
                  full(wdw), full(bdw), full(lng), full(lnb), full(wpw), full(bpw), full(wpool),
                  full(pscale), full(wout)],
        out_specs=cur(D_MODEL),
        out_shape=jax.ShapeDtypeStruct(x.shape, F32),
        scratch_shapes=[pltpu.VMEM((CONV_HALO + t + SUBLANES, D_CONV), F32),
                        pltpu.VMEM((POOL_HALO + t, D_POOL), F32),
                        pltpu.VMEM((t, d_mix), BF16)],
        compiler_params=pltpu.CompilerParams(dimension_semantics=("arbitrary", "arbitrary"),
                                             vmem_limit_bytes=VMEM_LIMIT),
        name="mix_out_prompt",
    )(x, glu, glu, b_in, b_in, sz, o_attn, wdw, bdw, lng, lnb, wpw, bpw, wpool, pscale, wout)


def _convpool_sample_kernel(sc_ref, sp_ref, glu_ref, bin_ref, sz_ref, wdw_ref, bdw_ref, lng_ref, lnb_ref,
                            wpw_ref, bpw_ref, wpool_ref, pscale_ref, y_ref, sc_out, sp_out, *, pos):
    glu = glu_ref[...]
    acc = wdw_ref[CONV_STATE:CONV_WIDTH, :] * glu
    for k in range(CONV_STATE):
        acc = acc + wdw_ref[k:k + 1, :] * sc_ref[k]
    sz = sz_ref[...]
    y_ref[:, 0:D_CONV] = _conv_tail(acc, sz[:, 0:D_CONV], bdw_ref[...], lng_ref[...], lnb_ref[...],
                                    wpw_ref[...], bpw_ref[...])
    for k in range(CONV_STATE - 1):
        sc_out[k] = sc_ref[k + 1]
    sc_out[CONV_STATE - 1] = glu

    xcur = bin_ref[...]
    shape = xcur.shape
    sums = []
    run = xcur
    j = 1
    for w in POOL_WINDOWS:
        while j < w:
            run = run + sp_ref[POOL_STATE - j]
            j += 1
        sums.append(run)
    total = _pool_select(sums, shape)
    cnt = jnp.minimum(_pool_window_of_lane(shape), pos + 1).astype(F32)
    y_ref[:, D_CONV:D_CONV + D_POOL] = _pool_tail(total, cnt, xcur, sz[:, D_CONV:D_CONV + D_POOL],
                                                  wpool_ref[...], pscale_ref[...])
    for k in range(POOL_STATE - 1):
        sp_out[k] = sp_ref[k + 1]
    sp_out[POOL_STATE - 1] = xcur


def _convpool_sample(sc, sp, glu, b_in, sz, wdw, bdw, lng, lnb, wpw, bpw, wpool, pscale, pos):
    nb = glu.shape[0]
    return pl.pallas_call(
        functools.partial(_convpool_sample_kernel, pos=pos),
        out_shape=[jax.ShapeDtypeStruct((nb, D_CONV + D_POOL), F32),
                   jax.ShapeDtypeStruct(sc.shape, F32), jax.ShapeDtypeStruct(sp.shape, F32)],
        compiler_params=pltpu.CompilerParams(vmem_limit_bytes=VMEM_LIMIT),
        name="convpool_sample",
    )(sc, sp, glu, b_in, sz[:, 0:D_CONV + D_POOL], wdw, bdw, lng, lnb, wpw, bpw, wpool, pscale)


def _out_proj_kernel(x_ref, yab_ref, oat_ref, szc_ref, w_ref, o_ref):
    ya = yab_ref[...].astype(BF16)
    yc = (oat_ref[...] * szc_ref[...]).astype(BF16)
    nab = D_CONV + D_POOL
    o_ref[...] = x_ref[...] + _dot(ya, w_ref[0:nab, :]) + _dot(yc, w_ref[nab:nab + D_ATTN, :])


def _out_proj(x, yab, o_attn, sz, w_out, tm):
    n, t, _ = x.shape
    nab = D_CONV + D_POOL
    assert nab == D_ATTN
    grid = (n, t // tm)
    tok = lambda w, cb=0: pl.BlockSpec((None, tm, w), lambda i, j: (i, j, cb))
    return pl.pallas_call(
        _out_proj_kernel,
        grid=grid,
        in_specs=[tok(D_MODEL), tok(nab), tok(D_ATTN), tok(D_ATTN, 1),
                  pl.BlockSpec(w_out.shape, lambda i, j: (0, 0))],
        out_specs=tok(D_MODEL),
        out_shape=jax.ShapeDtypeStruct(x.shape, F32),
        compiler_params=pltpu.CompilerParams(dimension_semantics=("arbitrary", "arbitrary"),
                                             vmem_limit_bytes=VMEM_LIMIT),
        name="out_proj",
    )(x, yab, o_attn, sz, w_out)


GD = N_KV_HEADS * HEAD_DIM
CMP_FLAT = CMP_STRIDE * GD


FLAT_SPAN = CMP_STRIDE * CMP_STRIDE


def _piece_perm():
    m = np.arange(FLAT_SPAN)
    src = (m % CMP_STRIDE) * CMP_STRIDE + m // CMP_STRIDE
    return jnp.asarray((src[:, None] == np.arange(FLAT_SPAN)[None, :]).astype(np.float32), BF16)


def _fill_flat(x_t, perm, flat_ref, piece0):
    y = _dot_nt(perm, x_t).astype(BF16)
    for kv in range(2):
        for r in range(CMP_STRIDE):
            flat_ref[kv, piece0:piece0 + CMP_STRIDE, r * GD:(r + 1) * GD] = (
                y[r * CMP_STRIDE:(r + 1) * CMP_STRIDE, kv * GD:(kv + 1) * GD])


def _compress_mlp(flat, w1ab, bias, w2):
    n_pc = flat.shape[0]
    hb = _dot(flat, w1ab)
    h = hb[:, 0:GD] + jnp.concatenate([hb[1:n_pc, GD:2 * GD], jnp.zeros((1, GD), F32)], axis=0) + bias
    out = _dot(_gelu_tanh(h).astype(BF16), w2)
    row = lax.broadcasted_iota(jnp.int32, out.shape, 0)
    return jnp.where(row < n_pc - 1, out, 0.0)


def _cmp_bias_kernel(pe_ref, w1_ref, o_ref):
    for kv in range(2):
        o_ref[kv] = _dot(pe_ref[kv].astype(BF16), w1_ref[kv])


def _cmp_bias(pe_flat, w1):
    return pl.pallas_call(_cmp_bias_kernel, out_shape=jax.ShapeDtypeStruct((2, 1, HEAD_DIM), F32),
                          name="cmp_bias")(pe_flat, w1)


def _compress_prompt_kernel(x_ref, perm_ref, w1ab_ref, bias_ref, w2_ref, o_ref, flat_ref, *, s):
    for t in range(s // FLAT_SPAN):
        _fill_flat(x_ref[:, t * FLAT_SPAN:(t + 1) * FLAT_SPAN].astype(BF16), perm_ref[...], flat_ref, t * CMP_STRIDE)
    for kv in range(2):
        o_ref[:, kv * GD:(kv + 1) * GD] = _compress_mlp(flat_ref[kv], w1ab_ref[kv], bias_ref[kv], w2_ref[kv])


def _compress_prompt(cmp_t, l, w1ab, bias, w2):
    _, n, _, s = cmp_t.shape
    assert s % FLAT_SPAN == 0
    n_pc = s // CMP_STRIDE
    perm = _piece_perm()
    full = lambda a: pl.BlockSpec(a.shape, lambda i: (0,) * a.ndim)
    return pl.pallas_call(
        functools.partial(_compress_prompt_kernel, s=s),
        grid=(n,),
        in_specs=[pl.BlockSpec((None, None, D_ROWS, s), lambda i: (l, i, 0, 0)),
                  full(perm), full(w1ab), full(bias), full(w2)],
        out_specs=pl.BlockSpec((None, n_pc, 2 * GD), lambda i: (i, 0, 0)),
        out_shape=jax.ShapeDtypeStruct((n, n_pc, 2 * GD), F32),
        scratch_shapes=[pltpu.VMEM((2, n_pc, CMP_FLAT), BF16)],
        compiler_params=pltpu.CompilerParams(dimension_semantics=("arbitrary",), vmem_limit_bytes=VMEM_LIMIT),
        name="compress_prompt",
    )(cmp_t, perm, w1ab, bias, w2)


def _select_blocks(imp_t, pos_row, n_blocks):
    shape = imp_t.shape
    assert n_blocks % 8 == 0
    j = lax.broadcasted_iota(jnp.int32, shape, 0)
    cur = pos_row // SLC_BLOCK
    forced = (j == 0) | (j == cur) | (j == cur - 1)
    score = jnp.where(j <= cur, jnp.where(forced, BIG, imp_t), -BIG)
    tiles = [score[8 * v:8 * v + 8] for v in range(n_blocks // 8)]
    j8 = lax.broadcasted_iota(jnp.int32, (8, shape[1]), 0)
    ranks = [jnp.zeros((8, shape[1]), F32) for _ in tiles]
    for jp in range(n_blocks):
        sj = score[jp:jp + 1, :]
        for v, tile in enumerate(tiles):
            if jp < 8 * v:
                beats = sj >= tile
            elif jp >= 8 * v + 8:
                beats = sj > tile
            else:
                beats = (sj > tile) | ((sj == tile) & (j8 > jp - 8 * v))
            ranks[v] = ranks[v] + jnp.where(beats, 1.0, 0.0)
    rank = jnp.concatenate(ranks, axis=0)
    return (rank < float(min(N_SEL, n_blocks))).astype(F32)


def _col_softmax(s):
    m = jnp.max(s, axis=0, keepdims=True)
    e = jnp.exp(s - m)
    inv = jnp.where(m > 0.5 * NEG, 1.0 / jnp.maximum(jnp.sum(e, axis=0, keepdims=True), 1e-30), 0.0)
    return e, inv


ONES_ROWS = 16
WIN_TILE = 128
BLOCK_CODE = 64


def _with_ones(v_t):
    return jnp.concatenate([v_t, jnp.ones((ONES_ROWS, v_t.shape[1]), BF16)], axis=0)


def _normalised(acc):
    return acc[0:HEAD_DIM] * (1.0 / jnp.maximum(acc[HEAD_DIM:HEAD_DIM + 1], 1e-30))


def _mask_heads(sc, ok, tq):
    return jnp.concatenate([jnp.where(ok, sc[:, r * tq:(r + 1) * tq], NEG) for r in range(N_REP)], axis=1)


def _attn_prompt_kernel(qt_ref, kcv_ref, slc_ref, win_ref, gates_ref, ovt_ref, o_ref,
                        ks_ref, kw_ref, s_ref, acc_ref, m_ref, *, s, n_blk):
    tq = Q_BLOCK
    cols = N_REP * tq
    groups = range(N_KV_HEADS)
    qc = pl.program_id(1)
    q0 = qc * tq
    pos_row = q0 + lax.broadcasted_iota(jnp.int32, (1, tq), 1)
    n_slc = s // SLC_BLOCK
    n_cmp = kcv_ref.shape[0]
    scale = HEAD_DIM ** -0.5
    v_row0 = lambda g: (N_KV_HEADS + g) * HEAD_DIM

    @pl.when(qc == 0)
    def _():
        for g in range(N_KV_HEADS):
            for t in range(s // 128):
                cols_t = slice(t * 128, (t + 1) * 128)
                ks_ref[g, cols_t, 0:HEAD_DIM] = slc_ref[g * HEAD_DIM:(g + 1) * HEAD_DIM, cols_t].T.astype(BF16)
                block_of_key = (t * 128 + lax.broadcasted_iota(jnp.int32, (128, BLOCK_CODE), 0)) // SLC_BLOCK
                code = block_of_key == lax.broadcasted_iota(jnp.int32, (128, BLOCK_CODE), 1)
                ks_ref[g, cols_t, HEAD_DIM:HEAD_DIM + BLOCK_CODE] = jnp.where(code, 1.0, 0.0).astype(BF16)
                kw_ref[g, cols_t, :] = win_ref[g * HEAD_DIM:(g + 1) * HEAD_DIM, cols_t].T.astype(BF16)

    gates_t = gates_ref[...]
    vc_t = kcv_ref[:, GD:2 * GD].T.astype(BF16)
    q_ts = []
    for g in groups:
        q_t = jnp.concatenate([qt_ref[(g * N_REP + r) * HEAD_DIM:(g * N_REP + r + 1) * HEAD_DIM, :]
                               for r in range(N_REP)], axis=1)
        q_ts.append((q_t * scale).astype(BF16))

    c_idx = lax.broadcasted_iota(jnp.int32, (n_cmp, 1), 0)
    valid_c = (c_idx * CMP_STRIDE + (CMP_BLOCK - 1) <= pos_row) & (c_idx < n_blk)
    ovt = ovt_ref[...]
    o_cs, q_codes = [], []
    for g in groups:
        kc = kcv_ref[:, g * HEAD_DIM:(g + 1) * HEAD_DIM].astype(BF16)
        e_c, inv_c = _col_softmax(_mask_heads(_dot(kc, q_ts[g]), valid_c, tq))
        o_cs.append(_dot(vc_t[g * HEAD_DIM:(g + 1) * HEAD_DIM, :], e_c.astype(BF16)) * inv_c)
        pc_sum = e_c[:, 0:tq] * inv_c[:, 0:tq]
        for r in range(1, N_REP):
            pc_sum = pc_sum + e_c[:, r * tq:(r + 1) * tq] * inv_c[:, r * tq:(r + 1) * tq]
        pc_hi = pc_sum.astype(BF16)
        pc_lo = (pc_sum - pc_hi.astype(F32)).astype(BF16)
        imp_t = _dot(ovt, pc_hi) + _dot(ovt, pc_lo)
        sel_t = _select_blocks(imp_t, pos_row, n_slc)
        bias = jnp.concatenate([jnp.where(sel_t > 0.5, 0.0, NEG), jnp.zeros((BLOCK_CODE - n_slc, tq), F32)], axis=0)
        q_codes.append(jnp.concatenate([q_ts[g], jnp.concatenate([bias] * N_REP, axis=1).astype(BF16)], axis=0))

    n_keys = q0 + tq
    rest = n_keys % SEL_KEYS
    half_tail = (rest > 0) & (rest <= SEL_KEYS // 2)
    n_steps = n_keys // SEL_KEYS + (rest > SEL_KEYS // 2).astype(jnp.int32)
    tail_k0 = pl.multiple_of((n_keys // SEL_KEYS) * SEL_KEYS, SEL_KEYS)
    last_full = jnp.logical_not(half_tail)

    def score_chunk(k0, size, diagonal):
        key = k0 + lax.broadcasted_iota(jnp.int32, (size, 1), 0)
        for g in groups:
            sc = _dot(ks_ref[g, pl.ds(k0, size), :], q_codes[g])
            if diagonal:
                sc = _mask_heads(sc, key <= pos_row, tq)
            s_ref[g, pl.ds(k0, size), :] = sc.astype(BF16)
            m_ref[g] = jnp.maximum(m_ref[g], jnp.max(sc, axis=0, keepdims=True))

    m_ref[...] = jnp.full(m_ref.shape, NEG, F32)

    def score_step(i, carry):
        score_chunk(pl.multiple_of(i * SEL_KEYS, SEL_KEYS), SEL_KEYS, False)
        return carry

    lax.fori_loop(0, n_steps - last_full.astype(jnp.int32), score_step, 0)

    @pl.when(last_full)
    def _():
        score_chunk(pl.multiple_of((n_steps - 1) * SEL_KEYS, SEL_KEYS), SEL_KEYS, True)

    @pl.when(half_tail)
    def _():
        score_chunk(tail_k0, SEL_KEYS // 2, True)

    m_sel = [m_ref[g].astype(BF16) for g in groups]

    acc_ref[...] = jnp.zeros(acc_ref.shape, F32)

    def value_chunk(k0, size):
        for g in groups:
            p = jnp.exp(s_ref[g, pl.ds(k0, size), :] - m_sel[g])
            v_t = slc_ref[v_row0(g):v_row0(g) + HEAD_DIM, pl.ds(k0, size)].astype(BF16)
            acc_ref[g] += _dot(_with_ones(v_t), p)

    def value_step(i, carry):
        value_chunk(pl.multiple_of(i * SEL_KEYS, SEL_KEYS), SEL_KEYS)
        return carry

    lax.fori_loop(0, n_steps, value_step, 0)

    @pl.when(half_tail)
    def _():
        value_chunk(tail_k0, SEL_KEYS // 2)

    n_prev = -(-(WINDOW - 1) // WIN_TILE)
    sub_tiles = tq // WIN_TILE
    win_k0, win_ok = {}, {}
    for u in range(sub_tiles):
        pos_u = pos_row[:, u * WIN_TILE:(u + 1) * WIN_TILE]
        for b in range(n_prev + 1):
            kb = q0 // WIN_TILE + u - n_prev + b
            win_k0[u, b] = pl.multiple_of(jnp.maximum(kb, 0) * WIN_TILE, WIN_TILE)
            kp = kb * WIN_TILE + lax.broadcasted_iota(jnp.int32, (WIN_TILE, 1), 0)
            win_ok[u, b] = (kp <= pos_u) & (kp > pos_u - WINDOW) & (kp >= 0)

    out_blocks = []
    for g in groups:
        o_s = _normalised(acc_ref[g])
        o_ws = []
        for u in range(sub_tiles):
            q_u = jnp.concatenate([q_ts[g][:, r * tq + u * WIN_TILE:r * tq + (u + 1) * WIN_TILE]
                                   for r in range(N_REP)], axis=1)
            s_w = jnp.concatenate([_mask_heads(_dot(kw_ref[g, pl.ds(win_k0[u, b], WIN_TILE), :], q_u),
                                               win_ok[u, b], WIN_TILE) for b in range(n_prev + 1)], axis=0)
            e_w = jnp.exp(s_w.astype(BF16) - jnp.max(s_w, axis=0, keepdims=True).astype(BF16))
            o_w = None
            for b in range(n_prev + 1):
                v_t = win_ref[v_row0(g):v_row0(g) + HEAD_DIM, pl.ds(win_k0[u, b], WIN_TILE)].astype(BF16)
                part = _dot(_with_ones(v_t), e_w[b * WIN_TILE:(b + 1) * WIN_TILE])
                o_w = part if o_w is None else o_w + part
            o_ws.append(_normalised(o_w))
        for r in range(N_REP):
            h = g * N_REP + r
            c = slice(r * tq, (r + 1) * tq)
            o_w = jnp.concatenate([o_ws[u][:, r * WIN_TILE:(r + 1) * WIN_TILE] for u in range(sub_tiles)], axis=1)
            out_blocks.append(gates_t[h:h + 1] * o_cs[g][:, c] + gates_t[N_HEADS + h:N_HEADS + h + 1] * o_s[:, c]
                              + gates_t[2 * N_HEADS + h:2 * N_HEADS + h + 1] * o_w)
    o_ref[...] = jnp.concatenate(out_blocks, axis=0).T


def _overlap_t(n_cmp_rows, n_blk, n_slc):
    c = np.arange(n_cmp_rows)
    start = c * CMP_STRIDE
    end = start + CMP_BLOCK - 1
    j0 = np.arange(n_slc) * SLC_BLOCK
    ov = (end[None, :] >= j0[:, None]) & (start[None, :] < j0[:, None] + SLC_BLOCK) & (c[None, :] < n_blk)
    return jnp.asarray(ov.astype(np.float32), BF16)


def _attn_prompt(q_t, kcv, slc_t, win_t, gates, l):
    n, _, s = q_t.shape
    n_cmp = kcv.shape[1]
    n_blk = s // CMP_STRIDE - CMP_BLOCK // CMP_STRIDE + 1
    n_slc = s // SLC_BLOCK
    ovt = _overlap_t(n_cmp, n_blk, n_slc)
    tq = Q_BLOCK
    cols = N_REP * tq
    assert s % SEL_KEYS == 0 and (SEL_KEYS // 2) % tq == 0 and n_slc <= BLOCK_CODE
    return pl.pallas_call(
        functools.partial(_attn_prompt_kernel, s=s, n_blk=n_blk),
        grid=(n, s // tq),
        in_specs=[pl.BlockSpec((None, D_ATTN, tq), lambda i, j: (i, 0, j)),
                  pl.BlockSpec((None,) + kcv.shape[1:], lambda i, j: (i, 0, 0)),
                  pl.BlockSpec((None, None, D_ROWS, s), lambda i, j: (l, i, 0, 0)),
                  pl.BlockSpec((None, None, D_ROWS, s), lambda i, j: (l, i, 0, 0)),
                  pl.BlockSpec((None, GATE_ROWS, tq), lambda i, j: (i, 0, j)),
                  pl.BlockSpec(ovt.shape, lambda i, j: (0, 0))],
        out_specs=pl.BlockSpec((None, tq, D_ATTN), lambda i, j: (i, j, 0)),
        out_shape=jax.ShapeDtypeStruct((n, s, D_ATTN), F32),
        scratch_shapes=[pltpu.VMEM((N_KV_HEADS, s, HEAD_DIM + BLOCK_CODE), BF16),
                        pltpu.VMEM((N_KV_HEADS, s, HEAD_DIM), BF16),
                        pltpu.VMEM((N_KV_HEADS, s, cols), BF16),
                        pltpu.VMEM((N_KV_HEADS, HEAD_DIM + ONES_ROWS, cols), F32),
                        pltpu.VMEM((N_KV_HEADS, 1, cols), F32)],
        compiler_params=pltpu.CompilerParams(dimension_semantics=("arbitrary", "arbitrary"),
                                             vmem_limit_bytes=VMEM_LIMIT),
        name="attn_prompt",
    )(q_t, kcv, slc_t, win_t, gates, ovt)


PAGE = 128
SAMPLE_SEQS = 4


def _sample_geometry(past):
    length = past + 1
    padded = -(-length // SLC_BLOCK) * SLC_BLOCK
    n_pc = padded // CMP_STRIDE
    n_pc_pad = -(-n_pc // CMP_STRIDE) * CMP_STRIDE
    return dict(n_pc=n_pc, n_pc_pad=n_pc_pad, n_blk=n_pc - CMP_BLOCK // CMP_STRIDE + 1, n_slc=padded // SLC_BLOCK)


def _attn_sample_kernel(pt_ref, q_ref, new_ref, gates_ref, wincol_ref, cwin_ref, ov_ref,
                        perm_ref, w1ab_ref, bias_ref, w2_ref, ccmp_hbm, cslc_hbm, *rest, l, past, n_pages, nseq):
    o_ref, wout_ref, flat_ref, cmp_buf, slc_buf, sem = rest[-6:]
    step = pl.program_id(0)
    slot = step % 2

    def page_copies(of_step, into_slot, lookup):
        copies = []
        for u in range(nseq):
            for j in range(n_pages):
                page = pt_ref[of_step * nseq + u, j] if lookup else 0
                k = u * n_pages + j
                copies.append(pltpu.make_async_copy(ccmp_hbm.at[l, page], cmp_buf.at[into_slot, k], sem.at[into_slot, 0]))
                copies.append(pltpu.make_async_copy(cslc_hbm.at[l, page], slc_buf.at[into_slot, k], sem.at[into_slot, 1]))
        return copies

    @pl.when(step == 0)
    def _():
        for c in page_copies(0, 0, True):
            c.start()

    @pl.when(step + 1 < pl.num_programs(0))
    def _():
        for c in page_copies(step + 1, 1 - slot, True):
            c.start()

    for c in page_copies(step, slot, False):
        c.wait()
    cmp_pages = [[cmp_buf.at[slot, u * n_pages + j] for j in range(n_pages)] for u in range(nseq)]
    slc_pages = [[slc_buf.at[slot, u * n_pages + j] for j in range(n_pages)] for u in range(nseq)]
    geo = _sample_geometry(past)
    n_pc_pad, n_blk, n_slc = geo["n_pc_pad"], geo["n_blk"], geo["n_slc"]
    pos = past
    lb = cwin_ref.shape[-1]
    scale = HEAD_DIM ** -0.5
    head = lax.broadcasted_iota(jnp.int32, (N_HEADS, 1), 0)
    lane = lax.broadcasted_iota(jnp.int32, (1, GD), 1)
    lane_h = lax.broadcasted_iota(jnp.int32, (N_HEADS, GD), 1)
    own = (lane_h // HEAD_DIM) == (head // N_REP)

    def new_row(u, branch, kv):
        o = (branch * 2 + kv) * GD
        return new_ref[u, :, o:o + GD]

    def rounded(x):
        return x.astype(BF16).astype(F32)

    kcv = []
    pages_per_span = FLAT_SPAN // PAGE
    past_pc = past // CMP_STRIDE
    for u in range(nseq):
        row0 = u * n_pc_pad
        for t in range(n_pages // pages_per_span):
            x_t = jnp.concatenate([cmp_pages[u][t * pages_per_span + w][...].reshape(2 * GD, PAGE)
                                   for w in range(pages_per_span)], axis=1)
            _fill_flat(x_t.astype(BF16), perm_ref[...], flat_ref, row0 + t * CMP_STRIDE)
        for kv in range(2):
            flat_ref[kv, row0 + past_pc:row0 + n_pc_pad, :] = jnp.zeros((n_pc_pad - past_pc, CMP_FLAT), BF16)
            flat_ref[kv, row0 + past_pc:row0 + past_pc + 1, 0:GD] = new_row(u, 0, kv).astype(BF16)
    for kv in range(2):
        kcv.append(_compress_mlp(flat_ref[kv], w1ab_ref[kv], bias_ref[kv], w2_ref[kv]).astype(BF16))
    for u in range(nseq):
        _attn_sample_one(u, kcv[0][u * n_pc_pad:(u + 1) * n_pc_pad], kcv[1][u * n_pc_pad:(u + 1) * n_pc_pad],
                         q_ref, new_row, rounded, gates_ref, wincol_ref, cwin_ref, ov_ref,
                         slc_pages[u], o_ref, wout_ref, geo=geo, past=past, consts=(head, lane, lane_h, own))


def _attn_sample_one(u, kc, vc, q_ref, new_row, rounded, gates_ref, wincol_ref, cwin_ref, ov_ref,
                     slc_pages, o_ref, wout_ref, *, geo, past, consts):
    head, lane, lane_h, own = consts
    n_pc_pad, n_blk, n_slc = geo["n_pc_pad"], geo["n_blk"], geo["n_slc"]
    n_pages = len(slc_pages)
    pos = past
    lb = cwin_ref.shape[-1]
    scale = HEAD_DIM ** -0.5
    q_rows = jnp.concatenate([q_ref[u, :, h * HEAD_DIM:(h + 1) * HEAD_DIM] for h in range(N_HEADS)], axis=0)
    qx = (jnp.where(own, jnp.concatenate([q_rows] * N_KV_HEADS, axis=1), 0.0) * scale).astype(BF16)
    qx32 = qx.astype(F32)
    c_idx = lax.broadcasted_iota(jnp.int32, (1, n_pc_pad), 1)
    valid = (c_idx * CMP_STRIDE + (CMP_BLOCK - 1) <= pos) & (c_idx < n_blk)
    p_c = _masked_softmax(_dot_nt(qx, kc), valid)
    o_c = _dot(p_c.astype(BF16), vc)

    cur = pos // SLC_BLOCK
    forced = (lane == 0) | (lane == cur) | (lane == cur - 1)
    jp = lax.broadcasted_iota(jnp.int32, (GD, GD), 0)
    jj = lax.broadcasted_iota(jnp.int32, (GD, GD), 1)
    sels = []
    for g in range(N_KV_HEADS):
        pc_sum = jnp.sum(p_c[g * N_REP:(g + 1) * N_REP], axis=0, keepdims=True)
        pc_hi = pc_sum.astype(BF16)
        pc_lo = (pc_sum - pc_hi.astype(F32)).astype(BF16)
        imp = jnp.sum(_dot(jnp.concatenate([pc_hi, pc_lo], axis=0), ov_ref[...]), axis=0, keepdims=True)
        score = jnp.where(lane <= cur, jnp.where(forced, BIG, imp), -BIG)
        score = jnp.where(lane < n_slc, score, -2.0 * BIG)
        score_b = jnp.broadcast_to(score, (GD, GD))
        score_a = score_b.T
        beats = (score_a > score_b) | ((score_a == score_b) & (jp < jj))
        rank = jnp.sum(beats.astype(F32), axis=0, keepdims=True)
        sel = ((rank < float(min(N_SEL, n_slc))) & (lane < n_slc)).astype(F32)
        sels.append(jnp.broadcast_to(sel, (N_REP, GD)))
    sel_h = jnp.concatenate(sels, axis=0)

    blocks_per_tile = GD // SLC_BLOCK
    chosen = []
    for t in range(past // GD):
        tile = sel_h[:, t * blocks_per_tile:t * blocks_per_tile + 1]
        for b in range(1, blocks_per_tile):
            tile = jnp.where(lane >= b * SLC_BLOCK, sel_h[:, t * blocks_per_tile + b:t * blocks_per_tile + b + 1], tile)
        chosen.append(jnp.broadcast_to(tile, (N_HEADS, GD)))
    chosen = jnp.concatenate(chosen, axis=-1) > 0.5
    k_past = jnp.concatenate([slc_pages[j][0].astype(BF16) for j in range(n_pages)], axis=-1)
    s_past = jnp.where(chosen, _dot(qx, k_past), NEG)
    new_ok = jnp.sum(jnp.where(lane == pos // SLC_BLOCK, sel_h, 0.0), axis=-1, keepdims=True) > 0.5
    s_new = jnp.where(new_ok, jnp.sum(qx32 * rounded(new_row(u, 1, 0)), axis=-1, keepdims=True), NEG)
    m = jnp.maximum(jnp.max(s_past, axis=-1, keepdims=True), s_new)
    p_past = jnp.where(chosen, jnp.exp(s_past - m), 0.0)
    p_new = jnp.where(new_ok, jnp.exp(s_new - m), 0.0)
    denom = jnp.maximum(jnp.sum(p_past, axis=-1, keepdims=True) + p_new, 1e-30)
    o_s = rounded(p_new) * rounded(new_row(u, 1, 1))
    for j in range(n_pages):
        o_s = o_s + _dot_nt(p_past[:, j * PAGE:(j + 1) * PAGE].astype(BF16), slc_pages[j][1].astype(BF16))
    o_s = o_s / denom

    kpos = (past - lb) + lax.broadcasted_iota(jnp.int32, (1, lb), 1)
    w_ok = (kpos <= pos) & (kpos > pos - WINDOW)
    s_w = jnp.where(w_ok, _dot(qx, cwin_ref[u, 0].astype(BF16)), NEG)
    s_wn = jnp.sum(qx32 * rounded(new_row(u, 2, 0)), axis=-1, keepdims=True)
    m = jnp.maximum(jnp.max(s_w, axis=-1, keepdims=True), s_wn)
    p_w = jnp.where(w_ok, jnp.exp(s_w - m), 0.0)
    p_wn = jnp.exp(s_wn - m)
    denom = jnp.maximum(jnp.sum(p_w, axis=-1, keepdims=True) + p_wn, 1e-30)
    o_w = (_dot_nt(p_w.astype(BF16), cwin_ref[u, 1].astype(BF16))
           + rounded(p_wn) * rounded(new_row(u, 2, 1))) / denom

    gates = gates_ref[u]
    gate_id = lax.broadcasted_iota(jnp.int32, (N_HEADS, GATE_ROWS), 1)
    gate = lambda br: jnp.sum(jnp.where(gate_id == head + br * N_HEADS, gates, 0.0), axis=-1, keepdims=True)
    o = gate(0) * o_c + gate(1) * o_s + gate(2) * o_w
    for h in range(N_HEADS):
        g = h // N_REP
        o_ref[u, :, h * HEAD_DIM:(h + 1) * HEAD_DIM] = o[h:h + 1, g * HEAD_DIM:(g + 1) * HEAD_DIM]

    for kv in range(2):
        wout_ref[u, kv, :, 0:lb - 1] = cwin_ref[u, kv, :, 1:lb]
        wout_ref[u, kv, :, lb - 1:lb] = wincol_ref[u, kv * GD:(kv + 1) * GD, :]


def _attn_sample(l, q_tok, new_tok, gates, win_col, cwin_t, ccmp_t, cslc_t, page_table, w1ab, bias, w2, win_prev):
    nb = q_tok.shape[0]
    n_pages = page_table.shape[1]
    past = n_pages * PAGE
    geo = _sample_geometry(past)
    lb = cwin_t.shape[-1]
    assert FLAT_SPAN % PAGE == 0 and n_pages % (FLAT_SPAN // PAGE) == 0
    perm = _piece_perm()
    ov = _overlap_t(geo["n_pc_pad"], geo["n_blk"], GD).T
    ov = jnp.where(jnp.arange(GD)[None, :] < geo["n_slc"], ov, 0).astype(BF16)
    assert GD % SLC_BLOCK == 0 and past % GD == 0
    nseq = SAMPLE_SEQS if nb % SAMPLE_SEQS == 0 else 1
    seq = lambda a: pl.BlockSpec((nseq,) + a.shape[1:], lambda i, pt: (i,) + (0,) * (a.ndim - 1))
    full = lambda a: pl.BlockSpec(a.shape, lambda i, pt: (0,) * a.ndim)
    operands = [page_table, q_tok, new_tok, gates, win_col, cwin_t, ov, perm, w1ab, bias, w2, ccmp_t, cslc_t]
    win_prev = () if win_prev is None else (win_prev,)
    page_buf = pltpu.VMEM((2, nseq * n_pages, 2, GD, PAGE), F32)
    grid_spec = pltpu.PrefetchScalarGridSpec(
        num_scalar_prefetch=1,
        grid=(nb // nseq,),
        in_specs=[seq(q_tok), seq(new_tok), seq(gates), seq(win_col),
                  pl.BlockSpec((None, nseq, 2, GD, lb), lambda i, pt: (l, i, 0, 0, 0)),
                  full(ov), full(perm), full(w1ab), full(bias), full(w2)]
                 + [pl.BlockSpec(memory_space=pl.ANY)] * (2 + len(win_prev)),
        out_specs=[pl.BlockSpec((nseq, 1, D_ATTN), lambda i, pt: (i, 0, 0)),
                   pl.BlockSpec((None, nseq, 2, GD, lb), lambda i, pt: (l, i, 0, 0, 0))],
        scratch_shapes=[pltpu.VMEM((2, nseq * geo["n_pc_pad"], CMP_FLAT), BF16), page_buf, page_buf,
                        pltpu.SemaphoreType.DMA((2, 2))],
    )
    return pl.pallas_call(
        functools.partial(_attn_sample_kernel, l=l, past=past, n_pages=n_pages, nseq=nseq),
        grid_spec=grid_spec,
        out_shape=[jax.ShapeDtypeStruct((nb, 1, D_ATTN), F32),
                   jax.ShapeDtypeStruct((cwin_t.shape[0], nb, 2, GD, lb), F32)],
        input_output_aliases={len(operands): 1} if win_prev else {},
        compiler_params=pltpu.CompilerParams(dimension_semantics=("arbitrary",), vmem_limit_bytes=VMEM_LIMIT),
        name="attn_sample",
    )(*operands, *win_prev)


def _prep_layer(l, w_norm, w_in, w_out, w_dw, b_dw, ln_g, ln_b, w_pw, b_pw, w_pool, pool_scale,
                g_q, g_k, cmp_pe, cmp_w1, cmp_w2):
    sizes = (D_CONV, D_CONV, D_CONV, D_POOL, D_POOL, D_ATTN, D_KV, N_BRANCH * N_HEADS, D_ATTN)
    offs = [0] + [int(v) for v in np.cumsum(sizes)]
    col = lambda k: w_in[l][:, offs[k]:offs[k + 1]]
    a_val, a_gate, z_a, b_in, z_b, q, kv, gate, z_c = (col(k) for k in range(9))
    pad = jnp.zeros((D_MODEL, GATE_ROWS - N_BRANCH * N_HEADS), F32)
    wtok = jnp.concatenate([a_val, a_gate, b_in, z_a, z_b, z_c], axis=1).astype(BF16)
    wfeat = jnp.concatenate([q, kv, gate, pad], axis=1).T.astype(BF16)
    ones = jnp.ones((N_KV_HEADS * HEAD_DIM,), F32)
    gcol = jnp.concatenate([jnp.tile(g_q[l], N_HEADS)]
                           + [piece for br in range(N_BRANCH)
                              for piece in (jnp.tile(g_k[l, br], N_KV_HEADS), ones)])[:, None]
    wpool_bd = jnp.zeros((D_POOL, D_POOL), F32)
    for gi in range(len(POOL_WINDOWS)):
        wpool_bd = wpool_bd.at[gi * POOL_GROUP:(gi + 1) * POOL_GROUP, gi * POOL_GROUP:(gi + 1) * POOL_GROUP].set(w_pool[l, gi])
    row = lambda v: v[None, :]

    def both_groups(w):
        z = jnp.zeros_like(w)
        return jnp.stack([jnp.concatenate([w, z], axis=2), jnp.concatenate([z, w], axis=2)], axis=1).reshape(-1, GD)

    half = CMP_STRIDE * HEAD_DIM
    w1 = cmp_w1[l].astype(BF16)
    w1ab = jnp.stack([jnp.concatenate(
        [both_groups(w1[kv, :half].reshape(CMP_STRIDE, HEAD_DIM, HEAD_DIM)),
         both_groups(w1[kv, half:].reshape(CMP_STRIDE, HEAD_DIM, HEAD_DIM))], axis=1) for kv in range(2)])
    w2p = jnp.stack([both_groups(cmp_w2[l, kv].astype(BF16)[None]) for kv in range(2)])
    cbias = jnp.tile(_cmp_bias(cmp_pe[l].reshape(2, 1, CMP_BLOCK * HEAD_DIM), w1), (1, 1, N_KV_HEADS))
    return dict(
        wnorm=row(w_norm[l]), wtok=wtok, wfeat=wfeat, gcol=gcol,
        wdw=w_dw[l], bdw=row(b_dw[l]), lng=row(ln_g[l]), lnb=row(ln_b[l]),
        wpw=w_pw[l].astype(BF16), bpw=row(b_pw[l]),
        wpool=wpool_bd.astype(BF16), pscale=row(pool_scale[l]),
        w1ab=w1ab, cbias=cbias, w2p=w2p,
        wout=w_out[l].astype(BF16),
    )


def _rope_tables(pos):
    inv = ROPE_THETA ** (-jnp.arange(ROT_HALF, dtype=F32) * 2.0 / ROT_DIM)
    ang = pos.astype(F32)[:, None] * inv[None, :]
    return jnp.cos(ang).T, jnp.sin(ang).T


def _prompt_layer(l, depth, x, p, cos_t, sin_t, kv_prev):
    glu, b_in, sz, gates, q_t, *kv = _in_proj(
        x, p["wnorm"], p["wtok"], p["wfeat"], p["gcol"], cos_t, sin_t, min(IN_PROJ_ROWS, x.shape[1]), l, depth, kv_prev)
    cmp_t, slc_t, win_t = kv
    kcv = _compress_prompt(cmp_t, l, p["w1ab"], p["cbias"], p["w2p"])
    o_attn = _attn_prompt(q_t, kcv, slc_t, win_t, gates, l)
    y = _mix_out_prompt(x, glu, b_in, sz, o_attn, p["wdw"], p["bdw"], p["lng"], p["lnb"], p["wpw"], p["bpw"],
                        p["wpool"], p["pscale"], p["wout"], t=min(MIX_OUT_ROWS, x.shape[1]))
    return y, glu, b_in, kv


def _sample_layer(l, depth, x, p, cos_t, sin_t, sc, sp, cwin_t, ccmp_t, cslc_t, page_table, past, kv_prev, win_prev):
    nb = x.shape[0]
    glu, b_in, sz, gates, q_t, *kv = _in_proj(
        x[None], p["wnorm"], p["wtok"], p["wfeat"], p["gcol"], cos_t, sin_t, nb, l, depth, kv_prev)
    glu, b_in, sz, gates = glu[0], b_in[0], sz[0], gates[0]
    yab, sc_new, sp_new = _convpool_sample(sc, sp, glu, b_in, sz, p["wdw"], p["bdw"], p["lng"], p["lnb"],
                                           p["wpw"], p["bpw"], p["wpool"], p["pscale"], pos=past)
    new_t = jnp.concatenate([a[l, 0] for a in kv], axis=0)
    new_tok = new_t.T[:, None, :]
    q_tok = q_t[0].T[:, None, :]
    win_col = kv[2][l, 0].T[:, :, None]
    o_attn, win_next = _attn_sample(l, q_tok, new_tok, gates.T[:, None, :], win_col, cwin_t, ccmp_t, cslc_t,
                                    page_table, p["w1ab"], p["cbias"], p["w2p"], win_prev)
    y = _out_proj(x[None], yab[None], o_attn.reshape(1, nb, D_ATTN), sz[None], p["wout"], tm=nb)[0]
    return y, sc_new, sp_new, win_next, kv


def _rows_from_feat(a):
    lead = a.shape[:-2]
    a = a.reshape(lead + (2, N_KV_HEADS, HEAD_DIM, a.shape[-1]))
    nl = len(lead)
    return jnp.transpose(a, tuple(range(nl)) + (nl + 3, nl, nl + 1, nl + 2))


def kernel(x_prompt, x_sample, state_conv, state_pool, cache_win_kv, cache_cmp_kv, cache_slc_kv, page_table,
           w_norm, w_in, w_out, w_dw, b_dw, ln_g, ln_b, w_pw, b_pw, w_pool, pool_scale,
           g_q, g_k, cmp_pe, cmp_w1, cmp_w2):
    bp, s, _ = x_prompt.shape
    bs, t_new, _ = x_sample.shape
    depth = w_in.shape[0]
    assert t_new == 1 and cache_cmp_kv.shape[2] == PAGE and s % SEL_KEYS == 0 and s % MIX_OUT_ROWS == 0
    past = page_table.shape[1] * PAGE
    lb = cache_win_kv.shape[2]

    def to_feat(c):
        c = jnp.transpose(c, (0, 1, 3, 4, 5, 2))
        return c.reshape(c.shape[:3] + (GD, c.shape[-1]))
    cwin_t, ccmp_t, cslc_t = to_feat(cache_win_kv), to_feat(cache_cmp_kv), to_feat(cache_slc_kv)
    sc_all = jnp.transpose(state_conv, (0, 2, 1, 3))
    sp_all = jnp.transpose(state_pool, (0, 2, 1, 3))
    cos_p, sin_p = _rope_tables(jnp.arange(s))
    cos_s, sin_s = _rope_tables(jnp.full((bs,), past))

    xp, xs = x_prompt, x_sample[:, 0, :]
    conv_p, pool_p, conv_s, pool_s = [], [], [], []
    kv_p = kv_s = win_s = None
    for l in range(depth):
        p = _prep_layer(l, w_norm, w_in, w_out, w_dw, b_dw, ln_g, ln_b, w_pw, b_pw, w_pool, pool_scale,
                        g_q, g_k, cmp_pe, cmp_w1, cmp_w2)
        xp, glu, b_in, kv_p = _prompt_layer(l, depth, xp, p, cos_p, sin_p, kv_p)
        conv_p.append(glu[:, s - CONV_STATE:, :])
        pool_p.append(b_in[:, s - POOL_STATE:, :])
        xs, sc_new, sp_new, win_s, kv_s = _sample_layer(
            l, depth, xs, p, cos_s, sin_s, sc_all[l], sp_all[l], cwin_t, ccmp_t, cslc_t, page_table, past,
            kv_s, win_s)
        conv_s.append(jnp.transpose(sc_new, (1, 0, 2)))
        pool_s.append(jnp.transpose(sp_new, (1, 0, 2)))
    cmp_p, slc_p, win_p = kv_p
    cmp_s, slc_s, _ = kv_s
    new_rows = lambda a: _rows_from_feat(a[:, 0])[:, :, None]
    return (xp, xs[:, None, :], jnp.stack(conv_p), jnp.stack(pool_p),
            _rows_from_feat(win_p[:, :, :, s - min(WINDOW, s):]), _rows_from_feat(cmp_p), _rows_from_feat(slc_p),
            jnp.stack(conv_s), jnp.stack(pool_s), _rows_from_feat(win_s.reshape(depth, bs, D_ROWS, lb)),
            new_rows(cmp_s), new_rows(slc_s))
```

```python
import functools

import numpy as np
import jax
import jax.numpy as jnp
from jax import lax
from jax.experimental import pallas as pl
from jax.experimental.pallas import tpu as pltpu

D_MODEL = 1024
D_CONV = 256
D_POOL = 256
N_HEADS = 8
N_KV_HEADS = 2
HEAD_DIM = 64
D_ATTN = N_HEADS * HEAD_DIM
CONV_WIDTH = 31
CONV_STATE = CONV_WIDTH - 1
POOL_WINDOWS = (2, 4, 8, 16)
POOL_GROUP = D_POOL // len(POOL_WINDOWS)
POOL_STATE = max(POOL_WINDOWS) - 1
CMP_BLOCK = 32
CMP_STRIDE = 16
SLC_BLOCK = 64
N_SEL = 8
WINDOW = 256
Q_BLOCK = 256
ROT_DIM = HEAD_DIM // 4
ROT_HALF = ROT_DIM // 2
ROPE_THETA = 500000.0
N_BRANCH = 3
D_KV = N_BRANCH * 2 * N_KV_HEADS * HEAD_DIM
D_ROWS = 2 * N_KV_HEADS * HEAD_DIM
N_REP = N_HEADS // N_KV_HEADS
EPS = 1e-6
NEG = -1e30
BIG = 1e4

GATE_ROWS = 32
D_TOK = 3 * D_CONV + 2 * D_POOL + D_ATTN
D_QKV = D_ATTN + D_KV
D_FEAT = D_QKV + GATE_ROWS
N_FEAT_BLOCKS = D_QKV // HEAD_DIM
SEL_KEYS = 512
IN_PROJ_ROWS = 1024
MIX_OUT_ROWS = 512
VMEM_V7X = 64 * 1024 * 1024
VMEM_LIMIT = VMEM_V7X * 7 // 8

F32 = jnp.float32
BF16 = jnp.bfloat16
NT_DIMS = (((1,), (1,)), ((), ()))


def _sigmoid(x):
    return 1.0 / (1.0 + jnp.exp(-x))


def _silu(x):
    return x * _sigmoid(x)


def _gelu_tanh(x):
    return 0.5 * x * (1.0 + jnp.tanh(np.sqrt(2.0 / np.pi).astype(np.float32) * (x + 0.044715 * (x * x * x))))


def _dot(a, b):
    return jnp.dot(a, b, preferred_element_type=F32)


def _dot_nt(a, b):
    return lax.dot_general(a, b, NT_DIMS, preferred_element_type=F32)


def _masked_softmax(s, mask):
    s = jnp.where(mask, s, NEG)
    m = jnp.max(s, axis=-1, keepdims=True)
    p = jnp.where(mask, jnp.exp(s - m), 0.0)
    return p / jnp.maximum(jnp.sum(p, axis=-1, keepdims=True), 1e-30)


def _in_proj_kernel(x_ref, wnorm_ref, wtok_ref, wfeat_ref, gcol_ref, cos_ref, sin_ref, *rest):
    glu_ref, bin_ref, sz_ref, gates_ref, qt_ref, cmp_ref, slc_ref, win_ref = rest[-8:]
    x = x_ref[...]
    ms = jnp.mean(x * x, axis=-1, keepdims=True)
    h = (x * lax.rsqrt(ms + EPS) * wnorm_ref[...]).astype(BF16)

    a = _dot(h, wtok_ref[...])
    o = 0
    glu_ref[...] = a[:, o:o + D_CONV] * _sigmoid(a[:, o + D_CONV:o + 2 * D_CONV])
    o += 2 * D_CONV
    bin_ref[...] = a[:, o:o + D_POOL]
    o += D_POOL
    z = a[:, o:o + D_CONV + D_POOL + D_ATTN]
    sz_ref[...] = _silu(z)

    f = _dot_nt(wfeat_ref[...], h)
    gates_ref[...] = _sigmoid(f[D_QKV:D_FEAT, :])
    cos = cos_ref[...]
    sin = sin_ref[...]
    kv_refs = (cmp_ref, slc_ref, win_ref)
    for hb in range(N_FEAT_BLOCKS):
        blk = f[hb * HEAD_DIM:(hb + 1) * HEAD_DIM, :]
        kv_blk = hb - N_HEADS
        is_value = kv_blk >= 0 and (kv_blk % (2 * N_KV_HEADS)) >= N_KV_HEADS
        if not is_value:
            bms = jnp.mean(blk * blk, axis=0, keepdims=True)
            y = blk * lax.rsqrt(bms + EPS) * gcol_ref[hb * HEAD_DIM:(hb + 1) * HEAD_DIM, :]
            x1 = y[0:ROT_HALF]
            x2 = y[ROT_HALF:ROT_DIM]
            blk = jnp.concatenate([x1 * cos - x2 * sin, x2 * cos + x1 * sin, y[ROT_DIM:]], axis=0)
        if kv_blk < 0:
            qt_ref[hb * HEAD_DIM:(hb + 1) * HEAD_DIM, :] = blk
        else:
            r = kv_blk % (2 * N_KV_HEADS)
            kv_refs[kv_blk // (2 * N_KV_HEADS)][r * HEAD_DIM:(r + 1) * HEAD_DIM, :] = blk


def _in_proj(x, wnorm, wtok, wfeat, gcol, cos_t, sin_t, tm, l, depth, kv_prev):
    n, t, _ = x.shape
    assert t % tm == 0
    grid = (n, t // tm)
    tok = lambda w: pl.BlockSpec((None, tm, w), lambda i, j: (i, j, 0))
    feat = lambda w: pl.BlockSpec((None, w, tm), lambda i, j: (i, 0, j))
    kv_spec = pl.BlockSpec((None, None, D_ROWS, tm), lambda i, j: (l, i, 0, j))
    full = lambda a: pl.BlockSpec(a.shape, lambda i, j: (0,) * a.ndim)
    tok_shape = lambda w: jax.ShapeDtypeStruct((n, t, w), F32)
    kv_shape = jax.ShapeDtypeStruct((depth, n, D_ROWS, t), F32)
    n_in = 7
    kv_prev = () if kv_prev is None else tuple(kv_prev)
    return pl.pallas_call(
        _in_proj_kernel,
        grid=grid,
        in_specs=[tok(D_MODEL), full(wnorm), full(wtok), full(wfeat), full(gcol),
                  pl.BlockSpec((ROT_HALF, tm), lambda i, j: (0, j)),
                  pl.BlockSpec((ROT_HALF, tm), lambda i, j: (0, j))]
                 + [pl.BlockSpec(memory_space=pl.ANY)] * len(kv_prev),
        out_specs=[tok(D_CONV), tok(D_POOL), tok(D_CONV + D_POOL + D_ATTN), feat(GATE_ROWS),
                   feat(D_ATTN), kv_spec, kv_spec, kv_spec],
        out_shape=[tok_shape(D_CONV), tok_shape(D_POOL), tok_shape(D_CONV + D_POOL + D_ATTN),
                   jax.ShapeDtypeStruct((n, GATE_ROWS, t), F32),
                   jax.ShapeDtypeStruct((n, D_ATTN, t), F32), kv_shape, kv_shape, kv_shape],
        input_output_aliases={n_in + k: 5 + k for k in range(len(kv_prev))},
        compiler_params=pltpu.CompilerParams(dimension_semantics=("arbitrary", "arbitrary"),
                                             vmem_limit_bytes=VMEM_LIMIT),
        name="in_proj",
    )(x, wnorm, wtok, wfeat, gcol, cos_t, sin_t, *kv_prev)


def _conv_tail(acc, sz_a, bdw, lng, lnb, wpw, bpw):
    y = acc + bdw
    mu = jnp.mean(y, axis=-1, keepdims=True)
    yc = y - mu
    var = jnp.mean(yc * yc, axis=-1, keepdims=True)
    y = yc * lax.rsqrt(var + EPS) * lng + lnb
    y = _dot(_silu(y).astype(BF16), wpw) + bpw
    return y * sz_a


def _pool_tail(total, cnt, xcur, sz_b, wpool, pscale):
    d = total / cnt - xcur
    y = _dot(d.astype(BF16), wpool) * pscale
    return y * sz_b


def _pool_window_of_lane(shape):
    lane = lax.broadcasted_iota(jnp.int32, shape, len(shape) - 1)
    w = jnp.full(shape, POOL_WINDOWS[0], jnp.int32)
    for gi in range(1, len(POOL_WINDOWS)):
        w = jnp.where(lane >= gi * POOL_GROUP, POOL_WINDOWS[gi], w)
    return w


def _pool_select(sums, shape):
    lane = lax.broadcasted_iota(jnp.int32, shape, len(shape) - 1)
    total = sums[0]
    for gi in range(1, len(POOL_WINDOWS)):
        total = jnp.where(lane >= gi * POOL_GROUP, sums[gi], total)
    return total


CONV_HALO = 32
POOL_HALO = 16


MIX_ROWS = 256
SUBLANES = 8


def _tap_rows(buf, w_ref, r0, rows, off, taps, lanes):
    y = None
    for c in range(SUBLANES):
        part = None
        for k in taps:
            if (k + off) % SUBLANES != c:
                continue
            base = r0 + k + off - c
            term = w_ref[k:k + 1, lanes] * buf[base:base + rows + SUBLANES, lanes]
            part = term if part is None else part + term
        if part is not None:
            y = part[c:c + rows] if y is None else y + part[c:c + rows]
    return y


def _mix_out_prompt_kernel(x_ref, glu_ref, gprev_ref, bin_ref, bprev_ref, sz_ref, oat_ref, wdw_ref, bdw_ref, lng_ref,
                           lnb_ref, wpw_ref, bpw_ref, wpool_ref, pscale_ref, wout_ref, o_ref,
                           cbuf, pbuf, ymix, *, t):
    i = pl.program_id(1)
    keep = (i > 0).astype(F32)
    cbuf[0:CONV_HALO, :] = gprev_ref[...] * keep
    cbuf[CONV_HALO:CONV_HALO + t, :] = glu_ref[...]
    cbuf[CONV_HALO + t:CONV_HALO + t + SUBLANES, :] = jnp.zeros((SUBLANES, D_CONV), F32)
    pbuf[0:POOL_HALO, :] = bprev_ref[...] * keep
    pbuf[POOL_HALO:POOL_HALO + t, :] = bin_ref[...]
    nab = D_CONV + D_POOL
    for r0 in range(0, t, MIX_ROWS):
        rows = slice(r0, r0 + MIX_ROWS)
        sz = sz_ref[rows, :]
        acc = _tap_rows(cbuf, wdw_ref, r0, MIX_ROWS, CONV_HALO - CONV_STATE, range(CONV_WIDTH), slice(0, D_CONV))
        y_a = _conv_tail(acc, sz[:, 0:D_CONV], bdw_ref[...], lng_ref[...], lnb_ref[...], wpw_ref[...], bpw_ref[...])
        ymix[rows, 0:D_CONV] = y_a.astype(BF16)

        sums = []
        run = jnp.zeros((MIX_ROWS, D_POOL), F32)
        j = 0
        for w in POOL_WINDOWS:
            while j < w:
                run = run + pbuf[pl.ds(POOL_HALO + r0 - j, MIX_ROWS), :]
                j += 1
            sums.append(run)
        total = _pool_select(sums, (MIX_ROWS, D_POOL))
        pos = i * t + r0 + lax.broadcasted_iota(jnp.int32, (MIX_ROWS, D_POOL), 0)
        cnt = jnp.minimum(_pool_window_of_lane((MIX_ROWS, D_POOL)), pos + 1).astype(F32)
        y_b = _pool_tail(total, cnt, bin_ref[rows, :], sz[:, D_CONV:nab], wpool_ref[...], pscale_ref[...])
        ymix[rows, D_CONV:nab] = y_b.astype(BF16)
        ymix[rows, nab:nab + D_ATTN] = (oat_ref[rows, :] * sz[:, nab:nab + D_ATTN]).astype(BF16)
    o_ref[...] = x_ref[...] + _dot(ymix[...], wout_ref[...])


def _mix_out_prompt(x, glu, b_in, sz, o_attn, wdw, bdw, lng, lnb, wpw, bpw, wpool, pscale, wout, t):
    n, s, _ = glu.shape
    assert t % MIX_ROWS == 0 and s % t == 0
    grid = (n, s // t)
    cur = lambda w: pl.BlockSpec((None, t, w), lambda i, j: (i, j, 0))
    prev = lambda rows, w: pl.BlockSpec((None, rows, w), lambda i, j: (i, jnp.maximum(j * (t // rows) - 1, 0), 0))
    full = lambda a: pl.BlockSpec(a.shape, lambda i, j: (0,) * a.ndim)
    d_mix = D_CONV + D_POOL + D_ATTN
    return pl.pallas_call(
        functools.partial(_mix_out_prompt_kernel, t=t),
        grid=grid,
        in_specs=[cur(D_MODEL), cur(D_CONV), prev(CONV_HALO, D_CONV), cur(D_POOL), prev(POOL_HALO, D_POOL),
                  cur(d_mix), cur(D_ATTN),
                  full(wdw), full(bdw), full(lng), full(lnb), full(wpw), full(bpw), full(wpool),
                  full(pscale), full(wout)],
        out_specs=cur(D_MODEL),
        out_shape=jax.ShapeDtypeStruct(x.shape, F32),
        scratch_shapes=[pltpu.VMEM((CONV_HALO + t + SUBLANES, D_CONV), F32),
                        pltpu.VMEM((POOL_HALO + t, D_POOL), F32),
                        pltpu.VMEM((t, d_mix), BF16)],
        compiler_params=pltpu.CompilerParams(dimension_semantics=("arbitrary", "arbitrary"),
                                             vmem_limit_bytes=VMEM_LIMIT),
        name="mix_out_prompt",
    )(x, glu, glu, b_in, b_in, sz, o_attn, wdw, bdw, lng, lnb, wpw, bpw, wpool, pscale, wout)


def _convpool_sample_kernel(sc_ref, sp_ref, glu_ref, bin_ref, sz_ref, wdw_ref, bdw_ref, lng_ref, lnb_ref,
                            wpw_ref, bpw_ref, wpool_ref, pscale_ref, y_ref, sc_out, sp_out, *, pos):
    glu = glu_ref[...]
    acc = wdw_ref[CONV_STATE:CONV_WIDTH, :] * glu
    for k in range(CONV_STATE):
        acc = acc + wdw_ref[k:k + 1, :] * sc_ref[k]
    sz = sz_ref[...]
    y_ref[:, 0:D_CONV] = _conv_tail(acc, sz[:, 0:D_CONV], bdw_ref[...], lng_ref[...], lnb_ref[...],
                                    wpw_ref[...], bpw_ref[...])
    for k in range(CONV_STATE - 1):
        sc_out[k] = sc_ref[k + 1]
    sc_out[CONV_STATE - 1] = glu

    xcur = bin_ref[...]
    shape = xcur.shape
    sums = []
    run = xcur
    j = 1
    for w in POOL_WINDOWS:
        while j < w:
            run = run + sp_ref[POOL_STATE - j]
            j += 1
        sums.append(run)
    total = _pool_select(sums, shape)
    cnt = jnp.minimum(_pool_window_of_lane(shape), pos + 1).astype(F32)
    y_ref[:, D_CONV:D_CONV + D_POOL] = _pool_tail(total, cnt, xcur, sz[:, D_CONV:D_CONV + D_POOL],
                                                  wpool_ref[...], pscale_ref[...])
    for k in range(POOL_STATE - 1):
        sp_out[k] = sp_ref[k + 1]
    sp_out[POOL_STATE - 1] = xcur


def _convpool_sample(sc, sp, glu, b_in, sz, wdw, bdw, lng, lnb, wpw, bpw, wpool, pscale, pos):
    nb = glu.shape[0]
    return pl.pallas_call(
        functools.partial(_convpool_sample_kernel, pos=pos),
        out_shape=[jax.ShapeDtypeStruct((nb, D_CONV + D_POOL), F32),
                   jax.ShapeDtypeStruct(sc.shape, F32), jax.ShapeDtypeStruct(sp.shape, F32)],
        compiler_params=pltpu.CompilerParams(vmem_limit_bytes=VMEM_LIMIT),
        name="convpool_sample",
    )(sc, sp, glu, b_in, sz[:, 0:D_CONV + D_POOL], wdw, bdw, lng, lnb, wpw, bpw, wpool, pscale)


def _out_proj_kernel(x_ref, yab_ref, oat_ref, szc_ref, w_ref, o_ref):
    ya = yab_ref[...].astype(BF16)
    yc = (oat_ref[...] * szc_ref[...]).astype(BF16)
    nab = D_CONV + D_POOL
    o_ref[...] = x_ref[...] + _dot(ya, w_ref[0:nab, :]) + _dot(yc, w_ref[nab:nab + D_ATTN, :])


def _out_proj(x, yab, o_attn, sz, w_out, tm):
    n, t, _ = x.shape
    nab = D_CONV + D_POOL
    assert nab == D_ATTN
    grid = (n, t // tm)
    tok = lambda w, cb=0: pl.BlockSpec((None, tm, w), lambda i, j: (i, j, cb))
    return pl.pallas_call(
        _out_proj_kernel,
        grid=grid,
        in_specs=[tok(D_MODEL), tok(nab), tok(D_ATTN), tok(D_ATTN, 1),
                  pl.BlockSpec(w_out.shape, lambda i, j: (0, 0))],
        out_specs=tok(D_MODEL),
        out_shape=jax.ShapeDtypeStruct(x.shape, F32),
        compiler_params=pltpu.CompilerParams(dimension_semantics=("arbitrary", "arbitrary"),
                                             vmem_limit_bytes=VMEM_LIMIT),
        name="out_proj",
    )(x, yab, o_attn, sz, w_out)


GD = N_KV_HEADS * HEAD_DIM
CMP_FLAT = CMP_STRIDE * GD


FLAT_SPAN = CMP_STRIDE * CMP_STRIDE


def _piece_perm():
    m = np.arange(FLAT_SPAN)
    src = (m % CMP_STRIDE) * CMP_STRIDE + m // CMP_STRIDE
    return jnp.asarray((src[:, None] == np.arange(FLAT_SPAN)[None, :]).astype(np.float32), BF16)


def _fill_flat(x_t, perm, flat_ref, piece0):
    y = _dot_nt(perm, x_t).astype(BF16)
    for kv in range(2):
        for r in range(CMP_STRIDE):
            flat_ref[kv, piece0:piece0 + CMP_STRIDE, r * GD:(r + 1) * GD] = (
                y[r * CMP_STRIDE:(r + 1) * CMP_STRIDE, kv * GD:(kv + 1) * GD])


def _compress_mlp(flat, w1ab, bias, w2):
    n_pc = flat.shape[0]
    hb = _dot(flat, w1ab)
    h = hb[:, 0:GD] + jnp.concatenate([hb[1:n_pc, GD:2 * GD], jnp.zeros((1, GD), F32)], axis=0) + bias
    out = _dot(_gelu_tanh(h).astype(BF16), w2)
    row = lax.broadcasted_iota(jnp.int32, out.shape, 0)
    return jnp.where(row < n_pc - 1, out, 0.0)


def _cmp_bias_kernel(pe_ref, w1_ref, o_ref):
    for kv in range(2):
        o_ref[kv] = _dot(pe_ref[kv].astype(BF16), w1_ref[kv])


def _cmp_bias(pe_flat, w1):
    return pl.pallas_call(_cmp_bias_kernel, out_shape=jax.ShapeDtypeStruct((2, 1, HEAD_DIM), F32),
                          name="cmp_bias")(pe_flat, w1)


def _compress_prompt_kernel(x_ref, perm_ref, w1ab_ref, bias_ref, w2_ref, o_ref, flat_ref, *, s):
    for t in range(s // FLAT_SPAN):
        _fill_flat(x_ref[:, t * FLAT_SPAN:(t + 1) * FLAT_SPAN].astype(BF16), perm_ref[...], flat_ref, t * CMP_STRIDE)
    for kv in range(2):
        o_ref[:, kv * GD:(kv + 1) * GD] = _compress_mlp(flat_ref[kv], w1ab_ref[kv], bias_ref[kv], w2_ref[kv])


def _compress_prompt(cmp_t, l, w1ab, bias, w2):
    _, n, _, s = cmp_t.shape
    assert s % FLAT_SPAN == 0
    n_pc = s // CMP_STRIDE
    perm = _piece_perm()
    full = lambda a: pl.BlockSpec(a.shape, lambda i: (0,) * a.ndim)
    return pl.pallas_call(
        functools.partial(_compress_prompt_kernel, s=s),
        grid=(n,),
        in_specs=[pl.BlockSpec((None, None, D_ROWS, s), lambda i: (l, i, 0, 0)),
                  full(perm), full(w1ab), full(bias), full(w2)],
        out_specs=pl.BlockSpec((None, n_pc, 2 * GD), lambda i: (i, 0, 0)),
        out_shape=jax.ShapeDtypeStruct((n, n_pc, 2 * GD), F32),
        scratch_shapes=[pltpu.VMEM((2, n_pc, CMP_FLAT), BF16)],
        compiler_params=pltpu.CompilerParams(dimension_semantics=("arbitrary",), vmem_limit_bytes=VMEM_LIMIT),
        name="compress_prompt",
    )(cmp_t, perm, w1ab, bias, w2)


def _select_blocks(imp_t, pos_row, n_blocks):
    shape = imp_t.shape
    assert n_blocks % 8 == 0
    j = lax.broadcasted_iota(jnp.int32, shape, 0)
    cur = pos_row // SLC_BLOCK
    forced = (j == 0) | (j == cur) | (j == cur - 1)
    score = jnp.where(j <= cur, jnp.where(forced, BIG, imp_t), -BIG)
    tiles = [score[8 * v:8 * v + 8] for v in range(n_blocks // 8)]
    j8 = lax.broadcasted_iota(jnp.int32, (8, shape[1]), 0)
    ranks = [jnp.zeros((8, shape[1]), F32) for _ in tiles]
    for jp in range(n_blocks):
        sj = score[jp:jp + 1, :]
        for v, tile in enumerate(tiles):
            if jp < 8 * v:
                beats = sj >= tile
            elif jp >= 8 * v + 8:
                beats = sj > tile
            else:
                beats = (sj > tile) | ((sj == tile) & (j8 > jp - 8 * v))
            ranks[v] = ranks[v] + jnp.where(beats, 1.0, 0.0)
    rank = jnp.concatenate(ranks, axis=0)
    return (rank < float(min(N_SEL, n_blocks))).astype(F32)


def _col_softmax(s):
    m = jnp.max(s, axis=0, keepdims=True)
    e = jnp.exp(s - m)
    inv = jnp.where(m > 0.5 * NEG, 1.0 / jnp.maximum(jnp.sum(e, axis=0, keepdims=True), 1e-30), 0.0)
    return e, inv


ONES_ROWS = 16
WIN_TILE = 128
BLOCK_CODE = 64


def _with_ones(v_t):
    return jnp.concatenate([v_t, jnp.ones((ONES_ROWS, v_t.shape[1]), BF16)], axis=0)


def _normalised(acc):
    return acc[0:HEAD_DIM] * (1.0 / jnp.maximum(acc[HEAD_DIM:HEAD_DIM + 1], 1e-30))


def _mask_heads(sc, ok, tq):
    return jnp.concatenate([jnp.where(ok, sc[:, r * tq:(r + 1) * tq], NEG) for r in range(N_REP)], axis=1)


def _attn_prompt_kernel(qt_ref, kcv_ref, slc_ref, win_ref, gates_ref, ovt_ref, o_ref,
                        ks_ref, kw_ref, s_ref, acc_ref, m_ref, *, s, n_blk):
    tq = Q_BLOCK
    cols = N_REP * tq
    groups = range(N_KV_HEADS)
    qc = pl.program_id(1)
    q0 = qc * tq
    pos_row = q0 + lax.broadcasted_iota(jnp.int32, (1, tq), 1)
    n_slc = s // SLC_BLOCK
    n_cmp = kcv_ref.shape[0]
    scale = HEAD_DIM ** -0.5
    v_row0 = lambda g: (N_KV_HEADS + g) * HEAD_DIM

    @pl.when(qc == 0)
    def _():
        for g in range(N_KV_HEADS):
            for t in range(s // 128):
                cols_t = slice(t * 128, (t + 1) * 128)
                ks_ref[g, cols_t, 0:HEAD_DIM] = slc_ref[g * HEAD_DIM:(g + 1) * HEAD_DIM, cols_t].T.astype(BF16)
                block_of_key = (t * 128 + lax.broadcasted_iota(jnp.int32, (128, BLOCK_CODE), 0)) // SLC_BLOCK
                code = block_of_key == lax.broadcasted_iota(jnp.int32, (128, BLOCK_CODE), 1)
                ks_ref[g, cols_t, HEAD_DIM:HEAD_DIM + BLOCK_CODE] = jnp.where(code, 1.0, 0.0).astype(BF16)
                kw_ref[g, cols_t, :] = win_ref[g * HEAD_DIM:(g + 1) * HEAD_DIM, cols_t].T.astype(BF16)

    gates_t = gates_ref[...]
    vc_t = kcv_ref[:, GD:2 * GD].T.astype(BF16)
    q_ts = []
    for g in groups:
        q_t = jnp.concatenate([qt_ref[(g * N_REP + r) * HEAD_DIM:(g * N_REP + r + 1) * HEAD_DIM, :]
                               for r in range(N_REP)], axis=1)
        q_ts.append((q_t * scale).astype(BF16))

    c_idx = lax.broadcasted_iota(jnp.int32, (n_cmp, 1), 0)
    valid_c = (c_idx * CMP_STRIDE + (CMP_BLOCK - 1) <= pos_row) & (c_idx < n_blk)
    ovt = ovt_ref[...]
    o_cs, q_codes = [], []
    for g in groups:
        kc = kcv_ref[:, g * HEAD_DIM:(g + 1) * HEAD_DIM].astype(BF16)
        e_c, inv_c = _col_softmax(_mask_heads(_dot(kc, q_ts[g]), valid_c, tq))
        o_cs.append(_dot(vc_t[g * HEAD_DIM:(g + 1) * HEAD_DIM, :], e_c.astype(BF16)) * inv_c)
        pc_sum = e_c[:, 0:tq] * inv_c[:, 0:tq]
        for r in range(1, N_REP):
            pc_sum = pc_sum + e_c[:, r * tq:(r + 1) * tq] * inv_c[:, r * tq:(r + 1) * tq]
        pc_hi = pc_sum.astype(BF16)
        pc_lo = (pc_sum - pc_hi.astype(F32)).astype(BF16)
        imp_t = _dot(ovt, pc_hi) + _dot(ovt, pc_lo)
        sel_t = _select_blocks(imp_t, pos_row, n_slc)
        bias = jnp.concatenate([jnp.where(sel_t > 0.5, 0.0, NEG), jnp.zeros((BLOCK_CODE - n_slc, tq), F32)], axis=0)
        q_codes.append(jnp.concatenate([q_ts[g], jnp.concatenate([bias] * N_REP, axis=1).astype(BF16)], axis=0))

    n_keys = q0 + tq
    rest = n_keys % SEL_KEYS
    half_tail = (rest > 0) & (rest <= SEL_KEYS // 2)
    n_steps = n_keys // SEL_KEYS + (rest > SEL_KEYS // 2).astype(jnp.int32)
    tail_k0 = pl.multiple_of((n_keys // SEL_KEYS) * SEL_KEYS, SEL_KEYS)
    last_full = jnp.logical_not(half_tail)

    def score_chunk(k0, size, diagonal):
        key = k0 + lax.broadcasted_iota(jnp.int32, (size, 1), 0)
        for g in groups:
            sc = _dot(ks_ref[g, pl.ds(k0, size), :], q_codes[g])
            if diagonal:
                sc = _mask_heads(sc, key <= pos_row, tq)
            s_ref[g, pl.ds(k0, size), :] = sc.astype(BF16)
            m_ref[g] = jnp.maximum(m_ref[g], jnp.max(sc, axis=0, keepdims=True))

    m_ref[...] = jnp.full(m_ref.shape, NEG, F32)

    def score_step(i, carry):
        score_chunk(pl.multiple_of(i * SEL_KEYS, SEL_KEYS), SEL_KEYS, False)
        return carry

    lax.fori_loop(0, n_steps - last_full.astype(jnp.int32), score_step, 0)

    @pl.when(last_full)
    def _():
        score_chunk(pl.multiple_of((n_steps - 1) * SEL_KEYS, SEL_KEYS), SEL_KEYS, True)

    @pl.when(half_tail)
    def _():
        score_chunk(tail_k0, SEL_KEYS // 2, True)

    m_sel = [m_ref[g].astype(BF16) for g in groups]

    acc_ref[...] = jnp.zeros(acc_ref.shape, F32)

    def value_chunk(k0, size):
        for g in groups:
            p = jnp.exp(s_ref[g, pl.ds(k0, size), :] - m_sel[g])
            v_t = slc_ref[v_row0(g):v_row0(g) + HEAD_DIM, pl.ds(k0, size)].astype(BF16)
            acc_ref[g] += _dot(_with_ones(v_t), p)

    def value_step(i, carry):
        value_chunk(pl.multiple_of(i * SEL_KEYS, SEL_KEYS), SEL_KEYS)
        return carry

    lax.fori_loop(0, n_steps, value_step, 0)

    @pl.when(half_tail)
    def _():
        value_chunk(tail_k0, SEL_KEYS // 2)

    n_prev = -(-(WINDOW - 1) // WIN_TILE)
    sub_tiles = tq // WIN_TILE
    win_k0, win_ok = {}, {}
    for u in range(sub_tiles):
        pos_u = pos_row[:, u * WIN_TILE:(u + 1) * WIN_TILE]
        for b in range(n_prev + 1):
            kb = q0 // WIN_TILE + u - n_prev + b
            win_k0[u, b] = pl.multiple_of(jnp.maximum(kb, 0) * WIN_TILE, WIN_TILE)
            kp = kb * WIN_TILE + lax.broadcasted_iota(jnp.int32, (WIN_TILE, 1), 0)
            win_ok[u, b] = (kp <= pos_u) & (kp > pos_u - WINDOW) & (kp >= 0)

    out_blocks = []
    for g in groups:
        o_s = _normalised(acc_ref[g])
        o_ws = []
        for u in range(sub_tiles):
            q_u = jnp.concatenate([q_ts[g][:, r * tq + u * WIN_TILE:r * tq + (u + 1) * WIN_TILE]
                                   for r in range(N_REP)], axis=1)
            s_w = jnp.concatenate([_mask_heads(_dot(kw_ref[g, pl.ds(win_k0[u, b], WIN_TILE), :], q_u),
                                               win_ok[u, b], WIN_TILE) for b in range(n_prev + 1)], axis=0)
            e_w = jnp.exp(s_w.astype(BF16) - jnp.max(s_w, axis=0, keepdims=True).astype(BF16))
            o_w = None
            for b in range(n_prev + 1):
                v_t = win_ref[v_row0(g):v_row0(g) + HEAD_DIM, pl.ds(win_k0[u, b], WIN_TILE)].astype(BF16)
                part = _dot(_with_ones(v_t), e_w[b * WIN_TILE:(b + 1) * WIN_TILE])
                o_w = part if o_w is None else o_w + part
            o_ws.append(_normalised(o_w))
        for r in range(N_REP):
            h = g * N_REP + r
            c = slice(r * tq, (r + 1) * tq)
            o_w = jnp.concatenate([o_ws[u][:, r * WIN_TILE:(r + 1) * WIN_TILE] for u in range(sub_tiles)], axis=1)
            out_blocks.append(gates_t[h:h + 1] * o_cs[g][:, c] + gates_t[N_HEADS + h:N_HEADS + h + 1] * o_s[:, c]
                              + gates_t[2 * N_HEADS + h:2 * N_HEADS + h + 1] * o_w)
    o_ref[...] = jnp.concatenate(out_blocks, axis=0).T


def _overlap_t(n_cmp_rows, n_blk, n_slc):
    c = np.arange(n_cmp_rows)
    start = c * CMP_STRIDE
    end = start + CMP_BLOCK - 1
    j0 = np.arange(n_slc) * SLC_BLOCK
    ov = (end[None, :] >= j0[:, None]) & (start[None, :] < j0[:, None] + SLC_BLOCK) & (c[None, :] < n_blk)
    return jnp.asarray(ov.astype(np.float32), BF16)


def _attn_prompt(q_t, kcv, slc_t, win_t, gates, l):
    n, _, s = q_t.shape
    n_cmp = kcv.shape[1]
    n_blk = s // CMP_STRIDE - CMP_BLOCK // CMP_STRIDE + 1
    n_slc = s // SLC_BLOCK
    ovt = _overlap_t(n_cmp, n_blk, n_slc)
    tq = Q_BLOCK
    cols = N_REP * tq
    assert s % SEL_KEYS == 0 and (SEL_KEYS // 2) % tq == 0 and n_slc <= BLOCK_CODE
    return pl.pallas_call(
        functools.partial(_attn_prompt_kernel, s=s, n_blk=n_blk),
        grid=(n, s // tq),
        in_specs=[pl.BlockSpec((None, D_ATTN, tq), lambda i, j: (i, 0, j)),
                  pl.BlockSpec((None,) + kcv.shape[1:], lambda i, j: (i, 0, 0)),
                  pl.BlockSpec((None, None, D_ROWS, s), lambda i, j: (l, i, 0, 0)),
                  pl.BlockSpec((None, None, D_ROWS, s), lambda i, j: (l, i, 0, 0)),
                  pl.BlockSpec((None, GATE_ROWS, tq), lambda i, j: (i, 0, j)),
                  pl.BlockSpec(ovt.shape, lambda i, j: (0, 0))],
        out_specs=pl.BlockSpec((None, tq, D_ATTN), lambda i, j: (i, j, 0)),
        out_shape=jax.ShapeDtypeStruct((n, s, D_ATTN), F32),
        scratch_shapes=[pltpu.VMEM((N_KV_HEADS, s, HEAD_DIM + BLOCK_CODE), BF16),
                        pltpu.VMEM((N_KV_HEADS, s, HEAD_DIM), BF16),
                        pltpu.VMEM((N_KV_HEADS, s, cols), BF16),
                        pltpu.VMEM((N_KV_HEADS, HEAD_DIM + ONES_ROWS, cols), F32),
                        pltpu.VMEM((N_KV_HEADS, 1, cols), F32)],
        compiler_params=pltpu.CompilerParams(dimension_semantics=("arbitrary", "arbitrary"),
                                             vmem_limit_bytes=VMEM_LIMIT),
        name="attn_prompt",
    )(q_t, kcv, slc_t, win_t, gates, ovt)


PAGE = 128
SAMPLE_SEQS = 4
PV_PAGES = 2


def _sample_geometry(past):
    length = past + 1
    padded = -(-length // SLC_BLOCK) * SLC_BLOCK
    n_pc = padded // CMP_STRIDE
    n_pc_pad = -(-n_pc // CMP_STRIDE) * CMP_STRIDE
    return dict(n_pc=n_pc, n_pc_pad=n_pc_pad, n_blk=n_pc - CMP_BLOCK // CMP_STRIDE + 1, n_slc=padded // SLC_BLOCK)


def _attn_sample_kernel(pt_ref, q_ref, new_ref, gates_ref, wincol_ref, cwin_ref, ov_ref,
                        perm_ref, w1ab_ref, bias_ref, w2_ref, ccmp_hbm, cslc_hbm, *rest, l, past, n_pages, nseq):
    o_ref, wout_ref, flat_ref, cmp_buf, slc_buf, sem = rest[-6:]
    step = pl.program_id(0)
    slot = step % 2

    def page_copies(of_step, into_slot, lookup):
        copies = []
        for u in range(nseq):
            for j in range(n_pages):
                page = pt_ref[of_step * nseq + u, j] if lookup else 0
                k = u * n_pages + j
                copies.append(pltpu.make_async_copy(ccmp_hbm.at[l, page], cmp_buf.at[into_slot, k], sem.at[into_slot, 0]))
                copies.append(pltpu.make_async_copy(cslc_hbm.at[l, page], slc_buf.at[into_slot, k], sem.at[into_slot, 1]))
        return copies

    @pl.when(step == 0)
    def _():
        for c in page_copies(0, 0, True):
            c.start()

    @pl.when(step + 1 < pl.num_programs(0))
    def _():
        for c in page_copies(step + 1, 1 - slot, True):
            c.start()

    for c in page_copies(step, slot, False):
        c.wait()
    cmp_pages = [[cmp_buf.at[slot, u * n_pages + j] for j in range(n_pages)] for u in range(nseq)]
    slc_pages = [[slc_buf.at[slot, u * n_pages + j] for j in range(n_pages)] for u in range(nseq)]
    geo = _sample_geometry(past)
    n_pc_pad, n_blk, n_slc = geo["n_pc_pad"], geo["n_blk"], geo["n_slc"]
    pos = past
    lb = cwin_ref.shape[-1]
    scale = HEAD_DIM ** -0.5
    head = lax.broadcasted_iota(jnp.int32, (N_HEADS, 1), 0)
    lane = lax.broadcasted_iota(jnp.int32, (1, GD), 1)
    lane_h = lax.broadcasted_iota(jnp.int32, (N_HEADS, GD), 1)
    own = (lane_h // HEAD_DIM) == (head // N_REP)

    def new_row(u, branch, kv):
        o = (branch * 2 + kv) * GD
        return new_ref[u, :, o:o + GD]

    def rounded(x):
        return x.astype(BF16).astype(F32)

    kcv = []
    pages_per_span = FLAT_SPAN // PAGE
    past_pc = past // CMP_STRIDE
    for u in range(nseq):
        row0 = u * n_pc_pad
        for t in range(n_pages // pages_per_span):
            x_t = jnp.concatenate([cmp_pages[u][t * pages_per_span + w][...].reshape(2 * GD, PAGE)
                                   for w in range(pages_per_span)], axis=1)
            _fill_flat(x_t.astype(BF16), perm_ref[...], flat_ref, row0 + t * CMP_STRIDE)
        for kv in range(2):
            flat_ref[kv, row0 + past_pc:row0 + n_pc_pad, :] = jnp.zeros((n_pc_pad - past_pc, CMP_FLAT), BF16)
            flat_ref[kv, row0 + past_pc:row0 + past_pc + 1, 0:GD] = new_row(u, 0, kv).astype(BF16)
    for kv in range(2):
        kcv.append(_compress_mlp(flat_ref[kv], w1ab_ref[kv], bias_ref[kv], w2_ref[kv]).astype(BF16))
    for u in range(nseq):
        _attn_sample_one(u, kcv[0][u * n_pc_pad:(u + 1) * n_pc_pad], kcv[1][u * n_pc_pad:(u + 1) * n_pc_pad],
                         q_ref, new_row, rounded, gates_ref, wincol_ref, cwin_ref, ov_ref,
                         slc_pages[u], o_ref, wout_ref, geo=geo, past=past, consts=(head, lane, lane_h, own))


def _attn_sample_one(u, kc, vc, q_ref, new_row, rounded, gates_ref, wincol_ref, cwin_ref, ov_ref,
                     slc_pages, o_ref, wout_ref, *, geo, past, consts):
    head, lane, lane_h, own = consts
    n_pc_pad, n_blk, n_slc = geo["n_pc_pad"], geo["n_blk"], geo["n_slc"]
    n_pages = len(slc_pages)
    pos = past
    lb = cwin_ref.shape[-1]
    scale = HEAD_DIM ** -0.5
    q_rows = jnp.concatenate([q_ref[u, :, h * HEAD_DIM:(h + 1) * HEAD_DIM] for h in range(N_HEADS)], axis=0)
    qx = (jnp.where(own, jnp.concatenate([q_rows] * N_KV_HEADS, axis=1), 0.0) * scale).astype(BF16)
    qx32 = qx.astype(F32)
    c_idx = lax.broadcasted_iota(jnp.int32, (1, n_pc_pad), 1)
    valid = (c_idx * CMP_STRIDE + (CMP_BLOCK - 1) <= pos) & (c_idx < n_blk)
    p_c = _masked_softmax(_dot_nt(qx, kc), valid)
    o_c = _dot(p_c.astype(BF16), vc)

    cur = pos // SLC_BLOCK
    forced = (lane == 0) | (lane == cur) | (lane == cur - 1)
    jp = lax.broadcasted_iota(jnp.int32, (GD, GD), 0)
    jj = lax.broadcasted_iota(jnp.int32, (GD, GD), 1)
    sels = []
    for g in range(N_KV_HEADS):
        pc_sum = jnp.sum(p_c[g * N_REP:(g + 1) * N_REP], axis=0, keepdims=True)
        pc_hi = pc_sum.astype(BF16)
        pc_lo = (pc_sum - pc_hi.astype(F32)).astype(BF16)
        imp = jnp.sum(_dot(jnp.concatenate([pc_hi, pc_lo], axis=0), ov_ref[...]), axis=0, keepdims=True)
        score = jnp.where(lane <= cur, jnp.where(forced, BIG, imp), -BIG)
        score = jnp.where(lane < n_slc, score, -2.0 * BIG)
        score_b = jnp.broadcast_to(score, (GD, GD))
        score_a = score_b.T
        beats = (score_a > score_b) | ((score_a == score_b) & (jp < jj))
        rank = jnp.sum(beats.astype(F32), axis=0, keepdims=True)
        sel = ((rank < float(min(N_SEL, n_slc))) & (lane < n_slc)).astype(F32)
        sels.append(jnp.broadcast_to(sel, (N_REP, GD)))
    sel_h = jnp.concatenate(sels, axis=0)

    blocks_per_tile = GD // SLC_BLOCK
    chosen = []
    for t in range(past // GD):
        tile = sel_h[:, t * blocks_per_tile:t * blocks_per_tile + 1]
        for b in range(1, blocks_per_tile):
            tile = jnp.where(lane >= b * SLC_BLOCK, sel_h[:, t * blocks_per_tile + b:t * blocks_per_tile + b + 1], tile)
        chosen.append(jnp.broadcast_to(tile, (N_HEADS, GD)))
    chosen = jnp.concatenate(chosen, axis=-1) > 0.5
    k_past = jnp.concatenate([slc_pages[j][0].astype(BF16) for j in range(n_pages)], axis=-1)
    s_past = jnp.where(chosen, _dot(qx, k_past), NEG)
    new_ok = jnp.sum(jnp.where(lane == pos // SLC_BLOCK, sel_h, 0.0), axis=-1, keepdims=True) > 0.5
    s_new = jnp.where(new_ok, jnp.sum(qx32 * rounded(new_row(u, 1, 0)), axis=-1, keepdims=True), NEG)
    m = jnp.maximum(jnp.max(s_past, axis=-1, keepdims=True), s_new)
    p_past = jnp.where(chosen, jnp.exp(s_past - m), 0.0)
    p_new = jnp.where(new_ok, jnp.exp(s_new - m), 0.0)
    denom = jnp.maximum(jnp.sum(p_past, axis=-1, keepdims=True) + p_new, 1e-30)
    o_s = rounded(p_new) * rounded(new_row(u, 1, 1))
    for j in range(0, n_pages, PV_PAGES):
        v_run = jnp.concatenate([slc_pages[j + w][1].astype(BF16) for w in range(PV_PAGES)], axis=-1)
        o_s = o_s + _dot_nt(p_past[:, j * PAGE:(j + PV_PAGES) * PAGE].astype(BF16), v_run)
    o_s = o_s / denom

    kpos = (past - lb) + lax.broadcasted_iota(jnp.int32, (1, lb), 1)
    w_ok = (kpos <= pos) & (kpos > pos - WINDOW)
    s_w = jnp.where(w_ok, _dot(qx, cwin_ref[u, 0].astype(BF16)), NEG)
    s_wn = jnp.sum(qx32 * rounded(new_row(u, 2, 0)), axis=-1, keepdims=True)
    m = jnp.maximum(jnp.max(s_w, axis=-1, keepdims=True), s_wn)
    p_w = jnp.where(w_ok, jnp.exp(s_w - m), 0.0)
    p_wn = jnp.exp(s_wn - m)
    denom = jnp.maximum(jnp.sum(p_w, axis=-1, keepdims=True) + p_wn, 1e-30)
    o_w = (_dot_nt(p_w.astype(BF16), cwin_ref[u, 1].astype(BF16))
           + rounded(p_wn) * rounded(new_row(u, 2, 1))) / denom

    gates = gates_ref[u]
    gate_id = lax.broadcasted_iota(jnp.int32, (N_HEADS, GATE_ROWS), 1)
    gate = lambda br: jnp.sum(jnp.where(gate_id == head + br * N_HEADS, gates, 0.0), axis=-1, keepdims=True)
    o = gate(0) * o_c + gate(1) * o_s + gate(2) * o_w
    for h in range(N_HEADS):
        g = h // N_REP
        o_ref[u, :, h * HEAD_DIM:(h + 1) * HEAD_DIM] = o[h:h + 1, g * HEAD_DIM:(g + 1) * HEAD_DIM]

    for kv in range(2):
        wout_ref[u, kv, :, 0:lb - 1] = cwin_ref[u, kv, :, 1:lb]
        wout_ref[u, kv, :, lb - 1:lb] = wincol_ref[u, kv * GD:(kv + 1) * GD, :]


def _attn_sample(l, q_tok, new_tok, gates, win_col, cwin_t, ccmp_t, cslc_t, page_table, w1ab, bias, w2, win_prev):
    nb = q_tok.shape[0]
    n_pages = page_table.shape[1]
    past = n_pages * PAGE
    geo = _sample_geometry(past)
    lb = cwin_t.shape[-1]
    assert FLAT_SPAN % PAGE == 0 and n_pages % (FLAT_SPAN // PAGE) == 0
    perm = _piece_perm()
    ov = _overlap_t(geo["n_pc_pad"], geo["n_blk"], GD).T
    ov = jnp.where(jnp.arange(GD)[None, :] < geo["n_slc"], ov, 0).astype(BF16)
    assert GD % SLC_BLOCK == 0 and past % GD == 0
    nseq = SAMPLE_SEQS if nb % SAMPLE_SEQS == 0 else 1
    seq = lambda a: pl.BlockSpec((nseq,) + a.shape[1:], lambda i, pt: (i,) + (0,) * (a.ndim - 1))
    full = lambda a: pl.BlockSpec(a.shape, lambda i, pt: (0,) * a.ndim)
    operands = [page_table, q_tok, new_tok, gates, win_col, cwin_t, ov, perm, w1ab, bias, w2, ccmp_t, cslc_t]
    win_prev = () if win_prev is None else (win_prev,)
    page_buf = pltpu.VMEM((2, nseq * n_pages, 2, GD, PAGE), F32)
    grid_spec = pltpu.PrefetchScalarGridSpec(
        num_scalar_prefetch=1,
        grid=(nb // nseq,),
        in_specs=[seq(q_tok), seq(new_tok), seq(gates), seq(win_col),
                  pl.BlockSpec((None, nseq, 2, GD, lb), lambda i, pt: (l, i, 0, 0, 0)),
                  full(ov), full(perm), full(w1ab), full(bias), full(w2)]
                 + [pl.BlockSpec(memory_space=pl.ANY)] * (2 + len(win_prev)),
        out_specs=[pl.BlockSpec((nseq, 1, D_ATTN), lambda i, pt: (i, 0, 0)),
                   pl.BlockSpec((None, nseq, 2, GD, lb), lambda i, pt: (l, i, 0, 0, 0))],
        scratch_shapes=[pltpu.VMEM((2, nseq * geo["n_pc_pad"], CMP_FLAT), BF16), page_buf, page_buf,
                        pltpu.SemaphoreType.DMA((2, 2))],
    )
    return pl.pallas_call(
        functools.partial(_attn_sample_kernel, l=l, past=past, n_pages=n_pages, nseq=nseq),
        grid_spec=grid_spec,
        out_shape=[jax.ShapeDtypeStruct((nb, 1, D_ATTN), F32),
                   jax.ShapeDtypeStruct((cwin_t.shape[0], nb, 2, GD, lb), F32)],
        input_output_aliases={len(operands): 1} if win_prev else {},
        compiler_params=pltpu.CompilerParams(dimension_semantics=("arbitrary",), vmem_limit_bytes=VMEM_LIMIT),
        name="attn_sample",
    )(*operands, *win_prev)


def _prep_layer(l, w_norm, w_in, w_out, w_dw, b_dw, ln_g, ln_b, w_pw, b_pw, w_pool, pool_scale,
                g_q, g_k, cmp_pe, cmp_w1, cmp_w2):
    sizes = (D_CONV, D_CONV, D_CONV, D_POOL, D_POOL, D_ATTN, D_KV, N_BRANCH * N_HEADS, D_ATTN)
    offs = [0] + [int(v) for v in np.cumsum(sizes)]
    col = lambda k: w_in[l][:, offs[k]:offs[k + 1]]
    a_val, a_gate, z_a, b_in, z_b, q, kv, gate, z_c = (col(k) for k in range(9))
    pad = jnp.zeros((D_MODEL, GATE_ROWS - N_BRANCH * N_HEADS), F32)
    wtok = jnp.concatenate([a_val, a_gate, b_in, z_a, z_b, z_c], axis=1).astype(BF16)
    wfeat = jnp.concatenate([q, kv, gate, pad], axis=1).T.astype(BF16)
    ones = jnp.ones((N_KV_HEADS * HEAD_DIM,), F32)
    gcol = jnp.concatenate([jnp.tile(g_q[l], N_HEADS)]
                           + [piece for br in range(N_BRANCH)
                              for piece in (jnp.tile(g_k[l, br], N_KV_HEADS), ones)])[:, None]
    wpool_bd = jnp.zeros((D_POOL, D_POOL), F32)
    for gi in range(len(POOL_WINDOWS)):
        wpool_bd = wpool_bd.at[gi * POOL_GROUP:(gi + 1) * POOL_GROUP, gi * POOL_GROUP:(gi + 1) * POOL_GROUP].set(w_pool[l, gi])
    row = lambda v: v[None, :]

    def both_groups(w):
        z = jnp.zeros_like(w)
        return jnp.stack([jnp.concatenate([w, z], axis=2), jnp.concatenate([z, w], axis=2)], axis=1).reshape(-1, GD)

    half = CMP_STRIDE * HEAD_DIM
    w1 = cmp_w1[l].astype(BF16)
    w1ab = jnp.stack([jnp.concatenate(
        [both_groups(w1[kv, :half].reshape(CMP_STRIDE, HEAD_DIM, HEAD_DIM)),
         both_groups(w1[kv, half:].reshape(CMP_STRIDE, HEAD_DIM, HEAD_DIM))], axis=1) for kv in range(2)])
    w2p = jnp.stack([both_groups(cmp_w2[l, kv].astype(BF16)[None]) for kv in range(2)])
    cbias = jnp.tile(_cmp_bias(cmp_pe[l].reshape(2, 1, CMP_BLOCK * HEAD_DIM), w1), (1, 1, N_KV_HEADS))
    return dict(
        wnorm=row(w_norm[l]), wtok=wtok, wfeat=wfeat, gcol=gcol,
        wdw=w_dw[l], bdw=row(b_dw[l]), lng=row(ln_g[l]), lnb=row(ln_b[l]),
        wpw=w_pw[l].astype(BF16), bpw=row(b_pw[l]),
        wpool=wpool_bd.astype(BF16), pscale=row(pool_scale[l]),
        w1ab=w1ab, cbias=cbias, w2p=w2p,
        wout=w_out[l].astype(BF16),
    )


def _rope_tables(pos):
    inv = ROPE_THETA ** (-jnp.arange(ROT_HALF, dtype=F32) * 2.0 / ROT_DIM)
    ang = pos.astype(F32)[:, None] * inv[None, :]
    return jnp.cos(ang).T, jnp.sin(ang).T


def _prompt_layer(l, depth, x, p, cos_t, sin_t, kv_prev):
    glu, b_in, sz, gates, q_t, *kv = _in_proj(
        x, p["wnorm"], p["wtok"], p["wfeat"], p["gcol"], cos_t, sin_t, min(IN_PROJ_ROWS, x.shape[1]), l, depth, kv_prev)
    cmp_t, slc_t, win_t = kv
    kcv = _compress_prompt(cmp_t, l, p["w1ab"], p["cbias"], p["w2p"])
    o_attn = _attn_prompt(q_t, kcv, slc_t, win_t, gates, l)
    y = _mix_out_prompt(x, glu, b_in, sz, o_attn, p["wdw"], p["bdw"], p["lng"], p["lnb"], p["wpw"], p["bpw"],
                        p["wpool"], p["pscale"], p["wout"], t=min(MIX_OUT_ROWS, x.shape[1]))
    return y, glu, b_in, kv


def _sample_layer(l, depth, x, p, cos_t, sin_t, sc, sp, cwin_t, ccmp_t, cslc_t, page_table, past, kv_prev, win_prev):
    nb = x.shape[0]
    glu, b_in, sz, gates, q_t, *kv = _in_proj(
        x[None], p["wnorm"], p["wtok"], p["wfeat"], p["gcol"], cos_t, sin_t, nb, l, depth, kv_prev)
    glu, b_in, sz, gates = glu[0], b_in[0], sz[0], gates[0]
    yab, sc_new, sp_new = _convpool_sample(sc, sp, glu, b_in, sz, p["wdw"], p["bdw"], p["lng"], p["lnb"],
                                           p["wpw"], p["bpw"], p["wpool"], p["pscale"], pos=past)
    new_t = jnp.concatenate([a[l, 0] for a in kv], axis=0)
    new_tok = new_t.T[:, None, :]
    q_tok = q_t[0].T[:, None, :]
    win_col = kv[2][l, 0].T[:, :, None]
    o_attn, win_next = _attn_sample(l, q_tok, new_tok, gates.T[:, None, :], win_col, cwin_t, ccmp_t, cslc_t,
                                    page_table, p["w1ab"], p["cbias"], p["w2p"], win_prev)
    y = _out_proj(x[None], yab[None], o_attn.reshape(1, nb, D_ATTN), sz[None], p["wout"], tm=nb)[0]
    return y, sc_new, sp_new, win_next, kv


def _rows_from_feat(a):
    lead = a.shape[:-2]
    a = a.reshape(lead + (2, N_KV_HEADS, HEAD_DIM, a.shape[-1]))
    nl = len(lead)
    return jnp.transpose(a, tuple(range(nl)) + (nl + 3, nl, nl + 1, nl + 2))


def kernel(x_prompt, x_sample, state_conv, state_pool, cache_win_kv, cache_cmp_kv, cache_slc_kv, page_table,
           w_norm, w_in, w_out, w_dw, b_dw, ln_g, ln_b, w_pw, b_pw, w_pool, pool_scale,
           g_q, g_k, cmp_pe, cmp_w1, cmp_w2):
    bp, s, _ = x_prompt.shape
    bs, t_new, _ = x_sample.shape
    depth = w_in.shape[0]
    assert t_new == 1 and cache_cmp_kv.shape[2] == PAGE and s % SEL_KEYS == 0 and s % MIX_OUT_ROWS == 0
    past = page_table.shape[1] * PAGE
    lb = cache_win_kv.shape[2]

    def to_feat(c):
        c = jnp.transpose(c, (0, 1, 3, 4, 5, 2))
        return c.reshape(c.shape[:3] + (GD, c.shape[-1]))
    cwin_t, ccmp_t, cslc_t = to_feat(cache_win_kv), to_feat(cache_cmp_kv), to_feat(cache_slc_kv)
    sc_all = jnp.transpose(state_conv, (0, 2, 1, 3))
    sp_all = jnp.transpose(state_pool, (0, 2, 1, 3))
    cos_p, sin_p = _rope_tables(jnp.arange(s))
    cos_s, sin_s = _rope_tables(jnp.full((bs,), past))

    xp, xs = x_prompt, x_sample[:, 0, :]
    conv_p, pool_p, conv_s, pool_s = [], [], [], []
    kv_p = kv_s = win_s = None
    for l in range(depth):
        p = _prep_layer(l, w_norm, w_in, w_out, w_dw, b_dw, ln_g, ln_b, w_pw, b_pw, w_pool, pool_scale,
                        g_q, g_k, cmp_pe, cmp_w1, cmp_w2)
        xp, glu, b_in, kv_p = _prompt_layer(l, depth, xp, p, cos_p, sin_p, kv_p)
        conv_p.append(glu[:, s - CONV_STATE:, :])
        pool_p.append(b_in[:, s - POOL_STATE:, :])
        xs, sc_new, sp_new, win_s, kv_s = _sample_layer(
            l, depth, xs, p, cos_s, sin_s, sc_all[l], sp_all[l], cwin_t, ccmp_t, cslc_t, page_table, past,
            kv_s, win_s)
        conv_s.append(jnp.transpose(sc_new, (1, 0, 2)))
        pool_s.append(jnp.transpose(sp_new, (1, 0, 2)))
    cmp_p, slc_p, win_p = kv_p
    cmp_s, slc_s, _ = kv_s
    new_rows = lambda a: _rows_from_feat(a[:, 0])[:, :, None]
    return (xp, xs[:, None, :], jnp.stack(conv_p), jnp.stack(pool_p),
            _rows_from_feat(win_p[:, :, :, s - min(WINDOW, s):]), _rows_from_feat(cmp_p), _rows_from_feat(slc_p),
            jnp.stack(conv_s), jnp.stack(pool_s), _rows_from_feat(win_s.reshape(depth, bs, D_ROWS, lb)),
            new_rows(cmp_s), new_rows(slc_s))
```

```python
import functools

import numpy as np
import jax
import jax.numpy as jnp
from jax import lax
from jax.experimental import pallas as pl
from jax.experimental.pallas import tpu as pltpu

D_MODEL = 1024
D_CONV = 256
D_POOL = 256
N_HEADS = 8
N_KV_HEADS = 2
HEAD_DIM = 64
D_ATTN = N_HEADS * HEAD_DIM
CONV_WIDTH = 31
CONV_STATE = CONV_WIDTH - 1
POOL_WINDOWS = (2, 4, 8, 16)
POOL_GROUP = D_POOL // len(POOL_WINDOWS)
POOL_STATE = max(POOL_WINDOWS) - 1
CMP_BLOCK = 32
CMP_STRIDE = 16
SLC_BLOCK = 64
N_SEL = 8
WINDOW = 256
Q_BLOCK = 256
ROT_DIM = HEAD_DIM // 4
ROT_HALF = ROT_DIM // 2
ROPE_THETA = 500000.0
N_BRANCH = 3
D_KV = N_BRANCH * 2 * N_KV_HEADS * HEAD_DIM
D_ROWS = 2 * N_KV_HEADS * HEAD_DIM
N_REP = N_HEADS // N_KV_HEADS
EPS = 1e-6
NEG = -1e30
BIG = 1e4

GATE_ROWS = 32
D_TOK = 3 * D_CONV + 2 * D_POOL + D_ATTN
D_QKV = D_ATTN + D_KV
D_FEAT = D_QKV + GATE_ROWS
N_FEAT_BLOCKS = D_QKV // HEAD_DIM
SEL_KEYS = 512
IN_PROJ_ROWS = 1024
MIX_OUT_ROWS = 512
VMEM_V7X = 64 * 1024 * 1024
VMEM_LIMIT = VMEM_V7X * 7 // 8

F32 = jnp.float32
BF16 = jnp.bfloat16
NT_DIMS = (((1,), (1,)), ((), ()))


def _sigmoid(x):
    return 1.0 / (1.0 + jnp.exp(-x))


def _silu(x):
    return x * _sigmoid(x)


def _gelu_tanh(x):
    return 0.5 * x * (1.0 + jnp.tanh(np.sqrt(2.0 / np.pi).astype(np.float32) * (x + 0.044715 * (x * x * x))))


def _dot(a, b):
    return jnp.dot(a, b, preferred_element_type=F32)


def _dot_nt(a, b):
    return lax.dot_general(a, b, NT_DIMS, preferred_element_type=F32)


def _masked_softmax(s, mask):
    s = jnp.where(mask, s, NEG)
    m = jnp.max(s, axis=-1, keepdims=True)
    p = jnp.where(mask, jnp.exp(s - m), 0.0)
    return p / jnp.maximum(jnp.sum(p, axis=-1, keepdims=True), 1e-30)


def _in_proj_kernel(x_ref, wnorm_ref, wtok_ref, wfeat_ref, gcol_ref, cos_ref, sin_ref, *rest):
    glu_ref, bin_ref, sz_ref, gates_ref, qt_ref, cmp_ref, slc_ref, win_ref = rest[-8:]
    x = x_ref[...]
    ms = jnp.mean(x * x, axis=-1, keepdims=True)
    h = (x * lax.rsqrt(ms + EPS) * wnorm_ref[...]).astype(BF16)

    a = _dot(h, wtok_ref[...])
    o = 0
    glu_ref[...] = a[:, o:o + D_CONV] * _sigmoid(a[:, o + D_CONV:o + 2 * D_CONV])
    o += 2 * D_CONV
    bin_ref[...] = a[:, o:o + D_POOL]
    o += D_POOL
    z = a[:, o:o + D_CONV + D_POOL + D_ATTN]
    sz_ref[...] = _silu(z)

    f = _dot_nt(wfeat_ref[...], h)
    gates_ref[...] = _sigmoid(f[D_QKV:D_FEAT, :])
    cos = cos_ref[...]
    sin = sin_ref[...]
    kv_refs = (cmp_ref, slc_ref, win_ref)
    for hb in range(N_FEAT_BLOCKS):
        blk = f[hb * HEAD_DIM:(hb + 1) * HEAD_DIM, :]
        kv_blk = hb - N_HEADS
        is_value = kv_blk >= 0 and (kv_blk % (2 * N_KV_HEADS)) >= N_KV_HEADS
        if not is_value:
            bms = jnp.mean(blk * blk, axis=0, keepdims=True)
            y = blk * lax.rsqrt(bms + EPS) * gcol_ref[hb * HEAD_DIM:(hb + 1) * HEAD_DIM, :]
            x1 = y[0:ROT_HALF]
            x2 = y[ROT_HALF:ROT_DIM]
            blk = jnp.concatenate([x1 * cos - x2 * sin, x2 * cos + x1 * sin, y[ROT_DIM:]], axis=0)
        if kv_blk < 0:
            qt_ref[hb * HEAD_DIM:(hb + 1) * HEAD_DIM, :] = blk
        else:
            r = kv_blk % (2 * N_KV_HEADS)
            kv_refs[kv_blk // (2 * N_KV_HEADS)][r * HEAD_DIM:(r + 1) * HEAD_DIM, :] = blk


def _in_proj(x, wnorm, wtok, wfeat, gcol, cos_t, sin_t, tm, l, depth, kv_prev):
    n, t, _ = x.shape
    assert t % tm == 0
    grid = (n, t // tm)
    tok = lambda w: pl.BlockSpec((None, tm, w), lambda i, j: (i, j, 0))
    feat = lambda w: pl.BlockSpec((None, w, tm), lambda i, j: (i, 0, j))
    kv_spec = pl.BlockSpec((None, None, D_ROWS, tm), lambda i, j: (l, i, 0, j))
    full = lambda a: pl.BlockSpec(a.shape, lambda i, j: (0,) * a.ndim)
    tok_shape = lambda w: jax.ShapeDtypeStruct((n, t, w), F32)
    kv_shape = jax.ShapeDtypeStruct((depth, n, D_ROWS, t), F32)
    n_in = 7
    kv_prev = () if kv_prev is None else tuple(kv_prev)
    return pl.pallas_call(
        _in_proj_kernel,
        grid=grid,
        in_specs=[tok(D_MODEL), full(wnorm), full(wtok), full(wfeat), full(gcol),
                  pl.BlockSpec((ROT_HALF, tm), lambda i, j: (0, j)),
                  pl.BlockSpec((ROT_HALF, tm), lambda i, j: (0, j))]
                 + [pl.BlockSpec(memory_space=pl.ANY)] * len(kv_prev),
        out_specs=[tok(D_CONV), tok(D_POOL), tok(D_CONV + D_POOL + D_ATTN), feat(GATE_ROWS),
                   feat(D_ATTN), kv_spec, kv_spec, kv_spec],
        out_shape=[tok_shape(D_CONV), tok_shape(D_POOL), tok_shape(D_CONV + D_POOL + D_ATTN),
                   jax.ShapeDtypeStruct((n, GATE_ROWS, t), F32),
                   jax.ShapeDtypeStruct((n, D_ATTN, t), F32), kv_shape, kv_shape, kv_shape],
        input_output_aliases={n_in + k: 5 + k for k in range(len(kv_prev))},
        compiler_params=pltpu.CompilerParams(dimension_semantics=("arbitrary", "arbitrary"),
                                             vmem_limit_bytes=VMEM_LIMIT),
        name="in_proj",
    )(x, wnorm, wtok, wfeat, gcol, cos_t, sin_t, *kv_prev)


def _conv_tail(acc, sz_a, bdw, lng, lnb, wpw, bpw):
    y = acc + bdw
    mu = jnp.mean(y, axis=-1, keepdims=True)
    yc = y - mu
    var = jnp.mean(yc * yc, axis=-1, keepdims=True)
    y = yc * lax.rsqrt(var + EPS) * lng + lnb
    y = _dot(_silu(y).astype(BF16), wpw) + bpw
    return y * sz_a


def _pool_tail(total, cnt, xcur, sz_b, wpool, pscale):
    d = total / cnt - xcur
    y = _dot(d.astype(BF16), wpool) * pscale
    return y * sz_b


def _pool_window_of_lane(shape):
    lane = lax.broadcasted_iota(jnp.int32, shape, len(shape) - 1)
    w = jnp.full(shape, POOL_WINDOWS[0], jnp.int32)
    for gi in range(1, len(POOL_WINDOWS)):
        w = jnp.where(lane >= gi * POOL_GROUP, POOL_WINDOWS[gi], w)
    return w


def _pool_select(sums, shape):
    lane = lax.broadcasted_iota(jnp.int32, shape, len(shape) - 1)
    total = sums[0]
    for gi in range(1, len(POOL_WINDOWS)):
        total = jnp.where(lane >= gi * POOL_GROUP, sums[gi], total)
    return total


CONV_HALO = 32
POOL_HALO = 16


MIX_ROWS = 256
SUBLANES = 8


def _tap_rows(buf, w_ref, r0, rows, off, taps, lanes):
    y = None
    for c in range(SUBLANES):
        part = None
        for k in taps:
            if (k + off) % SUBLANES != c:
                continue
            base = r0 + k + off - c
            term = w_ref[k:k + 1, lanes] * buf[base:base + rows + SUBLANES, lanes]
            part = term if part is None else part + term
        if part is not None:
            y = part[c:c + rows] if y is None else y + part[c:c + rows]
    return y


def _mix_out_prompt_kernel(x_ref, glu_ref, gprev_ref, bin_ref, bprev_ref, sz_ref, oat_ref, wdw_ref, bdw_ref, lng_ref,
                           lnb_ref, wpw_ref, bpw_ref, wpool_ref, pscale_ref, wout_ref, o_ref,
                           cbuf, pbuf, ymix, *, t):
    i = pl.program_id(1)
    keep = (i > 0).astype(F32)
    cbuf[0:CONV_HALO, :] = gprev_ref[...] * keep
    cbuf[CONV_HALO:CONV_HALO + t, :] = glu_ref[...]
    cbuf[CONV_HALO + t:CONV_HALO + t + SUBLANES, :] = jnp.zeros((SUBLANES, D_CONV), F32)
    pbuf[0:POOL_HALO, :] = bprev_ref[...] * keep
    pbuf[POOL_HALO:POOL_HALO + t, :] = bin_ref[...]
    nab = D_CONV + D_POOL
    for r0 in range(0, t, MIX_ROWS):
        rows = slice(r0, r0 + MIX_ROWS)
        sz = sz_ref[rows, :]
        acc = _tap_rows(cbuf, wdw_ref, r0, MIX_ROWS, CONV_HALO - CONV_STATE, range(CONV_WIDTH), slice(0, D_CONV))
        y_a = _conv_tail(acc, sz[:, 0:D_CONV], bdw_ref[...], lng_ref[...], lnb_ref[...], wpw_ref[...], bpw_ref[...])
        ymix[rows, 0:D_CONV] = y_a.astype(BF16)

        sums = []
        run = jnp.zeros((MIX_ROWS, D_POOL), F32)
        j = 0
        for w in POOL_WINDOWS:
            while j < w:
                run = run + pbuf[pl.ds(POOL_HALO + r0 - j, MIX_ROWS), :]
                j += 1
            sums.append(run)
        total = _pool_select(sums, (MIX_ROWS, D_POOL))
        pos = i * t + r0 + lax.broadcasted_iota(jnp.int32, (MIX_ROWS, D_POOL), 0)
        cnt = jnp.minimum(_pool_window_of_lane((MIX_ROWS, D_POOL)), pos + 1).astype(F32)
        y_b = _pool_tail(total, cnt, bin_ref[rows, :], sz[:, D_CONV:nab], wpool_ref[...], pscale_ref[...])
        ymix[rows, D_CONV:nab] = y_b.astype(BF16)
        ymix[rows, nab:nab + D_ATTN] = (oat_ref[rows, :] * sz[:, nab:nab + D_ATTN]).astype(BF16)
    o_ref[...] = x_ref[...] + _dot(ymix[...], wout_ref[...])


def _mix_out_prompt(x, glu, b_in, sz, o_attn, wdw, bdw, lng, lnb, wpw, bpw, wpool, pscale, wout, t):
    n, s, _ = glu.shape
    assert t % MIX_ROWS == 0 and s % t == 0
    grid = (n, s // t)
    cur = lambda w: pl.BlockSpec((None, t, w), lambda i, j: (i, j, 0))
    prev = lambda rows, w: pl.BlockSpec((None, rows, w), lambda i, j: (i, jnp.maximum(j * (t // rows) - 1, 0), 0))
    full = lambda a: pl.BlockSpec(a.shape, lambda i, j: (0,) * a.ndim)
    d_mix = D_CONV + D_POOL + D_ATTN
    return pl.pallas_call(
        functools.partial(_mix_out_prompt_kernel, t=t),
        grid=grid,
        in_specs=[cur(D_MODEL), cur(D_CONV), prev(CONV_HALO, D_CONV), cur(D_POOL), prev(POOL_HALO, D_POOL),
                  cur(d_mix), cur(D_ATTN),
                  full(wdw), full(bdw), full(lng), full(lnb), full(wpw), full(bpw), full(wpool),
                  full(pscale), full(wout)],
        out_specs=cur(D_MODEL),
        out_shape=jax.ShapeDtypeStruct(x.shape, F32),
        scratch_shapes=[pltpu.VMEM((CONV_HALO + t + SUBLANES, D_CONV), F32),
                        pltpu.VMEM((POOL_HALO + t, D_POOL), F32),
                        pltpu.VMEM((t, d_mix), BF16)],
        compiler_params=pltpu.CompilerParams(dimension_semantics=("arbitrary", "arbitrary"),
                                             vmem_limit_bytes=VMEM_LIMIT),
        name="mix_out_prompt",
    )(x, glu, glu, b_in, b_in, sz, o_attn, wdw, bdw, lng, lnb, wpw, bpw, wpool, pscale, wout)


def _convpool_sample_kernel(sc_ref, sp_ref, glu_ref, bin_ref, sz_ref, wdw_ref, bdw_ref, lng_ref, lnb_ref,
                            wpw_ref, bpw_ref, wpool_ref, pscale_ref, y_ref, sc_out, sp_out, *, pos):
    glu = glu_ref[...]
    acc = wdw_ref[CONV_STATE:CONV_WIDTH, :] * glu
    for k in range(CONV_STATE):
        acc = acc + wdw_ref[k:k + 1, :] * sc_ref[k]
    sz = sz_ref[...]
    y_ref[:, 0:D_CONV] = _conv_tail(acc, sz[:, 0:D_CONV], bdw_ref[...], lng_ref[...], lnb_ref[...],
                                    wpw_ref[...], bpw_ref[...])
    for k in range(CONV_STATE - 1):
        sc_out[k] = sc_ref[k + 1]
    sc_out[CONV_STATE - 1] = glu

    xcur = bin_ref[...]
    shape = xcur.shape
    sums = []
    run = xcur
    j = 1
    for w in POOL_WINDOWS:
        while j < w:
            run = run + sp_ref[POOL_STATE - j]
            j += 1
        sums.append(run)
    total = _pool_select(sums, shape)
    cnt = jnp.minimum(_pool_window_of_lane(shape), pos + 1).astype(F32)
    y_ref[:, D_CONV:D_CONV + D_POOL] = _pool_tail(total, cnt, xcur, sz[:, D_CONV:D_CONV + D_POOL],
                                                  wpool_ref[...], pscale_ref[...])
    for k in range(POOL_STATE - 1):
        sp_out[k] = sp_ref[k + 1]
    sp_out[POOL_STATE - 1] = xcur


def _convpool_sample(sc, sp, glu, b_in, sz, wdw, bdw, lng, lnb, wpw, bpw, wpool, pscale, pos):
    nb = glu.shape[0]
    return pl.pallas_call(
        functools.partial(_convpool_sample_kernel, pos=pos),
        out_shape=[jax.ShapeDtypeStruct((nb, D_CONV + D_POOL), F32),
                   jax.ShapeDtypeStruct(sc.shape, F32), jax.ShapeDtypeStruct(sp.shape, F32)],
        compiler_params=pltpu.CompilerParams(vmem_limit_bytes=VMEM_LIMIT),
        name="convpool_sample",
    )(sc, sp, glu, b_in, sz[:, 0:D_CONV + D_POOL], wdw, bdw, lng, lnb, wpw, bpw, wpool, pscale)


def _out_proj_kernel(x_ref, yab_ref, oat_ref, szc_ref, w_ref, o_ref):
    ya = yab_ref[...].astype(BF16)
    yc = (oat_ref[...] * szc_ref[...]).astype(BF16)
    nab = D_CONV + D_POOL
    o_ref[...] = x_ref[...] + _dot(ya, w_ref[0:nab, :]) + _dot(yc, w_ref[nab:nab + D_ATTN, :])


def _out_proj(x, yab, o_attn, sz, w_out, tm):
    n, t, _ = x.shape
    nab = D_CONV + D_POOL
    assert nab == D_ATTN
    grid = (n, t // tm)
    tok = lambda w, cb=0: pl.BlockSpec((None, tm, w), lambda i, j: (i, j, cb))
    return pl.pallas_call(
        _out_proj_kernel,
        grid=grid,
        in_specs=[tok(D_MODEL), tok(nab), tok(D_ATTN), tok(D_ATTN, 1),
                  pl.BlockSpec(w_out.shape, lambda i, j: (0, 0))],
        out_specs=tok(D_MODEL),
        out_shape=jax.ShapeDtypeStruct(x.shape, F32),
        compiler_params=pltpu.CompilerParams(dimension_semantics=("arbitrary", "arbitrary"),
                                             vmem_limit_bytes=VMEM_LIMIT),
        name="out_proj",
    )(x, yab, o_attn, sz, w_out)


GD = N_KV_HEADS * HEAD_DIM
CMP_FLAT = CMP_STRIDE * GD


FLAT_SPAN = CMP_STRIDE * CMP_STRIDE


def _piece_perm():
    m = np.arange(FLAT_SPAN)
    src = (m % CMP_STRIDE) * CMP_STRIDE + m // CMP_STRIDE
    return jnp.asarray((src[:, None] == np.arange(FLAT_SPAN)[None, :]).astype(np.float32), BF16)


def _fill_flat(x_t, perm, flat_ref, piece0):
    y = _dot_nt(perm, x_t).astype(BF16)
    for kv in range(2):
        for r in range(CMP_STRIDE):
            flat_ref[kv, piece0:piece0 + CMP_STRIDE, r * GD:(r + 1) * GD] = (
                y[r * CMP_STRIDE:(r + 1) * CMP_STRIDE, kv * GD:(kv + 1) * GD])


def _compress_mlp(flat, w1ab, bias, w2):
    n_pc = flat.shape[0]
    hb = _dot(flat, w1ab)
    h = hb[:, 0:GD] + jnp.concatenate([hb[1:n_pc, GD:2 * GD], jnp.zeros((1, GD), F32)], axis=0) + bias
    out = _dot(_gelu_tanh(h).astype(BF16), w2)
    row = lax.broadcasted_iota(jnp.int32, out.shape, 0)
    return jnp.where(row < n_pc - 1, out, 0.0)


def _cmp_bias_kernel(pe_ref, w1_ref, o_ref):
    for kv in range(2):
        o_ref[kv] = _dot(pe_ref[kv].astype(BF16), w1_ref[kv])


def _cmp_bias(pe_flat, w1):
    return pl.pallas_call(_cmp_bias_kernel, out_shape=jax.ShapeDtypeStruct((2, 1, HEAD_DIM), F32),
                          name="cmp_bias")(pe_flat, w1)


def _compress_prompt_kernel(x_ref, perm_ref, w1ab_ref, bias_ref, w2_ref, o_ref, flat_ref, *, s):
    for t in range(s // FLAT_SPAN):
        _fill_flat(x_ref[:, t * FLAT_SPAN:(t + 1) * FLAT_SPAN].astype(BF16), perm_ref[...], flat_ref, t * CMP_STRIDE)
    for kv in range(2):
        o_ref[:, kv * GD:(kv + 1) * GD] = _compress_mlp(flat_ref[kv], w1ab_ref[kv], bias_ref[kv], w2_ref[kv])


def _compress_prompt(cmp_t, l, w1ab, bias, w2):
    _, n, _, s = cmp_t.shape
    assert s % FLAT_SPAN == 0
    n_pc = s // CMP_STRIDE
    perm = _piece_perm()
    full = lambda a: pl.BlockSpec(a.shape, lambda i: (0,) * a.ndim)
    return pl.pallas_call(
        functools.partial(_compress_prompt_kernel, s=s),
        grid=(n,),
        in_specs=[pl.BlockSpec((None, None, D_ROWS, s), lambda i: (l, i, 0, 0)),
                  full(perm), full(w1ab), full(bias), full(w2)],
        out_specs=pl.BlockSpec((None, n_pc, 2 * GD), lambda i: (i, 0, 0)),
        out_shape=jax.ShapeDtypeStruct((n, n_pc, 2 * GD), F32),
        scratch_shapes=[pltpu.VMEM((2, n_pc, CMP_FLAT), BF16)],
        compiler_params=pltpu.CompilerParams(dimension_semantics=("arbitrary",), vmem_limit_bytes=VMEM_LIMIT),
        name="compress_prompt",
    )(cmp_t, perm, w1ab, bias, w2)


def _select_blocks(imp_t, pos_row, n_blocks):
    shape = imp_t.shape
    assert n_blocks % 8 == 0
    j = lax.broadcasted_iota(jnp.int32, shape, 0)
    cur = pos_row // SLC_BLOCK
    forced = (j == 0) | (j == cur) | (j == cur - 1)
    score = jnp.where(j <= cur, jnp.where(forced, BIG, imp_t), -BIG)
    tiles = [score[8 * v:8 * v + 8] for v in range(n_blocks // 8)]
    j8 = lax.broadcasted_iota(jnp.int32, (8, shape[1]), 0)
    ranks = [jnp.zeros((8, shape[1]), F32) for _ in tiles]
    for jp in range(n_blocks):
        sj = score[jp:jp + 1, :]
        for v, tile in enumerate(tiles):
            if jp < 8 * v:
                beats = sj >= tile
            elif jp >= 8 * v + 8:
                beats = sj > tile
            else:
                beats = (sj > tile) | ((sj == tile) & (j8 > jp - 8 * v))
            ranks[v] = ranks[v] + jnp.where(beats, 1.0, 0.0)
    rank = jnp.concatenate(ranks, axis=0)
    return (rank < float(min(N_SEL, n_blocks))).astype(F32)


def _col_softmax(s):
    m = jnp.max(s, axis=0, keepdims=True)
    e = jnp.exp(s - m)
    inv = jnp.where(m > 0.5 * NEG, 1.0 / jnp.maximum(jnp.sum(e, axis=0, keepdims=True), 1e-30), 0.0)
    return e, inv


ONES_ROWS = 16
WIN_TILE = 128
BLOCK_CODE = 64


def _with_ones(v_t):
    return jnp.concatenate([v_t, jnp.ones((ONES_ROWS, v_t.shape[1]), BF16)], axis=0)


def _normalised(acc):
    return acc[0:HEAD_DIM] * (1.0 / jnp.maximum(acc[HEAD_DIM:HEAD_DIM + 1], 1e-30))


def _mask_heads(sc, ok, tq):
    return jnp.concatenate([jnp.where(ok, sc[:, r * tq:(r + 1) * tq], NEG) for r in range(N_REP)], axis=1)


def _attn_prompt_kernel(qt_ref, kcv_ref, slc_ref, win_ref, gates_ref, ovt_ref, o_ref,
                        ks_ref, kw_ref, s_ref, acc_ref, m_ref, *, s, n_blk):
    tq = Q_BLOCK
    cols = N_REP * tq
    groups = range(N_KV_HEADS)
    qc = pl.program_id(1)
    q0 = qc * tq
    pos_row = q0 + lax.broadcasted_iota(jnp.int32, (1, tq), 1)
    n_slc = s // SLC_BLOCK
    n_cmp = kcv_ref.shape[0]
    scale = HEAD_DIM ** -0.5
    v_row0 = lambda g: (N_KV_HEADS + g) * HEAD_DIM

    @pl.when(qc == 0)
    def _():
        for g in range(N_KV_HEADS):
            for t in range(s // 128):
                cols_t = slice(t * 128, (t + 1) * 128)
                ks_ref[g, cols_t, 0:HEAD_DIM] = slc_ref[g * HEAD_DIM:(g + 1) * HEAD_DIM, cols_t].T.astype(BF16)
                block_of_key = (t * 128 + lax.broadcasted_iota(jnp.int32, (128, BLOCK_CODE), 0)) // SLC_BLOCK
                code = block_of_key == lax.broadcasted_iota(jnp.int32, (128, BLOCK_CODE), 1)
                ks_ref[g, cols_t, HEAD_DIM:HEAD_DIM + BLOCK_CODE] = jnp.where(code, 1.0, 0.0).astype(BF16)
                kw_ref[g, cols_t, :] = win_ref[g * HEAD_DIM:(g + 1) * HEAD_DIM, cols_t].T.astype(BF16)

    gates_t = gates_ref[...]
    vc_t = kcv_ref[:, GD:2 * GD].T.astype(BF16)
    q_ts = []
    for g in groups:
        q_t = jnp.concatenate([qt_ref[(g * N_REP + r) * HEAD_DIM:(g * N_REP + r + 1) * HEAD_DIM, :]
                               for r in range(N_REP)], axis=1)
        q_ts.append((q_t * scale).astype(BF16))

    c_idx = lax.broadcasted_iota(jnp.int32, (n_cmp, 1), 0)
    valid_c = (c_idx * CMP_STRIDE + (CMP_BLOCK - 1) <= pos_row) & (c_idx < n_blk)
    ovt = ovt_ref[...]
    o_cs, q_codes = [], []
    for g in groups:
        kc = kcv_ref[:, g * HEAD_DIM:(g + 1) * HEAD_DIM].astype(BF16)
        e_c, inv_c = _col_softmax(_mask_heads(_dot(kc, q_ts[g]), valid_c, tq))
        o_cs.append(_dot(vc_t[g * HEAD_DIM:(g + 1) * HEAD_DIM, :], e_c.astype(BF16)) * inv_c)
        pc_sum = e_c[:, 0:tq] * inv_c[:, 0:tq]
        for r in range(1, N_REP):
            pc_sum = pc_sum + e_c[:, r * tq:(r + 1) * tq] * inv_c[:, r * tq:(r + 1) * tq]
        pc_hi = pc_sum.astype(BF16)
        pc_lo = (pc_sum - pc_hi.astype(F32)).astype(BF16)
        imp_t = _dot(ovt, pc_hi) + _dot(ovt, pc_lo)
        sel_t = _select_blocks(imp_t, pos_row, n_slc)
        bias = jnp.concatenate([jnp.where(sel_t > 0.5, 0.0, NEG), jnp.zeros((BLOCK_CODE - n_slc, tq), F32)], axis=0)
        q_codes.append(jnp.concatenate([q_ts[g], jnp.concatenate([bias] * N_REP, axis=1).astype(BF16)], axis=0))

    n_keys = q0 + tq
    rest = n_keys % SEL_KEYS
    half_tail = (rest > 0) & (rest <= SEL_KEYS // 2)
    n_steps = n_keys // SEL_KEYS + (rest > SEL_KEYS // 2).astype(jnp.int32)
    tail_k0 = pl.multiple_of((n_keys // SEL_KEYS) * SEL_KEYS, SEL_KEYS)
    last_full = jnp.logical_not(half_tail)

    def score_chunk(k0, size, diagonal):
        key = k0 + lax.broadcasted_iota(jnp.int32, (size, 1), 0)
        for g in groups:
            sc = _dot(ks_ref[g, pl.ds(k0, size), :], q_codes[g])
            if diagonal:
                sc = _mask_heads(sc, key <= pos_row, tq)
            s_ref[g, pl.ds(k0, size), :] = sc.astype(BF16)
            m_ref[g] = jnp.maximum(m_ref[g], jnp.max(sc, axis=0, keepdims=True))

    m_ref[...] = jnp.full(m_ref.shape, NEG, F32)

    def score_step(i, carry):
        score_chunk(pl.multiple_of(i * SEL_KEYS, SEL_KEYS), SEL_KEYS, False)
        return carry

    lax.fori_loop(0, n_steps - last_full.astype(jnp.int32), score_step, 0)

    @pl.when(last_full)
    def _():
        score_chunk(pl.multiple_of((n_steps - 1) * SEL_KEYS, SEL_KEYS), SEL_KEYS, True)

    @pl.when(half_tail)
    def _():
        score_chunk(tail_k0, SEL_KEYS // 2, True)

    m_sel = [m_ref[g].astype(BF16) for g in groups]

    acc_ref[...] = jnp.zeros(acc_ref.shape, F32)

    def value_chunk(k0, size):
        for g in groups:
            p = jnp.exp(s_ref[g, pl.ds(k0, size), :] - m_sel[g])
            v_t = slc_ref[v_row0(g):v_row0(g) + HEAD_DIM, pl.ds(k0, size)].astype(BF16)
            acc_ref[g] += _dot(_with_ones(v_t), p)

    def value_step(i, carry):
        value_chunk(pl.multiple_of(i * SEL_KEYS, SEL_KEYS), SEL_KEYS)
        return carry

    lax.fori_loop(0, n_steps, value_step, 0)

    @pl.when(half_tail)
    def _():
        value_chunk(tail_k0, SEL_KEYS // 2)

    n_prev = -(-(WINDOW - 1) // WIN_TILE)
    sub_tiles = tq // WIN_TILE
    win_k0, win_ok = {}, {}
    for u in range(sub_tiles):
        pos_u = pos_row[:, u * WIN_TILE:(u + 1) * WIN_TILE]
        for b in range(n_prev + 1):
            kb = q0 // WIN_TILE + u - n_prev + b
            win_k0[u, b] = pl.multiple_of(jnp.maximum(kb, 0) * WIN_TILE, WIN_TILE)
            kp = kb * WIN_TILE + lax.broadcasted_iota(jnp.int32, (WIN_TILE, 1), 0)
            win_ok[u, b] = (kp <= pos_u) & (kp > pos_u - WINDOW) & (kp >= 0)

    out_blocks = []
    for g in groups:
        o_s = _normalised(acc_ref[g])
        o_ws = []
        for u in range(sub_tiles):
            q_u = jnp.concatenate([q_ts[g][:, r * tq + u * WIN_TILE:r * tq + (u + 1) * WIN_TILE]
                                   for r in range(N_REP)], axis=1)
            s_w = jnp.concatenate([_mask_heads(_dot(kw_ref[g, pl.ds(win_k0[u, b], WIN_TILE), :], q_u),
                                               win_ok[u, b], WIN_TILE) for b in range(n_prev + 1)], axis=0)
            e_w = jnp.exp(s_w.astype(BF16) - jnp.max(s_w, axis=0, keepdims=True).astype(BF16))
            o_w = None
            for b in range(n_prev + 1):
                v_t = win_ref[v_row0(g):v_row0(g) + HEAD_DIM, pl.ds(win_k0[u, b], WIN_TILE)].astype(BF16)
                part = _dot(_with_ones(v_t), e_w[b * WIN_TILE:(b + 1) * WIN_TILE])
                o_w = part if o_w is None else o_w + part
            o_ws.append(_normalised(o_w))
        for r in range(N_REP):
            h = g * N_REP + r
            c = slice(r * tq, (r + 1) * tq)
            o_w = jnp.concatenate([o_ws[u][:, r * WIN_TILE:(r + 1) * WIN_TILE] for u in range(sub_tiles)], axis=1)
            out_blocks.append(gates_t[h:h + 1] * o_cs[g][:, c] + gates_t[N_HEADS + h:N_HEADS + h + 1] * o_s[:, c]
                              + gates_t[2 * N_HEADS + h:2 * N_HEADS + h + 1] * o_w)
    o_ref[...] = jnp.concatenate(out_blocks, axis=0).T


def _overlap_t(n_cmp_rows, n_blk, n_slc):
    c = np.arange(n_cmp_rows)
    start = c * CMP_STRIDE
    end = start + CMP_BLOCK - 1
    j0 = np.arange(n_slc) * SLC_BLOCK
    ov = (end[None, :] >= j0[:, None]) & (start[None, :] < j0[:, None] + SLC_BLOCK) & (c[None, :] < n_blk)
    return jnp.asarray(ov.astype(np.float32), BF16)


def _attn_prompt(q_t, kcv, slc_t, win_t, gates, l):
    n, _, s = q_t.shape
    n_cmp = kcv.shape[1]
    n_blk = s // CMP_STRIDE - CMP_BLOCK // CMP_STRIDE + 1
    n_slc = s // SLC_BLOCK
    ovt = _overlap_t(n_cmp, n_blk, n_slc)
    tq = Q_BLOCK
    cols = N_REP * tq
    assert s % SEL_KEYS == 0 and (SEL_KEYS // 2) % tq == 0 and n_slc <= BLOCK_CODE
    return pl.pallas_call(
        functools.partial(_attn_prompt_kernel, s=s, n_blk=n_blk),
        grid=(n, s // tq),
        in_specs=[pl.BlockSpec((None, D_ATTN, tq), lambda i, j: (i, 0, j)),
                  pl.BlockSpec((None,) + kcv.shape[1:], lambda i, j: (i, 0, 0)),
                  pl.BlockSpec((None, None, D_ROWS, s), lambda i, j: (l, i, 0, 0)),
                  pl.BlockSpec((None, None, D_ROWS, s), lambda i, j: (l, i, 0, 0)),
                  pl.BlockSpec((None, GATE_ROWS, tq), lambda i, j: (i, 0, j)),
                  pl.BlockSpec(ovt.shape, lambda i, j: (0, 0))],
        out_specs=pl.BlockSpec((None, tq, D_ATTN), lambda i, j: (i, j, 0)),
        out_shape=jax.ShapeDtypeStruct((n, s, D_ATTN), F32),
        scratch_shapes=[pltpu.VMEM((N_KV_HEADS, s, HEAD_DIM + BLOCK_CODE), BF16),
                        pltpu.VMEM((N_KV_HEADS, s, HEAD_DIM), BF16),
                        pltpu.VMEM((N_KV_HEADS, s, cols), BF16),
                        pltpu.VMEM((N_KV_HEADS, HEAD_DIM + ONES_ROWS, cols), F32),
                        pltpu.VMEM((N_KV_HEADS, 1, cols), F32)],
        compiler_params=pltpu.CompilerParams(dimension_semantics=("arbitrary", "arbitrary"),
                                             vmem_limit_bytes=VMEM_LIMIT),
        name="attn_prompt",
    )(q_t, kcv, slc_t, win_t, gates, ovt)


PAGE = 128
SAMPLE_SEQS = 4
PV_PAGES = 2


def _sample_geometry(past):
    length = past + 1
    padded = -(-length // SLC_BLOCK) * SLC_BLOCK
    n_pc = padded // CMP_STRIDE
    n_pc_pad = -(-n_pc // CMP_STRIDE) * CMP_STRIDE
    return dict(n_pc=n_pc, n_pc_pad=n_pc_pad, n_blk=n_pc - CMP_BLOCK // CMP_STRIDE + 1, n_slc=padded // SLC_BLOCK)


def _attn_sample_kernel(pt_ref, q_ref, new_ref, gates_ref, wincol_ref, cwin_ref, ov_ref,
                        perm_ref, w1ab_ref, bias_ref, w2_ref, ccmp_hbm, cslc_hbm, *rest, l, past, n_pages, nseq):
    o_ref, wout_ref, flat_ref, cmp_buf, slc_buf, sem = rest[-6:]
    step = pl.program_id(0)
    slot = step % 2

    def page_copies(of_step, into_slot, lookup):
        copies = []
        for u in range(nseq):
            for j in range(n_pages):
                page = pt_ref[of_step * nseq + u, j] if lookup else 0
                k = u * n_pages + j
                copies.append(pltpu.make_async_copy(ccmp_hbm.at[l, page], cmp_buf.at[into_slot, k], sem.at[into_slot, 0]))
                copies.append(pltpu.make_async_copy(cslc_hbm.at[l, page], slc_buf.at[into_slot, k], sem.at[into_slot, 1]))
        return copies

    @pl.when(step == 0)
    def _():
        for k, c in enumerate(page_copies(0, 0, True)):
            c.start(priority=k % 2)

    @pl.when(step + 1 < pl.num_programs(0))
    def _():
        for k, c in enumerate(page_copies(step + 1, 1 - slot, True)):
            c.start(priority=k % 2)

    for c in page_copies(step, slot, False):
        c.wait()
    cmp_pages = [[cmp_buf.at[slot, u * n_pages + j] for j in range(n_pages)] for u in range(nseq)]
    slc_pages = [[slc_buf.at[slot, u * n_pages + j] for j in range(n_pages)] for u in range(nseq)]
    geo = _sample_geometry(past)
    n_pc_pad, n_blk, n_slc = geo["n_pc_pad"], geo["n_blk"], geo["n_slc"]
    pos = past
    lb = cwin_ref.shape[-1]
    scale = HEAD_DIM ** -0.5
    head = lax.broadcasted_iota(jnp.int32, (N_HEADS, 1), 0)
    lane = lax.broadcasted_iota(jnp.int32, (1, GD), 1)
    lane_h = lax.broadcasted_iota(jnp.int32, (N_HEADS, GD), 1)
    own = (lane_h // HEAD_DIM) == (head // N_REP)

    def new_row(u, branch, kv):
        o = (branch * 2 + kv) * GD
        return new_ref[u, :, o:o + GD]

    def rounded(x):
        return x.astype(BF16).astype(F32)

    kcv = []
    pages_per_span = FLAT_SPAN // PAGE
    past_pc = past // CMP_STRIDE
    for u in range(nseq):
        row0 = u * n_pc_pad
        for t in range(n_pages // pages_per_span):
            x_t = jnp.concatenate([cmp_pages[u][t * pages_per_span + w][...].reshape(2 * GD, PAGE)
                                   for w in range(pages_per_span)], axis=1)
            _fill_flat(x_t.astype(BF16), perm_ref[...], flat_ref, row0 + t * CMP_STRIDE)
        for kv in range(2):
            flat_ref[kv, row0 + past_pc:row0 + n_pc_pad, :] = jnp.zeros((n_pc_pad - past_pc, CMP_FLAT), BF16)
            flat_ref[kv, row0 + past_pc:row0 + past_pc + 1, 0:GD] = new_row(u, 0, kv).astype(BF16)
    for kv in range(2):
        kcv.append(_compress_mlp(flat_ref[kv], w1ab_ref[kv], bias_ref[kv], w2_ref[kv]).astype(BF16))
    for u in range(nseq):
        _attn_sample_one(u, kcv[0][u * n_pc_pad:(u + 1) * n_pc_pad], kcv[1][u * n_pc_pad:(u + 1) * n_pc_pad],
                         q_ref, new_row, rounded, gates_ref, wincol_ref, cwin_ref, ov_ref,
                         slc_pages[u], o_ref, wout_ref, geo=geo, past=past, consts=(head, lane, lane_h, own))


def _attn_sample_one(u, kc, vc, q_ref, new_row, rounded, gates_ref, wincol_ref, cwin_ref, ov_ref,
                     slc_pages, o_ref, wout_ref, *, geo, past, consts):
    head, lane, lane_h, own = consts
    n_pc_pad, n_blk, n_slc = geo["n_pc_pad"], geo["n_blk"], geo["n_slc"]
    n_pages = len(slc_pages)
    pos = past
    lb = cwin_ref.shape[-1]
    scale = HEAD_DIM ** -0.5
    q_rows = jnp.concatenate([q_ref[u, :, h * HEAD_DIM:(h + 1) * HEAD_DIM] for h in range(N_HEADS)], axis=0)
    qx = (jnp.where(own, jnp.concatenate([q_rows] * N_KV_HEADS, axis=1), 0.0) * scale).astype(BF16)
    qx32 = qx.astype(F32)
    c_idx = lax.broadcasted_iota(jnp.int32, (1, n_pc_pad), 1)
    valid = (c_idx * CMP_STRIDE + (CMP_BLOCK - 1) <= pos) & (c_idx < n_blk)
    p_c = _masked_softmax(_dot_nt(qx, kc), valid)
    o_c = _dot(p_c.astype(BF16), vc)

    cur = pos // SLC_BLOCK
    forced = (lane == 0) | (lane == cur) | (lane == cur - 1)
    jp = lax.broadcasted_iota(jnp.int32, (GD, GD), 0)
    jj = lax.broadcasted_iota(jnp.int32, (GD, GD), 1)
    sels = []
    for g in range(N_KV_HEADS):
        pc_sum = jnp.sum(p_c[g * N_REP:(g + 1) * N_REP], axis=0, keepdims=True)
        pc_hi = pc_sum.astype(BF16)
        pc_lo = (pc_sum - pc_hi.astype(F32)).astype(BF16)
        imp = jnp.sum(_dot(jnp.concatenate([pc_hi, pc_lo], axis=0), ov_ref[...]), axis=0, keepdims=True)
        score = jnp.where(lane <= cur, jnp.where(forced, BIG, imp), -BIG)
        score = jnp.where(lane < n_slc, score, -2.0 * BIG)
        score_b = jnp.broadcast_to(score, (GD, GD))
        score_a = score_b.T
        beats = (score_a > score_b) | ((score_a == score_b) & (jp < jj))
        rank = jnp.sum(beats.astype(F32), axis=0, keepdims=True)
        sel = ((rank < float(min(N_SEL, n_slc))) & (lane < n_slc)).astype(F32)
        sels.append(jnp.broadcast_to(sel, (N_REP, GD)))
    sel_h = jnp.concatenate(sels, axis=0)

    blocks_per_tile = GD // SLC_BLOCK
    chosen = []
    for t in range(past // GD):
        tile = sel_h[:, t * blocks_per_tile:t * blocks_per_tile + 1]
        for b in range(1, blocks_per_tile):
            tile = jnp.where(lane >= b * SLC_BLOCK, sel_h[:, t * blocks_per_tile + b:t * blocks_per_tile + b + 1], tile)
        chosen.append(jnp.broadcast_to(tile, (N_HEADS, GD)))
    chosen = jnp.concatenate(chosen, axis=-1) > 0.5
    k_past = jnp.concatenate([slc_pages[j][0].astype(BF16) for j in range(n_pages)], axis=-1)
    s_past = jnp.where(chosen, _dot(qx, k_past), NEG)
    new_ok = jnp.sum(jnp.where(lane == pos // SLC_BLOCK, sel_h, 0.0), axis=-1, keepdims=True) > 0.5
    s_new = jnp.where(new_ok, jnp.sum(qx32 * rounded(new_row(u, 1, 0)), axis=-1, keepdims=True), NEG)
    m = jnp.maximum(jnp.max(s_past, axis=-1, keepdims=True), s_new)
    p_past = jnp.where(chosen, jnp.exp(s_past - m), 0.0)
    p_new = jnp.where(new_ok, jnp.exp(s_new - m), 0.0)
    denom = jnp.maximum(jnp.sum(p_past, axis=-1, keepdims=True) + p_new, 1e-30)
    o_s = rounded(p_new) * rounded(new_row(u, 1, 1))
    for j in range(0, n_pages, PV_PAGES):
        v_run = jnp.concatenate([slc_pages[j + w][1].astype(BF16) for w in range(PV_PAGES)], axis=-1)
        o_s = o_s + _dot_nt(p_past[:, j * PAGE:(j + PV_PAGES) * PAGE].astype(BF16), v_run)
    o_s = o_s / denom

    kpos = (past - lb) + lax.broadcasted_iota(jnp.int32, (1, lb), 1)
    w_ok = (kpos <= pos) & (kpos > pos - WINDOW)
    s_w = jnp.where(w_ok, _dot(qx, cwin_ref[u, 0].astype(BF16)), NEG)
    s_wn = jnp.sum(qx32 * rounded(new_row(u, 2, 0)), axis=-1, keepdims=True)
    m = jnp.maximum(jnp.max(s_w, axis=-1, keepdims=True), s_wn)
    p_w = jnp.where(w_ok, jnp.exp(s_w - m), 0.0)
    p_wn = jnp.exp(s_wn - m)
    denom = jnp.maximum(jnp.sum(p_w, axis=-1, keepdims=True) + p_wn, 1e-30)
    o_w = (_dot_nt(p_w.astype(BF16), cwin_ref[u, 1].astype(BF16))
           + rounded(p_wn) * rounded(new_row(u, 2, 1))) / denom

    gates = gates_ref[u]
    gate_id = lax.broadcasted_iota(jnp.int32, (N_HEADS, GATE_ROWS), 1)
    gate = lambda br: jnp.sum(jnp.where(gate_id == head + br * N_HEADS, gates, 0.0), axis=-1, keepdims=True)
    o = gate(0) * o_c + gate(1) * o_s + gate(2) * o_w
    for h in range(N_HEADS):
        g = h // N_REP
        o_ref[u, :, h * HEAD_DIM:(h + 1) * HEAD_DIM] = o[h:h + 1, g * HEAD_DIM:(g + 1) * HEAD_DIM]

    for kv in range(2):
        wout_ref[u, kv, :, 0:lb - 1] = cwin_ref[u, kv, :, 1:lb]
        wout_ref[u, kv, :, lb - 1:lb] = wincol_ref[u, kv * GD:(kv + 1) * GD, :]


def _attn_sample(l, q_tok, new_tok, gates, win_col, cwin_t, ccmp_t, cslc_t, page_table, w1ab, bias, w2, win_prev):
    nb = q_tok.shape[0]
    n_pages = page_table.shape[1]
    past = n_pages * PAGE
    geo = _sample_geometry(past)
    lb = cwin_t.shape[-1]
    assert FLAT_SPAN % PAGE == 0 and n_pages % (FLAT_SPAN // PAGE) == 0
    perm = _piece_perm()
    ov = _overlap_t(geo["n_pc_pad"], geo["n_blk"], GD).T
    ov = jnp.where(jnp.arange(GD)[None, :] < geo["n_slc"], ov, 0).astype(BF16)
    assert GD % SLC_BLOCK == 0 and past % GD == 0
    nseq = SAMPLE_SEQS if nb % SAMPLE_SEQS == 0 else 1
    seq = lambda a: pl.BlockSpec((nseq,) + a.shape[1:], lambda i, pt: (i,) + (0,) * (a.ndim - 1))
    full = lambda a: pl.BlockSpec(a.shape, lambda i, pt: (0,) * a.ndim)
    operands = [page_table, q_tok, new_tok, gates, win_col, cwin_t, ov, perm, w1ab, bias, w2, ccmp_t, cslc_t]
    win_prev = () if win_prev is None else (win_prev,)
    page_buf = pltpu.VMEM((2, nseq * n_pages, 2, GD, PAGE), F32)
    grid_spec = pltpu.PrefetchScalarGridSpec(
        num_scalar_prefetch=1,
        grid=(nb // nseq,),
        in_specs=[seq(q_tok), seq(new_tok), seq(gates), seq(win_col),
                  pl.BlockSpec((None, nseq, 2, GD, lb), lambda i, pt: (l, i, 0, 0, 0)),
                  full(ov), full(perm), full(w1ab), full(bias), full(w2)]
                 + [pl.BlockSpec(memory_space=pl.ANY)] * (2 + len(win_prev)),
        out_specs=[pl.BlockSpec((nseq, 1, D_ATTN), lambda i, pt: (i, 0, 0)),
                   pl.BlockSpec((None, nseq, 2, GD, lb), lambda i, pt: (l, i, 0, 0, 0))],
        scratch_shapes=[pltpu.VMEM((2, nseq * geo["n_pc_pad"], CMP_FLAT), BF16), page_buf, page_buf,
                        pltpu.SemaphoreType.DMA((2, 2))],
    )
    return pl.pallas_call(
        functools.partial(_attn_sample_kernel, l=l, past=past, n_pages=n_pages, nseq=nseq),
        grid_spec=grid_spec,
        out_shape=[jax.ShapeDtypeStruct((nb, 1, D_ATTN), F32),
                   jax.ShapeDtypeStruct((cwin_t.shape[0], nb, 2, GD, lb), F32)],
        input_output_aliases={len(operands): 1} if win_prev else {},
        compiler_params=pltpu.CompilerParams(dimension_semantics=("arbitrary",), vmem_limit_bytes=VMEM_LIMIT),
        name="attn_sample",
    )(*operands, *win_prev)


def _prep_layer(l, w_norm, w_in, w_out, w_dw, b_dw, ln_g, ln_b, w_pw, b_pw, w_pool, pool_scale,
                g_q, g_k, cmp_pe, cmp_w1, cmp_w2):
    sizes = (D_CONV, D_CONV, D_CONV, D_POOL, D_POOL, D_ATTN, D_KV, N_BRANCH * N_HEADS, D_ATTN)
    offs = [0] + [int(v) for v in np.cumsum(sizes)]
    col = lambda k: w_in[l][:, offs[k]:offs[k + 1]]
    a_val, a_gate, z_a, b_in, z_b, q, kv, gate, z_c = (col(k) for k in range(9))
    pad = jnp.zeros((D_MODEL, GATE_ROWS - N_BRANCH * N_HEADS), F32)
    wtok = jnp.concatenate([a_val, a_gate, b_in, z_a, z_b, z_c], axis=1).astype(BF16)
    wfeat = jnp.concatenate([q, kv, gate, pad], axis=1).T.astype(BF16)
    ones = jnp.ones((N_KV_HEADS * HEAD_DIM,), F32)
    gcol = jnp.concatenate([jnp.tile(g_q[l], N_HEADS)]
                           + [piece for br in range(N_BRANCH)
                              for piece in (jnp.tile(g_k[l, br], N_KV_HEADS), ones)])[:, None]
    wpool_bd = jnp.zeros((D_POOL, D_POOL), F32)
    for gi in range(len(POOL_WINDOWS)):
        wpool_bd = wpool_bd.at[gi * POOL_GROUP:(gi + 1) * POOL_GROUP, gi * POOL_GROUP:(gi + 1) * POOL_GROUP].set(w_pool[l, gi])
    row = lambda v: v[None, :]

    def both_groups(w):
        z = jnp.zeros_like(w)
        return jnp.stack([jnp.concatenate([w, z], axis=2), jnp.concatenate([z, w], axis=2)], axis=1).reshape(-1, GD)

    half = CMP_STRIDE * HEAD_DIM
    w1 = cmp_w1[l].astype(BF16)
    w1ab = jnp.stack([jnp.concatenate(
        [both_groups(w1[kv, :half].reshape(CMP_STRIDE, HEAD_DIM, HEAD_DIM)),
         both_groups(w1[kv, half:].reshape(CMP_STRIDE, HEAD_DIM, HEAD_DIM))], axis=1) for kv in range(2)])
    w2p = jnp.stack([both_groups(cmp_w2[l, kv].astype(BF16)[None]) for kv in range(2)])
    cbias = jnp.tile(_cmp_bias(cmp_pe[l].reshape(2, 1, CMP_BLOCK * HEAD_DIM), w1), (1, 1, N_KV_HEADS))
    return dict(
        wnorm=row(w_norm[l]), wtok=wtok, wfeat=wfeat, gcol=gcol,
        wdw=w_dw[l], bdw=row(b_dw[l]), lng=row(ln_g[l]), lnb=row(ln_b[l]),
        wpw=w_pw[l].astype(BF16), bpw=row(b_pw[l]),
        wpool=wpool_bd.astype(BF16), pscale=row(pool_scale[l]),
        w1ab=w1ab, cbias=cbias, w2p=w2p,
        wout=w_out[l].astype(BF16),
    )


def _rope_tables(pos):
    inv = ROPE_THETA ** (-jnp.arange(ROT_HALF, dtype=F32) * 2.0 / ROT_DIM)
    ang = pos.astype(F32)[:, None] * inv[None, :]
    return jnp.cos(ang).T, jnp.sin(ang).T


def _prompt_layer(l, depth, x, p, cos_t, sin_t, kv_prev):
    glu, b_in, sz, gates, q_t, *kv = _in_proj(
        x, p["wnorm"], p["wtok"], p["wfeat"], p["gcol"], cos_t, sin_t, min(IN_PROJ_ROWS, x.shape[1]), l, depth, kv_prev)
    cmp_t, slc_t, win_t = kv
    kcv = _compress_prompt(cmp_t, l, p["w1ab"], p["cbias"], p["w2p"])
    o_attn = _attn_prompt(q_t, kcv, slc_t, win_t, gates, l)
    y = _mix_out_prompt(x, glu, b_in, sz, o_attn, p["wdw"], p["bdw"], p["lng"], p["lnb"], p["wpw"], p["bpw"],
                        p["wpool"], p["pscale"], p["wout"], t=min(MIX_OUT_ROWS, x.shape[1]))
    return y, glu, b_in, kv


def _sample_layer(l, depth, x, p, cos_t, sin_t, sc, sp, cwin_t, ccmp_t, cslc_t, page_table, past, kv_prev, win_prev):
    nb = x.shape[0]
    glu, b_in, sz, gates, q_t, *kv = _in_proj(
        x[None], p["wnorm"], p["wtok"], p["wfeat"], p["gcol"], cos_t, sin_t, nb, l, depth, kv_prev)
    glu, b_in, sz, gates = glu[0], b_in[0], sz[0], gates[0]
    yab, sc_new, sp_new = _convpool_sample(sc, sp, glu, b_in, sz, p["wdw"], p["bdw"], p["lng"], p["lnb"],
                                           p["wpw"], p["bpw"], p["wpool"], p["pscale"], pos=past)
    new_t = jnp.concatenate([a[l, 0] for a in kv], axis=0)
    new_tok = new_t.T[:, None, :]
    q_tok = q_t[0].T[:, None, :]
    win_col = kv[2][l, 0].T[:, :, None]
    o_attn, win_next = _attn_sample(l, q_tok, new_tok, gates.T[:, None, :], win_col, cwin_t, ccmp_t, cslc_t,
                                    page_table, p["w1ab"], p["cbias"], p["w2p"], win_prev)
    y = _out_proj(x[None], yab[None], o_attn.reshape(1, nb, D_ATTN), sz[None], p["wout"], tm=nb)[0]
    return y, sc_new, sp_new, win_next, kv


def _rows_from_feat(a):
    lead = a.shape[:-2]
    a = a.reshape(lead + (2, N_KV_HEADS, HEAD_DIM, a.shape[-1]))
    nl = len(lead)
    return jnp.transpose(a, tuple(range(nl)) + (nl + 3, nl, nl + 1, nl + 2))


def kernel(x_prompt, x_sample, state_conv, state_pool, cache_win_kv, cache_cmp_kv, cache_slc_kv, page_table,
           w_norm, w_in, w_out, w_dw, b_dw, ln_g, ln_b, w_pw, b_pw, w_pool, pool_scale,
           g_q, g_k, cmp_pe, cmp_w1, cmp_w2):
    bp, s, _ = x_prompt.shape
    bs, t_new, _ = x_sample.shape
    depth = w_in.shape[0]
    assert t_new == 1 and cache_cmp_kv.shape[2] == PAGE and s % SEL_KEYS == 0 and s % MIX_OUT_ROWS == 0
    past = page_table.shape[1] * PAGE
    lb = cache_win_kv.shape[2]

    def to_feat(c):
        c = jnp.transpose(c, (0, 1, 3, 4, 5, 2))
        return c.reshape(c.shape[:3] + (GD, c.shape[-1]))
    cwin_t, ccmp_t, cslc_t = to_feat(cache_win_kv), to_feat(cache_cmp_kv), to_feat(cache_slc_kv)
    sc_all = jnp.transpose(state_conv, (0, 2, 1, 3))
    sp_all = jnp.transpose(state_pool, (0, 2, 1, 3))
    cos_p, sin_p = _rope_tables(jnp.arange(s))
    cos_s, sin_s = _rope_tables(jnp.full((bs,), past))

    xp, xs = x_prompt, x_sample[:, 0, :]
    conv_p, pool_p, conv_s, pool_s = [], [], [], []
    kv_p = kv_s = win_s = None
    for l in range(depth):
        p = _prep_layer(l, w_norm, w_in, w_out, w_dw, b_dw, ln_g, ln_b, w_pw, b_pw, w_pool, pool_scale,
                        g_q, g_k, cmp_pe, cmp_w1, cmp_w2)
        xp, glu, b_in, kv_p = _prompt_layer(l, depth, xp, p, cos_p, sin_p, kv_p)
        conv_p.append(glu[:, s - CONV_STATE:, :])
        pool_p.append(b_in[:, s - POOL_STATE:, :])
        xs, sc_new, sp_new, win_s, kv_s = _sample_layer(
            l, depth, xs, p, cos_s, sin_s, sc_all[l], sp_all[l], cwin_t, ccmp_t, cslc_t, page_table, past,
            kv_s, win_s)
        conv_s.append(jnp.transpose(sc_new, (1, 0, 2)))
        pool_s.append(jnp.transpose(sp_new, (1, 0, 2)))
    cmp_p, slc_p, win_p = kv_p
    cmp_s, slc_s, _ = kv_s
    new_rows = lambda a: _rows_from_feat(a[:, 0])[:, :, None]
    return (xp, xs[:, None, :], jnp.stack(conv_p), jnp.stack(pool_p),
            _rows_from_feat(win_p[:, :, :, s - min(WINDOW, s):]), _rows_from_feat(cmp_p), _rows_from_feat(slc_p),
            jnp.stack(conv_s), jnp.stack(pool_s), _rows_from_feat(win_s.reshape(depth, bs, D_ROWS, lb)),
            new_rows(cmp_s), new_rows(slc_s))
```
